```python
import jax, jax.numpy as jnp
from jax import lax
import numpy as np

D_MODEL = 1024
BATCH = 16
SEQ = 256
DEPTH = 1
DEC_BATCH = 4
DEC_SEQ = 4096
PAST_LEN = 256

GRID_W = 64
HEAD_DIM = 64
N_Q_HEADS = 8
N_KV_HEADS = 2
GQA_GROUP = N_Q_HEADS // N_KV_HEADS
D_ATTN = N_Q_HEADS * HEAD_DIM
D_KV = N_KV_HEADS * HEAD_DIM
RWKV_HEAD = 64
N_RWKV_HEADS = 8
D_RWKV = N_RWKV_HEADS * RWKV_HEAD
DECAY_LORA = 32
AAA_LORA = 32
GATE_LORA = 96
D_RWKV_IN = 3 * D_RWKV + 2 * DECAY_LORA + 2 * AAA_LORA + GATE_LORA
D_IN = D_ATTN + 2 * D_KV + D_RWKV_IN
D_MIX = D_ATTN + D_RWKV
RWKV_SPLITS = [D_RWKV, 2 * D_RWKV, 3 * D_RWKV, 3 * D_RWKV + 2 * DECAY_LORA, 3 * D_RWKV + 2 * DECAY_LORA + 2 * AAA_LORA]
N_GROUPS = 4
EXPERTS_PER_GROUP = 4
N_EXPERTS = N_GROUPS * EXPERTS_PER_GROUP
TOP_K_FINE = 2
D_EXPERT = 512
Q_BLOCK = 128
ROPE_THETA = 10000.0
NORM_EPS = 1e-6
GN_EPS = 64e-5
DECAY_SCALE = 0.6065306597

kernel_name = 'hymba_gqa_rwkv7_hmoe_diffusion_step'


def rms_norm(x, g):
    xf = x.astype(jnp.float32)
    y = xf * lax.rsqrt(jnp.mean(xf * xf, axis=-1, keepdims=True) + NORM_EPS)
    return (y * g.astype(jnp.float32)).astype(x.dtype)


def modulate(x, g, shift, scale):
    return rms_norm(x, g) * (1 + scale[:, None, :]) + shift[:, None, :]


def rope_2d(x, row, col):
    half = HEAD_DIM // 2
    inv = ROPE_THETA ** (-jnp.arange(0, half, 2, dtype=jnp.float32) / half)

    def rot(xa, pos):
        ang = pos.astype(jnp.float32)[:, None] * inv[None, :]
        cos, sin = jnp.cos(ang), jnp.sin(ang)
        x1, x2 = jnp.split(xa.astype(jnp.float32), 2, axis=-1)
        return jnp.concatenate([x1 * cos - x2 * sin, x1 * sin + x2 * cos], axis=-1)

    return jnp.concatenate([rot(x[..., :half], row), rot(x[..., half:], col)], axis=-1).astype(x.dtype)


def block_attention(q, k, v):
    B, Hk, G, Tq, hd = q.shape
    nb = Tq // Q_BLOCK
    qb = jnp.moveaxis(q.reshape(B, Hk, G, nb, Q_BLOCK, hd), 3, 0)
    scale = HEAD_DIM ** -0.5

    def one(qblk):
        s = jnp.einsum('bkgqd,bksd->bkgqs', qblk, k).astype(jnp.float32) * scale
        p = jax.nn.softmax(s, axis=-1).astype(v.dtype)
        return jnp.einsum('bkgqs,bksd->bkgqd', p, v)

    ob = lax.map(one, qb)
    return jnp.moveaxis(ob, 0, 3).reshape(B, Hk, G, Tq, hd)


def centred_shift(p, mu):
    zero = jnp.zeros_like(p[:, :1])
    prev = jnp.concatenate([zero, p[:, :-1]], axis=1)
    nxt = jnp.concatenate([p[:, 1:], zero], axis=1)
    return p + mu[0] * (prev - p) + mu[1] * (nxt - p)


def rwkv7_scan(r, w, k, v, kk, b, s0, reverse):
    def step(S, inp):
        r_t, w_t, k_t, v_t, kk_t, b_t = inp
        sa = jnp.einsum('bhvk,bhk->bhv', S, kk_t)
        S = S * w_t[:, :, None, :] - sa[..., None] * b_t[:, :, None, :] + v_t[..., None] * k_t[:, :, None, :]
        return S, jnp.einsum('bhvk,bhk->bhv', S, r_t)

    xs = tuple(jnp.moveaxis(a, 1, 0) for a in (r, w, k, v, kk, b))
    S, y = lax.scan(step, s0, xs, reverse=reverse)
    return jnp.moveaxis(y, 0, 1), S


def rwkv7_group(rw, lp, ctx_state):
    B, T, _ = rw.shape
    f32 = jnp.float32
    hd = lambda t: t.reshape(B, T, N_RWKV_HEADS, RWKV_HEAD)
    rw = centred_shift(rw, lp['mu_shift'])
    r, k, v, wd, ad, gd = jnp.split(rw, RWKV_SPLITS, axis=-1)
    decay_logit = lp['w0'] + jnp.einsum('btzl,zlc->btzc', jnp.tanh(wd.reshape(B, T, 2, DECAY_LORA)), lp['w_lora_up'])
    decay = jnp.exp(-DECAY_SCALE * jax.nn.sigmoid(decay_logit.astype(f32)))
    a = jax.nn.sigmoid((lp['a0'] + jnp.einsum('btzl,zlc->btzc', ad.reshape(B, T, 2, AAA_LORA), lp['a_lora_up'])).astype(f32))
    g = jnp.einsum('btl,lc->btc', jax.nn.sigmoid(gd), lp['g_lora_up']).astype(f32)
    r, k, v = r.astype(f32), k.astype(f32), v.astype(f32)
    kk = hd(k * lp['k_k'].astype(f32))
    kk = (kk * lax.rsqrt(jnp.sum(kk * kk, axis=-1, keepdims=True) + 1e-12)).reshape(B, T, D_RWKV)
    k_dir = k[:, :, None, :] * (1.0 + (a - 1.0) * lp['k_a'].astype(f32))
    b_dir = kk[:, :, None, :] * a
    if ctx_state is None:
        s0 = jnp.zeros((B, 2, N_RWKV_HEADS, RWKV_HEAD, RWKV_HEAD), f32)
    else:
        s0 = ctx_state.astype(f32)
    y_f, s_f = rwkv7_scan(hd(r), hd(decay[:, :, 0]), hd(k_dir[:, :, 0]), hd(v), hd(kk), hd(b_dir[:, :, 0]), s0[:, 0], False)
    y_b, s_b = rwkv7_scan(hd(r), hd(decay[:, :, 1]), hd(k_dir[:, :, 1]), hd(v), hd(kk), hd(b_dir[:, :, 1]), s0[:, 1], True)
    y = y_f + y_b
    mean = jnp.mean(y, axis=-1, keepdims=True)
    var = jnp.mean(jnp.square(y - mean), axis=-1, keepdims=True)
    y = ((y - mean) * lax.rsqrt(var + GN_EPS)).reshape(B, T, D_RWKV) * lp['ln_x_g'].astype(f32) + lp['ln_x_b'].astype(f32)
    bonus = jnp.sum(hd(r * k) * lp['r_k'].astype(f32), axis=-1, keepdims=True) * hd(v)
    y = (y + bonus.reshape(B, T, D_RWKV)) * g
    return y.astype(rw.dtype), jnp.stack([s_f, s_b], axis=1)


def token_mixer(h, lp, row, col, ctx_k, ctx_v, ctx_state):
    B, T, _ = h.shape
    proj = jnp.einsum('btd,de->bte', h, lp['w_in'])
    q, k, v, rw = jnp.split(proj, [D_ATTN, D_ATTN + D_KV, D_ATTN + 2 * D_KV], axis=-1)
    q = rms_norm(q.reshape(B, T, N_Q_HEADS, HEAD_DIM), lp['q_norm_g']).transpose(0, 2, 1, 3)
    k = rms_norm(k.reshape(B, T, N_KV_HEADS, HEAD_DIM), lp['k_norm_g'])
    v = v.reshape(B, T, N_KV_HEADS, HEAD_DIM)
    if ctx_k is None:
        keys, vals = k.transpose(0, 2, 1, 3), v.transpose(0, 2, 1, 3)
    else:
        q = rope_2d(q, row, col)
        keys = jnp.concatenate([rope_2d(k.transpose(0, 2, 1, 3), row, col), ctx_k.astype(k.dtype).transpose(0, 2, 1, 3)], axis=2)
        vals = jnp.concatenate([v.transpose(0, 2, 1, 3), ctx_v.astype(v.dtype).transpose(0, 2, 1, 3)], axis=2)
    o = block_attention(q.reshape(B, N_KV_HEADS, GQA_GROUP, T, HEAD_DIM), keys, vals)
    attn_out = o.reshape(B, N_Q_HEADS, T, HEAD_DIM).transpose(0, 2, 1, 3).reshape(B, T, D_ATTN)
    rwkv_out, state = rwkv7_group(rw, lp, ctx_state)
    out = jnp.einsum('btc,cd->btd', jnp.concatenate([attn_out, rwkv_out], axis=-1), lp['w_out'])
    return out, (k, v, state)


def hier_moe(h, lp):
    B, T, D = h.shape
    f32 = jnp.float32
    x = h.reshape(B * T, D)
    lc = jnp.einsum('nd,dg->ng', x, lp['router_c']).astype(f32) + lp['router_c_b'].astype(f32)
    pc = jax.nn.softmax(lc, axis=-1)
    g_idx = jnp.argmax(lc, axis=-1)
    g_hot = jax.nn.one_hot(g_idx, N_GROUPS, dtype=f32)
    g_w = jnp.sum(pc * g_hot, axis=-1, keepdims=True)
    lf = (jnp.einsum('nd,de->ne', x, lp['router_f']).astype(f32) + lp['router_f_b'].astype(f32)).reshape(-1, N_GROUPS, EXPERTS_PER_GROUP)
    lf_sel = jnp.einsum('nge,ng->ne', lf, g_hot)
    top_w, top_i = lax.top_k(jax.nn.softmax(lf_sel, axis=-1), TOP_K_FINE)
    top_w = top_w / jnp.sum(top_w, axis=-1, keepdims=True)
    expert_id = g_idx[:, None] * EXPERTS_PER_GROUP + top_i
    combine = jnp.sum(jax.nn.one_hot(expert_id, N_EXPERTS, dtype=f32) * (g_w * top_w)[..., None], axis=1)
    out = jnp.zeros_like(x)
    for e in range(N_EXPERTS):
        hid = jax.nn.silu(x @ lp['exp_gate'][e]) * (x @ lp['exp_up'][e])
        out = out + combine[:, e:e + 1].astype(x.dtype) * (hid @ lp['exp_down'][e])
    return out.reshape(B, T, D)


def trunk_layer(x, cond, lp, row, col, ctx_k, ctx_v, ctx_state):
    mod = jnp.einsum('bd,de->be', jax.nn.silu(cond), lp['w_mod']) + lp['b_mod']
    sh1, sc1, g1, sh2, sc2, g2 = jnp.split(mod, 6, axis=-1)
    mix, ctx_out = token_mixer(modulate(x, lp['norm1_g'], sh1, sc1), lp, row, col, ctx_k, ctx_v, ctx_state)
    x = x + g1[:, None, :] * mix
    x = x + g2[:, None, :] * hier_moe(modulate(x, lp['norm2_g'], sh2, sc2), lp)
    return x, ctx_out


def setup_inputs(seed: int = 0) -> dict:
    key = jax.random.key(seed)
    ks = iter(jax.random.split(key, 48))
    f32 = jnp.float32
    nrm = lambda shape, s: jax.random.normal(next(ks), shape, f32) * s
    D, L = D_MODEL, DEPTH
    return {
        'x_prompt': nrm((BATCH, SEQ, D), 1.0),
        'x_sample': nrm((DEC_BATCH, DEC_SEQ, D), 1.0),
        'cache_k': nrm((DEC_BATCH, L, PAST_LEN, N_KV_HEADS, HEAD_DIM), 1.0),
        'cache_v': nrm((DEC_BATCH, L, PAST_LEN, N_KV_HEADS, HEAD_DIM), 1.0),
        'state_rwkv': nrm((DEC_BATCH, L, 2, N_RWKV_HEADS, RWKV_HEAD, RWKV_HEAD), 0.3),
        'c': nrm((DEC_BATCH, D), 1.0),
        'c_ctx': nrm((D,), 1.0),
        'w_mod': nrm((L, D, 6 * D), D ** -0.5),
        'b_mod': nrm((L, 6 * D), 0.02),
        'norm1_g': 1.0 + nrm((L, D), 0.02),
        'norm2_g': 1.0 + nrm((L, D), 0.02),
        'w_in': nrm((L, D, D_IN), D ** -0.5),
        'mu_shift': jax.random.uniform(next(ks), (L, 2, D_RWKV_IN), f32, 0.0, 0.5),
        'q_norm_g': 1.0 + nrm((L, HEAD_DIM), 0.02),
        'k_norm_g': 1.0 + nrm((L, HEAD_DIM), 0.02),
        'w0': nrm((L, 2, D_RWKV), 0.5),
        'w_lora_up': nrm((L, 2, DECAY_LORA, D_RWKV), DECAY_LORA ** -0.5),
        'a0': nrm((L, 2, D_RWKV), 0.5),
        'a_lora_up': nrm((L, 2, AAA_LORA, D_RWKV), AAA_LORA ** -0.5),
        'g_lora_up': nrm((L, GATE_LORA, D_RWKV), GATE_LORA ** -0.5),
        'k_k': 0.85 + nrm((L, D_RWKV), 0.02),
        'k_a': 1.0 + nrm((L, D_RWKV), 0.02),
        'r_k': nrm((L, N_RWKV_HEADS, RWKV_HEAD), 0.1),
        'ln_x_g': 1.0 + nrm((L, D_RWKV), 0.02),
        'ln_x_b': nrm((L, D_RWKV), 0.02),
        'w_out': nrm((L, D_MIX, D), D_MIX ** -0.5),
        'router_c': nrm((L, D, N_GROUPS), D ** -0.5),
        'router_c_b': nrm((L, N_GROUPS), 0.01),
        'router_f': nrm((L, D, N_EXPERTS), D ** -0.5),
        'router_f_b': nrm((L, N_EXPERTS), 0.01),
        'exp_gate': nrm((L, N_EXPERTS, D, D_EXPERT), D ** -0.5),
        'exp_up': nrm((L, N_EXPERTS, D, D_EXPERT), D ** -0.5),
        'exp_down': nrm((L, N_EXPERTS, D_EXPERT, D), D_EXPERT ** -0.5),
    }


def reference(x_prompt, x_sample, cache_k, cache_v, state_rwkv, c, c_ctx, w_mod, b_mod, norm1_g, norm2_g, w_in, mu_shift, q_norm_g, k_norm_g, w0, w_lora_up, a0, a_lora_up, g_lora_up, k_k, k_a, r_k, ln_x_g, ln_x_b, w_out, router_c, router_c_b, router_f, router_f_b, exp_gate, exp_up, exp_down):
    rows = x_sample.shape[1] // GRID_W
    row = jnp.repeat(jnp.arange(rows, dtype=jnp.int32), GRID_W)
    col = jnp.tile(jnp.arange(GRID_W, dtype=jnp.int32), rows)
    cond_ctx = c_ctx[None, :]
    y_prompt, y_sample = x_prompt, x_sample
    ks, vs, ss = [], [], []
    for l in range(DEPTH):
        lp = dict(w_mod=w_mod[l], b_mod=b_mod[l], norm1_g=norm1_g[l], norm2_g=norm2_g[l], w_in=w_in[l],
                  mu_shift=mu_shift[l], q_norm_g=q_norm_g[l], k_norm_g=k_norm_g[l], w0=w0[l], w_lora_up=w_lora_up[l],
                  a0=a0[l], a_lora_up=a_lora_up[l], g_lora_up=g_lora_up[l], k_k=k_k[l], k_a=k_a[l], r_k=r_k[l],
                  ln_x_g=ln_x_g[l], ln_x_b=ln_x_b[l], w_out=w_out[l], router_c=router_c[l], router_c_b=router_c_b[l],
                  router_f=router_f[l], router_f_b=router_f_b[l], exp_gate=exp_gate[l], exp_up=exp_up[l], exp_down=exp_down[l])
        y_prompt, (k_l, v_l, s_l) = trunk_layer(y_prompt, cond_ctx, lp, None, None, None, None, None)
        ks.append(k_l)
        vs.append(v_l)
        ss.append(s_l)
        y_sample, _ = trunk_layer(y_sample, c, lp, row, col, cache_k[:, l], cache_v[:, l], state_rwkv[:, l])
    new_cache_k = jnp.stack(ks, axis=1)
    new_cache_v = jnp.stack(vs, axis=1)
    new_state_rwkv = jnp.stack(ss, axis=1)
    return (y_prompt, y_sample, new_cache_k, new_cache_v, new_state_rwkv)
```

```python
import functools

import numpy as np
import jax
import jax.numpy as jnp
from jax import lax
from jax.experimental import pallas as pl
from jax.experimental.pallas import tpu as pltpu

F32 = jnp.float32
BF16 = jnp.bfloat16
HIGHEST = lax.Precision.HIGHEST

D_MODEL = 1024
HEAD_DIM = 64
N_Q_HEADS = 8
N_KV_HEADS = 2
GQA_GROUP = N_Q_HEADS // N_KV_HEADS
D_ATTN = N_Q_HEADS * HEAD_DIM
D_KV = N_KV_HEADS * HEAD_DIM
N_RWKV_HEADS = 8
D_RWKV = 512
DECAY_LORA = 32
AAA_LORA = 32
GATE_LORA = 96
D_RWKV_IN = 3 * D_RWKV + 2 * DECAY_LORA + 2 * AAA_LORA + GATE_LORA
D_RWKV_PAD = 1792
D_QKV = D_ATTN + 2 * D_KV
D_IN_PAD = D_QKV + D_RWKV_PAD
N_GROUPS = 4
EXPERTS_PER_GROUP = 4
N_EXPERTS = 16
D_EXPERT = 512
GRID_W = 64
ROPE_THETA = 10000.0
NORM_EPS = 1e-6
GN_EPS = 64e-5
DECAY_SCALE = 0.6065306597
LANES = 128
ROUTER_LANES = 128
VMEM_LIMIT = 56 * 1024 * 1024


def _cparams(sem):
    return pltpu.CompilerParams(dimension_semantics=sem, vmem_limit_bytes=VMEM_LIMIT)


def _dot(a, b, precision=None):
    return jnp.dot(a, b, preferred_element_type=F32, precision=precision)


def _dot_nt(a, b, precision=None):
    return lax.dot_general(a, b, (((1,), (1,)), ((), ())), preferred_element_type=F32, precision=precision)


def _dot_tn(a, b, precision=None):
    return lax.dot_general(a, b, (((0,), (0,)), ((), ())), preferred_element_type=F32, precision=precision)


def _head_sum(x, g):
    n = x.shape[-1] // LANES
    return jnp.concatenate([_dot(x[:, j * LANES:(j + 1) * LANES], g, HIGHEST) for j in range(n)], axis=-1)


def _mod_kernel(c_ref, w_ref, b_ref, o_ref):
    c = c_ref[...]
    s = c * jax.nn.sigmoid(c)
    o_ref[...] = _dot(s, w_ref[...], HIGHEST) + b_ref[...]


def _modulation(cond, w_mod, b_mod):
    n = w_mod.shape[1]
    tn = 1024
    return pl.pallas_call(
        _mod_kernel,
        grid=(n // tn,),
        in_specs=[pl.BlockSpec((8, D_MODEL), lambda j: (0, 0)),
                  pl.BlockSpec((D_MODEL, tn), lambda j: (0, j)),
                  pl.BlockSpec((1, tn), lambda j: (0, j))],
        out_specs=pl.BlockSpec((8, tn), lambda j: (0, j)),
        out_shape=jax.ShapeDtypeStruct((8, n), F32),
        compiler_params=_cparams(("arbitrary",)),
        name="mod",
    )(cond, w_mod, b_mod.reshape(1, n))


def _rope_tables(t_len):
    half = HEAD_DIM // 2
    inv = ROPE_THETA ** (-np.arange(0, half, 2, dtype=np.float64) / half)
    t = np.arange(t_len)
    row, col = t // GRID_W, t % GRID_W
    lane = np.arange(LANES)
    i = lane % HEAD_DIM
    pos = np.where((i // half)[None, :] == 0, row[:, None], col[:, None]).astype(np.float64)
    j = i % half
    ang = pos * inv[j % (half // 2)][None, :]
    cos, sin = np.cos(ang), np.sin(ang)
    first = (j < half // 2)[None, :]
    s_up = np.where(first, -sin, 0.0)
    s_dn = np.where(first, 0.0, sin)
    return (jnp.asarray(cos, F32), jnp.asarray(s_up, F32), jnp.asarray(s_dn, F32))


def _inproj_kernel(x_ref, sh_ref, sc_ref, g_ref, w_ref, gqk_ref, grp_ref, *rest, rope):
    if rope:
        cos_ref, sup_ref, sdn_ref, q_ref, k_ref, v_ref, rw_ref = rest
    else:
        q_ref, k_ref, v_ref, rw_ref = rest
    x = x_ref[...]
    ms = jnp.mean(x * x, axis=-1, keepdims=True)
    h = x * lax.rsqrt(ms + NORM_EPS) * g_ref[...]
    h = h * (1.0 + sc_ref[0]) + sh_ref[0]
    proj = _dot(h.astype(BF16), w_ref[...])
    grp = grp_ref[...]
    for j in range((D_ATTN + D_KV) // LANES):
        blk = proj[:, j * LANES:(j + 1) * LANES]
        ss = _dot(blk * blk, grp, HIGHEST) * (1.0 / HEAD_DIM)
        nb = blk * lax.rsqrt(ss + NORM_EPS) * gqk_ref[:, j * LANES:(j + 1) * LANES]
        if rope:
            nb = (nb * cos_ref[...] + pltpu.roll(nb, LANES - 16, 1) * sup_ref[...]
                  + pltpu.roll(nb, 16, 1) * sdn_ref[...])
        if j < D_ATTN // LANES:
            q_ref[:, j * LANES:(j + 1) * LANES] = nb
        else:
            k_ref[...] = nb
    v_ref[...] = proj[:, D_ATTN + D_KV:D_QKV]
    rw_ref[...] = proj[:, D_QKV:]


def _in_proj(x2, shift, scale, norm_g, w_in_p, gqk, grp, t_len, rope, tm):
    n = x2.shape[0]
    per = t_len // tm
    bm = shift.shape[0]
    mod_map = (lambda i: (i // per, 0, 0)) if bm > 1 else (lambda i: (0, 0, 0))
    in_specs = [pl.BlockSpec((tm, D_MODEL), lambda i: (i, 0)),
                pl.BlockSpec((1, 1, D_MODEL), mod_map),
                pl.BlockSpec((1, 1, D_MODEL), mod_map),
                pl.BlockSpec((1, D_MODEL), lambda i: (0, 0)),
                pl.BlockSpec((D_MODEL, D_IN_PAD), lambda i: (0, 0)),
                pl.BlockSpec((1, D_ATTN + D_KV), lambda i: (0, 0)),
                pl.BlockSpec((LANES, LANES), lambda i: (0, 0))]
    args = [x2, shift, scale, norm_g, w_in_p, gqk, grp]
    if rope:
        tabs = _rope_tables(t_len)
        in_specs += [pl.BlockSpec((tm, LANES), lambda i: (i % per, 0))] * 3
        args += list(tabs)
    out_shape = (jax.ShapeDtypeStruct((n, D_ATTN), F32), jax.ShapeDtypeStruct((n, D_KV), F32),
                 jax.ShapeDtypeStruct((n, D_KV), F32), jax.ShapeDtypeStruct((n, D_RWKV_PAD), F32))
    out_specs = (pl.BlockSpec((tm, D_ATTN), lambda i: (i, 0)), pl.BlockSpec((tm, D_KV), lambda i: (i, 0)),
                 pl.BlockSpec((tm, D_KV), lambda i: (i, 0)), pl.BlockSpec((tm, D_RWKV_PAD), lambda i: (i, 0)))
    return pl.pallas_call(
        functools.partial(_inproj_kernel, rope=rope),
        grid=(n // tm,), in_specs=in_specs, out_specs=out_specs, out_shape=out_shape,
        compiler_params=_cparams(("parallel",)),
        name="in_proj_rope" if rope else "in_proj",
    )(*args)


def _attn_kernel(q_ref, k_ref, v_ref, o_ref):
    g, tq, hd = q_ref.shape[2:]
    q = q_ref[0, 0].reshape(g * tq, hd).astype(BF16)
    k = k_ref[0, 0].astype(BF16)
    v = v_ref[0, 0].astype(BF16)
    s = _dot_nt(q, k) * (HEAD_DIM ** -0.5)
    m = jnp.max(s, axis=-1, keepdims=True)
    e = jnp.exp(s - m)
    l = jnp.sum(e, axis=-1, keepdims=True)
    o = _dot(e.astype(BF16), v) / l
    o_ref[0, 0] = o.reshape(g, tq, hd)


def _attention(q5, k4, v4, tq):
    b, hk, g, t, hd = q5.shape
    tk = k4.shape[2]
    return pl.pallas_call(
        _attn_kernel,
        grid=(b, hk, t // tq),
        in_specs=[pl.BlockSpec((1, 1, g, tq, hd), lambda i, j, l: (i, j, 0, l, 0)),
                  pl.BlockSpec((1, 1, tk, hd), lambda i, j, l: (i, j, 0, 0)),
                  pl.BlockSpec((1, 1, tk, hd), lambda i, j, l: (i, j, 0, 0))],
        out_specs=pl.BlockSpec((1, 1, g, tq, hd), lambda i, j, l: (i, j, 0, l, 0)),
        out_shape=jax.ShapeDtypeStruct(q5.shape, F32),
        compiler_params=_cparams(("parallel", "parallel", "arbitrary")),
        name="attn",
    )(q5, k4, v4)


def _prep_kernel(rw_ref, hp_ref, hn_ref, mu_ref, wd_ref, w0_ref, wa_ref, a0_ref, wg_ref, kk_ref, ka_ref, rk_ref,
                 grp_ref, r_o, lw_o, kd_o, v_o, kk_o, bd_o, g_o, bonus_o):
    i = pl.program_id(1)
    n = pl.num_programs(1)
    cur = rw_ref[0]
    tt = cur.shape[0]
    rid = lax.broadcasted_iota(jnp.int32, cur.shape, 0)
    prev_row = jnp.where(i > 0, hp_ref[0, 7:8, :], 0.0)
    next_row = jnp.where(i < n - 1, hn_ref[0, 0:1, :], 0.0)
    prev = jnp.where(rid == 0, prev_row, pltpu.roll(cur, 1, 0))
    nxt = jnp.where(rid == tt - 1, next_row, pltpu.roll(cur, tt - 1, 0))
    p = cur + mu_ref[0:1, :] * (prev - cur) + mu_ref[1:2, :] * (nxt - cur)
    r = p[:, 0:D_RWKV]
    k = p[:, D_RWKV:2 * D_RWKV]
    v = p[:, 2 * D_RWKV:3 * D_RWKV]
    lo = p[:, 3 * D_RWKV:3 * D_RWKV + LANES]
    gd = p[:, 3 * D_RWKV + LANES:]
    grp = grp_ref[...]
    wlog = _dot(jnp.tanh(lo), wd_ref[...], HIGHEST) + w0_ref[...]
    alog = _dot(lo, wa_ref[...], HIGHEST) + a0_ref[...]
    g_o[0] = _dot(jax.nn.sigmoid(gd), wg_ref[...], HIGHEST)
    kx = k * kk_ref[...]
    kk = kx * lax.rsqrt(_head_sum(kx * kx, grp) + 1e-12)
    r_o[0] = r
    v_o[0] = v
    kk_o[0] = kk
    bonus_o[0] = _head_sum(r * k * rk_ref[...], grp) * v
    for z in range(2):
        a = jax.nn.sigmoid(alog[:, z * D_RWKV:(z + 1) * D_RWKV])
        lw_o[z, 0] = -DECAY_SCALE * jax.nn.sigmoid(wlog[:, z * D_RWKV:(z + 1) * D_RWKV])
        kd_o[z, 0] = k * (1.0 + (a - 1.0) * ka_ref[...])
        bd_o[z, 0] = kk * a


def _rwkv_prep(rw3, mu_p, wd_cat, w0_cat, wa_cat, a0_cat, wg_p, k_k, k_a, r_k, grp, tt):
    b, t, _ = rw3.shape
    nt = t // tt
    hb = tt // 8
    one = jax.ShapeDtypeStruct((b, t, D_RWKV), F32)
    two = jax.ShapeDtypeStruct((2, b, t, D_RWKV), F32)
    s_one = pl.BlockSpec((1, tt, D_RWKV), lambda i, j: (i, j, 0))
    s_two = pl.BlockSpec((2, 1, tt, D_RWKV), lambda i, j: (0, i, j, 0))
    full = lambda a: pl.BlockSpec(a.shape, lambda i, j: (0,) * a.ndim)
    consts = [mu_p, wd_cat, w0_cat, wa_cat, a0_cat, wg_p, k_k, k_a, r_k, grp]
    return pl.pallas_call(
        _prep_kernel,
        grid=(b, nt),
        in_specs=[pl.BlockSpec((1, tt, D_RWKV_PAD), lambda i, j: (i, j, 0)),
                  pl.BlockSpec((1, 8, D_RWKV_PAD), lambda i, j: (i, jnp.maximum(j * hb - 1, 0), 0)),
                  pl.BlockSpec((1, 8, D_RWKV_PAD), lambda i, j: (i, jnp.minimum((j + 1) * hb, t // 8 - 1), 0))]
                 + [full(a) for a in consts],
        out_specs=(s_one, s_two, s_two, s_one, s_one, s_two, s_one, s_one),
        out_shape=(one, two, two, one, one, two, one, one),
        compiler_params=_cparams(("parallel", "parallel")),
        name="rwkv_prep",
    )(rw3, rw3, rw3, *consts)


def _scan_kernel(r_ref, lw_ref, k_ref, v_ref, a_ref, b_ref, s0_ref, y_ref, sT_ref, s_scr, *, chunk):
    d = pl.program_id(1)
    i = pl.program_id(2)
    c = chunk

    @pl.when(i == 0)
    def _():
        s_scr[...] = s0_ref[0, 0]

    row = lax.broadcasted_iota(jnp.int32, (c, c), 0)
    col = lax.broadcasted_iota(jnp.int32, (c, c), 1)
    ahead = (row - col) * (1 - 2 * d)
    incl = ahead >= 0
    strict = ahead > 0
    eye = (row == col).astype(F32)

    lw = lw_ref[0, 0]
    cl = _dot(incl.astype(F32), lw, HIGHEST)
    tot = jnp.sum(lw, axis=0, keepdims=True)
    mid = 0.5 * tot
    e_abs = jnp.exp(cl)
    e_abx = jnp.exp(cl - lw)
    e_inv = jnp.exp(mid - cl)
    e_end = jnp.exp(tot - cl)
    p_tot = jnp.exp(tot)
    s_mid = jnp.exp(-mid)

    r_abs = r_ref[0] * e_abs
    a_abs = a_ref[0] * e_abx
    r_mid = r_abs * s_mid
    a_mid = a_abs * s_mid
    k_inv = k_ref[0, 0] * e_inv
    b_inv = b_ref[0, 0] * e_inv
    k_end = k_ref[0, 0] * e_end
    b_end = b_ref[0, 0] * e_end
    v = v_ref[0]

    ys = []
    for h in range(N_RWKV_HEADS):
        sl = slice(h * HEAD_DIM, (h + 1) * HEAD_DIM)
        s_h = s_scr[h]
        a_ak = _dot_nt(a_mid[:, sl], k_inv[:, sl], HIGHEST)
        a_ab = _dot_nt(a_mid[:, sl], b_inv[:, sl], HIGHEST)
        a_rk = _dot_nt(r_mid[:, sl], k_inv[:, sl], HIGHEST)
        a_rb = _dot_nt(r_mid[:, sl], b_inv[:, sl], HIGHEST)
        lmat = jnp.where(strict, a_ab, 0.0)
        x = eye - lmat
        pw = _dot(lmat, lmat, HIGHEST)
        x = x + _dot(x, pw, HIGHEST)
        span = 4
        while span < c:
            pw = _dot(pw, pw, HIGHEST)
            x = x + _dot(x, pw, HIGHEST)
            span *= 2
        vh = v[:, sl]
        rhs = _dot_nt(a_abs[:, sl], s_h, HIGHEST) + _dot(jnp.where(strict, a_ak, 0.0), vh, HIGHEST)
        u = _dot(x, rhs, HIGHEST)
        y = (_dot_nt(r_abs[:, sl], s_h, HIGHEST) + _dot(jnp.where(incl, a_rk, 0.0), vh, HIGHEST)
             - _dot(jnp.where(incl, a_rb, 0.0), u, HIGHEST))
        ys.append(y)
        s_scr[h] = (s_h * p_tot[:, sl] + _dot_tn(vh, k_end[:, sl], HIGHEST)
                    - _dot_tn(u, b_end[:, sl], HIGHEST))
    y_ref[0, 0] = jnp.concatenate(ys, axis=-1)

    @pl.when(i == pl.num_programs(2) - 1)
    def _():
        sT_ref[0, 0] = s_scr[...]


def _rwkv_scan(r, lw2, kd2, v, kk, bd2, s0, chunk):
    b, t, _ = r.shape
    nc = t // chunk
    tmap = lambda d, j: j + d * (nc - 1 - 2 * j)
    s_one = pl.BlockSpec((1, chunk, D_RWKV), lambda i, d, j: (i, tmap(d, j), 0))
    s_two = pl.BlockSpec((1, 1, chunk, D_RWKV), lambda i, d, j: (d, i, tmap(d, j), 0))
    s_st = pl.BlockSpec((1, 1, N_RWKV_HEADS, HEAD_DIM, HEAD_DIM), lambda i, d, j: (i, d, 0, 0, 0))
    return pl.pallas_call(
        functools.partial(_scan_kernel, chunk=chunk),
        grid=(b, 2, nc),
        in_specs=[s_one, s_two, s_two, s_one, s_one, s_two, s_st],
        out_specs=(s_two, s_st),
        out_shape=(jax.ShapeDtypeStruct((2, b, t, D_RWKV), F32), jax.ShapeDtypeStruct(s0.shape, F32)),
        scratch_shapes=[pltpu.VMEM((N_RWKV_HEADS, HEAD_DIM, HEAD_DIM), F32)],
        compiler_params=_cparams(("parallel", "parallel", "arbitrary")),
        name="rwkv_scan",
    )(r, lw2, kd2, v, kk, bd2, s0)


def _outproj_kernel(x_ref, at_ref, y2_ref, bonus_ref, gate_ref, lng_ref, lnb_ref, grp_ref, wo_ref,
                    g1_ref, sh_ref, sc_ref, n2_ref, wr_ref, br_ref, x1_ref, h2_ref, cmb_ref):
    grp = grp_ref[...]
    y = y2_ref[0, 0] + y2_ref[1, 0]
    mean = _head_sum(y, grp) * (1.0 / HEAD_DIM)
    yc = y - mean
    var = _head_sum(yc * yc, grp) * (1.0 / HEAD_DIM)
    yn = yc * lax.rsqrt(var + GN_EPS) * lng_ref[...] + lnb_ref[...]
    rw_out = (yn + bonus_ref[0]) * gate_ref[0]
    mix = (_dot(at_ref[0].astype(BF16), wo_ref[0:D_ATTN, :])
           + _dot(rw_out.astype(BF16), wo_ref[D_ATTN:, :]))
    x1 = x_ref[0] + g1_ref[0] * mix
    x1_ref[0] = x1
    ms = jnp.mean(x1 * x1, axis=-1, keepdims=True)
    h2 = x1 * lax.rsqrt(ms + NORM_EPS) * n2_ref[...]
    h2 = h2 * (1.0 + sc_ref[0]) + sh_ref[0]
    h2_ref[0] = h2.astype(BF16)
    logits = _dot(h2, wr_ref[...], HIGHEST) + br_ref[...]
    lane = lax.broadcasted_iota(jnp.int32, logits.shape, 1)
    neg = -jnp.inf
    big = jnp.int32(1 << 20)
    lc = jnp.where(lane < N_GROUPS, logits, neg)
    mc = jnp.max(lc, axis=-1, keepdims=True)
    g_w = 1.0 / jnp.sum(jnp.exp(lc - mc), axis=-1, keepdims=True)
    g_idx = jnp.min(jnp.where(lc == mc, lane, big), axis=-1, keepdims=True)
    eid = lane - N_GROUPS
    in_grp = (eid >= 0) & (eid < N_EXPERTS) & (lax.shift_right_arithmetic(eid, 2) == g_idx)
    lf = jnp.where(in_grp, logits, neg)
    m1 = jnp.max(lf, axis=-1, keepdims=True)
    i1 = jnp.min(jnp.where(lf == m1, lane, big), axis=-1, keepdims=True)
    lf2 = jnp.where(lane == i1, neg, lf)
    m2 = jnp.max(lf2, axis=-1, keepdims=True)
    i2 = jnp.min(jnp.where(lf2 == m2, lane, big), axis=-1, keepdims=True)
    e2 = jnp.exp(m2 - m1)
    w1 = 1.0 / (1.0 + e2)
    w2 = e2 * w1
    cmb_ref[0] = g_w * (jnp.where(lane == i1, w1, 0.0) + jnp.where(lane == i2, w2, 0.0))


def _out_proj(x3, attn3, y2, bonus, gate, ln_g, ln_b, grp, w_out_b, g1, sh2, sc2, norm2_g, w_r, b_r, tm):
    b, t, _ = x3.shape
    bm = g1.shape[0]
    mod_map = (lambda i, j: (i, 0, 0)) if bm > 1 else (lambda i, j: (0, 0, 0))
    tok = lambda w: pl.BlockSpec((1, tm, w), lambda i, j: (i, j, 0))
    full = lambda a: pl.BlockSpec(a.shape, lambda i, j: (0,) * a.ndim)
    mod = pl.BlockSpec((1, 1, D_MODEL), mod_map)
    return pl.pallas_call(
        _outproj_kernel,
        grid=(b, t // tm),
        in_specs=[tok(D_MODEL), tok(D_ATTN),
                  pl.BlockSpec((2, 1, tm, D_RWKV), lambda i, j: (0, i, j, 0)),
                  tok(D_RWKV), tok(D_RWKV), full(ln_g), full(ln_b), full(grp), full(w_out_b),
                  mod, mod, mod, full(norm2_g), full(w_r), full(b_r)],
        out_specs=(tok(D_MODEL), tok(D_MODEL), tok(ROUTER_LANES)),
        out_shape=(jax.ShapeDtypeStruct((b, t, D_MODEL), F32), jax.ShapeDtypeStruct((b, t, D_MODEL), BF16),
                   jax.ShapeDtypeStruct((b, t, ROUTER_LANES), F32)),
        compiler_params=_cparams(("parallel", "parallel")),
        name="out_proj",
    )(x3, attn3, y2, bonus, gate, ln_g, ln_b, grp, w_out_b, g1, sh2, sc2, norm2_g, w_r, b_r)


def _moe_kernel(h_ref, cmb_ref, x1_ref, g2_ref, wg_ref, wu_ref, wd_ref, o_ref, acc_ref):
    e = pl.program_id(2)

    @pl.when(e == 0)
    def _():
        acc_ref[...] = jnp.zeros_like(acc_ref)

    cmb = cmb_ref[0]
    lane = lax.broadcasted_iota(jnp.int32, cmb.shape, 1)
    c_e = jnp.sum(jnp.where(lane == e + N_GROUPS, cmb, 0.0), axis=-1, keepdims=True)
    h = h_ref[0]
    a = _dot(h, wg_ref[0])
    hid = a * jax.nn.sigmoid(a) * _dot(h, wu_ref[0])
    acc_ref[...] += _dot((hid * c_e).astype(BF16), wd_ref[0])

    @pl.when(e == pl.num_programs(2) - 1)
    def _():
        o_ref[0] = x1_ref[0] + g2_ref[0] * acc_ref[...]


def _moe(h2, cmb, x1, g2, wg_b, wu_b, wd_b, tm):
    b, t, _ = h2.shape
    bm = g2.shape[0]
    mod_map = (lambda i, j, e: (i, 0, 0)) if bm > 1 else (lambda i, j, e: (0, 0, 0))
    tok = lambda w: pl.BlockSpec((1, tm, w), lambda i, j, e: (i, j, 0))
    return pl.pallas_call(
        _moe_kernel,
        grid=(b, t // tm, N_EXPERTS),
        in_specs=[tok(D_MODEL), tok(ROUTER_LANES), tok(D_MODEL), pl.BlockSpec((1, 1, D_MODEL), mod_map),
                  pl.BlockSpec((1, D_MODEL, D_EXPERT), lambda i, j, e: (e, 0, 0)),
                  pl.BlockSpec((1, D_MODEL, D_EXPERT), lambda i, j, e: (e, 0, 0)),
                  pl.BlockSpec((1, D_EXPERT, D_MODEL), lambda i, j, e: (e, 0, 0))],
        out_specs=tok(D_MODEL),
        out_shape=jax.ShapeDtypeStruct((b, t, D_MODEL), F32),
        scratch_shapes=[pltpu.VMEM((tm, D_MODEL), F32)],
        compiler_params=_cparams(("parallel", "parallel", "arbitrary")),
        name="moe",
    )(h2, cmb, x1, g2, wg_b, wu_b, wd_b)


def _layer(x3, mod6, lp, t_tiles, ctx):
    b, t, _ = x3.shape
    sh1, sc1, g1, sh2, sc2, g2 = mod6
    rope = ctx is not None
    n = b * t
    q, k, v, rw = _in_proj(x3.reshape(n, D_MODEL), sh1, sc1, lp["norm1_g"], lp["w_in_p"], lp["gqk"], lp["grp"],
                           t, rope, t_tiles["in_proj"])
    q5 = q.reshape(b, t, N_KV_HEADS, GQA_GROUP, HEAD_DIM).transpose(0, 2, 3, 1, 4)
    k4 = k.reshape(b, t, N_KV_HEADS, HEAD_DIM).transpose(0, 2, 1, 3)
    v4 = v.reshape(b, t, N_KV_HEADS, HEAD_DIM).transpose(0, 2, 1, 3)
    if ctx is not None:
        ctx_k, ctx_v, ctx_state = ctx
        k4 = jnp.concatenate([k4, ctx_k.transpose(0, 2, 1, 3)], axis=2)
        v4 = jnp.concatenate([v4, ctx_v.transpose(0, 2, 1, 3)], axis=2)
        s0 = ctx_state
    else:
        s0 = jnp.zeros((b, 2, N_RWKV_HEADS, HEAD_DIM, HEAD_DIM), F32)
    o5 = _attention(q5, k4, v4, t_tiles["attn"])
    attn3 = o5.transpose(0, 3, 1, 2, 4).reshape(b, t, D_ATTN)
    r, lw2, kd2, vv, kk, bd2, gate, bonus = _rwkv_prep(
        rw.reshape(b, t, D_RWKV_PAD), lp["mu_p"], lp["wd_cat"], lp["w0_cat"], lp["wa_cat"], lp["a0_cat"],
        lp["wg_p"], lp["k_k"], lp["k_a"], lp["r_k"], lp["grp"], t_tiles["prep"])
    y2, s_t = _rwkv_scan(r, lw2, kd2, vv, kk, bd2, s0, t_tiles["chunk"])
    fb, ft = (1, n) if g1.shape[0] == 1 else (b, t)
    flat = lambda a: a.reshape(a.shape[:-3] + (fb, ft, a.shape[-1]))
    x1, h2, cmb = _out_proj(flat(x3), flat(attn3), flat(y2), flat(bonus), flat(gate), lp["ln_g"], lp["ln_b"],
                            lp["grp"], lp["w_out_b"], g1, sh2, sc2, lp["norm2_g"], lp["w_r"], lp["b_r"],
                            t_tiles["out_proj"])
    out = _moe(h2, cmb, x1, g2, lp["wg_b"], lp["wu_b"], lp["wd_b"], t_tiles["moe"]).reshape(b, t, D_MODEL)
    return out, k.reshape(b, t, N_KV_HEADS, HEAD_DIM), v.reshape(b, t, N_KV_HEADS, HEAD_DIM), s_t


def _block_diag2(w):
    z, l, c = w.shape
    out = jnp.zeros((LANES, z * c), F32)
    for i in range(z):
        out = out.at[i * l:(i + 1) * l, i * c:(i + 1) * c].set(w[i])
    return out


def _layer_params(l, w_in, norm1_g, norm2_g, mu_shift, q_norm_g, k_norm_g, w0, w_lora_up, a0, a_lora_up, g_lora_up,
                  k_k, k_a, r_k, ln_x_g, ln_x_b, w_out, router_c, router_c_b, router_f, router_f_b,
                  exp_gate, exp_up, exp_down):
    lane = np.arange(LANES)
    grp = jnp.asarray((lane[:, None] // HEAD_DIM) == (lane[None, :] // HEAD_DIM), F32)
    pad_in = D_IN_PAD - w_in.shape[2]
    wd_cat = _block_diag2(w_lora_up[l])
    wa_cat = jnp.roll(_block_diag2(a_lora_up[l]), 2 * DECAY_LORA, axis=0)
    w_r = jnp.zeros((D_MODEL, ROUTER_LANES), F32)
    w_r = w_r.at[:, :N_GROUPS].set(router_c[l]).at[:, N_GROUPS:N_GROUPS + N_EXPERTS].set(router_f[l])
    b_r = jnp.zeros((1, ROUTER_LANES), F32)
    b_r = b_r.at[0, :N_GROUPS].set(router_c_b[l]).at[0, N_GROUPS:N_GROUPS + N_EXPERTS].set(router_f_b[l])
    return dict(
        grp=grp,
        norm1_g=norm1_g[l].reshape(1, D_MODEL), norm2_g=norm2_g[l].reshape(1, D_MODEL),
        w_in_p=jnp.pad(w_in[l], ((0, 0), (0, pad_in))).astype(BF16),
        gqk=jnp.concatenate([jnp.tile(q_norm_g[l], N_Q_HEADS), jnp.tile(k_norm_g[l], N_KV_HEADS)]).reshape(1, -1),
        mu_p=jnp.pad(mu_shift[l], ((0, 0), (0, D_RWKV_PAD - D_RWKV_IN))),
        wd_cat=wd_cat, w0_cat=w0[l].reshape(1, 2 * D_RWKV),
        wa_cat=wa_cat, a0_cat=a0[l].reshape(1, 2 * D_RWKV),
        wg_p=jnp.pad(g_lora_up[l], ((0, LANES - GATE_LORA), (0, 0))),
        k_k=k_k[l].reshape(1, D_RWKV), k_a=k_a[l].reshape(1, D_RWKV), r_k=r_k[l].reshape(1, D_RWKV),
        ln_g=ln_x_g[l].reshape(1, D_RWKV), ln_b=ln_x_b[l].reshape(1, D_RWKV),
        w_out_b=w_out[l].astype(BF16), w_r=w_r, b_r=b_r,
        wg_b=exp_gate[l].astype(BF16), wu_b=exp_up[l].astype(BF16), wd_b=exp_down[l].astype(BF16),
    )


CTX_TILES = dict(in_proj=512, attn=256, prep=256, chunk=128, out_proj=512, moe=1024)
SMP_TILES = dict(in_proj=512, attn=128, prep=256, chunk=128, out_proj=512, moe=1024)


def kernel(x_prompt, x_sample, cache_k, cache_v, state_rwkv, c, c_ctx, w_mod, b_mod, norm1_g, norm2_g, w_in, mu_shift, q_norm_g, k_norm_g, w0, w_lora_up, a0, a_lora_up, g_lora_up, k_k, k_a, r_k, ln_x_g, ln_x_b, w_out, router_c, router_c_b, router_f, router_f_b, exp_gate, exp_up, exp_down):
    depth = w_mod.shape[0]
    db = x_sample.shape[0]
    y_prompt, y_sample = x_prompt, x_sample
    ks, vs, ss = [], [], []
    cond = jnp.zeros((8, D_MODEL), F32).at[:db].set(c).at[db].set(c_ctx)
    for l in range(depth):
        lp = _layer_params(l, w_in, norm1_g, norm2_g, mu_shift, q_norm_g, k_norm_g, w0, w_lora_up, a0, a_lora_up,
                           g_lora_up, k_k, k_a, r_k, ln_x_g, ln_x_b, w_out, router_c, router_c_b, router_f,
                           router_f_b, exp_gate, exp_up, exp_down)
        mod = _modulation(cond, w_mod[l], b_mod[l])
        mod_s = [mod[:db, i * D_MODEL:(i + 1) * D_MODEL].reshape(db, 1, D_MODEL) for i in range(6)]
        mod_c = [mod[db:db + 1, i * D_MODEL:(i + 1) * D_MODEL].reshape(1, 1, D_MODEL) for i in range(6)]
        y_prompt, k_l, v_l, s_l = _layer(y_prompt, mod_c, lp, CTX_TILES, None)
        ks.append(k_l)
        vs.append(v_l)
        ss.append(s_l)
        y_sample, _, _, _ = _layer(y_sample, mod_s, lp, SMP_TILES,
                                   (cache_k[:, l], cache_v[:, l], state_rwkv[:, l]))
    return (y_prompt, y_sample, jnp.stack(ks, axis=1), jnp.stack(vs, axis=1), jnp.stack(ss, axis=1))
```

```python
import functools

import numpy as np
import jax
import jax.numpy as jnp
from jax import lax
from jax.experimental import pallas as pl
from jax.experimental.pallas import tpu as pltpu

F32 = jnp.float32
BF16 = jnp.bfloat16
HIGHEST = lax.Precision.HIGHEST

D_MODEL = 1024
HEAD_DIM = 64
N_Q_HEADS = 8
N_KV_HEADS = 2
GQA_GROUP = N_Q_HEADS // N_KV_HEADS
D_ATTN = N_Q_HEADS * HEAD_DIM
D_KV = N_KV_HEADS * HEAD_DIM
N_RWKV_HEADS = 8
D_RWKV = 512
DECAY_LORA = 32
AAA_LORA = 32
GATE_LORA = 96
D_RWKV_IN = 3 * D_RWKV + 2 * DECAY_LORA + 2 * AAA_LORA + GATE_LORA
D_RWKV_PAD = 1792
D_QKV = D_ATTN + 2 * D_KV
D_IN_PAD = D_QKV + D_RWKV_PAD
N_GROUPS = 4
EXPERTS_PER_GROUP = 4
N_EXPERTS = 16
D_EXPERT = 512
GRID_W = 64
ROPE_THETA = 10000.0
NORM_EPS = 1e-6
GN_EPS = 64e-5
DECAY_SCALE = 0.6065306597
LANES = 128
ROUTER_LANES = 128
VMEM_LIMIT = 56 * 1024 * 1024


def _cparams(sem):
    return pltpu.CompilerParams(dimension_semantics=sem, vmem_limit_bytes=VMEM_LIMIT)


def _dot(a, b, precision=None):
    return jnp.dot(a, b, preferred_element_type=F32, precision=precision)


def _dot_nt(a, b, precision=None):
    return lax.dot_general(a, b, (((1,), (1,)), ((), ())), preferred_element_type=F32, precision=precision)


def _dot_tn(a, b, precision=None):
    return lax.dot_general(a, b, (((0,), (0,)), ((), ())), preferred_element_type=F32, precision=precision)


def _head_sum(x, g):
    n = x.shape[-1] // LANES
    return jnp.concatenate([_dot(x[:, j * LANES:(j + 1) * LANES], g, HIGHEST) for j in range(n)], axis=-1)


def _mod_kernel(c_ref, w_ref, b_ref, o_ref):
    c = c_ref[...]
    s = c * jax.nn.sigmoid(c)
    o_ref[...] = _dot(s, w_ref[...], HIGHEST) + b_ref[...]


def _modulation(cond, w_mod, b_mod):
    n = w_mod.shape[1]
    tn = 1024
    return pl.pallas_call(
        _mod_kernel,
        grid=(n // tn,),
        in_specs=[pl.BlockSpec((8, D_MODEL), lambda j: (0, 0)),
                  pl.BlockSpec((D_MODEL, tn), lambda j: (0, j)),
                  pl.BlockSpec((1, tn), lambda j: (0, j))],
        out_specs=pl.BlockSpec((8, tn), lambda j: (0, j)),
        out_shape=jax.ShapeDtypeStruct((8, n), F32),
        compiler_params=_cparams(("arbitrary",)),
        name="mod",
    )(cond, w_mod, b_mod.reshape(1, n))


def _rope_tables(t_len):
    half = HEAD_DIM // 2
    inv = ROPE_THETA ** (-np.arange(0, half, 2, dtype=np.float64) / half)
    t = np.arange(t_len)
    row, col = t // GRID_W, t % GRID_W
    lane = np.arange(LANES)
    i = lane % HEAD_DIM
    pos = np.where((i // half)[None, :] == 0, row[:, None], col[:, None]).astype(np.float64)
    j = i % half
    ang = pos * inv[j % (half // 2)][None, :]
    cos, sin = np.cos(ang), np.sin(ang)
    first = (j < half // 2)[None, :]
    s_up = np.where(first, -sin, 0.0)
    s_dn = np.where(first, 0.0, sin)
    return (jnp.asarray(cos, F32), jnp.asarray(s_up, F32), jnp.asarray(s_dn, F32))


def _inproj_kernel(x_ref, sh_ref, sc_ref, g_ref, w_ref, gqk_ref, grp_ref, *rest, rope):
    if rope:
        cos_ref, sup_ref, sdn_ref, q_ref, k_ref, v_ref, rw_ref = rest
    else:
        q_ref, k_ref, v_ref, rw_ref = rest
    x = x_ref[...]
    ms = jnp.mean(x * x, axis=-1, keepdims=True)
    h = x * lax.rsqrt(ms + NORM_EPS) * g_ref[...]
    h = h * (1.0 + sc_ref[0]) + sh_ref[0]
    proj = _dot(h.astype(BF16), w_ref[...])
    grp = grp_ref[...]
    for j in range((D_ATTN + D_KV) // LANES):
        blk = proj[:, j * LANES:(j + 1) * LANES]
        ss = _dot(blk * blk, grp, HIGHEST) * (1.0 / HEAD_DIM)
        nb = blk * lax.rsqrt(ss + NORM_EPS) * gqk_ref[:, j * LANES:(j + 1) * LANES]
        if rope:
            nb = (nb * cos_ref[...] + pltpu.roll(nb, LANES - 16, 1) * sup_ref[...]
                  + pltpu.roll(nb, 16, 1) * sdn_ref[...])
        if j < D_ATTN // LANES:
            q_ref[:, j * LANES:(j + 1) * LANES] = nb
        else:
            k_ref[...] = nb
    v_ref[...] = proj[:, D_ATTN + D_KV:D_QKV]
    rw_ref[...] = proj[:, D_QKV:]


def _in_proj(x2, shift, scale, norm_g, w_in_p, gqk, grp, t_len, rope, tm):
    n = x2.shape[0]
    per = t_len // tm
    bm = shift.shape[0]
    mod_map = (lambda i: (i // per, 0, 0)) if bm > 1 else (lambda i: (0, 0, 0))
    in_specs = [pl.BlockSpec((tm, D_MODEL), lambda i: (i, 0)),
                pl.BlockSpec((1, 1, D_MODEL), mod_map),
                pl.BlockSpec((1, 1, D_MODEL), mod_map),
                pl.BlockSpec((1, D_MODEL), lambda i: (0, 0)),
                pl.BlockSpec((D_MODEL, D_IN_PAD), lambda i: (0, 0)),
                pl.BlockSpec((1, D_ATTN + D_KV), lambda i: (0, 0)),
                pl.BlockSpec((LANES, LANES), lambda i: (0, 0))]
    args = [x2, shift, scale, norm_g, w_in_p, gqk, grp]
    if rope:
        tabs = _rope_tables(t_len)
        in_specs += [pl.BlockSpec((tm, LANES), lambda i: (i % per, 0))] * 3
        args += list(tabs)
    out_shape = (jax.ShapeDtypeStruct((n, D_ATTN), F32), jax.ShapeDtypeStruct((n, D_KV), F32),
                 jax.ShapeDtypeStruct((n, D_KV), F32), jax.ShapeDtypeStruct((n, D_RWKV_PAD), F32))
    out_specs = (pl.BlockSpec((tm, D_ATTN), lambda i: (i, 0)), pl.BlockSpec((tm, D_KV), lambda i: (i, 0)),
                 pl.BlockSpec((tm, D_KV), lambda i: (i, 0)), pl.BlockSpec((tm, D_RWKV_PAD), lambda i: (i, 0)))
    return pl.pallas_call(
        functools.partial(_inproj_kernel, rope=rope),
        grid=(n // tm,), in_specs=in_specs, out_specs=out_specs, out_shape=out_shape,
        compiler_params=_cparams(("parallel",)),
        name="in_proj_rope" if rope else "in_proj",
    )(*args)


def _attn_kernel(q_ref, k_ref, v_ref, o_ref):
    g, tq, hd = q_ref.shape[2:]
    q = q_ref[0, 0].reshape(g * tq, hd).astype(BF16)
    k = k_ref[0, 0].astype(BF16)
    v = v_ref[0, 0].astype(BF16)
    s = _dot_nt(q, k) * (HEAD_DIM ** -0.5)
    m = jnp.max(s, axis=-1, keepdims=True)
    e = jnp.exp(s - m)
    l = jnp.sum(e, axis=-1, keepdims=True)
    o = _dot(e.astype(BF16), v) / l
    o_ref[0, 0] = o.reshape(g, tq, hd)


def _attention(q5, k4, v4, tq):
    b, hk, g, t, hd = q5.shape
    tk = k4.shape[2]
    return pl.pallas_call(
        _attn_kernel,
        grid=(b, hk, t // tq),
        in_specs=[pl.BlockSpec((1, 1, g, tq, hd), lambda i, j, l: (i, j, 0, l, 0)),
                  pl.BlockSpec((1, 1, tk, hd), lambda i, j, l: (i, j, 0, 0)),
                  pl.BlockSpec((1, 1, tk, hd), lambda i, j, l: (i, j, 0, 0))],
        out_specs=pl.BlockSpec((1, 1, g, tq, hd), lambda i, j, l: (i, j, 0, l, 0)),
        out_shape=jax.ShapeDtypeStruct(q5.shape, F32),
        compiler_params=_cparams(("parallel", "parallel", "arbitrary")),
        name="attn",
    )(q5, k4, v4)


def _prep_kernel(rw_ref, hp_ref, hn_ref, mu_ref, wd_ref, w0_ref, wa_ref, a0_ref, wg_ref, kk_ref, ka_ref, rk_ref,
                 grp_ref, r_o, lw_o, kd_o, v_o, kk_o, bd_o, g_o, bonus_o):
    i = pl.program_id(1)
    n = pl.num_programs(1)
    cur = rw_ref[0]
    tt = cur.shape[0]
    rid = lax.broadcasted_iota(jnp.int32, cur.shape, 0)
    prev_row = jnp.where(i > 0, hp_ref[0, 7:8, :], 0.0)
    next_row = jnp.where(i < n - 1, hn_ref[0, 0:1, :], 0.0)
    prev = jnp.where(rid == 0, prev_row, pltpu.roll(cur, 1, 0))
    nxt = jnp.where(rid == tt - 1, next_row, pltpu.roll(cur, tt - 1, 0))
    p = cur + mu_ref[0:1, :] * (prev - cur) + mu_ref[1:2, :] * (nxt - cur)
    r = p[:, 0:D_RWKV]
    k = p[:, D_RWKV:2 * D_RWKV]
    v = p[:, 2 * D_RWKV:3 * D_RWKV]
    lo = p[:, 3 * D_RWKV:3 * D_RWKV + LANES]
    gd = p[:, 3 * D_RWKV + LANES:]
    grp = grp_ref[...]
    wlog = _dot(jnp.tanh(lo), wd_ref[...], HIGHEST) + w0_ref[...]
    alog = _dot(lo, wa_ref[...], HIGHEST) + a0_ref[...]
    g_o[0] = _dot(jax.nn.sigmoid(gd), wg_ref[...], HIGHEST)
    kx = k * kk_ref[...]
    kk = kx * lax.rsqrt(_head_sum(kx * kx, grp) + 1e-12)
    r_o[0] = r
    v_o[0] = v
    kk_o[0] = kk
    bonus_o[0] = _head_sum(r * k * rk_ref[...], grp) * v
    for z in range(2):
        a = jax.nn.sigmoid(alog[:, z * D_RWKV:(z + 1) * D_RWKV])
        lw_o[z, 0] = -DECAY_SCALE * jax.nn.sigmoid(wlog[:, z * D_RWKV:(z + 1) * D_RWKV])
        kd_o[z, 0] = k * (1.0 + (a - 1.0) * ka_ref[...])
        bd_o[z, 0] = kk * a


def _rwkv_prep(rw3, mu_p, wd_cat, w0_cat, wa_cat, a0_cat, wg_p, k_k, k_a, r_k, grp, tt):
    b, t, _ = rw3.shape
    nt = t // tt
    hb = tt // 8
    one = jax.ShapeDtypeStruct((b, t, D_RWKV), F32)
    two = jax.ShapeDtypeStruct((2, b, t, D_RWKV), F32)
    s_one = pl.BlockSpec((1, tt, D_RWKV), lambda i, j: (i, j, 0))
    s_two = pl.BlockSpec((2, 1, tt, D_RWKV), lambda i, j: (0, i, j, 0))
    full = lambda a: pl.BlockSpec(a.shape, lambda i, j: (0,) * a.ndim)
    consts = [mu_p, wd_cat, w0_cat, wa_cat, a0_cat, wg_p, k_k, k_a, r_k, grp]
    return pl.pallas_call(
        _prep_kernel,
        grid=(b, nt),
        in_specs=[pl.BlockSpec((1, tt, D_RWKV_PAD), lambda i, j: (i, j, 0)),
                  pl.BlockSpec((1, 8, D_RWKV_PAD), lambda i, j: (i, jnp.maximum(j * hb - 1, 0), 0)),
                  pl.BlockSpec((1, 8, D_RWKV_PAD), lambda i, j: (i, jnp.minimum((j + 1) * hb, t // 8 - 1), 0))]
                 + [full(a) for a in consts],
        out_specs=(s_one, s_two, s_two, s_one, s_one, s_two, s_one, s_one),
        out_shape=(one, two, two, one, one, two, one, one),
        compiler_params=_cparams(("parallel", "parallel")),
        name="rwkv_prep",
    )(rw3, rw3, rw3, *consts)


_NN = (((1,), (0,)), ((), ()))
_NT = (((1,), (1,)), ((), ()))
_TN = (((0,), (0,)), ((), ()))
SCAN_MODES = dict(gram=1, inv_base=1, inv_merge=1, from_s=1, from_v=1, u=1, y_u=1, state=1)
INV_BASE = 16


def _mm(a, b, passes, dims=_NN):
    dg = lambda x, y: lax.dot_general(x, y, dims, preferred_element_type=F32)
    a_hi = a.astype(BF16)
    b_hi = b.astype(BF16)
    if passes == 1:
        return dg(a_hi, b_hi)
    a_lo = (a - a_hi.astype(F32)).astype(BF16)
    b_lo = (b - b_hi.astype(F32)).astype(BF16)
    return dg(a_hi, b_hi) + (dg(a_hi, b_lo) + dg(a_lo, b_hi))


def _scan_kernel(r_ref, lw_ref, k_ref, v_ref, a_ref, b_ref, s0_ref, y_ref, sT_ref, s_scr, *, chunk):
    d = pl.program_id(1)
    i = pl.program_id(2)
    c = chunk

    @pl.when(i == 0)
    def _():
        s_scr[...] = s0_ref[0, 0]

    row = lax.broadcasted_iota(jnp.int32, (c, c), 0)
    col = lax.broadcasted_iota(jnp.int32, (c, c), 1)
    ahead = (row - col) * (1 - 2 * d)
    incl = ahead >= 0
    strict = ahead > 0
    eye = (row == col).astype(F32)
    same_blk = {}
    n = INV_BASE
    while n <= c:
        sh = jnp.int32(n.bit_length() - 1)
        same_blk[n] = lax.shift_right_logical(row, sh) == lax.shift_right_logical(col, sh)
        n *= 2

    lw = lw_ref[0, 0]
    cl = _dot(incl.astype(F32), lw, HIGHEST)
    tot = jnp.sum(lw, axis=0, keepdims=True)
    mid = 0.5 * tot
    e_abs = jnp.exp(cl)
    e_abx = jnp.exp(cl - lw)
    e_inv = jnp.exp(mid - cl)
    e_end = jnp.exp(tot - cl)
    p_tot = jnp.exp(tot)
    s_mid = jnp.exp(-mid)

    r_abs = r_ref[0] * e_abs
    a_abs = a_ref[0] * e_abx
    ar_abs = jnp.concatenate([a_abs, r_abs], axis=0)
    ar_mid = jnp.concatenate([a_abs * s_mid, r_abs * s_mid], axis=0)
    kb_inv = jnp.concatenate([k_ref[0, 0] * e_inv, b_ref[0, 0] * e_inv], axis=0)
    kb_end = jnp.concatenate([k_ref[0, 0] * e_end, b_ref[0, 0] * e_end], axis=0)
    v = v_ref[0]
    md = SCAN_MODES

    heads = range(N_RWKV_HEADS)
    sls = [slice(h * HEAD_DIM, (h + 1) * HEAD_DIM) for h in heads]
    s_old = [s_scr[h] for h in heads]
    vhs = [v[:, sl] for sl in sls]
    grams = [_mm(ar_mid[:, sl], kb_inv[:, sl], md["gram"], _NT) for sl in sls]
    from_s = [_mm(ar_abs[:, sl], s_old[h], md["from_s"], _NT) for h, sl in enumerate(sls)]
    masked = [jnp.concatenate([jnp.where(strict, g[:c, :c], 0.0), jnp.where(incl, g[c:, :c], 0.0)], axis=0)
              for g in grams]
    from_v = [_mm(masked[h], vhs[h], md["from_v"]) for h in heads]
    lmats = [jnp.where(strict, g[:c, c:], 0.0) for g in grams]
    l0s = [jnp.where(same_blk[INV_BASE], lm, 0.0) for lm in lmats]
    xs = [eye - l0 for l0 in l0s]
    pws = [_mm(l0, l0, md["inv_base"]) for l0 in l0s]
    span = 2
    while 2 * span < INV_BASE:
        both = [_mm(jnp.concatenate([xs[h], pws[h]], axis=0), pws[h], md["inv_base"]) for h in heads]
        xs = [xs[h] + both[h][:c] for h in heads]
        pws = [both[h][c:] for h in heads]
        span *= 2
    xs = [xs[h] + _mm(xs[h], pws[h], md["inv_base"]) for h in heads]
    n = INV_BASE
    while n < c:
        pair = same_blk[2 * n] & jnp.logical_not(same_blk[n])
        ts = [_mm(jnp.where(pair, lmats[h], 0.0), xs[h], md["inv_merge"]) for h in heads]
        xs = [xs[h] - _mm(xs[h], ts[h], md["inv_merge"]) for h in heads]
        n *= 2
    us = [_mm(xs[h], from_s[h][:c] + from_v[h][:c], md["u"]) for h in heads]
    yu = [_mm(jnp.where(incl, grams[h][c:, c:], 0.0), us[h], md["y_u"]) for h in heads]
    ds = [_mm(jnp.concatenate([vhs[h], -us[h]], axis=0), kb_end[:, sls[h]], md["state"], _TN) for h in heads]
    y_ref[0, 0] = jnp.concatenate([from_s[h][c:] + from_v[h][c:] - yu[h] for h in heads], axis=-1)
    for h in heads:
        s_scr[h] = s_old[h] * p_tot[:, sls[h]] + ds[h]

    @pl.when(i == pl.num_programs(2) - 1)
    def _():
        sT_ref[0, 0] = s_scr[...]


def _rwkv_scan(r, lw2, kd2, v, kk, bd2, s0, chunk):
    b, t, _ = r.shape
    nc = t // chunk
    tmap = lambda d, j: j + d * (nc - 1 - 2 * j)
    s_one = pl.BlockSpec((1, chunk, D_RWKV), lambda i, d, j: (i, tmap(d, j), 0))
    s_two = pl.BlockSpec((1, 1, chunk, D_RWKV), lambda i, d, j: (d, i, tmap(d, j), 0))
    s_st = pl.BlockSpec((1, 1, N_RWKV_HEADS, HEAD_DIM, HEAD_DIM), lambda i, d, j: (i, d, 0, 0, 0))
    return pl.pallas_call(
        functools.partial(_scan_kernel, chunk=chunk),
        grid=(b, 2, nc),
        in_specs=[s_one, s_two, s_two, s_one, s_one, s_two, s_st],
        out_specs=(s_two, s_st),
        out_shape=(jax.ShapeDtypeStruct((2, b, t, D_RWKV), F32), jax.ShapeDtypeStruct(s0.shape, F32)),
        scratch_shapes=[pltpu.VMEM((N_RWKV_HEADS, HEAD_DIM, HEAD_DIM), F32)],
        compiler_params=_cparams(("parallel", "parallel", "arbitrary")),
        name="rwkv_scan",
    )(r, lw2, kd2, v, kk, bd2, s0)


def _outproj_kernel(x_ref, at_ref, y2_ref, bonus_ref, gate_ref, lng_ref, lnb_ref, grp_ref, wo_ref,
                    g1_ref, sh_ref, sc_ref, n2_ref, wr_ref, br_ref, x1_ref, h2_ref, cmb_ref):
    grp = grp_ref[...]
    y = y2_ref[0, 0] + y2_ref[1, 0]
    mean = _head_sum(y, grp) * (1.0 / HEAD_DIM)
    yc = y - mean
    var = _head_sum(yc * yc, grp) * (1.0 / HEAD_DIM)
    yn = yc * lax.rsqrt(var + GN_EPS) * lng_ref[...] + lnb_ref[...]
    rw_out = (yn + bonus_ref[0]) * gate_ref[0]
    mix = (_dot(at_ref[0].astype(BF16), wo_ref[0:D_ATTN, :])
           + _dot(rw_out.astype(BF16), wo_ref[D_ATTN:, :]))
    x1 = x_ref[0] + g1_ref[0] * mix
    x1_ref[0] = x1
    ms = jnp.mean(x1 * x1, axis=-1, keepdims=True)
    h2 = x1 * lax.rsqrt(ms + NORM_EPS) * n2_ref[...]
    h2 = h2 * (1.0 + sc_ref[0]) + sh_ref[0]
    h2_ref[0] = h2.astype(BF16)
    logits = _dot(h2, wr_ref[...], HIGHEST) + br_ref[...]
    lane = lax.broadcasted_iota(jnp.int32, logits.shape, 1)
    neg = -jnp.inf
    big = jnp.int32(1 << 20)
    lc = jnp.where(lane < N_GROUPS, logits, neg)
    mc = jnp.max(lc, axis=-1, keepdims=True)
    g_w = 1.0 / jnp.sum(jnp.exp(lc - mc), axis=-1, keepdims=True)
    g_idx = jnp.min(jnp.where(lc == mc, lane, big), axis=-1, keepdims=True)
    eid = lane - N_GROUPS
    in_grp = (eid >= 0) & (eid < N_EXPERTS) & (lax.shift_right_arithmetic(eid, 2) == g_idx)
    lf = jnp.where(in_grp, logits, neg)
    m1 = jnp.max(lf, axis=-1, keepdims=True)
    i1 = jnp.min(jnp.where(lf == m1, lane, big), axis=-1, keepdims=True)
    lf2 = jnp.where(lane == i1, neg, lf)
    m2 = jnp.max(lf2, axis=-1, keepdims=True)
    i2 = jnp.min(jnp.where(lf2 == m2, lane, big), axis=-1, keepdims=True)
    e2 = jnp.exp(m2 - m1)
    w1 = 1.0 / (1.0 + e2)
    w2 = e2 * w1
    cmb_ref[0] = g_w * (jnp.where(lane == i1, w1, 0.0) + jnp.where(lane == i2, w2, 0.0))


def _out_proj(x3, attn3, y2, bonus, gate, ln_g, ln_b, grp, w_out_b, g1, sh2, sc2, norm2_g, w_r, b_r, tm):
    b, t, _ = x3.shape
    bm = g1.shape[0]
    mod_map = (lambda i, j: (i, 0, 0)) if bm > 1 else (lambda i, j: (0, 0, 0))
    tok = lambda w: pl.BlockSpec((1, tm, w), lambda i, j: (i, j, 0))
    full = lambda a: pl.BlockSpec(a.shape, lambda i, j: (0,) * a.ndim)
    mod = pl.BlockSpec((1, 1, D_MODEL), mod_map)
    return pl.pallas_call(
        _outproj_kernel,
        grid=(b, t // tm),
        in_specs=[tok(D_MODEL), tok(D_ATTN),
                  pl.BlockSpec((2, 1, tm, D_RWKV), lambda i, j: (0, i, j, 0)),
                  tok(D_RWKV), tok(D_RWKV), full(ln_g), full(ln_b), full(grp), full(w_out_b),
                  mod, mod, mod, full(norm2_g), full(w_r), full(b_r)],
        out_specs=(tok(D_MODEL), tok(D_MODEL), tok(ROUTER_LANES)),
        out_shape=(jax.ShapeDtypeStruct((b, t, D_MODEL), F32), jax.ShapeDtypeStruct((b, t, D_MODEL), BF16),
                   jax.ShapeDtypeStruct((b, t, ROUTER_LANES), F32)),
        compiler_params=_cparams(("parallel", "parallel")),
        name="out_proj",
    )(x3, attn3, y2, bonus, gate, ln_g, ln_b, grp, w_out_b, g1, sh2, sc2, norm2_g, w_r, b_r)


def _moe_kernel(h_ref, cmb_ref, x1_ref, g2_ref, wg_ref, wu_ref, wd_ref, o_ref, acc_ref):
    e = pl.program_id(2)

    @pl.when(e == 0)
    def _():
        acc_ref[...] = jnp.zeros_like(acc_ref)

    cmb = cmb_ref[0]
    lane = lax.broadcasted_iota(jnp.int32, cmb.shape, 1)
    c_e = jnp.sum(jnp.where(lane == e + N_GROUPS, cmb, 0.0), axis=-1, keepdims=True)
    h = h_ref[0]
    a = _dot(h, wg_ref[0])
    hid = a * jax.nn.sigmoid(a) * _dot(h, wu_ref[0])
    acc_ref[...] += _dot((hid * c_e).astype(BF16), wd_ref[0])

    @pl.when(e == pl.num_programs(2) - 1)
    def _():
        o_ref[0] = x1_ref[0] + g2_ref[0] * acc_ref[...]


def _moe(h2, cmb, x1, g2, wg_b, wu_b, wd_b, tm):
    b, t, _ = h2.shape
    bm = g2.shape[0]
    mod_map = (lambda i, j, e: (i, 0, 0)) if bm > 1 else (lambda i, j, e: (0, 0, 0))
    tok = lambda w: pl.BlockSpec((1, tm, w), lambda i, j, e: (i, j, 0))
    return pl.pallas_call(
        _moe_kernel,
        grid=(b, t // tm, N_EXPERTS),
        in_specs=[tok(D_MODEL), tok(ROUTER_LANES), tok(D_MODEL), pl.BlockSpec((1, 1, D_MODEL), mod_map),
                  pl.BlockSpec((1, D_MODEL, D_EXPERT), lambda i, j, e: (e, 0, 0)),
                  pl.BlockSpec((1, D_MODEL, D_EXPERT), lambda i, j, e: (e, 0, 0)),
                  pl.BlockSpec((1, D_EXPERT, D_MODEL), lambda i, j, e: (e, 0, 0))],
        out_specs=tok(D_MODEL),
        out_shape=jax.ShapeDtypeStruct((b, t, D_MODEL), F32),
        scratch_shapes=[pltpu.VMEM((tm, D_MODEL), F32)],
        compiler_params=_cparams(("parallel", "parallel", "arbitrary")),
        name="moe",
    )(h2, cmb, x1, g2, wg_b, wu_b, wd_b)


def _layer(x3, mod6, lp, t_tiles, ctx):
    b, t, _ = x3.shape
    sh1, sc1, g1, sh2, sc2, g2 = mod6
    rope = ctx is not None
    n = b * t
    q, k, v, rw = _in_proj(x3.reshape(n, D_MODEL), sh1, sc1, lp["norm1_g"], lp["w_in_p"], lp["gqk"], lp["grp"],
                           t, rope, t_tiles["in_proj"])
    q5 = q.reshape(b, t, N_KV_HEADS, GQA_GROUP, HEAD_DIM).transpose(0, 2, 3, 1, 4)
    k4 = k.reshape(b, t, N_KV_HEADS, HEAD_DIM).transpose(0, 2, 1, 3)
    v4 = v.reshape(b, t, N_KV_HEADS, HEAD_DIM).transpose(0, 2, 1, 3)
    if ctx is not None:
        ctx_k, ctx_v, ctx_state = ctx
        k4 = jnp.concatenate([k4, ctx_k.transpose(0, 2, 1, 3)], axis=2)
        v4 = jnp.concatenate([v4, ctx_v.transpose(0, 2, 1, 3)], axis=2)
        s0 = ctx_state
    else:
        s0 = jnp.zeros((b, 2, N_RWKV_HEADS, HEAD_DIM, HEAD_DIM), F32)
    o5 = _attention(q5, k4, v4, t_tiles["attn"])
    attn3 = o5.transpose(0, 3, 1, 2, 4).reshape(b, t, D_ATTN)
    r, lw2, kd2, vv, kk, bd2, gate, bonus = _rwkv_prep(
        rw.reshape(b, t, D_RWKV_PAD), lp["mu_p"], lp["wd_cat"], lp["w0_cat"], lp["wa_cat"], lp["a0_cat"],
        lp["wg_p"], lp["k_k"], lp["k_a"], lp["r_k"], lp["grp"], t_tiles["prep"])
    y2, s_t = _rwkv_scan(r, lw2, kd2, vv, kk, bd2, s0, t_tiles["chunk"])
    fb, ft = (1, n) if g1.shape[0] == 1 else (b, t)
    flat = lambda a: a.reshape(a.shape[:-3] + (fb, ft, a.shape[-1]))
    x1, h2, cmb = _out_proj(flat(x3), flat(attn3), flat(y2), flat(bonus), flat(gate), lp["ln_g"], lp["ln_b"],
                            lp["grp"], lp["w_out_b"], g1, sh2, sc2, lp["norm2_g"], lp["w_r"], lp["b_r"],
                            t_tiles["out_proj"])
    out = _moe(h2, cmb, x1, g2, lp["wg_b"], lp["wu_b"], lp["wd_b"], t_tiles["moe"]).reshape(b, t, D_MODEL)
    return out, k.reshape(b, t, N_KV_HEADS, HEAD_DIM), v.reshape(b, t, N_KV_HEADS, HEAD_DIM), s_t


def _block_diag2(w):
    z, l, c = w.shape
    out = jnp.zeros((LANES, z * c), F32)
    for i in range(z):
        out = out.at[i * l:(i + 1) * l, i * c:(i + 1) * c].set(w[i])
    return out


def _layer_params(l, w_in, norm1_g, norm2_g, mu_shift, q_norm_g, k_norm_g, w0, w_lora_up, a0, a_lora_up, g_lora_up,
                  k_k, k_a, r_k, ln_x_g, ln_x_b, w_out, router_c, router_c_b, router_f, router_f_b,
                  exp_gate, exp_up, exp_down):
    lane = np.arange(LANES)
    grp = jnp.asarray((lane[:, None] // HEAD_DIM) == (lane[None, :] // HEAD_DIM), F32)
    pad_in = D_IN_PAD - w_in.shape[2]
    wd_cat = _block_diag2(w_lora_up[l])
    wa_cat = jnp.roll(_block_diag2(a_lora_up[l]), 2 * DECAY_LORA, axis=0)
    w_r = jnp.zeros((D_MODEL, ROUTER_LANES), F32)
    w_r = w_r.at[:, :N_GROUPS].set(router_c[l]).at[:, N_GROUPS:N_GROUPS + N_EXPERTS].set(router_f[l])
    b_r = jnp.zeros((1, ROUTER_LANES), F32)
    b_r = b_r.at[0, :N_GROUPS].set(router_c_b[l]).at[0, N_GROUPS:N_GROUPS + N_EXPERTS].set(router_f_b[l])
    return dict(
        grp=grp,
        norm1_g=norm1_g[l].reshape(1, D_MODEL), norm2_g=norm2_g[l].reshape(1, D_MODEL),
        w_in_p=jnp.pad(w_in[l], ((0, 0), (0, pad_in))).astype(BF16),
        gqk=jnp.concatenate([jnp.tile(q_norm_g[l], N_Q_HEADS), jnp.tile(k_norm_g[l], N_KV_HEADS)]).reshape(1, -1),
        mu_p=jnp.pad(mu_shift[l], ((0, 0), (0, D_RWKV_PAD - D_RWKV_IN))),
        wd_cat=wd_cat, w0_cat=w0[l].reshape(1, 2 * D_RWKV),
        wa_cat=wa_cat, a0_cat=a0[l].reshape(1, 2 * D_RWKV),
        wg_p=jnp.pad(g_lora_up[l], ((0, LANES - GATE_LORA), (0, 0))),
        k_k=k_k[l].reshape(1, D_RWKV), k_a=k_a[l].reshape(1, D_RWKV), r_k=r_k[l].reshape(1, D_RWKV),
        ln_g=ln_x_g[l].reshape(1, D_RWKV), ln_b=ln_x_b[l].reshape(1, D_RWKV),
        w_out_b=w_out[l].astype(BF16), w_r=w_r, b_r=b_r,
        wg_b=exp_gate[l].astype(BF16), wu_b=exp_up[l].astype(BF16), wd_b=exp_down[l].astype(BF16),
    )


CTX_TILES = dict(in_proj=512, attn=256, prep=256, chunk=128, out_proj=512, moe=1024)
SMP_TILES = dict(in_proj=512, attn=128, prep=256, chunk=128, out_proj=512, moe=1024)


def kernel(x_prompt, x_sample, cache_k, cache_v, state_rwkv, c, c_ctx, w_mod, b_mod, norm1_g, norm2_g, w_in, mu_shift, q_norm_g, k_norm_g, w0, w_lora_up, a0, a_lora_up, g_lora_up, k_k, k_a, r_k, ln_x_g, ln_x_b, w_out, router_c, router_c_b, router_f, router_f_b, exp_gate, exp_up, exp_down):
    depth = w_mod.shape[0]
    db = x_sample.shape[0]
    y_prompt, y_sample = x_prompt, x_sample
    ks, vs, ss = [], [], []
    cond = jnp.zeros((8, D_MODEL), F32).at[:db].set(c).at[db].set(c_ctx)
    for l in range(depth):
        lp = _layer_params(l, w_in, norm1_g, norm2_g, mu_shift, q_norm_g, k_norm_g, w0, w_lora_up, a0, a_lora_up,
                           g_lora_up, k_k, k_a, r_k, ln_x_g, ln_x_b, w_out, router_c, router_c_b, router_f,
                           router_f_b, exp_gate, exp_up, exp_down)
        mod = _modulation(cond, w_mod[l], b_mod[l])
        mod_s = [mod[:db, i * D_MODEL:(i + 1) * D_MODEL].reshape(db, 1, D_MODEL) for i in range(6)]
        mod_c = [mod[db:db + 1, i * D_MODEL:(i + 1) * D_MODEL].reshape(1, 1, D_MODEL) for i in range(6)]
        y_prompt, k_l, v_l, s_l = _layer(y_prompt, mod_c, lp, CTX_TILES, None)
        ks.append(k_l)
        vs.append(v_l)
        ss.append(s_l)
        y_sample, _, _, _ = _layer(y_sample, mod_s, lp, SMP_TILES,
                                   (cache_k[:, l], cache_v[:, l], state_rwkv[:, l]))
    return (y_prompt, y_sample, jnp.stack(ks, axis=1), jnp.stack(vs, axis=1), jnp.stack(ss, axis=1))
```

```python
import functools

import numpy as np
import jax
import jax.numpy as jnp
from jax import lax
from jax.experimental import pallas as pl
from jax.experimental.pallas import tpu as pltpu

F32 = jnp.float32
BF16 = jnp.bfloat16
HIGHEST = lax.Precision.HIGHEST

D_MODEL = 1024
HEAD_DIM = 64
N_Q_HEADS = 8
N_KV_HEADS = 2
GQA_GROUP = N_Q_HEADS // N_KV_HEADS
D_ATTN = N_Q_HEADS * HEAD_DIM
D_KV = N_KV_HEADS * HEAD_DIM
N_RWKV_HEADS = 8
D_RWKV = 512
DECAY_LORA = 32
AAA_LORA = 32
GATE_LORA = 96
D_RWKV_IN = 3 * D_RWKV + 2 * DECAY_LORA + 2 * AAA_LORA + GATE_LORA
D_RWKV_PAD = 1792
D_QKV = D_ATTN + 2 * D_KV
D_IN_PAD = D_QKV + D_RWKV_PAD
N_GROUPS = 4
EXPERTS_PER_GROUP = 4
N_EXPERTS = 16
D_EXPERT = 512
GRID_W = 64
ROPE_THETA = 10000.0
NORM_EPS = 1e-6
GN_EPS = 64e-5
DECAY_SCALE = 0.6065306597
QK_EXP2_SCALE = (HEAD_DIM ** -0.5) * float(np.log2(np.e))
LANES = 128
ROUTER_LANES = 128
VMEM_LIMIT = 56 * 1024 * 1024


def _cparams(sem):
    return pltpu.CompilerParams(dimension_semantics=sem, vmem_limit_bytes=VMEM_LIMIT)


def _dot(a, b, precision=None):
    return jnp.dot(a, b, preferred_element_type=F32, precision=precision)


def _dot_nt(a, b, precision=None):
    return lax.dot_general(a, b, (((1,), (1,)), ((), ())), preferred_element_type=F32, precision=precision)


def _dot_tn(a, b, precision=None):
    return lax.dot_general(a, b, (((0,), (0,)), ((), ())), preferred_element_type=F32, precision=precision)


def _head_sum(x, g):
    n = x.shape[-1] // LANES
    return jnp.concatenate([_dot(x[:, j * LANES:(j + 1) * LANES], g, HIGHEST) for j in range(n)], axis=-1)


def _mod_kernel(c_ref, w_ref, b_ref, o_ref):
    c = c_ref[...]
    s = c * jax.nn.sigmoid(c)
    o_ref[...] = _dot(s, w_ref[...], HIGHEST) + b_ref[...]


def _modulation(cond, w_mod, b_mod):
    n = w_mod.shape[1]
    tn = 1024
    return pl.pallas_call(
        _mod_kernel,
        grid=(n // tn,),
        in_specs=[pl.BlockSpec((8, D_MODEL), lambda j: (0, 0)),
                  pl.BlockSpec((D_MODEL, tn), lambda j: (0, j)),
                  pl.BlockSpec((1, tn), lambda j: (0, j))],
        out_specs=pl.BlockSpec((8, tn), lambda j: (0, j)),
        out_shape=jax.ShapeDtypeStruct((8, n), F32),
        compiler_params=_cparams(("arbitrary",)),
        name="mod",
    )(cond, w_mod, b_mod.reshape(1, n))


def _rope_tables(t_len):
    half = HEAD_DIM // 2
    inv = ROPE_THETA ** (-np.arange(0, half, 2, dtype=np.float64) / half)
    t = np.arange(t_len)
    row, col = t // GRID_W, t % GRID_W
    lane = np.arange(LANES)
    i = lane % HEAD_DIM
    pos = np.where((i // half)[None, :] == 0, row[:, None], col[:, None]).astype(np.float64)
    j = i % half
    ang = pos * inv[j % (half // 2)][None, :]
    cos, sin = np.cos(ang), np.sin(ang)
    first = (j < half // 2)[None, :]
    s_up = np.where(first, -sin, 0.0)
    s_dn = np.where(first, 0.0, sin)
    return (jnp.asarray(cos, F32), jnp.asarray(s_up, F32), jnp.asarray(s_dn, F32))


def _inproj_kernel(x_ref, sh_ref, sc_ref, g_ref, w_ref, gqk_ref, grp_ref, *rest, rope):
    if rope:
        cos_ref, sup_ref, sdn_ref, q_ref, k_ref, v_ref, kf_ref, vf_ref, rw_ref = rest
    else:
        q_ref, k_ref, v_ref, kf_ref, vf_ref, rw_ref = rest
    x = x_ref[0]
    ms = jnp.mean(x * x, axis=-1, keepdims=True)
    h = x * lax.rsqrt(ms + NORM_EPS) * g_ref[...]
    h = h * (1.0 + sc_ref[0]) + sh_ref[0]
    proj = _dot(h.astype(BF16), w_ref[...])
    grp = grp_ref[...]
    lo_half = lax.broadcasted_iota(jnp.int32, (x.shape[0], LANES), 1) < HEAD_DIM
    for j in range((D_ATTN + D_KV) // LANES):
        blk = proj[:, j * LANES:(j + 1) * LANES]
        ss = _dot(blk * blk, grp, HIGHEST) * (1.0 / HEAD_DIM)
        nb = blk * lax.rsqrt(ss + NORM_EPS) * gqk_ref[:, j * LANES:(j + 1) * LANES]
        if rope:
            nb = (nb * cos_ref[...] + pltpu.roll(nb, LANES - 16, 1) * sup_ref[...]
                  + pltpu.roll(nb, 16, 1) * sdn_ref[...])
        if j < D_ATTN // LANES:
            nbq = nb * QK_EXP2_SCALE
            for half in range(2):
                hq = 2 * j + half
                q_ref[0, hq // GQA_GROUP, hq % GQA_GROUP] = nbq[:, half * HEAD_DIM:(half + 1) * HEAD_DIM].astype(BF16)
        else:
            kf_ref[0] = nb
            k_ref[0, 0] = nb[:, :HEAD_DIM].astype(BF16)
            k_ref[0, 1] = nb[:, HEAD_DIM:].astype(BF16)
    vblk = proj[:, D_ATTN + D_KV:D_QKV]
    vf_ref[0] = vblk
    v_ref[0, 0] = jnp.where(lo_half, vblk, 1.0).astype(BF16)
    v_ref[0, 1] = jnp.where(lo_half, pltpu.roll(vblk, HEAD_DIM, 1), 1.0).astype(BF16)
    rw_ref[0] = proj[:, D_QKV:]


def _in_proj(x3, shift, scale, norm_g, w_in_p, gqk, grp, rope, tm):
    b, t, _ = x3.shape
    bm = shift.shape[0]
    mod_map = (lambda i, j: (i, 0, 0)) if bm > 1 else (lambda i, j: (0, 0, 0))
    full = lambda a: pl.BlockSpec(a.shape, lambda i, j: (0,) * a.ndim)
    tok = lambda w: pl.BlockSpec((1, tm, w), lambda i, j: (i, j, 0))
    in_specs = [tok(D_MODEL), pl.BlockSpec((1, 1, D_MODEL), mod_map), pl.BlockSpec((1, 1, D_MODEL), mod_map),
                full(norm_g), full(w_in_p), full(gqk), full(grp)]
    args = [x3, shift, scale, norm_g, w_in_p, gqk, grp]
    if rope:
        in_specs += [pl.BlockSpec((tm, LANES), lambda i, j: (j, 0))] * 3
        args += list(_rope_tables(t))
    out_shape = (jax.ShapeDtypeStruct((b, N_KV_HEADS, GQA_GROUP, t, HEAD_DIM), BF16),
                 jax.ShapeDtypeStruct((b, N_KV_HEADS, t, HEAD_DIM), BF16),
                 jax.ShapeDtypeStruct((b, N_KV_HEADS, t, 2 * HEAD_DIM), BF16),
                 jax.ShapeDtypeStruct((b, t, D_KV), F32), jax.ShapeDtypeStruct((b, t, D_KV), F32),
                 jax.ShapeDtypeStruct((b, t, D_RWKV_PAD), F32))
    out_specs = (pl.BlockSpec((1, N_KV_HEADS, GQA_GROUP, tm, HEAD_DIM), lambda i, j: (i, 0, 0, j, 0)),
                 pl.BlockSpec((1, N_KV_HEADS, tm, HEAD_DIM), lambda i, j: (i, 0, j, 0)),
                 pl.BlockSpec((1, N_KV_HEADS, tm, 2 * HEAD_DIM), lambda i, j: (i, 0, j, 0)),
                 tok(D_KV), tok(D_KV), tok(D_RWKV_PAD))
    return pl.pallas_call(
        functools.partial(_inproj_kernel, rope=rope),
        grid=(b, t // tm), in_specs=in_specs, out_specs=out_specs, out_shape=out_shape,
        compiler_params=_cparams(("parallel", "parallel")),
        name="in_proj_rope" if rope else "in_proj",
    )(*args)


def _attn_kernel(q_ref, k_ref, v_ref, o_ref):
    g, tq, hd = q_ref.shape[2:]
    k = k_ref[0, 0]
    v = v_ref[0, 0]
    ss = [_dot_nt(q_ref[0, 0, i], k) for i in range(g)]
    ps = [jnp.exp2(s - jnp.max(s, axis=-1, keepdims=True)).astype(BF16) for s in ss]
    outs = []
    for i in range(g):
        o = _dot(ps[i], v)
        outs.append(o[:, :hd] / pltpu.roll(o, hd, 1)[:, :hd])
    o_ref[0] = jnp.concatenate(outs, axis=-1)


def _attention(q5, k4, v4, tq):
    b, hk, g, t, hd = q5.shape
    tk = k4.shape[2]
    return pl.pallas_call(
        _attn_kernel,
        grid=(b, hk, t // tq),
        in_specs=[pl.BlockSpec((1, 1, g, tq, hd), lambda i, j, l: (i, j, 0, l, 0)),
                  pl.BlockSpec((1, 1, tk, hd), lambda i, j, l: (i, j, 0, 0)),
                  pl.BlockSpec((1, 1, tk, 2 * hd), lambda i, j, l: (i, j, 0, 0))],
        out_specs=pl.BlockSpec((1, tq, g * hd), lambda i, j, l: (i, l, j)),
        out_shape=jax.ShapeDtypeStruct((b, t, hk * g * hd), F32),
        compiler_params=_cparams(("parallel", "parallel", "arbitrary")),
        name="attn",
    )(q5, k4, v4)


def _prep_kernel(rw_ref, hp_ref, hn_ref, mu_ref, wd_ref, w0_ref, wa_ref, a0_ref, wg_ref, kk_ref, ka_ref, rk_ref,
                 grp_ref, r_o, lw_o, kd_o, v_o, kk_o, bd_o, g_o, bonus_o):
    i = pl.program_id(1)
    n = pl.num_programs(1)
    cur = rw_ref[0]
    tt = cur.shape[0]
    rid = lax.broadcasted_iota(jnp.int32, cur.shape, 0)
    prev_row = jnp.where(i > 0, hp_ref[0, 7:8, :], 0.0)
    next_row = jnp.where(i < n - 1, hn_ref[0, 0:1, :], 0.0)
    prev = jnp.where(rid == 0, prev_row, pltpu.roll(cur, 1, 0))
    nxt = jnp.where(rid == tt - 1, next_row, pltpu.roll(cur, tt - 1, 0))
    p = cur + mu_ref[0:1, :] * (prev - cur) + mu_ref[1:2, :] * (nxt - cur)
    r = p[:, 0:D_RWKV]
    k = p[:, D_RWKV:2 * D_RWKV]
    v = p[:, 2 * D_RWKV:3 * D_RWKV]
    lo = p[:, 3 * D_RWKV:3 * D_RWKV + LANES]
    gd = p[:, 3 * D_RWKV + LANES:]
    grp = grp_ref[...]
    wlog = _dot(jnp.tanh(lo), wd_ref[...], HIGHEST) + w0_ref[...]
    alog = _dot(lo, wa_ref[...], HIGHEST) + a0_ref[...]
    g_o[0] = _dot(jax.nn.sigmoid(gd), wg_ref[...], HIGHEST)
    kx = k * kk_ref[...]
    kk = kx * lax.rsqrt(_head_sum(kx * kx, grp) + 1e-12)
    r_o[0] = r
    v_o[0] = v
    kk_o[0] = kk
    bonus_o[0] = _head_sum(r * k * rk_ref[...], grp) * v
    for z in range(2):
        a = jax.nn.sigmoid(alog[:, z * D_RWKV:(z + 1) * D_RWKV])
        lw_o[z, 0] = -DECAY_SCALE * jax.nn.sigmoid(wlog[:, z * D_RWKV:(z + 1) * D_RWKV])
        kd_o[z, 0] = k * (1.0 + (a - 1.0) * ka_ref[...])
        bd_o[z, 0] = kk * a


def _rwkv_prep(rw3, mu_p, wd_cat, w0_cat, wa_cat, a0_cat, wg_p, k_k, k_a, r_k, grp, tt):
    b, t, _ = rw3.shape
    nt = t // tt
    hb = tt // 8
    one = jax.ShapeDtypeStruct((b, t, D_RWKV), F32)
    two = jax.ShapeDtypeStruct((2, b, t, D_RWKV), F32)
    s_one = pl.BlockSpec((1, tt, D_RWKV), lambda i, j: (i, j, 0))
    s_two = pl.BlockSpec((2, 1, tt, D_RWKV), lambda i, j: (0, i, j, 0))
    full = lambda a: pl.BlockSpec(a.shape, lambda i, j: (0,) * a.ndim)
    consts = [mu_p, wd_cat, w0_cat, wa_cat, a0_cat, wg_p, k_k, k_a, r_k, grp]
    return pl.pallas_call(
        _prep_kernel,
        grid=(b, nt),
        in_specs=[pl.BlockSpec((1, tt, D_RWKV_PAD), lambda i, j: (i, j, 0)),
                  pl.BlockSpec((1, 8, D_RWKV_PAD), lambda i, j: (i, jnp.maximum(j * hb - 1, 0), 0)),
                  pl.BlockSpec((1, 8, D_RWKV_PAD), lambda i, j: (i, jnp.minimum((j + 1) * hb, t // 8 - 1), 0))]
                 + [full(a) for a in consts],
        out_specs=(s_one, s_two, s_two, s_one, s_one, s_two, s_one, s_one),
        out_shape=(one, two, two, one, one, two, one, one),
        compiler_params=_cparams(("parallel", "parallel")),
        name="rwkv_prep",
    )(rw3, rw3, rw3, *consts)


_NN = (((1,), (0,)), ((), ()))
_NT = (((1,), (1,)), ((), ()))
_TN = (((0,), (0,)), ((), ()))
SCAN_MODES = dict(gram=1, inv_base=1, inv_merge=1, from_s=1, from_v=1, u=1, y_u=1, state=1)
INV_BASE = 16


def _mm(a, b, passes, dims=_NN):
    dg = lambda x, y: lax.dot_general(x, y, dims, preferred_element_type=F32)
    a_hi = a.astype(BF16)
    b_hi = b.astype(BF16)
    if passes == 1:
        return dg(a_hi, b_hi)
    a_lo = (a - a_hi.astype(F32)).astype(BF16)
    b_lo = (b - b_hi.astype(F32)).astype(BF16)
    return dg(a_hi, b_hi) + (dg(a_hi, b_lo) + dg(a_lo, b_hi))


def _scan_kernel(r_ref, lw_ref, k_ref, v_ref, a_ref, b_ref, s0_ref, y_ref, sT_ref, s_scr, *, chunk):
    d = pl.program_id(1)
    i = pl.program_id(2)
    c = chunk

    @pl.when(i == 0)
    def _():
        s_scr[...] = s0_ref[0, 0]

    row = lax.broadcasted_iota(jnp.int32, (c, c), 0)
    col = lax.broadcasted_iota(jnp.int32, (c, c), 1)
    ahead = (row - col) * (1 - 2 * d)
    incl = ahead >= 0
    strict = ahead > 0
    eye = (row == col).astype(F32)
    same_blk = {}
    n = INV_BASE
    while n <= c:
        sh = jnp.int32(n.bit_length() - 1)
        same_blk[n] = lax.shift_right_logical(row, sh) == lax.shift_right_logical(col, sh)
        n *= 2

    lw = lw_ref[0, 0]
    cl = _dot(incl.astype(F32), lw, HIGHEST)
    tot = jnp.sum(lw, axis=0, keepdims=True)
    mid = 0.5 * tot
    e_abs = jnp.exp(cl)
    e_abx = jnp.exp(cl - lw)
    e_inv = jnp.exp(mid - cl)
    e_end = jnp.exp(tot - cl)
    p_tot = jnp.exp(tot)
    s_mid = jnp.exp(-mid)

    r_abs = r_ref[0] * e_abs
    a_abs = a_ref[0] * e_abx
    ar_abs = jnp.concatenate([a_abs, r_abs], axis=0)
    ar_mid = jnp.concatenate([a_abs * s_mid, r_abs * s_mid], axis=0)
    kb_inv = jnp.concatenate([k_ref[0, 0] * e_inv, b_ref[0, 0] * e_inv], axis=0)
    kb_end = jnp.concatenate([k_ref[0, 0] * e_end, b_ref[0, 0] * e_end], axis=0)
    v = v_ref[0]
    md = SCAN_MODES

    heads = range(N_RWKV_HEADS)
    sls = [slice(h * HEAD_DIM, (h + 1) * HEAD_DIM) for h in heads]
    s_old = [s_scr[h] for h in heads]
    vhs = [v[:, sl] for sl in sls]
    grams = [_mm(ar_mid[:, sl], kb_inv[:, sl], md["gram"], _NT) for sl in sls]
    from_s = [_mm(ar_abs[:, sl], s_old[h], md["from_s"], _NT) for h, sl in enumerate(sls)]
    masked = [jnp.concatenate([jnp.where(strict, g[:c, :c], 0.0), jnp.where(incl, g[c:, :c], 0.0)], axis=0)
              for g in grams]
    from_v = [_mm(masked[h], vhs[h], md["from_v"]) for h in heads]
    lmats = [jnp.where(strict, g[:c, c:], 0.0) for g in grams]
    l0s = [jnp.where(same_blk[INV_BASE], lm, 0.0) for lm in lmats]
    xs = [eye - l0 for l0 in l0s]
    pws = [_mm(l0, l0, md["inv_base"]) for l0 in l0s]
    span = 2
    while 2 * span < INV_BASE:
        both = [_mm(jnp.concatenate([xs[h], pws[h]], axis=0), pws[h], md["inv_base"]) for h in heads]
        xs = [xs[h] + both[h][:c] for h in heads]
        pws = [both[h][c:] for h in heads]
        span *= 2
    xs = [xs[h] + _mm(xs[h], pws[h], md["inv_base"]) for h in heads]
    n = INV_BASE
    while n < c:
        pair = same_blk[2 * n] & jnp.logical_not(same_blk[n])
        ts = [_mm(jnp.where(pair, lmats[h], 0.0), xs[h], md["inv_merge"]) for h in heads]
        xs = [xs[h] - _mm(xs[h], ts[h], md["inv_merge"]) for h in heads]
        n *= 2
    us = [_mm(xs[h], from_s[h][:c] + from_v[h][:c], md["u"]) for h in heads]
    yu = [_mm(jnp.where(incl, grams[h][c:, c:], 0.0), us[h], md["y_u"]) for h in heads]
    ds = [_mm(jnp.concatenate([vhs[h], -us[h]], axis=0), kb_end[:, sls[h]], md["state"], _TN) for h in heads]
    y_ref[0, 0] = jnp.concatenate([from_s[h][c:] + from_v[h][c:] - yu[h] for h in heads], axis=-1)
    for h in heads:
        s_scr[h] = s_old[h] * p_tot[:, sls[h]] + ds[h]

    @pl.when(i == pl.num_programs(2) - 1)
    def _():
        sT_ref[0, 0] = s_scr[...]


def _rwkv_scan(r, lw2, kd2, v, kk, bd2, s0, chunk):
    b, t, _ = r.shape
    nc = t // chunk
    tmap = lambda d, j: j + d * (nc - 1 - 2 * j)
    s_one = pl.BlockSpec((1, chunk, D_RWKV), lambda i, d, j: (i, tmap(d, j), 0))
    s_two = pl.BlockSpec((1, 1, chunk, D_RWKV), lambda i, d, j: (d, i, tmap(d, j), 0))
    s_st = pl.BlockSpec((1, 1, N_RWKV_HEADS, HEAD_DIM, HEAD_DIM), lambda i, d, j: (i, d, 0, 0, 0))
    return pl.pallas_call(
        functools.partial(_scan_kernel, chunk=chunk),
        grid=(b, 2, nc),
        in_specs=[s_one, s_two, s_two, s_one, s_one, s_two, s_st],
        out_specs=(s_two, s_st),
        out_shape=(jax.ShapeDtypeStruct((2, b, t, D_RWKV), F32), jax.ShapeDtypeStruct(s0.shape, F32)),
        scratch_shapes=[pltpu.VMEM((N_RWKV_HEADS, HEAD_DIM, HEAD_DIM), F32)],
        compiler_params=_cparams(("parallel", "parallel", "arbitrary")),
        name="rwkv_scan",
    )(r, lw2, kd2, v, kk, bd2, s0)


def _outproj_kernel(x_ref, at_ref, y2_ref, bonus_ref, gate_ref, lng_ref, lnb_ref, grp_ref, wo_ref,
                    g1_ref, sh_ref, sc_ref, n2_ref, wr_ref, br_ref, x1_ref, h2_ref, cmb_ref):
    grp = grp_ref[...]
    y = y2_ref[0, 0] + y2_ref[1, 0]
    mean = _head_sum(y, grp) * (1.0 / HEAD_DIM)
    yc = y - mean
    var = _head_sum(yc * yc, grp) * (1.0 / HEAD_DIM)
    yn = yc * lax.rsqrt(var + GN_EPS) * lng_ref[...] + lnb_ref[...]
    rw_out = (yn + bonus_ref[0]) * gate_ref[0]
    mix = (_dot(at_ref[0].astype(BF16), wo_ref[0:D_ATTN, :])
           + _dot(rw_out.astype(BF16), wo_ref[D_ATTN:, :]))
    x1 = x_ref[0] + g1_ref[0] * mix
    x1_ref[0] = x1
    ms = jnp.mean(x1 * x1, axis=-1, keepdims=True)
    h2 = x1 * lax.rsqrt(ms + NORM_EPS) * n2_ref[...]
    h2 = h2 * (1.0 + sc_ref[0]) + sh_ref[0]
    h2_ref[0] = h2.astype(BF16)
    logits = _dot(h2, wr_ref[...], HIGHEST) + br_ref[...]
    lane = lax.broadcasted_iota(jnp.int32, logits.shape, 1)
    neg = -jnp.inf
    big = jnp.int32(1 << 20)
    lc = jnp.where(lane < N_GROUPS, logits, neg)
    mc = jnp.max(lc, axis=-1, keepdims=True)
    g_w = 1.0 / jnp.sum(jnp.exp(lc - mc), axis=-1, keepdims=True)
    g_idx = jnp.min(jnp.where(lc == mc, lane, big), axis=-1, keepdims=True)
    eid = lane - N_GROUPS
    in_grp = (eid >= 0) & (eid < N_EXPERTS) & (lax.shift_right_arithmetic(eid, 2) == g_idx)
    lf = jnp.where(in_grp, logits, neg)
    m1 = jnp.max(lf, axis=-1, keepdims=True)
    i1 = jnp.min(jnp.where(lf == m1, lane, big), axis=-1, keepdims=True)
    lf2 = jnp.where(lane == i1, neg, lf)
    m2 = jnp.max(lf2, axis=-1, keepdims=True)
    i2 = jnp.min(jnp.where(lf2 == m2, lane, big), axis=-1, keepdims=True)
    e2 = jnp.exp(m2 - m1)
    w1 = 1.0 / (1.0 + e2)
    w2 = e2 * w1
    cmb_ref[0] = g_w * (jnp.where(lane == i1, w1, 0.0) + jnp.where(lane == i2, w2, 0.0))


def _out_proj(x3, attn3, y2, bonus, gate, ln_g, ln_b, grp, w_out_b, g1, sh2, sc2, norm2_g, w_r, b_r, tm):
    b, t, _ = x3.shape
    bm = g1.shape[0]
    mod_map = (lambda i, j: (i, 0, 0)) if bm > 1 else (lambda i, j: (0, 0, 0))
    tok = lambda w: pl.BlockSpec((1, tm, w), lambda i, j: (i, j, 0))
    full = lambda a: pl.BlockSpec(a.shape, lambda i, j: (0,) * a.ndim)
    mod = pl.BlockSpec((1, 1, D_MODEL), mod_map)
    return pl.pallas_call(
        _outproj_kernel,
        grid=(b, t // tm),
        in_specs=[tok(D_MODEL), tok(D_ATTN),
                  pl.BlockSpec((2, 1, tm, D_RWKV), lambda i, j: (0, i, j, 0)),
                  tok(D_RWKV), tok(D_RWKV), full(ln_g), full(ln_b), full(grp), full(w_out_b),
                  mod, mod, mod, full(norm2_g), full(w_r), full(b_r)],
        out_specs=(tok(D_MODEL), tok(D_MODEL), tok(ROUTER_LANES)),
        out_shape=(jax.ShapeDtypeStruct((b, t, D_MODEL), F32), jax.ShapeDtypeStruct((b, t, D_MODEL), BF16),
                   jax.ShapeDtypeStruct((b, t, ROUTER_LANES), F32)),
        compiler_params=_cparams(("parallel", "parallel")),
        name="out_proj",
    )(x3, attn3, y2, bonus, gate, ln_g, ln_b, grp, w_out_b, g1, sh2, sc2, norm2_g, w_r, b_r)


def _moe_kernel(h_ref, cmb_ref, x1_ref, g2_ref, wg_ref, wu_ref, wd_ref, o_ref, acc_ref):
    e = pl.program_id(2)

    @pl.when(e == 0)
    def _():
        acc_ref[...] = jnp.zeros_like(acc_ref)

    cmb = cmb_ref[0]
    lane = lax.broadcasted_iota(jnp.int32, cmb.shape, 1)
    c_e = jnp.sum(jnp.where(lane == e + N_GROUPS, cmb, 0.0), axis=-1, keepdims=True)
    h = h_ref[0]
    a = _dot(h, wg_ref[0])
    hid = a * jax.nn.sigmoid(a) * _dot(h, wu_ref[0])
    acc_ref[...] += _dot((hid * c_e).astype(BF16), wd_ref[0])

    @pl.when(e == pl.num_programs(2) - 1)
    def _():
        o_ref[0] = x1_ref[0] + g2_ref[0] * acc_ref[...]


def _moe(h2, cmb, x1, g2, wg_b, wu_b, wd_b, tm):
    b, t, _ = h2.shape
    bm = g2.shape[0]
    mod_map = (lambda i, j, e: (i, 0, 0)) if bm > 1 else (lambda i, j, e: (0, 0, 0))
    tok = lambda w: pl.BlockSpec((1, tm, w), lambda i, j, e: (i, j, 0))
    return pl.pallas_call(
        _moe_kernel,
        grid=(b, t // tm, N_EXPERTS),
        in_specs=[tok(D_MODEL), tok(ROUTER_LANES), tok(D_MODEL), pl.BlockSpec((1, 1, D_MODEL), mod_map),
                  pl.BlockSpec((1, D_MODEL, D_EXPERT), lambda i, j, e: (e, 0, 0)),
                  pl.BlockSpec((1, D_MODEL, D_EXPERT), lambda i, j, e: (e, 0, 0)),
                  pl.BlockSpec((1, D_EXPERT, D_MODEL), lambda i, j, e: (e, 0, 0))],
        out_specs=tok(D_MODEL),
        out_shape=jax.ShapeDtypeStruct((b, t, D_MODEL), F32),
        scratch_shapes=[pltpu.VMEM((tm, D_MODEL), F32)],
        compiler_params=_cparams(("parallel", "parallel", "arbitrary")),
        name="moe",
    )(h2, cmb, x1, g2, wg_b, wu_b, wd_b)


def _layer(x3, mod6, lp, t_tiles, ctx):
    b, t, _ = x3.shape
    sh1, sc1, g1, sh2, sc2, g2 = mod6
    rope = ctx is not None
    n = b * t
    q5, k4, v4, k, v, rw = _in_proj(x3, sh1, sc1, lp["norm1_g"], lp["w_in_p"], lp["gqk"], lp["grp"],
                                    rope, t_tiles["in_proj"])
    if ctx is not None:
        ctx_k, ctx_v, ctx_state = ctx
        ck = ctx_k.transpose(0, 2, 1, 3)
        cv = ctx_v.transpose(0, 2, 1, 3)
        k4 = jnp.concatenate([k4, ck.astype(BF16)], axis=2)
        v4 = jnp.concatenate([v4, jnp.concatenate([cv, jnp.ones_like(cv)], axis=-1).astype(BF16)], axis=2)
        s0 = ctx_state
    else:
        s0 = jnp.zeros((b, 2, N_RWKV_HEADS, HEAD_DIM, HEAD_DIM), F32)
    attn3 = _attention(q5, k4, v4, t_tiles["attn"])
    r, lw2, kd2, vv, kk, bd2, gate, bonus = _rwkv_prep(
        rw, lp["mu_p"], lp["wd_cat"], lp["w0_cat"], lp["wa_cat"], lp["a0_cat"],
        lp["wg_p"], lp["k_k"], lp["k_a"], lp["r_k"], lp["grp"], t_tiles["prep"])
    y2, s_t = _rwkv_scan(r, lw2, kd2, vv, kk, bd2, s0, t_tiles["chunk"])
    fb, ft = (1, n) if g1.shape[0] == 1 else (b, t)
    flat = lambda a: a.reshape(a.shape[:-3] + (fb, ft, a.shape[-1]))
    x1, h2, cmb = _out_proj(flat(x3), flat(attn3), flat(y2), flat(bonus), flat(gate), lp["ln_g"], lp["ln_b"],
                            lp["grp"], lp["w_out_b"], g1, sh2, sc2, lp["norm2_g"], lp["w_r"], lp["b_r"],
                            t_tiles["out_proj"])
    out = _moe(h2, cmb, x1, g2, lp["wg_b"], lp["wu_b"], lp["wd_b"], t_tiles["moe"]).reshape(b, t, D_MODEL)
    return out, k.reshape(b, t, N_KV_HEADS, HEAD_DIM), v.reshape(b, t, N_KV_HEADS, HEAD_DIM), s_t


def _block_diag2(w):
    z, l, c = w.shape
    out = jnp.zeros((LANES, z * c), F32)
    for i in range(z):
        out = out.at[i * l:(i + 1) * l, i * c:(i + 1) * c].set(w[i])
    return out


def _layer_params(l, w_in, norm1_g, norm2_g, mu_shift, q_norm_g, k_norm_g, w0, w_lora_up, a0, a_lora_up, g_lora_up,
                  k_k, k_a, r_k, ln_x_g, ln_x_b, w_out, router_c, router_c_b, router_f, router_f_b,
                  exp_gate, exp_up, exp_down):
    lane = np.arange(LANES)
    grp = jnp.asarray((lane[:, None] // HEAD_DIM) == (lane[None, :] // HEAD_DIM), F32)
    pad_in = D_IN_PAD - w_in.shape[2]
    wd_cat = _block_diag2(w_lora_up[l])
    wa_cat = jnp.roll(_block_diag2(a_lora_up[l]), 2 * DECAY_LORA, axis=0)
    w_r = jnp.zeros((D_MODEL, ROUTER_LANES), F32)
    w_r = w_r.at[:, :N_GROUPS].set(router_c[l]).at[:, N_GROUPS:N_GROUPS + N_EXPERTS].set(router_f[l])
    b_r = jnp.zeros((1, ROUTER_LANES), F32)
    b_r = b_r.at[0, :N_GROUPS].set(router_c_b[l]).at[0, N_GROUPS:N_GROUPS + N_EXPERTS].set(router_f_b[l])
    return dict(
        grp=grp,
        norm1_g=norm1_g[l].reshape(1, D_MODEL), norm2_g=norm2_g[l].reshape(1, D_MODEL),
        w_in_p=jnp.pad(w_in[l], ((0, 0), (0, pad_in))).astype(BF16),
        gqk=jnp.concatenate([jnp.tile(q_norm_g[l], N_Q_HEADS), jnp.tile(k_norm_g[l], N_KV_HEADS)]).reshape(1, -1),
        mu_p=jnp.pad(mu_shift[l], ((0, 0), (0, D_RWKV_PAD - D_RWKV_IN))),
        wd_cat=wd_cat, w0_cat=w0[l].reshape(1, 2 * D_RWKV),
        wa_cat=wa_cat, a0_cat=a0[l].reshape(1, 2 * D_RWKV),
        wg_p=jnp.pad(g_lora_up[l], ((0, LANES - GATE_LORA), (0, 0))),
        k_k=k_k[l].reshape(1, D_RWKV), k_a=k_a[l].reshape(1, D_RWKV), r_k=r_k[l].reshape(1, D_RWKV),
        ln_g=ln_x_g[l].reshape(1, D_RWKV), ln_b=ln_x_b[l].reshape(1, D_RWKV),
        w_out_b=w_out[l].astype(BF16), w_r=w_r, b_r=b_r,
        wg_b=exp_gate[l].astype(BF16), wu_b=exp_up[l].astype(BF16), wd_b=exp_down[l].astype(BF16),
    )


CTX_TILES = dict(in_proj=256, attn=256, prep=256, chunk=128, out_proj=512, moe=1024)
SMP_TILES = dict(in_proj=512, attn=128, prep=256, chunk=128, out_proj=512, moe=1024)


def kernel(x_prompt, x_sample, cache_k, cache_v, state_rwkv, c, c_ctx, w_mod, b_mod, norm1_g, norm2_g, w_in, mu_shift, q_norm_g, k_norm_g, w0, w_lora_up, a0, a_lora_up, g_lora_up, k_k, k_a, r_k, ln_x_g, ln_x_b, w_out, router_c, router_c_b, router_f, router_f_b, exp_gate, exp_up, exp_down):
    depth = w_mod.shape[0]
    db = x_sample.shape[0]
    y_prompt, y_sample = x_prompt, x_sample
    ks, vs, ss = [], [], []
    cond = jnp.zeros((8, D_MODEL), F32).at[:db].set(c).at[db].set(c_ctx)
    for l in range(depth):
        lp = _layer_params(l, w_in, norm1_g, norm2_g, mu_shift, q_norm_g, k_norm_g, w0, w_lora_up, a0, a_lora_up,
                           g_lora_up, k_k, k_a, r_k, ln_x_g, ln_x_b, w_out, router_c, router_c_b, router_f,
                           router_f_b, exp_gate, exp_up, exp_down)
        mod = _modulation(cond, w_mod[l], b_mod[l])
        mod_s = [mod[:db, i * D_MODEL:(i + 1) * D_MODEL].reshape(db, 1, D_MODEL) for i in range(6)]
        mod_c = [mod[db:db + 1, i * D_MODEL:(i + 1) * D_MODEL].reshape(1, 1, D_MODEL) for i in range(6)]
        y_prompt, k_l, v_l, s_l = _layer(y_prompt, mod_c, lp, CTX_TILES, None)
        ks.append(k_l)
        vs.append(v_l)
        ss.append(s_l)
        y_sample, _, _, _ = _layer(y_sample, mod_s, lp, SMP_TILES,
                                   (cache_k[:, l], cache_v[:, l], state_rwkv[:, l]))
    return (y_prompt, y_sample, jnp.stack(ks, axis=1), jnp.stack(vs, axis=1), jnp.stack(ss, axis=1))
```

```python
import functools

import numpy as np
import jax
import jax.numpy as jnp
from jax import lax
from jax.experimental import pallas as pl
from jax.experimental.pallas import tpu as pltpu

F32 = jnp.float32
BF16 = jnp.bfloat16
HIGHEST = lax.Precision.HIGHEST

D_MODEL = 1024
HEAD_DIM = 64
N_Q_HEADS = 8
N_KV_HEADS = 2
GQA_GROUP = N_Q_HEADS // N_KV_HEADS
D_ATTN = N_Q_HEADS * HEAD_DIM
D_KV = N_KV_HEADS * HEAD_DIM
N_RWKV_HEADS = 8
D_RWKV = 512
DECAY_LORA = 32
AAA_LORA = 32
GATE_LORA = 96
D_RWKV_IN = 3 * D_RWKV + 2 * DECAY_LORA + 2 * AAA_LORA + GATE_LORA
D_RWKV_PAD = 1792
D_QKV = D_ATTN + 2 * D_KV
D_IN_PAD = D_QKV + D_RWKV_PAD
N_GROUPS = 4
EXPERTS_PER_GROUP = 4
N_EXPERTS = 16
D_EXPERT = 512
GRID_W = 64
ROPE_THETA = 10000.0
NORM_EPS = 1e-6
GN_EPS = 64e-5
DECAY_SCALE = 0.6065306597
QK_EXP2_SCALE = (HEAD_DIM ** -0.5) * float(np.log2(np.e))
LANES = 128
ROUTER_LANES = 128
VMEM_LIMIT = 56 * 1024 * 1024


def _cparams(sem):
    return pltpu.CompilerParams(dimension_semantics=sem, vmem_limit_bytes=VMEM_LIMIT)


def _dot(a, b, precision=None):
    return jnp.dot(a, b, preferred_element_type=F32, precision=precision)


def _dot_nt(a, b, precision=None):
    return lax.dot_general(a, b, (((1,), (1,)), ((), ())), preferred_element_type=F32, precision=precision)


def _dot_tn(a, b, precision=None):
    return lax.dot_general(a, b, (((0,), (0,)), ((), ())), preferred_element_type=F32, precision=precision)


def _split2(x):
    hi = x.astype(BF16)
    return hi, (x - hi.astype(F32)).astype(BF16)


def _dot3(a, b):
    a_hi, a_lo = _split2(a)
    b_hi, b_lo = _split2(b)
    return _dot(a_hi, b_hi) + (_dot(a_hi, b_lo) + _dot(a_lo, b_hi))


def _head_sum(x, g):
    hi, lo = _split2(x)
    n = x.shape[-1] // LANES
    cols = [slice(j * LANES, (j + 1) * LANES) for j in range(n)]
    return jnp.concatenate([_dot(hi[:, c], g) + _dot(lo[:, c], g) for c in cols], axis=-1)


def _mod_kernel(c_ref, w_ref, b_ref, o_ref):
    c = c_ref[...]
    s = c * jax.nn.sigmoid(c)
    o_ref[...] = _dot(s, w_ref[...], HIGHEST) + b_ref[...]


def _modulation(cond, w_mod, b_mod):
    n = w_mod.shape[1]
    tn = 1024
    return pl.pallas_call(
        _mod_kernel,
        grid=(n // tn,),
        in_specs=[pl.BlockSpec((8, D_MODEL), lambda j: (0, 0)),
                  pl.BlockSpec((D_MODEL, tn), lambda j: (0, j)),
                  pl.BlockSpec((1, tn), lambda j: (0, j))],
        out_specs=pl.BlockSpec((8, tn), lambda j: (0, j)),
        out_shape=jax.ShapeDtypeStruct((8, n), F32),
        compiler_params=_cparams(("arbitrary",)),
        name="mod",
    )(cond, w_mod, b_mod.reshape(1, n))


def _rope_tables(t_len):
    half = HEAD_DIM // 2
    inv = ROPE_THETA ** (-np.arange(0, half, 2, dtype=np.float64) / half)
    t = np.arange(t_len)
    row, col = t // GRID_W, t % GRID_W
    lane = np.arange(LANES)
    i = lane % HEAD_DIM
    pos = np.where((i // half)[None, :] == 0, row[:, None], col[:, None]).astype(np.float64)
    j = i % half
    ang = pos * inv[j % (half // 2)][None, :]
    cos, sin = np.cos(ang), np.sin(ang)
    first = (j < half // 2)[None, :]
    s_up = np.where(first, -sin, 0.0)
    s_dn = np.where(first, 0.0, sin)
    return (jnp.asarray(cos, F32), jnp.asarray(s_up, F32), jnp.asarray(s_dn, F32))


def _inproj_kernel(x_ref, sh_ref, sc_ref, g_ref, w_ref, gqk_ref, grp_ref, *rest, rope):
    if rope:
        cos_ref, sup_ref, sdn_ref, q_ref, k_ref, v_ref, kf_ref, vf_ref, rw_ref = rest
    else:
        q_ref, k_ref, v_ref, kf_ref, vf_ref, rw_ref = rest
    x = x_ref[0]
    ms = jnp.mean(x * x, axis=-1, keepdims=True)
    h = x * lax.rsqrt(ms + NORM_EPS) * g_ref[...]
    h = h * (1.0 + sc_ref[0]) + sh_ref[0]
    proj = _dot(h.astype(BF16), w_ref[...])
    grp = grp_ref[...]
    lo_half = lax.broadcasted_iota(jnp.int32, (x.shape[0], LANES), 1) < HEAD_DIM
    for j in range((D_ATTN + D_KV) // LANES):
        blk = proj[:, j * LANES:(j + 1) * LANES]
        ss = _head_sum(blk * blk, grp) * (1.0 / HEAD_DIM)
        nb = blk * lax.rsqrt(ss + NORM_EPS) * gqk_ref[:, j * LANES:(j + 1) * LANES]
        if rope:
            nb = (nb * cos_ref[...] + pltpu.roll(nb, LANES - 16, 1) * sup_ref[...]
                  + pltpu.roll(nb, 16, 1) * sdn_ref[...])
        if j < D_ATTN // LANES:
            nbq = nb * QK_EXP2_SCALE
            for half in range(2):
                hq = 2 * j + half
                q_ref[0, hq // GQA_GROUP, hq % GQA_GROUP] = nbq[:, half * HEAD_DIM:(half + 1) * HEAD_DIM].astype(BF16)
        else:
            kf_ref[0] = nb
            k_ref[0, 0] = nb[:, :HEAD_DIM].astype(BF16)
            k_ref[0, 1] = nb[:, HEAD_DIM:].astype(BF16)
    vblk = proj[:, D_ATTN + D_KV:D_QKV]
    vf_ref[0] = vblk
    v_ref[0, 0] = jnp.where(lo_half, vblk, 1.0).astype(BF16)
    v_ref[0, 1] = jnp.where(lo_half, pltpu.roll(vblk, HEAD_DIM, 1), 1.0).astype(BF16)
    rw_ref[0] = proj[:, D_QKV:]


def _in_proj(x3, shift, scale, norm_g, w_in_p, gqk, grp, rope, tm):
    b, t, _ = x3.shape
    bm = shift.shape[0]
    mod_map = (lambda i, j: (i, 0, 0)) if bm > 1 else (lambda i, j: (0, 0, 0))
    full = lambda a: pl.BlockSpec(a.shape, lambda i, j: (0,) * a.ndim)
    tok = lambda w: pl.BlockSpec((1, tm, w), lambda i, j: (i, j, 0))
    in_specs = [tok(D_MODEL), pl.BlockSpec((1, 1, D_MODEL), mod_map), pl.BlockSpec((1, 1, D_MODEL), mod_map),
                full(norm_g), full(w_in_p), full(gqk), full(grp)]
    args = [x3, shift, scale, norm_g, w_in_p, gqk, grp]
    if rope:
        in_specs += [pl.BlockSpec((tm, LANES), lambda i, j: (j, 0))] * 3
        args += list(_rope_tables(t))
    out_shape = (jax.ShapeDtypeStruct((b, N_KV_HEADS, GQA_GROUP, t, HEAD_DIM), BF16),
                 jax.ShapeDtypeStruct((b, N_KV_HEADS, t, HEAD_DIM), BF16),
                 jax.ShapeDtypeStruct((b, N_KV_HEADS, t, 2 * HEAD_DIM), BF16),
                 jax.ShapeDtypeStruct((b, t, D_KV), F32), jax.ShapeDtypeStruct((b, t, D_KV), F32),
                 jax.ShapeDtypeStruct((b, t, D_RWKV_PAD), F32))
    out_specs = (pl.BlockSpec((1, N_KV_HEADS, GQA_GROUP, tm, HEAD_DIM), lambda i, j: (i, 0, 0, j, 0)),
                 pl.BlockSpec((1, N_KV_HEADS, tm, HEAD_DIM), lambda i, j: (i, 0, j, 0)),
                 pl.BlockSpec((1, N_KV_HEADS, tm, 2 * HEAD_DIM), lambda i, j: (i, 0, j, 0)),
                 tok(D_KV), tok(D_KV), tok(D_RWKV_PAD))
    return pl.pallas_call(
        functools.partial(_inproj_kernel, rope=rope),
        grid=(b, t // tm), in_specs=in_specs, out_specs=out_specs, out_shape=out_shape,
        compiler_params=_cparams(("parallel", "parallel")),
        name="in_proj_rope" if rope else "in_proj",
    )(*args)


def _attn_kernel(q_ref, k_ref, v_ref, o_ref):
    g, tq, hd = q_ref.shape[2:]
    k = k_ref[0, 0]
    v = v_ref[0, 0]
    ss = [_dot_nt(q_ref[0, 0, i], k) for i in range(g)]
    ps = [jnp.exp2(s - jnp.max(s, axis=-1, keepdims=True)).astype(BF16) for s in ss]
    outs = []
    for i in range(g):
        o = _dot(ps[i], v)
        outs.append(o[:, :hd] / pltpu.roll(o, hd, 1)[:, :hd])
    o_ref[0] = jnp.concatenate(outs, axis=-1)


def _attention(q5, k4, v4, tq):
    b, hk, g, t, hd = q5.shape
    tk = k4.shape[2]
    return pl.pallas_call(
        _attn_kernel,
        grid=(b, hk, t // tq),
        in_specs=[pl.BlockSpec((1, 1, g, tq, hd), lambda i, j, l: (i, j, 0, l, 0)),
                  pl.BlockSpec((1, 1, tk, hd), lambda i, j, l: (i, j, 0, 0)),
                  pl.BlockSpec((1, 1, tk, 2 * hd), lambda i, j, l: (i, j, 0, 0))],
        out_specs=pl.BlockSpec((1, tq, g * hd), lambda i, j, l: (i, l, j)),
        out_shape=jax.ShapeDtypeStruct((b, t, hk * g * hd), F32),
        compiler_params=_cparams(("parallel", "parallel", "arbitrary")),
        name="attn",
    )(q5, k4, v4)


def _prep_kernel(rw_ref, hp_ref, hn_ref, mu_ref, wd_ref, w0_ref, wa_ref, a0_ref, wg_ref, kk_ref, ka_ref, rk_ref,
                 grp_ref, r_o, lw_o, kd_o, v_o, kk_o, bd_o, g_o, bonus_o):
    i = pl.program_id(1)
    n = pl.num_programs(1)
    cur = rw_ref[0]
    tt = cur.shape[0]
    rid = lax.broadcasted_iota(jnp.int32, cur.shape, 0)
    prev_row = jnp.where(i > 0, hp_ref[0, 7:8, :], 0.0)
    next_row = jnp.where(i < n - 1, hn_ref[0, 0:1, :], 0.0)
    prev = jnp.where(rid == 0, prev_row, pltpu.roll(cur, 1, 0))
    nxt = jnp.where(rid == tt - 1, next_row, pltpu.roll(cur, tt - 1, 0))
    p = cur + mu_ref[0:1, :] * (prev - cur) + mu_ref[1:2, :] * (nxt - cur)
    r = p[:, 0:D_RWKV]
    k = p[:, D_RWKV:2 * D_RWKV]
    v = p[:, 2 * D_RWKV:3 * D_RWKV]
    lo = p[:, 3 * D_RWKV:3 * D_RWKV + LANES]
    gd = p[:, 3 * D_RWKV + LANES:]
    grp = grp_ref[...]
    wlog = _dot3(jnp.tanh(lo), wd_ref[...]) + w0_ref[...]
    alog = _dot3(lo, wa_ref[...]) + a0_ref[...]
    g_o[0] = _dot3(jax.nn.sigmoid(gd), wg_ref[...])
    kx = k * kk_ref[...]
    kk = kx * lax.rsqrt(_head_sum(kx * kx, grp) + 1e-12)
    r_o[0] = r
    v_o[0] = v
    kk_o[0] = kk
    bonus_o[0] = _head_sum(r * k * rk_ref[...], grp) * v
    for z in range(2):
        a = jax.nn.sigmoid(alog[:, z * D_RWKV:(z + 1) * D_RWKV])
        lw_o[z, 0] = -DECAY_SCALE * jax.nn.sigmoid(wlog[:, z * D_RWKV:(z + 1) * D_RWKV])
        kd_o[z, 0] = k * (1.0 + (a - 1.0) * ka_ref[...])
        bd_o[z, 0] = kk * a


def _rwkv_prep(rw3, mu_p, wd_cat, w0_cat, wa_cat, a0_cat, wg_p, k_k, k_a, r_k, grp, tt):
    b, t, _ = rw3.shape
    nt = t // tt
    hb = tt // 8
    one = jax.ShapeDtypeStruct((b, t, D_RWKV), F32)
    two = jax.ShapeDtypeStruct((2, b, t, D_RWKV), F32)
    s_one = pl.BlockSpec((1, tt, D_RWKV), lambda i, j: (i, j, 0))
    s_two = pl.BlockSpec((2, 1, tt, D_RWKV), lambda i, j: (0, i, j, 0))
    full = lambda a: pl.BlockSpec(a.shape, lambda i, j: (0,) * a.ndim)
    consts = [mu_p, wd_cat, w0_cat, wa_cat, a0_cat, wg_p, k_k, k_a, r_k, grp]
    return pl.pallas_call(
        _prep_kernel,
        grid=(b, nt),
        in_specs=[pl.BlockSpec((1, tt, D_RWKV_PAD), lambda i, j: (i, j, 0)),
                  pl.BlockSpec((1, 8, D_RWKV_PAD), lambda i, j: (i, jnp.maximum(j * hb - 1, 0), 0)),
                  pl.BlockSpec((1, 8, D_RWKV_PAD), lambda i, j: (i, jnp.minimum((j + 1) * hb, t // 8 - 1), 0))]
                 + [full(a) for a in consts],
        out_specs=(s_one, s_two, s_two, s_one, s_one, s_two, s_one, s_one),
        out_shape=(one, two, two, one, one, two, one, one),
        compiler_params=_cparams(("parallel", "parallel")),
        name="rwkv_prep",
    )(rw3, rw3, rw3, *consts)


INV_BASE = 16


def _bdot(a, b):
    return _dot(a.astype(BF16), b.astype(BF16))


def _scan_kernel(r_ref, lw_ref, k_ref, v_ref, a_ref, b_ref, s0_ref, y_ref, sT_ref, s_scr, *, chunk):
    d = pl.program_id(1)
    i = pl.program_id(2)
    c = chunk

    @pl.when(i == 0)
    def _():
        s_scr[...] = s0_ref[0, 0]

    row = lax.broadcasted_iota(jnp.int32, (c, c), 0)
    col = lax.broadcasted_iota(jnp.int32, (c, c), 1)
    ahead = (row - col) * (1 - 2 * d)
    incl = ahead >= 0
    strict = ahead > 0
    eye = (row == col).astype(F32)
    same_blk = {}
    n = INV_BASE
    while n <= c:
        sh = jnp.int32(n.bit_length() - 1)
        same_blk[n] = lax.shift_right_logical(row, sh) == lax.shift_right_logical(col, sh)
        n *= 2

    lw = lw_ref[0, 0]
    lw_hi = lw.astype(BF16)
    lw_mid, lw_lo = _split2(lw - lw_hi.astype(F32))
    inclb = jnp.where(incl, 1.0, 0.0).astype(BF16)
    cl = _dot(inclb, lw_hi) + (_dot(inclb, lw_mid) + _dot(inclb, lw_lo))
    tot = jnp.sum(lw, axis=0, keepdims=True)
    mid = 0.5 * tot
    e_abs = jnp.exp(cl)
    e_abx = jnp.exp(cl - lw)
    e_inv = jnp.exp(mid - cl)
    e_end = jnp.exp(tot - cl)
    p_tot = jnp.exp(tot)
    s_mid = jnp.exp(-mid)

    r_abs = r_ref[0] * e_abs
    a_abs = a_ref[0] * e_abx
    ar_abs = jnp.concatenate([a_abs, r_abs], axis=0).astype(BF16)
    ar_mid = jnp.concatenate([a_abs * s_mid, r_abs * s_mid], axis=0).astype(BF16)
    kb_inv = jnp.concatenate([k_ref[0, 0] * e_inv, b_ref[0, 0] * e_inv], axis=0).astype(BF16)
    kb_end = jnp.concatenate([k_ref[0, 0] * e_end, b_ref[0, 0] * e_end], axis=0).astype(BF16)
    v = v_ref[0]

    heads = range(N_RWKV_HEADS)
    sls = [slice(h * HEAD_DIM, (h + 1) * HEAD_DIM) for h in heads]
    s_old = [s_scr[h] for h in heads]
    vhs = [v[:, sl].astype(BF16) for sl in sls]
    grams = [_dot_nt(ar_mid[:, sl], kb_inv[:, sl]) for sl in sls]
    from_s = [_dot_nt(ar_abs[:, sl], s_old[h].astype(BF16)) for h, sl in enumerate(sls)]
    masked = [jnp.concatenate([jnp.where(strict, g[:c, :c], 0.0), jnp.where(incl, g[c:, :c], 0.0)], axis=0)
              for g in grams]
    from_v = [_bdot(masked[h], vhs[h]) for h in heads]
    lmats = [jnp.where(strict, g[:c, c:], 0.0) for g in grams]
    l0s = [jnp.where(same_blk[INV_BASE], lm, 0.0) for lm in lmats]
    xs = [eye - l0 for l0 in l0s]
    pws = [_bdot(l0, l0) for l0 in l0s]
    span = 2
    while 2 * span < INV_BASE:
        both = [_bdot(jnp.concatenate([xs[h], pws[h]], axis=0), pws[h]) for h in heads]
        xs = [xs[h] + both[h][:c] for h in heads]
        pws = [both[h][c:] for h in heads]
        span *= 2
    xs = [xs[h] + _bdot(xs[h], pws[h]) for h in heads]
    n = INV_BASE
    while n < c:
        pair = same_blk[2 * n] & jnp.logical_not(same_blk[n])
        ts = [_bdot(jnp.where(pair, lmats[h], 0.0), xs[h]) for h in heads]
        xs = [xs[h] - _bdot(xs[h], ts[h]) for h in heads]
        n *= 2
    us = [_bdot(xs[h], from_s[h][:c] + from_v[h][:c]) for h in heads]
    yu = [_bdot(jnp.where(incl, grams[h][c:, c:], 0.0), us[h]) for h in heads]
    ds = [_dot_tn(jnp.concatenate([vhs[h], (-us[h]).astype(BF16)], axis=0), kb_end[:, sls[h]]) for h in heads]
    y_ref[0, 0] = jnp.concatenate([from_s[h][c:] + from_v[h][c:] - yu[h] for h in heads], axis=-1)
    for h in heads:
        s_scr[h] = s_old[h] * p_tot[:, sls[h]] + ds[h]

    @pl.when(i == pl.num_programs(2) - 1)
    def _():
        sT_ref[0, 0] = s_scr[...]


def _rwkv_scan(r, lw2, kd2, v, kk, bd2, s0, chunk):
    b, t, _ = r.shape
    nc = t // chunk
    tmap = lambda d, j: j + d * (nc - 1 - 2 * j)
    s_one = pl.BlockSpec((1, chunk, D_RWKV), lambda i, d, j: (i, tmap(d, j), 0))
    s_two = pl.BlockSpec((1, 1, chunk, D_RWKV), lambda i, d, j: (d, i, tmap(d, j), 0))
    s_st = pl.BlockSpec((1, 1, N_RWKV_HEADS, HEAD_DIM, HEAD_DIM), lambda i, d, j: (i, d, 0, 0, 0))
    return pl.pallas_call(
        functools.partial(_scan_kernel, chunk=chunk),
        grid=(b, 2, nc),
        in_specs=[s_one, s_two, s_two, s_one, s_one, s_two, s_st],
        out_specs=(s_two, s_st),
        out_shape=(jax.ShapeDtypeStruct((2, b, t, D_RWKV), F32), jax.ShapeDtypeStruct(s0.shape, F32)),
        scratch_shapes=[pltpu.VMEM((N_RWKV_HEADS, HEAD_DIM, HEAD_DIM), F32)],
        compiler_params=_cparams(("parallel", "parallel", "arbitrary")),
        name="rwkv_scan",
    )(r, lw2, kd2, v, kk, bd2, s0)


def _outproj_kernel(x_ref, at_ref, y2_ref, bonus_ref, gate_ref, lng_ref, lnb_ref, grp_ref, wo_ref,
                    g1_ref, sh_ref, sc_ref, n2_ref, wr_ref, br_ref, x1_ref, h2_ref, cmb_ref):
    grp = grp_ref[...]
    y = y2_ref[0, 0] + y2_ref[1, 0]
    mean = _head_sum(y, grp) * (1.0 / HEAD_DIM)
    yc = y - mean
    var = _head_sum(yc * yc, grp) * (1.0 / HEAD_DIM)
    yn = yc * lax.rsqrt(var + GN_EPS) * lng_ref[...] + lnb_ref[...]
    rw_out = (yn + bonus_ref[0]) * gate_ref[0]
    mix = (_dot(at_ref[0].astype(BF16), wo_ref[0:D_ATTN, :])
           + _dot(rw_out.astype(BF16), wo_ref[D_ATTN:, :]))
    x1 = x_ref[0] + g1_ref[0] * mix
    x1_ref[0] = x1
    ms = jnp.mean(x1 * x1, axis=-1, keepdims=True)
    h2 = x1 * lax.rsqrt(ms + NORM_EPS) * n2_ref[...]
    h2 = h2 * (1.0 + sc_ref[0]) + sh_ref[0]
    h2_ref[0] = h2.astype(BF16)
    logits = _dot3(h2, wr_ref[...]) + br_ref[...]
    lane = lax.broadcasted_iota(jnp.int32, logits.shape, 1)
    neg = -jnp.inf
    big = jnp.int32(1 << 20)
    lc = jnp.where(lane < N_GROUPS, logits, neg)
    mc = jnp.max(lc, axis=-1, keepdims=True)
    g_w = 1.0 / jnp.sum(jnp.exp(lc - mc), axis=-1, keepdims=True)
    g_idx = jnp.min(jnp.where(lc == mc, lane, big), axis=-1, keepdims=True)
    eid = lane - N_GROUPS
    in_grp = (eid >= 0) & (eid < N_EXPERTS) & (lax.shift_right_arithmetic(eid, 2) == g_idx)
    lf = jnp.where(in_grp, logits, neg)
    m1 = jnp.max(lf, axis=-1, keepdims=True)
    i1 = jnp.min(jnp.where(lf == m1, lane, big), axis=-1, keepdims=True)
    lf2 = jnp.where(lane == i1, neg, lf)
    m2 = jnp.max(lf2, axis=-1, keepdims=True)
    i2 = jnp.min(jnp.where(lf2 == m2, lane, big), axis=-1, keepdims=True)
    e2 = jnp.exp(m2 - m1)
    w1 = 1.0 / (1.0 + e2)
    w2 = e2 * w1
    cmb_ref[0] = g_w * (jnp.where(lane == i1, w1, 0.0) + jnp.where(lane == i2, w2, 0.0))


def _out_proj(x3, attn3, y2, bonus, gate, ln_g, ln_b, grp, w_out_b, g1, sh2, sc2, norm2_g, w_r, b_r, tm):
    b, t, _ = x3.shape
    bm = g1.shape[0]
    mod_map = (lambda i, j: (i, 0, 0)) if bm > 1 else (lambda i, j: (0, 0, 0))
    tok = lambda w: pl.BlockSpec((1, tm, w), lambda i, j: (i, j, 0))
    full = lambda a: pl.BlockSpec(a.shape, lambda i, j: (0,) * a.ndim)
    mod = pl.BlockSpec((1, 1, D_MODEL), mod_map)
    return pl.pallas_call(
        _outproj_kernel,
        grid=(b, t // tm),
        in_specs=[tok(D_MODEL), tok(D_ATTN),
                  pl.BlockSpec((2, 1, tm, D_RWKV), lambda i, j: (0, i, j, 0)),
                  tok(D_RWKV), tok(D_RWKV), full(ln_g), full(ln_b), full(grp), full(w_out_b),
                  mod, mod, mod, full(norm2_g), full(w_r), full(b_r)],
        out_specs=(tok(D_MODEL), tok(D_MODEL), tok(ROUTER_LANES)),
        out_shape=(jax.ShapeDtypeStruct((b, t, D_MODEL), F32), jax.ShapeDtypeStruct((b, t, D_MODEL), BF16),
                   jax.ShapeDtypeStruct((b, t, ROUTER_LANES), F32)),
        compiler_params=_cparams(("parallel", "parallel")),
        name="out_proj",
    )(x3, attn3, y2, bonus, gate, ln_g, ln_b, grp, w_out_b, g1, sh2, sc2, norm2_g, w_r, b_r)


def _moe_kernel(h_ref, cmb_ref, x1_ref, g2_ref, wg_ref, wu_ref, wd_ref, o_ref, acc_ref):
    e = pl.program_id(2)

    @pl.when(e == 0)
    def _():
        acc_ref[...] = jnp.zeros_like(acc_ref)

    cmb = cmb_ref[0]
    lane = lax.broadcasted_iota(jnp.int32, cmb.shape, 1)
    c_e = jnp.sum(jnp.where(lane == e + N_GROUPS, cmb, 0.0), axis=-1, keepdims=True)
    h = h_ref[0]
    a = _dot(h, wg_ref[0])
    hid = a * jax.nn.sigmoid(a) * _dot(h, wu_ref[0])
    acc_ref[...] += _dot((hid * c_e).astype(BF16), wd_ref[0])

    @pl.when(e == pl.num_programs(2) - 1)
    def _():
        o_ref[0] = x1_ref[0] + g2_ref[0] * acc_ref[...]


def _moe(h2, cmb, x1, g2, wg_b, wu_b, wd_b, tm):
    b, t, _ = h2.shape
    bm = g2.shape[0]
    mod_map = (lambda i, j, e: (i, 0, 0)) if bm > 1 else (lambda i, j, e: (0, 0, 0))
    tok = lambda w: pl.BlockSpec((1, tm, w), lambda i, j, e: (i, j, 0))
    return pl.pallas_call(
        _moe_kernel,
        grid=(b, t // tm, N_EXPERTS),
        in_specs=[tok(D_MODEL), tok(ROUTER_LANES), tok(D_MODEL), pl.BlockSpec((1, 1, D_MODEL), mod_map),
                  pl.BlockSpec((1, D_MODEL, D_EXPERT), lambda i, j, e: (e, 0, 0)),
                  pl.BlockSpec((1, D_MODEL, D_EXPERT), lambda i, j, e: (e, 0, 0)),
                  pl.BlockSpec((1, D_EXPERT, D_MODEL), lambda i, j, e: (e, 0, 0))],
        out_specs=tok(D_MODEL),
        out_shape=jax.ShapeDtypeStruct((b, t, D_MODEL), F32),
        scratch_shapes=[pltpu.VMEM((tm, D_MODEL), F32)],
        compiler_params=_cparams(("parallel", "parallel", "arbitrary")),
        name="moe",
    )(h2, cmb, x1, g2, wg_b, wu_b, wd_b)


def _layer(x3, mod6, lp, t_tiles, ctx):
    b, t, _ = x3.shape
    sh1, sc1, g1, sh2, sc2, g2 = mod6
    rope = ctx is not None
    n = b * t
    q5, k4, v4, k, v, rw = _in_proj(x3, sh1, sc1, lp["norm1_g"], lp["w_in_p"], lp["gqk"], lp["grp"],
                                    rope, t_tiles["in_proj"])
    if ctx is not None:
        ctx_k, ctx_v, ctx_state = ctx
        ck = ctx_k.transpose(0, 2, 1, 3)
        cv = ctx_v.transpose(0, 2, 1, 3)
        k4 = jnp.concatenate([k4, ck.astype(BF16)], axis=2)
        v4 = jnp.concatenate([v4, jnp.concatenate([cv, jnp.ones_like(cv)], axis=-1).astype(BF16)], axis=2)
        s0 = ctx_state
    else:
        s0 = jnp.zeros((b, 2, N_RWKV_HEADS, HEAD_DIM, HEAD_DIM), F32)
    attn3 = _attention(q5, k4, v4, t_tiles["attn"])
    r, lw2, kd2, vv, kk, bd2, gate, bonus = _rwkv_prep(
        rw, lp["mu_p"], lp["wd_cat"], lp["w0_cat"], lp["wa_cat"], lp["a0_cat"],
        lp["wg_p"], lp["k_k"], lp["k_a"], lp["r_k"], lp["grp"], t_tiles["prep"])
    y2, s_t = _rwkv_scan(r, lw2, kd2, vv, kk, bd2, s0, t_tiles["chunk"])
    fb, ft = (1, n) if g1.shape[0] == 1 else (b, t)
    flat = lambda a: a.reshape(a.shape[:-3] + (fb, ft, a.shape[-1]))
    x1, h2, cmb = _out_proj(flat(x3), flat(attn3), flat(y2), flat(bonus), flat(gate), lp["ln_g"], lp["ln_b"],
                            lp["grp"], lp["w_out_b"], g1, sh2, sc2, lp["norm2_g"], lp["w_r"], lp["b_r"],
                            t_tiles["out_proj"])
    out = _moe(h2, cmb, x1, g2, lp["wg_b"], lp["wu_b"], lp["wd_b"], t_tiles["moe"]).reshape(b, t, D_MODEL)
    return out, k.reshape(b, t, N_KV_HEADS, HEAD_DIM), v.reshape(b, t, N_KV_HEADS, HEAD_DIM), s_t


def _block_diag2(w):
    z, l, c = w.shape
    out = jnp.zeros((LANES, z * c), F32)
    for i in range(z):
        out = out.at[i * l:(i + 1) * l, i * c:(i + 1) * c].set(w[i])
    return out


def _layer_params(l, w_in, norm1_g, norm2_g, mu_shift, q_norm_g, k_norm_g, w0, w_lora_up, a0, a_lora_up, g_lora_up,
                  k_k, k_a, r_k, ln_x_g, ln_x_b, w_out, router_c, router_c_b, router_f, router_f_b,
                  exp_gate, exp_up, exp_down):
    lane = np.arange(LANES)
    grp = jnp.asarray((lane[:, None] // HEAD_DIM) == (lane[None, :] // HEAD_DIM), BF16)
    pad_in = D_IN_PAD - w_in.shape[2]
    wd_cat = _block_diag2(w_lora_up[l])
    wa_cat = jnp.roll(_block_diag2(a_lora_up[l]), 2 * DECAY_LORA, axis=0)
    w_r = jnp.zeros((D_MODEL, ROUTER_LANES), F32)
    w_r = w_r.at[:, :N_GROUPS].set(router_c[l]).at[:, N_GROUPS:N_GROUPS + N_EXPERTS].set(router_f[l])
    b_r = jnp.zeros((1, ROUTER_LANES), F32)
    b_r = b_r.at[0, :N_GROUPS].set(router_c_b[l]).at[0, N_GROUPS:N_GROUPS + N_EXPERTS].set(router_f_b[l])
    return dict(
        grp=grp,
        norm1_g=norm1_g[l].reshape(1, D_MODEL), norm2_g=norm2_g[l].reshape(1, D_MODEL),
        w_in_p=jnp.pad(w_in[l], ((0, 0), (0, pad_in))).astype(BF16),
        gqk=jnp.concatenate([jnp.tile(q_norm_g[l], N_Q_HEADS), jnp.tile(k_norm_g[l], N_KV_HEADS)]).reshape(1, -1),
        mu_p=jnp.pad(mu_shift[l], ((0, 0), (0, D_RWKV_PAD - D_RWKV_IN))),
        wd_cat=wd_cat, w0_cat=w0[l].reshape(1, 2 * D_RWKV),
        wa_cat=wa_cat, a0_cat=a0[l].reshape(1, 2 * D_RWKV),
        wg_p=jnp.pad(g_lora_up[l], ((0, LANES - GATE_LORA), (0, 0))),
        k_k=k_k[l].reshape(1, D_RWKV), k_a=k_a[l].reshape(1, D_RWKV), r_k=r_k[l].reshape(1, D_RWKV),
        ln_g=ln_x_g[l].reshape(1, D_RWKV), ln_b=ln_x_b[l].reshape(1, D_RWKV),
        w_out_b=w_out[l].astype(BF16), w_r=w_r, b_r=b_r,
        wg_b=exp_gate[l].astype(BF16), wu_b=exp_up[l].astype(BF16), wd_b=exp_down[l].astype(BF16),
    )


CTX_TILES = dict(in_proj=256, attn=256, prep=256, chunk=128, out_proj=512, moe=1024)
SMP_TILES = dict(in_proj=512, attn=128, prep=256, chunk=128, out_proj=512, moe=1024)


def kernel(x_prompt, x_sample, cache_k, cache_v, state_rwkv, c, c_ctx, w_mod, b_mod, norm1_g, norm2_g, w_in, mu_shift, q_norm_g, k_norm_g, w0, w_lora_up, a0, a_lora_up, g_lora_up, k_k, k_a, r_k, ln_x_g, ln_x_b, w_out, router_c, router_c_b, router_f, router_f_b, exp_gate, exp_up, exp_down):
    depth = w_mod.shape[0]
    db = x_sample.shape[0]
    y_prompt, y_sample = x_prompt, x_sample
    ks, vs, ss = [], [], []
    cond = jnp.zeros((8, D_MODEL), F32).at[:db].set(c).at[db].set(c_ctx)
    for l in range(depth):
        lp = _layer_params(l, w_in, norm1_g, norm2_g, mu_shift, q_norm_g, k_norm_g, w0, w_lora_up, a0, a_lora_up,
                           g_lora_up, k_k, k_a, r_k, ln_x_g, ln_x_b, w_out, router_c, router_c_b, router_f,
                           router_f_b, exp_gate, exp_up, exp_down)
        mod = _modulation(cond, w_mod[l], b_mod[l])
        mod_s = [mod[:db, i * D_MODEL:(i + 1) * D_MODEL].reshape(db, 1, D_MODEL) for i in range(6)]
        mod_c = [mod[db:db + 1, i * D_MODEL:(i + 1) * D_MODEL].reshape(1, 1, D_MODEL) for i in range(6)]
        y_prompt, k_l, v_l, s_l = _layer(y_prompt, mod_c, lp, CTX_TILES, None)
        ks.append(k_l)
        vs.append(v_l)
        ss.append(s_l)
        y_sample, _, _, _ = _layer(y_sample, mod_s, lp, SMP_TILES,
                                   (cache_k[:, l], cache_v[:, l], state_rwkv[:, l]))
    return (y_prompt, y_sample, jnp.stack(ks, axis=1), jnp.stack(vs, axis=1), jnp.stack(ss, axis=1))
```

```python
import functools

import numpy as np
import jax
import jax.numpy as jnp
from jax import lax
from jax.experimental import pallas as pl
from jax.experimental.pallas import tpu as pltpu

F32 = jnp.float32
BF16 = jnp.bfloat16
HIGHEST = lax.Precision.HIGHEST

D_MODEL = 1024
HEAD_DIM = 64
N_Q_HEADS = 8
N_KV_HEADS = 2
GQA_GROUP = N_Q_HEADS // N_KV_HEADS
D_ATTN = N_Q_HEADS * HEAD_DIM
D_KV = N_KV_HEADS * HEAD_DIM
N_RWKV_HEADS = 8
D_RWKV = 512
DECAY_LORA = 32
AAA_LORA = 32
GATE_LORA = 96
D_RWKV_IN = 3 * D_RWKV + 2 * DECAY_LORA + 2 * AAA_LORA + GATE_LORA
D_RWKV_PAD = 1792
D_QKV = D_ATTN + 2 * D_KV
D_IN_PAD = D_QKV + D_RWKV_PAD
N_GROUPS = 4
EXPERTS_PER_GROUP = 4
N_EXPERTS = 16
D_EXPERT = 512
GRID_W = 64
ROPE_THETA = 10000.0
NORM_EPS = 1e-6
GN_EPS = 64e-5
DECAY_SCALE = 0.6065306597
QK_EXP2_SCALE = (HEAD_DIM ** -0.5) * float(np.log2(np.e))
LANES = 128
ROUTER_LANES = 128
VMEM_LIMIT = 56 * 1024 * 1024
MOE_W = D_MODEL + 2 * ROUTER_LANES
MOE_TM = 256
MOE_TX = 512
SEG_ALIGN = 16


def _cparams(sem):
    return pltpu.CompilerParams(dimension_semantics=sem, vmem_limit_bytes=VMEM_LIMIT)


def _dot(a, b, precision=None):
    return jnp.dot(a, b, preferred_element_type=F32, precision=precision)


def _dot_nt(a, b, precision=None):
    return lax.dot_general(a, b, (((1,), (1,)), ((), ())), preferred_element_type=F32, precision=precision)


def _dot_tn(a, b, precision=None):
    return lax.dot_general(a, b, (((0,), (0,)), ((), ())), preferred_element_type=F32, precision=precision)


def _split2(x):
    hi = x.astype(BF16)
    return hi, (x - hi.astype(F32)).astype(BF16)


def _dot3(a, b):
    a_hi, a_lo = _split2(a)
    b_hi, b_lo = _split2(b)
    return _dot(a_hi, b_hi) + (_dot(a_hi, b_lo) + _dot(a_lo, b_hi))


def _head_sum(x, g):
    hi, lo = _split2(x)
    n = x.shape[-1] // LANES
    cols = [slice(j * LANES, (j + 1) * LANES) for j in range(n)]
    return jnp.concatenate([_dot(hi[:, c], g) + _dot(lo[:, c], g) for c in cols], axis=-1)


def _mod_kernel(c_ref, w_ref, b_ref, o_ref):
    c = c_ref[...]
    s = c * jax.nn.sigmoid(c)
    o_ref[...] = _dot(s, w_ref[...], HIGHEST) + b_ref[...]


def _modulation(cond, w_mod, b_mod):
    n = w_mod.shape[1]
    tn = 1024
    return pl.pallas_call(
        _mod_kernel,
        grid=(n // tn,),
        in_specs=[pl.BlockSpec((8, D_MODEL), lambda j: (0, 0)),
                  pl.BlockSpec((D_MODEL, tn), lambda j: (0, j)),
                  pl.BlockSpec((1, tn), lambda j: (0, j))],
        out_specs=pl.BlockSpec((8, tn), lambda j: (0, j)),
        out_shape=jax.ShapeDtypeStruct((8, n), F32),
        compiler_params=_cparams(("arbitrary",)),
        name="mod",
    )(cond, w_mod, b_mod.reshape(1, n))


def _rope_tables(t_len):
    half = HEAD_DIM // 2
    inv = ROPE_THETA ** (-np.arange(0, half, 2, dtype=np.float64) / half)
    t = np.arange(t_len)
    row, col = t // GRID_W, t % GRID_W
    lane = np.arange(LANES)
    i = lane % HEAD_DIM
    pos = np.where((i // half)[None, :] == 0, row[:, None], col[:, None]).astype(np.float64)
    j = i % half
    ang = pos * inv[j % (half // 2)][None, :]
    cos, sin = np.cos(ang), np.sin(ang)
    first = (j < half // 2)[None, :]
    s_up = np.where(first, -sin, 0.0)
    s_dn = np.where(first, 0.0, sin)
    return (jnp.asarray(cos, F32), jnp.asarray(s_up, F32), jnp.asarray(s_dn, F32))


def _inproj_kernel(x_ref, sh_ref, sc_ref, g_ref, w_ref, gqk_ref, grp_ref, *rest, rope):
    if rope:
        cos_ref, sup_ref, sdn_ref, q_ref, k_ref, v_ref, kf_ref, vf_ref, rw_ref = rest
    else:
        q_ref, k_ref, v_ref, kf_ref, vf_ref, rw_ref = rest
    x = x_ref[0]
    ms = jnp.mean(x * x, axis=-1, keepdims=True)
    h = x * lax.rsqrt(ms + NORM_EPS) * g_ref[...]
    h = h * (1.0 + sc_ref[0]) + sh_ref[0]
    proj = _dot(h.astype(BF16), w_ref[...])
    grp = grp_ref[...]
    lo_half = lax.broadcasted_iota(jnp.int32, (x.shape[0], LANES), 1) < HEAD_DIM
    for j in range((D_ATTN + D_KV) // LANES):
        blk = proj[:, j * LANES:(j + 1) * LANES]
        ss = _head_sum(blk * blk, grp) * (1.0 / HEAD_DIM)
        nb = blk * lax.rsqrt(ss + NORM_EPS) * gqk_ref[:, j * LANES:(j + 1) * LANES]
        if rope:
            nb = (nb * cos_ref[...] + pltpu.roll(nb, LANES - 16, 1) * sup_ref[...]
                  + pltpu.roll(nb, 16, 1) * sdn_ref[...])
        if j < D_ATTN // LANES:
            nbq = nb * QK_EXP2_SCALE
            for half in range(2):
                hq = 2 * j + half
                q_ref[0, hq // GQA_GROUP, hq % GQA_GROUP] = nbq[:, half * HEAD_DIM:(half + 1) * HEAD_DIM].astype(BF16)
        else:
            kf_ref[0] = nb
            k_ref[0, 0] = nb[:, :HEAD_DIM].astype(BF16)
            k_ref[0, 1] = nb[:, HEAD_DIM:].astype(BF16)
    vblk = proj[:, D_ATTN + D_KV:D_QKV]
    vf_ref[0] = vblk
    v_ref[0, 0] = jnp.where(lo_half, vblk, 1.0).astype(BF16)
    v_ref[0, 1] = jnp.where(lo_half, pltpu.roll(vblk, HEAD_DIM, 1), 1.0).astype(BF16)
    rw_ref[0] = proj[:, D_QKV:]


def _in_proj(x3, shift, scale, norm_g, w_in_p, gqk, grp, rope, tm):
    b, t, _ = x3.shape
    bm = shift.shape[0]
    mod_map = (lambda i, j: (i, 0, 0)) if bm > 1 else (lambda i, j: (0, 0, 0))
    full = lambda a: pl.BlockSpec(a.shape, lambda i, j: (0,) * a.ndim)
    tok = lambda w: pl.BlockSpec((1, tm, w), lambda i, j: (i, j, 0))
    in_specs = [tok(D_MODEL), pl.BlockSpec((1, 1, D_MODEL), mod_map), pl.BlockSpec((1, 1, D_MODEL), mod_map),
                full(norm_g), full(w_in_p), full(gqk), full(grp)]
    args = [x3, shift, scale, norm_g, w_in_p, gqk, grp]
    if rope:
        in_specs += [pl.BlockSpec((tm, LANES), lambda i, j: (j, 0))] * 3
        args += list(_rope_tables(t))
    out_shape = (jax.ShapeDtypeStruct((b, N_KV_HEADS, GQA_GROUP, t, HEAD_DIM), BF16),
                 jax.ShapeDtypeStruct((b, N_KV_HEADS, t, HEAD_DIM), BF16),
                 jax.ShapeDtypeStruct((b, N_KV_HEADS, t, 2 * HEAD_DIM), BF16),
                 jax.ShapeDtypeStruct((b, t, D_KV), F32), jax.ShapeDtypeStruct((b, t, D_KV), F32),
                 jax.ShapeDtypeStruct((b, t, D_RWKV_PAD), F32))
    out_specs = (pl.BlockSpec((1, N_KV_HEADS, GQA_GROUP, tm, HEAD_DIM), lambda i, j: (i, 0, 0, j, 0)),
                 pl.BlockSpec((1, N_KV_HEADS, tm, HEAD_DIM), lambda i, j: (i, 0, j, 0)),
                 pl.BlockSpec((1, N_KV_HEADS, tm, 2 * HEAD_DIM), lambda i, j: (i, 0, j, 0)),
                 tok(D_KV), tok(D_KV), tok(D_RWKV_PAD))
    return pl.pallas_call(
        functools.partial(_inproj_kernel, rope=rope),
        grid=(b, t // tm), in_specs=in_specs, out_specs=out_specs, out_shape=out_shape,
        compiler_params=_cparams(("parallel", "parallel")),
        name="in_proj_rope" if rope else "in_proj",
    )(*args)


def _attn_kernel(q_ref, k_ref, v_ref, o_ref):
    g, tq, hd = q_ref.shape[2:]
    k = k_ref[0, 0]
    v = v_ref[0, 0]
    ss = [_dot_nt(q_ref[0, 0, i], k) for i in range(g)]
    ps = [jnp.exp2(s - jnp.max(s, axis=-1, keepdims=True)).astype(BF16) for s in ss]
    outs = []
    for i in range(g):
        o = _dot(ps[i], v)
        outs.append(o[:, :hd] / pltpu.roll(o, hd, 1)[:, :hd])
    o_ref[0] = jnp.concatenate(outs, axis=-1)


def _attention(q5, k4, v4, tq):
    b, hk, g, t, hd = q5.shape
    tk = k4.shape[2]
    return pl.pallas_call(
        _attn_kernel,
        grid=(b, hk, t // tq),
        in_specs=[pl.BlockSpec((1, 1, g, tq, hd), lambda i, j, l: (i, j, 0, l, 0)),
                  pl.BlockSpec((1, 1, tk, hd), lambda i, j, l: (i, j, 0, 0)),
                  pl.BlockSpec((1, 1, tk, 2 * hd), lambda i, j, l: (i, j, 0, 0))],
        out_specs=pl.BlockSpec((1, tq, g * hd), lambda i, j, l: (i, l, j)),
        out_shape=jax.ShapeDtypeStruct((b, t, hk * g * hd), F32),
        compiler_params=_cparams(("parallel", "parallel", "arbitrary")),
        name="attn",
    )(q5, k4, v4)


def _prep_kernel(rw_ref, hp_ref, hn_ref, mu_ref, wd_ref, w0_ref, wa_ref, a0_ref, wg_ref, kk_ref, ka_ref, rk_ref,
                 grp_ref, r_o, lw_o, kd_o, v_o, kk_o, bd_o, g_o, bonus_o):
    i = pl.program_id(1)
    n = pl.num_programs(1)
    cur = rw_ref[0]
    tt = cur.shape[0]
    rid = lax.broadcasted_iota(jnp.int32, cur.shape, 0)
    prev_row = jnp.where(i > 0, hp_ref[0, 7:8, :], 0.0)
    next_row = jnp.where(i < n - 1, hn_ref[0, 0:1, :], 0.0)
    prev = jnp.where(rid == 0, prev_row, pltpu.roll(cur, 1, 0))
    nxt = jnp.where(rid == tt - 1, next_row, pltpu.roll(cur, tt - 1, 0))
    p = cur + mu_ref[0:1, :] * (prev - cur) + mu_ref[1:2, :] * (nxt - cur)
    r = p[:, 0:D_RWKV]
    k = p[:, D_RWKV:2 * D_RWKV]
    v = p[:, 2 * D_RWKV:3 * D_RWKV]
    lo = p[:, 3 * D_RWKV:3 * D_RWKV + LANES]
    gd = p[:, 3 * D_RWKV + LANES:]
    grp = grp_ref[...]
    wlog = _dot3(jnp.tanh(lo), wd_ref[...]) + w0_ref[...]
    alog = _dot3(lo, wa_ref[...]) + a0_ref[...]
    g_o[0] = _dot3(jax.nn.sigmoid(gd), wg_ref[...])
    kx = k * kk_ref[...]
    kk = kx * lax.rsqrt(_head_sum(kx * kx, grp) + 1e-12)
    r_o[0] = r
    v_o[0] = v
    kk_o[0] = kk
    bonus_o[0] = _head_sum(r * k * rk_ref[...], grp) * v
    for z in range(2):
        a = jax.nn.sigmoid(alog[:, z * D_RWKV:(z + 1) * D_RWKV])
        lw_o[z, 0] = -DECAY_SCALE * jax.nn.sigmoid(wlog[:, z * D_RWKV:(z + 1) * D_RWKV])
        kd_o[z, 0] = k * (1.0 + (a - 1.0) * ka_ref[...])
        bd_o[z, 0] = kk * a


def _rwkv_prep(rw3, mu_p, wd_cat, w0_cat, wa_cat, a0_cat, wg_p, k_k, k_a, r_k, grp, tt):
    b, t, _ = rw3.shape
    nt = t // tt
    hb = tt // 8
    one = jax.ShapeDtypeStruct((b, t, D_RWKV), F32)
    two = jax.ShapeDtypeStruct((2, b, t, D_RWKV), F32)
    s_one = pl.BlockSpec((1, tt, D_RWKV), lambda i, j: (i, j, 0))
    s_two = pl.BlockSpec((2, 1, tt, D_RWKV), lambda i, j: (0, i, j, 0))
    full = lambda a: pl.BlockSpec(a.shape, lambda i, j: (0,) * a.ndim)
    consts = [mu_p, wd_cat, w0_cat, wa_cat, a0_cat, wg_p, k_k, k_a, r_k, grp]
    return pl.pallas_call(
        _prep_kernel,
        grid=(b, nt),
        in_specs=[pl.BlockSpec((1, tt, D_RWKV_PAD), lambda i, j: (i, j, 0)),
                  pl.BlockSpec((1, 8, D_RWKV_PAD), lambda i, j: (i, jnp.maximum(j * hb - 1, 0), 0)),
                  pl.BlockSpec((1, 8, D_RWKV_PAD), lambda i, j: (i, jnp.minimum((j + 1) * hb, t // 8 - 1), 0))]
                 + [full(a) for a in consts],
        out_specs=(s_one, s_two, s_two, s_one, s_one, s_two, s_one, s_one),
        out_shape=(one, two, two, one, one, two, one, one),
        compiler_params=_cparams(("parallel", "parallel")),
        name="rwkv_prep",
    )(rw3, rw3, rw3, *consts)


INV_BASE = 16


def _bdot(a, b):
    return _dot(a.astype(BF16), b.astype(BF16))


def _scan_kernel(r_ref, lw_ref, k_ref, v_ref, a_ref, b_ref, s0_ref, y_ref, sT_ref, s_scr, *, chunk):
    d = pl.program_id(1)
    i = pl.program_id(2)
    c = chunk

    @pl.when(i == 0)
    def _():
        s_scr[...] = s0_ref[0, 0]

    row = lax.broadcasted_iota(jnp.int32, (c, c), 0)
    col = lax.broadcasted_iota(jnp.int32, (c, c), 1)
    ahead = (row - col) * (1 - 2 * d)
    incl = ahead >= 0
    strict = ahead > 0
    eye = (row == col).astype(F32)
    same_blk = {}
    n = INV_BASE
    while n <= c:
        sh = jnp.int32(n.bit_length() - 1)
        same_blk[n] = lax.shift_right_logical(row, sh) == lax.shift_right_logical(col, sh)
        n *= 2

    lw = lw_ref[0, 0]
    lw_hi = lw.astype(BF16)
    lw_mid, lw_lo = _split2(lw - lw_hi.astype(F32))
    inclb = jnp.where(incl, 1.0, 0.0).astype(BF16)
    cl = _dot(inclb, lw_hi) + (_dot(inclb, lw_mid) + _dot(inclb, lw_lo))
    tot = jnp.sum(lw, axis=0, keepdims=True)
    mid = 0.5 * tot
    e_abs = jnp.exp(cl)
    e_abx = jnp.exp(cl - lw)
    e_inv = jnp.exp(mid - cl)
    e_end = jnp.exp(tot - cl)
    p_tot = jnp.exp(tot)
    s_mid = jnp.exp(-mid)

    r_abs = r_ref[0] * e_abs
    a_abs = a_ref[0] * e_abx
    ar_abs = jnp.concatenate([a_abs, r_abs], axis=0).astype(BF16)
    ar_mid = jnp.concatenate([a_abs * s_mid, r_abs * s_mid], axis=0).astype(BF16)
    kb_inv = jnp.concatenate([k_ref[0, 0] * e_inv, b_ref[0, 0] * e_inv], axis=0).astype(BF16)
    kb_end = jnp.concatenate([k_ref[0, 0] * e_end, b_ref[0, 0] * e_end], axis=0).astype(BF16)
    v = v_ref[0]

    heads = range(N_RWKV_HEADS)
    sls = [slice(h * HEAD_DIM, (h + 1) * HEAD_DIM) for h in heads]
    s_old = [s_scr[h] for h in heads]
    vhs = [v[:, sl].astype(BF16) for sl in sls]
    grams = [_dot_nt(ar_mid[:, sl], kb_inv[:, sl]) for sl in sls]
    from_s = [_dot_nt(ar_abs[:, sl], s_old[h].astype(BF16)) for h, sl in enumerate(sls)]
    masked = [jnp.concatenate([jnp.where(strict, g[:c, :c], 0.0), jnp.where(incl, g[c:, :c], 0.0)], axis=0)
              for g in grams]
    from_v = [_bdot(masked[h], vhs[h]) for h in heads]
    lmats = [jnp.where(strict, g[:c, c:], 0.0) for g in grams]
    l0s = [jnp.where(same_blk[INV_BASE], lm, 0.0) for lm in lmats]
    xs = [eye - l0 for l0 in l0s]
    pws = [_bdot(l0, l0) for l0 in l0s]
    span = 2
    while 2 * span < INV_BASE:
        both = [_bdot(jnp.concatenate([xs[h], pws[h]], axis=0), pws[h]) for h in heads]
        xs = [xs[h] + both[h][:c] for h in heads]
        pws = [both[h][c:] for h in heads]
        span *= 2
    xs = [xs[h] + _bdot(xs[h], pws[h]) for h in heads]
    n = INV_BASE
    while n < c:
        pair = same_blk[2 * n] & jnp.logical_not(same_blk[n])
        ts = [_bdot(jnp.where(pair, lmats[h], 0.0), xs[h]) for h in heads]
        xs = [xs[h] - _bdot(xs[h], ts[h]) for h in heads]
        n *= 2
    us = [_bdot(xs[h], from_s[h][:c] + from_v[h][:c]) for h in heads]
    yu = [_bdot(jnp.where(incl, grams[h][c:, c:], 0.0), us[h]) for h in heads]
    ds = [_dot_tn(jnp.concatenate([vhs[h], (-us[h]).astype(BF16)], axis=0), kb_end[:, sls[h]]) for h in heads]
    y_ref[0, 0] = jnp.concatenate([from_s[h][c:] + from_v[h][c:] - yu[h] for h in heads], axis=-1)
    for h in heads:
        s_scr[h] = s_old[h] * p_tot[:, sls[h]] + ds[h]

    @pl.when(i == pl.num_programs(2) - 1)
    def _():
        sT_ref[0, 0] = s_scr[...]


def _rwkv_scan(r, lw2, kd2, v, kk, bd2, s0, chunk):
    b, t, _ = r.shape
    nc = t // chunk
    tmap = lambda d, j: j + d * (nc - 1 - 2 * j)
    s_one = pl.BlockSpec((1, chunk, D_RWKV), lambda i, d, j: (i, tmap(d, j), 0))
    s_two = pl.BlockSpec((1, 1, chunk, D_RWKV), lambda i, d, j: (d, i, tmap(d, j), 0))
    s_st = pl.BlockSpec((1, 1, N_RWKV_HEADS, HEAD_DIM, HEAD_DIM), lambda i, d, j: (i, d, 0, 0, 0))
    return pl.pallas_call(
        functools.partial(_scan_kernel, chunk=chunk),
        grid=(b, 2, nc),
        in_specs=[s_one, s_two, s_two, s_one, s_one, s_two, s_st],
        out_specs=(s_two, s_st),
        out_shape=(jax.ShapeDtypeStruct((2, b, t, D_RWKV), F32), jax.ShapeDtypeStruct(s0.shape, F32)),
        scratch_shapes=[pltpu.VMEM((N_RWKV_HEADS, HEAD_DIM, HEAD_DIM), F32)],
        compiler_params=_cparams(("parallel", "parallel", "arbitrary")),
        name="rwkv_scan",
    )(r, lw2, kd2, v, kk, bd2, s0)


def _outproj_kernel(x_ref, at_ref, y2_ref, bonus_ref, gate_ref, lng_ref, lnb_ref, grp_ref, wo_ref,
                    g1_ref, sh_ref, sc_ref, n2_ref, wr_ref, br_ref, x1_ref, hx_ref, cnt_ref):
    grp = grp_ref[...]
    y = y2_ref[0, 0] + y2_ref[1, 0]
    mean = _head_sum(y, grp) * (1.0 / HEAD_DIM)
    yc = y - mean
    var = _head_sum(yc * yc, grp) * (1.0 / HEAD_DIM)
    yn = yc * lax.rsqrt(var + GN_EPS) * lng_ref[...] + lnb_ref[...]
    rw_out = (yn + bonus_ref[0]) * gate_ref[0]
    mix = (_dot(at_ref[0].astype(BF16), wo_ref[0:D_ATTN, :])
           + _dot(rw_out.astype(BF16), wo_ref[D_ATTN:, :]))
    x1 = x_ref[0] + g1_ref[0] * mix
    x1_ref[0] = x1
    ms = jnp.mean(x1 * x1, axis=-1, keepdims=True)
    h2 = x1 * lax.rsqrt(ms + NORM_EPS) * n2_ref[...]
    h2 = h2 * (1.0 + sc_ref[0]) + sh_ref[0]
    hx_ref[0, :, 0:D_MODEL] = h2.astype(BF16)
    logits = _dot3(h2, wr_ref[...]) + br_ref[...]
    lane = lax.broadcasted_iota(jnp.int32, logits.shape, 1)
    neg = -jnp.inf
    big = jnp.int32(1 << 20)
    lc = jnp.where(lane < N_GROUPS, logits, neg)
    mc = jnp.max(lc, axis=-1, keepdims=True)
    g_w = 1.0 / jnp.sum(jnp.exp(lc - mc), axis=-1, keepdims=True)
    g_idx = jnp.min(jnp.where(lc == mc, lane, big), axis=-1, keepdims=True)
    eid = lane - N_GROUPS
    in_grp = (eid >= 0) & (eid < N_EXPERTS) & (lax.shift_right_arithmetic(eid, 2) == g_idx)
    lf = jnp.where(in_grp, logits, neg)
    m1 = jnp.max(lf, axis=-1, keepdims=True)
    i1 = jnp.min(jnp.where(lf == m1, lane, big), axis=-1, keepdims=True)
    lf2 = jnp.where(lane == i1, neg, lf)
    m2 = jnp.max(lf2, axis=-1, keepdims=True)
    i2 = jnp.min(jnp.where(lf2 == m2, lane, big), axis=-1, keepdims=True)
    e2 = jnp.exp(m2 - m1)
    w1 = 1.0 / (1.0 + e2)
    w2 = e2 * w1
    cmb = g_w * (jnp.where(lane == i1, w1, 0.0) + jnp.where(lane == i2, w2, 0.0))
    rec = jnp.where(lane == 0, g_idx.astype(F32), cmb)
    rec_hi, rec_lo = _split2(rec)
    hx_ref[0, :, D_MODEL:D_MODEL + ROUTER_LANES] = rec_hi
    hx_ref[0, :, D_MODEL + ROUTER_LANES:] = rec_lo
    hot = jnp.where((lane == g_idx) & (lane < N_GROUPS), 1.0, 0.0)
    cnt_ref[0, 0] = jnp.broadcast_to(jnp.sum(hot, axis=0, keepdims=True), cnt_ref.shape[2:])


def _out_proj(x3, attn3, y2, bonus, gate, ln_g, ln_b, grp, w_out_b, g1, sh2, sc2, norm2_g, w_r, b_r):
    tm = MOE_TM
    b, t, _ = x3.shape
    bm = g1.shape[0]
    mod_map = (lambda i, j: (i, 0, 0)) if bm > 1 else (lambda i, j: (0, 0, 0))
    tok = lambda w: pl.BlockSpec((1, tm, w), lambda i, j: (i, j, 0))
    full = lambda a: pl.BlockSpec(a.shape, lambda i, j: (0,) * a.ndim)
    mod = pl.BlockSpec((1, 1, D_MODEL), mod_map)
    return pl.pallas_call(
        _outproj_kernel,
        grid=(b, t // tm),
        in_specs=[tok(D_MODEL), tok(D_ATTN),
                  pl.BlockSpec((2, 1, tm, D_RWKV), lambda i, j: (0, i, j, 0)),
                  tok(D_RWKV), tok(D_RWKV), full(ln_g), full(ln_b), full(grp), full(w_out_b),
                  mod, mod, mod, full(norm2_g), full(w_r), full(b_r)],
        out_specs=(tok(D_MODEL), tok(MOE_W), pl.BlockSpec((1, 1, 8, LANES), lambda i, j: (i, j, 0, 0))),
        out_shape=(jax.ShapeDtypeStruct((b, t, D_MODEL), F32), jax.ShapeDtypeStruct((b, t, MOE_W), BF16),
                   jax.ShapeDtypeStruct((b, t // tm, 8, LANES), F32)),
        compiler_params=_cparams(("parallel", "parallel")),
        name="out_proj",
    )(x3, attn3, y2, bonus, gate, ln_g, ln_b, grp, w_out_b, g1, sh2, sc2, norm2_g, w_r, b_r)


def _group_one_hots(rec_hi):
    tm = rec_hi.shape[0]
    sel = ((lax.broadcasted_iota(jnp.int32, (8, LANES), 0) == 0)
           & (lax.broadcasted_iota(jnp.int32, (8, LANES), 1) == 0))
    g_row = _dot_nt(jnp.where(sel, 1.0, 0.0).astype(BF16), rec_hi)[0:1, :]
    g_col = rec_hi.astype(F32)[:, 0:1]
    r = lax.broadcasted_iota(jnp.int32, (tm, tm), 0)
    c = lax.broadcasted_iota(jnp.int32, (tm, tm), 1)
    earlier_same = jnp.where((g_col == g_row) & (r < c), 1.0, 0.0)
    rank_row = jnp.sum(earlier_same, axis=0, keepdims=True)
    rf = r.astype(F32)
    return [jnp.where((g_row == float(g)) & (rank_row == rf), 1.0, 0.0).astype(BF16) for g in range(N_GROUPS)]


def _dispatch_kernel(dst_ref, hx_ref, xs_in_ref, xs_ref, buf, sem, *, tile0):
    del xs_in_ref
    i = pl.program_id(0)
    x = hx_ref[...]
    hots = _group_one_hots(x[:, D_MODEL:D_MODEL + ROUTER_LANES])
    blocks = [_dot(hot, x).astype(BF16) for hot in hots]

    def copy(g, t):
        start = pl.multiple_of(dst_ref[(tile0 + t) * N_GROUPS + g], SEG_ALIGN)
        return pltpu.make_async_copy(buf.at[g], xs_ref.at[pl.ds(start, MOE_TM), :], sem.at[g])

    @pl.when(i > 0)
    def _():
        for g in range(N_GROUPS):
            copy(g, i - 1).wait()

    for g in range(N_GROUPS):
        buf[g] = blocks[g]
        copy(g, i).start()

    @pl.when(i == pl.num_programs(0) - 1)
    def _():
        for g in range(N_GROUPS):
            copy(g, i).wait()


def _moe_dispatch(dst, hx2, xs, tile0):
    n = hx2.shape[0]
    grid_spec = pltpu.PrefetchScalarGridSpec(
        num_scalar_prefetch=1, grid=(n // MOE_TM,),
        in_specs=[pl.BlockSpec((MOE_TM, MOE_W), lambda i, dst: (i, 0)), pl.BlockSpec(memory_space=pl.ANY)],
        out_specs=pl.BlockSpec(memory_space=pl.ANY),
        scratch_shapes=[pltpu.VMEM((N_GROUPS, MOE_TM, MOE_W), BF16), pltpu.SemaphoreType.DMA((N_GROUPS,))])
    return pl.pallas_call(
        functools.partial(_dispatch_kernel, tile0=tile0),
        grid_spec=grid_spec,
        out_shape=jax.ShapeDtypeStruct(xs.shape, xs.dtype),
        input_output_aliases={2: 0},
        compiler_params=_cparams(("arbitrary",)),
        name="moe_dispatch",
    )(dst, hx2, xs)


def _experts_kernel(tg_ref, xs_ref, wg_ref, wu_ref, wd_ref, ys_ref):
    g = tg_ref[pl.program_id(0)]

    @pl.when(g >= N_GROUPS)
    def _():
        ys_ref[...] = jnp.zeros_like(ys_ref)

    @pl.when(g < N_GROUPS)
    def _():
        x = xs_ref[...]
        h = x[:, 0:D_MODEL]
        rec = (x[:, D_MODEL:D_MODEL + ROUTER_LANES].astype(F32) + x[:, D_MODEL + ROUTER_LANES:].astype(F32))
        lane = lax.broadcasted_iota(jnp.int32, rec.shape, 1)
        a = _dot(h, wg_ref[0])
        hid = a * jax.nn.sigmoid(a) * _dot(h, wu_ref[0])
        first = N_GROUPS + EXPERTS_PER_GROUP * g
        scaled = []
        for e in range(EXPERTS_PER_GROUP):
            c_e = jnp.sum(jnp.where(lane == first + e, rec, 0.0), axis=-1, keepdims=True)
            scaled.append((hid[:, e * D_EXPERT:(e + 1) * D_EXPERT] * c_e).astype(BF16))
        ys_ref[...] = _dot(jnp.concatenate(scaled, axis=-1), wd_ref[0]).astype(BF16)


def _moe_experts(tile_group, xs, wg4, wu4, wd4):
    p = xs.shape[0]
    grid_spec = pltpu.PrefetchScalarGridSpec(
        num_scalar_prefetch=1, grid=(p // MOE_TX,),
        in_specs=[pl.BlockSpec((MOE_TX, MOE_W), lambda j, tg: (j, 0)),
                  pl.BlockSpec((1,) + wg4.shape[1:], lambda j, tg: (jnp.minimum(tg[j], N_GROUPS - 1), 0, 0)),
                  pl.BlockSpec((1,) + wu4.shape[1:], lambda j, tg: (jnp.minimum(tg[j], N_GROUPS - 1), 0, 0)),
                  pl.BlockSpec((1,) + wd4.shape[1:], lambda j, tg: (jnp.minimum(tg[j], N_GROUPS - 1), 0, 0))],
        out_specs=pl.BlockSpec((MOE_TX, D_MODEL), lambda j, tg: (j, 0)))
    return pl.pallas_call(
        _experts_kernel,
        grid_spec=grid_spec,
        out_shape=jax.ShapeDtypeStruct((p, D_MODEL), BF16),
        compiler_params=_cparams(("arbitrary",)),
        name="moe_experts",
    )(tile_group, xs, wg4, wu4, wd4)


def _combine_kernel(dst_ref, rec_ref, x1_ref, g2_ref, ys_ref, o_ref, buf, sem, *, tile0):
    lin = pl.program_id(0) * pl.num_programs(1) + pl.program_id(1)
    n = pl.num_programs(0) * pl.num_programs(1)
    slot = lin % 2

    def copy(g, t, s):
        start = pl.multiple_of(dst_ref[(tile0 + t) * N_GROUPS + g], SEG_ALIGN)
        return pltpu.make_async_copy(ys_ref.at[pl.ds(start, MOE_TM), :], buf.at[s, g], sem.at[s, g])

    @pl.when(lin == 0)
    def _():
        for g in range(N_GROUPS):
            copy(g, 0, 0).start()

    @pl.when(lin + 1 < n)
    def _():
        for g in range(N_GROUPS):
            copy(g, lin + 1, 1 - slot).start()

    hots = _group_one_hots(rec_ref[0])
    for g in range(N_GROUPS):
        copy(g, lin, slot).wait()
    y = _dot_tn(hots[0], buf[slot, 0])
    for g in range(1, N_GROUPS):
        y = y + _dot_tn(hots[g], buf[slot, g])
    o_ref[0] = x1_ref[0] + g2_ref[0] * y


def _moe_combine(dst, hx3, x1, g2, ys, tile0):
    b, t, _ = x1.shape
    bm = g2.shape[0]
    mod_map = (lambda i, j, dst: (i, 0, 0)) if bm > 1 else (lambda i, j, dst: (0, 0, 0))
    rec_blk = D_MODEL // ROUTER_LANES
    grid_spec = pltpu.PrefetchScalarGridSpec(
        num_scalar_prefetch=1, grid=(b, t // MOE_TM),
        in_specs=[pl.BlockSpec((1, MOE_TM, ROUTER_LANES), lambda i, j, dst: (i, j, rec_blk)),
                  pl.BlockSpec((1, MOE_TM, D_MODEL), lambda i, j, dst: (i, j, 0)),
                  pl.BlockSpec((1, 1, D_MODEL), mod_map),
                  pl.BlockSpec(memory_space=pl.ANY)],
        out_specs=pl.BlockSpec((1, MOE_TM, D_MODEL), lambda i, j, dst: (i, j, 0)),
        scratch_shapes=[pltpu.VMEM((2, N_GROUPS, MOE_TM, D_MODEL), BF16),
                        pltpu.SemaphoreType.DMA((2, N_GROUPS))])
    return pl.pallas_call(
        functools.partial(_combine_kernel, tile0=tile0),
        grid_spec=grid_spec,
        out_shape=jax.ShapeDtypeStruct(x1.shape, F32),
        compiler_params=_cparams(("arbitrary", "arbitrary")),
        name="moe_combine",
    )(dst, hx3, x1, g2, ys)


def _moe_plan(cnt, n_rows):
    seg = (cnt + (SEG_ALIGN - 1)) // SEG_ALIGN * SEG_ALIGN
    used = (jnp.sum(seg, axis=0) + (MOE_TX - 1)) // MOE_TX * MOE_TX
    size = used + MOE_TX
    base = jnp.cumsum(size) - size
    dst = base[None, :] + jnp.cumsum(seg, axis=0) - seg
    starts = jnp.arange(n_rows // MOE_TX, dtype=jnp.int32)[:, None] * MOE_TX
    inside = (starts >= base[None, :]) & (starts < (base + used)[None, :])
    tile_group = jnp.where(jnp.any(inside, axis=1), jnp.argmax(inside, axis=1), N_GROUPS)
    return dst.reshape(-1).astype(jnp.int32), tile_group.astype(jnp.int32)


def _moe_rows(n_tokens):
    n_tiles = n_tokens // MOE_TM
    bound = n_tokens + n_tiles * N_GROUPS * (SEG_ALIGN - 1) + N_GROUPS * 2 * MOE_TX
    return (bound + MOE_TX - 1) // MOE_TX * MOE_TX


def _layer(x3, mod6, lp, t_tiles, ctx):
    b, t, _ = x3.shape
    sh1, sc1, g1, sh2, sc2, _ = mod6
    t_tiles = {name: min(size, t) for name, size in t_tiles.items()}
    rope = ctx is not None
    n = b * t
    q5, k4, v4, k, v, rw = _in_proj(x3, sh1, sc1, lp["norm1_g"], lp["w_in_p"], lp["gqk"], lp["grp"],
                                    rope, t_tiles["in_proj"])
    if ctx is not None:
        ctx_k, ctx_v, ctx_state = ctx
        ck = ctx_k.transpose(0, 2, 1, 3)
        cv = ctx_v.transpose(0, 2, 1, 3)
        k4 = jnp.concatenate([k4, ck.astype(BF16)], axis=2)
        v4 = jnp.concatenate([v4, jnp.concatenate([cv, jnp.ones_like(cv)], axis=-1).astype(BF16)], axis=2)
        s0 = ctx_state
    else:
        s0 = jnp.zeros((b, 2, N_RWKV_HEADS, HEAD_DIM, HEAD_DIM), F32)
    attn3 = _attention(q5, k4, v4, t_tiles["attn"])
    r, lw2, kd2, vv, kk, bd2, gate, bonus = _rwkv_prep(
        rw, lp["mu_p"], lp["wd_cat"], lp["w0_cat"], lp["wa_cat"], lp["a0_cat"],
        lp["wg_p"], lp["k_k"], lp["k_a"], lp["r_k"], lp["grp"], t_tiles["prep"])
    y2, s_t = _rwkv_scan(r, lw2, kd2, vv, kk, bd2, s0, t_tiles["chunk"])
    fb, ft = (1, n) if g1.shape[0] == 1 else (b, t)
    flat = lambda a: a.reshape(a.shape[:-3] + (fb, ft, a.shape[-1]))
    x1, hx, cnt = _out_proj(flat(x3), flat(attn3), flat(y2), flat(bonus), flat(gate), lp["ln_g"], lp["ln_b"],
                            lp["grp"], lp["w_out_b"], g1, sh2, sc2, lp["norm2_g"], lp["w_r"], lp["b_r"])
    return (x1, hx, cnt), k.reshape(b, t, N_KV_HEADS, HEAD_DIM), v.reshape(b, t, N_KV_HEADS, HEAD_DIM), s_t


def _moe_both(passes, g2s, lp):
    counts = [p[2][:, :, 0, :N_GROUPS].reshape(-1, N_GROUPS) for p in passes]
    tiles = [c.shape[0] for c in counts]
    n_rows = _moe_rows(sum(tiles) * MOE_TM)
    dst, tile_group = _moe_plan(jnp.concatenate(counts, axis=0).astype(jnp.int32), n_rows)
    xs = jnp.zeros((n_rows, MOE_W), BF16)
    tile0 = 0
    for (x1, hx, _), nt in zip(passes, tiles):
        xs = _moe_dispatch(dst, hx.reshape(-1, MOE_W), xs, tile0)
        tile0 += nt
    ys = _moe_experts(tile_group, xs, lp["wg4"], lp["wu4"], lp["wd4"])
    outs, tile0 = [], 0
    for (x1, hx, _), g2, nt in zip(passes, g2s, tiles):
        outs.append(_moe_combine(dst, hx, x1, g2, ys, tile0))
        tile0 += nt
    return outs


def _block_diag2(w):
    z, l, c = w.shape
    out = jnp.zeros((LANES, z * c), F32)
    for i in range(z):
        out = out.at[i * l:(i + 1) * l, i * c:(i + 1) * c].set(w[i])
    return out


def _layer_params(l, w_in, norm1_g, norm2_g, mu_shift, q_norm_g, k_norm_g, w0, w_lora_up, a0, a_lora_up, g_lora_up,
                  k_k, k_a, r_k, ln_x_g, ln_x_b, w_out, router_c, router_c_b, router_f, router_f_b,
                  exp_gate, exp_up, exp_down):
    lane = np.arange(LANES)
    grp = jnp.asarray((lane[:, None] // HEAD_DIM) == (lane[None, :] // HEAD_DIM), BF16)
    pad_in = D_IN_PAD - w_in.shape[2]
    wd_cat = _block_diag2(w_lora_up[l])
    wa_cat = jnp.roll(_block_diag2(a_lora_up[l]), 2 * DECAY_LORA, axis=0)
    w_r = jnp.zeros((D_MODEL, ROUTER_LANES), F32)
    w_r = w_r.at[:, :N_GROUPS].set(router_c[l]).at[:, N_GROUPS:N_GROUPS + N_EXPERTS].set(router_f[l])
    b_r = jnp.zeros((1, ROUTER_LANES), F32)
    b_r = b_r.at[0, :N_GROUPS].set(router_c_b[l]).at[0, N_GROUPS:N_GROUPS + N_EXPERTS].set(router_f_b[l])

    def side_by_side(w):
        w = w.reshape(N_GROUPS, EXPERTS_PER_GROUP, D_MODEL, D_EXPERT).transpose(0, 2, 1, 3)
        return w.reshape(N_GROUPS, D_MODEL, EXPERTS_PER_GROUP * D_EXPERT).astype(BF16)

    return dict(
        grp=grp,
        norm1_g=norm1_g[l].reshape(1, D_MODEL), norm2_g=norm2_g[l].reshape(1, D_MODEL),
        w_in_p=jnp.pad(w_in[l], ((0, 0), (0, pad_in))).astype(BF16),
        gqk=jnp.concatenate([jnp.tile(q_norm_g[l], N_Q_HEADS), jnp.tile(k_norm_g[l], N_KV_HEADS)]).reshape(1, -1),
        mu_p=jnp.pad(mu_shift[l], ((0, 0), (0, D_RWKV_PAD - D_RWKV_IN))),
        wd_cat=wd_cat, w0_cat=w0[l].reshape(1, 2 * D_RWKV),
        wa_cat=wa_cat, a0_cat=a0[l].reshape(1, 2 * D_RWKV),
        wg_p=jnp.pad(g_lora_up[l], ((0, LANES - GATE_LORA), (0, 0))),
        k_k=k_k[l].reshape(1, D_RWKV), k_a=k_a[l].reshape(1, D_RWKV), r_k=r_k[l].reshape(1, D_RWKV),
        ln_g=ln_x_g[l].reshape(1, D_RWKV), ln_b=ln_x_b[l].reshape(1, D_RWKV),
        w_out_b=w_out[l].astype(BF16), w_r=w_r, b_r=b_r,
        wg4=side_by_side(exp_gate[l]), wu4=side_by_side(exp_up[l]),
        wd4=exp_down[l].reshape(N_GROUPS, EXPERTS_PER_GROUP * D_EXPERT, D_MODEL).astype(BF16),
    )


CTX_TILES = dict(in_proj=256, attn=256, prep=256, chunk=128)
SMP_TILES = dict(in_proj=512, attn=128, prep=256, chunk=128)


def kernel(x_prompt, x_sample, cache_k, cache_v, state_rwkv, c, c_ctx, w_mod, b_mod, norm1_g, norm2_g, w_in, mu_shift, q_norm_g, k_norm_g, w0, w_lora_up, a0, a_lora_up, g_lora_up, k_k, k_a, r_k, ln_x_g, ln_x_b, w_out, router_c, router_c_b, router_f, router_f_b, exp_gate, exp_up, exp_down):
    depth = w_mod.shape[0]
    db = x_sample.shape[0]
    y_prompt, y_sample = x_prompt, x_sample
    ks, vs, ss = [], [], []
    cond = jnp.zeros((8, D_MODEL), F32).at[:db].set(c).at[db].set(c_ctx)
    for l in range(depth):
        lp = _layer_params(l, w_in, norm1_g, norm2_g, mu_shift, q_norm_g, k_norm_g, w0, w_lora_up, a0, a_lora_up,
                           g_lora_up, k_k, k_a, r_k, ln_x_g, ln_x_b, w_out, router_c, router_c_b, router_f,
                           router_f_b, exp_gate, exp_up, exp_down)
        mod = _modulation(cond, w_mod[l], b_mod[l])
        mod_s = [mod[:db, i * D_MODEL:(i + 1) * D_MODEL].reshape(db, 1, D_MODEL) for i in range(6)]
        mod_c = [mod[db:db + 1, i * D_MODEL:(i + 1) * D_MODEL].reshape(1, 1, D_MODEL) for i in range(6)]
        pre_c, k_l, v_l, s_l = _layer(y_prompt, mod_c, lp, CTX_TILES, None)
        ks.append(k_l)
        vs.append(v_l)
        ss.append(s_l)
        pre_s, _, _, _ = _layer(y_sample, mod_s, lp, SMP_TILES, (cache_k[:, l], cache_v[:, l], state_rwkv[:, l]))
        out_c, out_s = _moe_both([pre_c, pre_s], [mod_c[5], mod_s[5]], lp)
        y_prompt, y_sample = out_c.reshape(y_prompt.shape), out_s.reshape(y_sample.shape)
    return (y_prompt, y_sample, jnp.stack(ks, axis=1), jnp.stack(vs, axis=1), jnp.stack(ss, axis=1))
```

```python
import functools

import numpy as np
import jax
import jax.numpy as jnp
from jax import lax
from jax.experimental import pallas as pl
from jax.experimental.pallas import tpu as pltpu

F32 = jnp.float32
BF16 = jnp.bfloat16
HIGHEST = lax.Precision.HIGHEST

D_MODEL = 1024
HEAD_DIM = 64
N_Q_HEADS = 8
N_KV_HEADS = 2
GQA_GROUP = N_Q_HEADS // N_KV_HEADS
D_ATTN = N_Q_HEADS * HEAD_DIM
D_KV = N_KV_HEADS * HEAD_DIM
N_RWKV_HEADS = 8
D_RWKV = 512
DECAY_LORA = 32
AAA_LORA = 32
GATE_LORA = 96
D_RWKV_IN = 3 * D_RWKV + 2 * DECAY_LORA + 2 * AAA_LORA + GATE_LORA
D_RWKV_PAD = 1792
D_QKV = D_ATTN + 2 * D_KV
D_IN_PAD = D_QKV + D_RWKV_PAD
N_GROUPS = 4
EXPERTS_PER_GROUP = 4
N_EXPERTS = 16
D_EXPERT = 512
GRID_W = 64
ROPE_THETA = 10000.0
NORM_EPS = 1e-6
GN_EPS = 64e-5
DECAY_SCALE = 0.6065306597
QK_EXP2_SCALE = (HEAD_DIM ** -0.5) * float(np.log2(np.e))
LANES = 128
ROUTER_LANES = 128
VMEM_LIMIT = 56 * 1024 * 1024
MOE_W = D_MODEL + 2 * ROUTER_LANES
MOE_TM = 256
MOE_TX = 512
SEG_ALIGN = 16


def _cparams(sem):
    return pltpu.CompilerParams(dimension_semantics=sem, vmem_limit_bytes=VMEM_LIMIT)


def _dot(a, b, precision=None):
    return jnp.dot(a, b, preferred_element_type=F32, precision=precision)


def _dot_nt(a, b, precision=None):
    return lax.dot_general(a, b, (((1,), (1,)), ((), ())), preferred_element_type=F32, precision=precision)


def _dot_tn(a, b, precision=None):
    return lax.dot_general(a, b, (((0,), (0,)), ((), ())), preferred_element_type=F32, precision=precision)


def _split2(x):
    hi = x.astype(BF16)
    return hi, (x - hi.astype(F32)).astype(BF16)


def _dot3(a, b):
    a_hi, a_lo = _split2(a)
    b_hi, b_lo = _split2(b)
    return _dot(a_hi, b_hi) + (_dot(a_hi, b_lo) + _dot(a_lo, b_hi))


def _head_sum(x, g):
    hi, lo = _split2(x)
    n = x.shape[-1] // LANES
    cols = [slice(j * LANES, (j + 1) * LANES) for j in range(n)]
    return jnp.concatenate([_dot(hi[:, c], g) + _dot(lo[:, c], g) for c in cols], axis=-1)


def _mod_kernel(c_ref, w_ref, b_ref, o_ref):
    c = c_ref[...]
    s = c * jax.nn.sigmoid(c)
    o_ref[...] = _dot(s, w_ref[...], HIGHEST) + b_ref[...]


def _modulation(cond, w_mod, b_mod):
    n = w_mod.shape[1]
    tn = 1024
    return pl.pallas_call(
        _mod_kernel,
        grid=(n // tn,),
        in_specs=[pl.BlockSpec((8, D_MODEL), lambda j: (0, 0)),
                  pl.BlockSpec((D_MODEL, tn), lambda j: (0, j)),
                  pl.BlockSpec((1, tn), lambda j: (0, j))],
        out_specs=pl.BlockSpec((8, tn), lambda j: (0, j)),
        out_shape=jax.ShapeDtypeStruct((8, n), F32),
        compiler_params=_cparams(("arbitrary",)),
        name="mod",
    )(cond, w_mod, b_mod.reshape(1, n))


def _rope_tables(t_len):
    half = HEAD_DIM // 2
    inv = ROPE_THETA ** (-np.arange(0, half, 2, dtype=np.float64) / half)
    t = np.arange(t_len)
    row, col = t // GRID_W, t % GRID_W
    lane = np.arange(LANES)
    i = lane % HEAD_DIM
    pos = np.where((i // half)[None, :] == 0, row[:, None], col[:, None]).astype(np.float64)
    j = i % half
    ang = pos * inv[j % (half // 2)][None, :]
    cos, sin = np.cos(ang), np.sin(ang)
    first = (j < half // 2)[None, :]
    s_up = np.where(first, -sin, 0.0)
    s_dn = np.where(first, 0.0, sin)
    return (jnp.asarray(cos, F32), jnp.asarray(s_up, F32), jnp.asarray(s_dn, F32))


def _inproj_kernel(x_ref, sh_ref, sc_ref, g_ref, w_ref, gqk_ref, grp_ref, *rest, rope):
    if rope:
        cos_ref, sup_ref, sdn_ref, q_ref, k_ref, v_ref, kf_ref, vf_ref, rw_ref = rest
    else:
        q_ref, k_ref, v_ref, kf_ref, vf_ref, rw_ref = rest
    x = x_ref[0]
    ms = jnp.mean(x * x, axis=-1, keepdims=True)
    h = x * lax.rsqrt(ms + NORM_EPS) * g_ref[...]
    h = h * (1.0 + sc_ref[0]) + sh_ref[0]
    proj = _dot(h.astype(BF16), w_ref[...])
    grp = grp_ref[...]
    lo_half = lax.broadcasted_iota(jnp.int32, (x.shape[0], LANES), 1) < HEAD_DIM
    for j in range((D_ATTN + D_KV) // LANES):
        blk = proj[:, j * LANES:(j + 1) * LANES]
        ss = _head_sum(blk * blk, grp) * (1.0 / HEAD_DIM)
        nb = blk * lax.rsqrt(ss + NORM_EPS) * gqk_ref[:, j * LANES:(j + 1) * LANES]
        if rope:
            nb = (nb * cos_ref[...] + pltpu.roll(nb, LANES - 16, 1) * sup_ref[...]
                  + pltpu.roll(nb, 16, 1) * sdn_ref[...])
        if j < D_ATTN // LANES:
            nbq = nb * QK_EXP2_SCALE
            for half in range(2):
                hq = 2 * j + half
                q_ref[0, hq // GQA_GROUP, hq % GQA_GROUP] = nbq[:, half * HEAD_DIM:(half + 1) * HEAD_DIM].astype(BF16)
        else:
            kf_ref[0] = nb
            k_ref[0, 0] = nb[:, :HEAD_DIM].astype(BF16)
            k_ref[0, 1] = nb[:, HEAD_DIM:].astype(BF16)
    vblk = proj[:, D_ATTN + D_KV:D_QKV]
    vf_ref[0] = vblk
    v_ref[0, 0] = jnp.where(lo_half, vblk, 1.0).astype(BF16)
    v_ref[0, 1] = jnp.where(lo_half, pltpu.roll(vblk, HEAD_DIM, 1), 1.0).astype(BF16)
    rw_ref[0] = proj[:, D_QKV:]


def _in_proj(x3, shift, scale, norm_g, w_in_p, gqk, grp, rope, tm):
    b, t, _ = x3.shape
    bm = shift.shape[0]
    mod_map = (lambda i, j: (i, 0, 0)) if bm > 1 else (lambda i, j: (0, 0, 0))
    full = lambda a: pl.BlockSpec(a.shape, lambda i, j: (0,) * a.ndim)
    tok = lambda w: pl.BlockSpec((1, tm, w), lambda i, j: (i, j, 0))
    in_specs = [tok(D_MODEL), pl.BlockSpec((1, 1, D_MODEL), mod_map), pl.BlockSpec((1, 1, D_MODEL), mod_map),
                full(norm_g), full(w_in_p), full(gqk), full(grp)]
    args = [x3, shift, scale, norm_g, w_in_p, gqk, grp]
    if rope:
        in_specs += [pl.BlockSpec((tm, LANES), lambda i, j: (j, 0))] * 3
        args += list(_rope_tables(t))
    out_shape = (jax.ShapeDtypeStruct((b, N_KV_HEADS, GQA_GROUP, t, HEAD_DIM), BF16),
                 jax.ShapeDtypeStruct((b, N_KV_HEADS, t, HEAD_DIM), BF16),
                 jax.ShapeDtypeStruct((b, N_KV_HEADS, t, 2 * HEAD_DIM), BF16),
                 jax.ShapeDtypeStruct((b, t, D_KV), F32), jax.ShapeDtypeStruct((b, t, D_KV), F32),
                 jax.ShapeDtypeStruct((b, t, D_RWKV_PAD), F32))
    out_specs = (pl.BlockSpec((1, N_KV_HEADS, GQA_GROUP, tm, HEAD_DIM), lambda i, j: (i, 0, 0, j, 0)),
                 pl.BlockSpec((1, N_KV_HEADS, tm, HEAD_DIM), lambda i, j: (i, 0, j, 0)),
                 pl.BlockSpec((1, N_KV_HEADS, tm, 2 * HEAD_DIM), lambda i, j: (i, 0, j, 0)),
                 tok(D_KV), tok(D_KV), tok(D_RWKV_PAD))
    return pl.pallas_call(
        functools.partial(_inproj_kernel, rope=rope),
        grid=(b, t // tm), in_specs=in_specs, out_specs=out_specs, out_shape=out_shape,
        compiler_params=_cparams(("parallel", "parallel")),
        name="in_proj_rope" if rope else "in_proj",
    )(*args)


def _attn_kernel(q_ref, k_ref, v_ref, o_ref):
    g, tq, hd = q_ref.shape[2:]
    k = k_ref[0, 0]
    v = v_ref[0, 0]
    ss = [_dot_nt(q_ref[0, 0, i], k) for i in range(g)]
    ps = [jnp.exp2(s - jnp.max(s, axis=-1, keepdims=True)).astype(BF16) for s in ss]
    outs = []
    for i in range(g):
        o = _dot(ps[i], v)
        outs.append(o[:, :hd] / pltpu.roll(o, hd, 1)[:, :hd])
    o_ref[0] = jnp.concatenate(outs, axis=-1)


def _attention(q5, k4, v4, tq):
    b, hk, g, t, hd = q5.shape
    tk = k4.shape[2]
    return pl.pallas_call(
        _attn_kernel,
        grid=(b, hk, t // tq),
        in_specs=[pl.BlockSpec((1, 1, g, tq, hd), lambda i, j, l: (i, j, 0, l, 0)),
                  pl.BlockSpec((1, 1, tk, hd), lambda i, j, l: (i, j, 0, 0)),
                  pl.BlockSpec((1, 1, tk, 2 * hd), lambda i, j, l: (i, j, 0, 0))],
        out_specs=pl.BlockSpec((1, tq, g * hd), lambda i, j, l: (i, l, j)),
        out_shape=jax.ShapeDtypeStruct((b, t, hk * g * hd), F32),
        compiler_params=_cparams(("parallel", "parallel", "arbitrary")),
        name="attn",
    )(q5, k4, v4)


def _prep_kernel(rw_ref, hp_ref, hn_ref, mu_ref, wd_ref, w0_ref, wa_ref, a0_ref, wg_ref, kk_ref, ka_ref, rk_ref,
                 grp_ref, r_o, lw_o, kd_o, v_o, kk_o, bd_o, g_o, bonus_o):
    i = pl.program_id(1)
    n = pl.num_programs(1)
    cur = rw_ref[0]
    tt = cur.shape[0]
    rid = lax.broadcasted_iota(jnp.int32, cur.shape, 0)
    prev_row = jnp.where(i > 0, hp_ref[0, 7:8, :], 0.0)
    next_row = jnp.where(i < n - 1, hn_ref[0, 0:1, :], 0.0)
    prev = jnp.where(rid == 0, prev_row, pltpu.roll(cur, 1, 0))
    nxt = jnp.where(rid == tt - 1, next_row, pltpu.roll(cur, tt - 1, 0))
    p = cur + mu_ref[0:1, :] * (prev - cur) + mu_ref[1:2, :] * (nxt - cur)
    r = p[:, 0:D_RWKV]
    k = p[:, D_RWKV:2 * D_RWKV]
    v = p[:, 2 * D_RWKV:3 * D_RWKV]
    lo = p[:, 3 * D_RWKV:3 * D_RWKV + LANES]
    gd = p[:, 3 * D_RWKV + LANES:]
    grp = grp_ref[...]
    wlog = _dot3(jnp.tanh(lo), wd_ref[...]) + w0_ref[...]
    alog = _dot3(lo, wa_ref[...]) + a0_ref[...]
    g_o[0] = _dot3(jax.nn.sigmoid(gd), wg_ref[...])
    kx = k * kk_ref[...]
    kk = kx * lax.rsqrt(_head_sum(kx * kx, grp) + 1e-12)
    r_o[0] = r
    v_o[0] = v
    kk_o[0] = kk
    bonus_o[0] = _head_sum(r * k * rk_ref[...], grp) * v
    for z in range(2):
        a = jax.nn.sigmoid(alog[:, z * D_RWKV:(z + 1) * D_RWKV])
        lw_o[z, 0] = -DECAY_SCALE * jax.nn.sigmoid(wlog[:, z * D_RWKV:(z + 1) * D_RWKV])
        kd_o[z, 0] = k * (1.0 + (a - 1.0) * ka_ref[...])
        bd_o[z, 0] = kk * a


def _rwkv_prep(rw3, mu_p, wd_cat, w0_cat, wa_cat, a0_cat, wg_p, k_k, k_a, r_k, grp, tt):
    b, t, _ = rw3.shape
    nt = t // tt
    hb = tt // 8
    one = jax.ShapeDtypeStruct((b, t, D_RWKV), F32)
    two = jax.ShapeDtypeStruct((2, b, t, D_RWKV), F32)
    s_one = pl.BlockSpec((1, tt, D_RWKV), lambda i, j: (i, j, 0))
    s_two = pl.BlockSpec((2, 1, tt, D_RWKV), lambda i, j: (0, i, j, 0))
    full = lambda a: pl.BlockSpec(a.shape, lambda i, j: (0,) * a.ndim)
    consts = [mu_p, wd_cat, w0_cat, wa_cat, a0_cat, wg_p, k_k, k_a, r_k, grp]
    return pl.pallas_call(
        _prep_kernel,
        grid=(b, nt),
        in_specs=[pl.BlockSpec((1, tt, D_RWKV_PAD), lambda i, j: (i, j, 0)),
                  pl.BlockSpec((1, 8, D_RWKV_PAD), lambda i, j: (i, jnp.maximum(j * hb - 1, 0), 0)),
                  pl.BlockSpec((1, 8, D_RWKV_PAD), lambda i, j: (i, jnp.minimum((j + 1) * hb, t // 8 - 1), 0))]
                 + [full(a) for a in consts],
        out_specs=(s_one, s_two, s_two, s_one, s_one, s_two, s_one, s_one),
        out_shape=(one, two, two, one, one, two, one, one),
        compiler_params=_cparams(("parallel", "parallel")),
        name="rwkv_prep",
    )(rw3, rw3, rw3, *consts)


INV_BASE = 16


def _bdot(a, b):
    return _dot(a.astype(BF16), b.astype(BF16))


def _scan_kernel(*refs, chunk):
    ins, (s0_ref, yf_ref, yb_ref, sT_ref, s_scr) = (refs[0:6], refs[6:12]), refs[12:]
    y_refs = (yf_ref, yb_ref)
    c = chunk

    @pl.when(pl.program_id(1) == 0)
    def _():
        s_scr[...] = s0_ref[0]

    row = lax.broadcasted_iota(jnp.int32, (c, c), 0)
    col = lax.broadcasted_iota(jnp.int32, (c, c), 1)
    eye = (row == col).astype(F32)
    same_blk = {}
    n = INV_BASE
    while n <= c:
        sh = jnp.int32(n.bit_length() - 1)
        same_blk[n] = lax.shift_right_logical(row, sh) == lax.shift_right_logical(col, sh)
        n *= 2
    heads = range(N_RWKV_HEADS)
    sls = [slice(h * HEAD_DIM, (h + 1) * HEAD_DIM) for h in heads]

    incl, strict, ar_abs, ar_mid, kb_inv, kb_end, v, p_tot = [], [], [], [], [], [], [], []
    for d, (r_ref, lw_ref, k_ref, v_ref, a_ref, b_ref) in enumerate(ins):
        incl.append(row >= col if d == 0 else row <= col)
        strict.append(row > col if d == 0 else row < col)
        lw = lw_ref[0, 0]
        lw_hi = lw.astype(BF16)
        lw_mid, lw_lo = _split2(lw - lw_hi.astype(F32))
        inclb = jnp.where(incl[d], 1.0, 0.0).astype(BF16)
        cl = _dot(inclb, lw_hi) + (_dot(inclb, lw_mid) + _dot(inclb, lw_lo))
        tot = jnp.sum(lw, axis=0, keepdims=True)
        mid = 0.5 * tot
        e_inv = jnp.exp(mid - cl)
        e_end = jnp.exp(tot - cl)
        s_mid = jnp.exp(-mid)
        r_abs = r_ref[0] * jnp.exp(cl)
        a_abs = a_ref[0] * jnp.exp(cl - lw)
        ar_abs.append(jnp.concatenate([a_abs, r_abs], axis=0).astype(BF16))
        ar_mid.append(jnp.concatenate([a_abs * s_mid, r_abs * s_mid], axis=0).astype(BF16))
        kb_inv.append(jnp.concatenate([k_ref[0, 0] * e_inv, b_ref[0, 0] * e_inv], axis=0).astype(BF16))
        kb_end.append(jnp.concatenate([k_ref[0, 0] * e_end, b_ref[0, 0] * e_end], axis=0).astype(BF16))
        v.append(v_ref[0].astype(BF16))
        p_tot.append(jnp.exp(tot))

    chains = [(d, h) for d in range(2) for h in heads]
    ids = range(len(chains))
    s_old = [s_scr[d, h] for d, h in chains]
    vhs = [v[d][:, sls[h]] for d, h in chains]
    grams = [_dot_nt(ar_mid[d][:, sls[h]], kb_inv[d][:, sls[h]]) for d, h in chains]
    from_s = [_dot_nt(ar_abs[d][:, sls[h]], s_old[i].astype(BF16)) for i, (d, h) in enumerate(chains)]
    masked = [jnp.concatenate([jnp.where(strict[d], grams[i][:c, :c], 0.0),
                               jnp.where(incl[d], grams[i][c:, :c], 0.0)], axis=0) for i, (d, h) in enumerate(chains)]
    from_v = [_bdot(masked[i], vhs[i]) for i in ids]
    lmats = [jnp.where(strict[d], grams[i][:c, c:], 0.0) for i, (d, h) in enumerate(chains)]
    l0s = [jnp.where(same_blk[INV_BASE], lm, 0.0) for lm in lmats]
    xs = [eye - l0 for l0 in l0s]
    pws = [_bdot(l0, l0) for l0 in l0s]
    span = 2
    while 2 * span < INV_BASE:
        both = [_bdot(jnp.concatenate([xs[i], pws[i]], axis=0), pws[i]) for i in ids]
        xs = [xs[i] + both[i][:c] for i in ids]
        pws = [both[i][c:] for i in ids]
        span *= 2
    xs = [xs[i] + _bdot(xs[i], pws[i]) for i in ids]
    n = INV_BASE
    while n < c:
        pair = same_blk[2 * n] & jnp.logical_not(same_blk[n])
        ts = [_bdot(jnp.where(pair, lmats[i], 0.0), xs[i]) for i in ids]
        xs = [xs[i] - _bdot(xs[i], ts[i]) for i in ids]
        n *= 2
    us = [_bdot(xs[i], from_s[i][:c] + from_v[i][:c]) for i in ids]
    yu = [_bdot(jnp.where(incl[d], grams[i][c:, c:], 0.0), us[i]) for i, (d, h) in enumerate(chains)]
    ds = [_dot_tn(jnp.concatenate([vhs[i], (-us[i]).astype(BF16)], axis=0), kb_end[d][:, sls[h]])
          for i, (d, h) in enumerate(chains)]
    for d in range(2):
        y_refs[d][0] = jnp.concatenate([from_s[i][c:] + from_v[i][c:] - yu[i]
                                        for i, (dd, h) in enumerate(chains) if dd == d], axis=-1)
    for i, (d, h) in enumerate(chains):
        s_scr[d, h] = s_old[i] * p_tot[d][:, sls[h]] + ds[i]

    @pl.when(pl.program_id(1) == pl.num_programs(1) - 1)
    def _():
        sT_ref[0] = s_scr[...]


def _rwkv_scan(r, lw2, kd2, v, kk, bd2, s0, chunk):
    b, t, _ = r.shape
    nc = t // chunk
    in_specs, args = [], []
    for d in range(2):
        tmap = (lambda j: j) if d == 0 else (lambda j: nc - 1 - j)
        s_one = pl.BlockSpec((1, chunk, D_RWKV), lambda i, j, tmap=tmap: (i, tmap(j), 0))
        s_two = pl.BlockSpec((1, 1, chunk, D_RWKV), lambda i, j, tmap=tmap, d=d: (d, i, tmap(j), 0))
        in_specs += [s_one, s_two, s_two, s_one, s_one, s_two]
        args += [r, lw2, kd2, v, kk, bd2]
    s_st = pl.BlockSpec((1, 2, N_RWKV_HEADS, HEAD_DIM, HEAD_DIM), lambda i, j: (i, 0, 0, 0, 0))
    y_specs = tuple(pl.BlockSpec((1, chunk, D_RWKV), lambda i, j, tmap=tmap: (i, tmap(j), 0))
                    for tmap in ((lambda j: j), (lambda j: nc - 1 - j)))
    y_shape = jax.ShapeDtypeStruct((b, t, D_RWKV), F32)
    return pl.pallas_call(
        functools.partial(_scan_kernel, chunk=chunk),
        grid=(b, nc),
        in_specs=in_specs + [s_st],
        out_specs=y_specs + (s_st,),
        out_shape=(y_shape, y_shape, jax.ShapeDtypeStruct(s0.shape, F32)),
        scratch_shapes=[pltpu.VMEM((2, N_RWKV_HEADS, HEAD_DIM, HEAD_DIM), F32)],
        compiler_params=_cparams(("parallel", "arbitrary")),
        name="rwkv_scan",
    )(*args, s0)


def _outproj_kernel(x_ref, at_ref, yf_ref, yb_ref, bonus_ref, gate_ref, lng_ref, lnb_ref, grp_ref, wo_ref,
                    g1_ref, sh_ref, sc_ref, n2_ref, wr_ref, br_ref, x1_ref, hx_ref, cnt_ref):
    grp = grp_ref[...]
    y = yf_ref[0] + yb_ref[0]
    mean = _head_sum(y, grp) * (1.0 / HEAD_DIM)
    yc = y - mean
    var = _head_sum(yc * yc, grp) * (1.0 / HEAD_DIM)
    yn = yc * lax.rsqrt(var + GN_EPS) * lng_ref[...] + lnb_ref[...]
    rw_out = (yn + bonus_ref[0]) * gate_ref[0]
    mix = (_dot(at_ref[0].astype(BF16), wo_ref[0:D_ATTN, :])
           + _dot(rw_out.astype(BF16), wo_ref[D_ATTN:, :]))
    x1 = x_ref[0] + g1_ref[0] * mix
    x1_ref[0] = x1
    ms = jnp.mean(x1 * x1, axis=-1, keepdims=True)
    h2 = x1 * lax.rsqrt(ms + NORM_EPS) * n2_ref[...]
    h2 = h2 * (1.0 + sc_ref[0]) + sh_ref[0]
    hx_ref[0, :, 0:D_MODEL] = h2.astype(BF16)
    logits = _dot3(h2, wr_ref[...]) + br_ref[...]
    lane = lax.broadcasted_iota(jnp.int32, logits.shape, 1)
    neg = -jnp.inf
    big = jnp.int32(1 << 20)
    lc = jnp.where(lane < N_GROUPS, logits, neg)
    mc = jnp.max(lc, axis=-1, keepdims=True)
    g_w = 1.0 / jnp.sum(jnp.exp(lc - mc), axis=-1, keepdims=True)
    g_idx = jnp.min(jnp.where(lc == mc, lane, big), axis=-1, keepdims=True)
    eid = lane - N_GROUPS
    in_grp = (eid >= 0) & (eid < N_EXPERTS) & (lax.shift_right_arithmetic(eid, 2) == g_idx)
    lf = jnp.where(in_grp, logits, neg)
    m1 = jnp.max(lf, axis=-1, keepdims=True)
    i1 = jnp.min(jnp.where(lf == m1, lane, big), axis=-1, keepdims=True)
    lf2 = jnp.where(lane == i1, neg, lf)
    m2 = jnp.max(lf2, axis=-1, keepdims=True)
    i2 = jnp.min(jnp.where(lf2 == m2, lane, big), axis=-1, keepdims=True)
    e2 = jnp.exp(m2 - m1)
    w1 = 1.0 / (1.0 + e2)
    w2 = e2 * w1
    cmb = g_w * (jnp.where(lane == i1, w1, 0.0) + jnp.where(lane == i2, w2, 0.0))
    rec = jnp.where(lane == 0, g_idx.astype(F32), cmb)
    rec_hi, rec_lo = _split2(rec)
    hx_ref[0, :, D_MODEL:D_MODEL + ROUTER_LANES] = rec_hi
    hx_ref[0, :, D_MODEL + ROUTER_LANES:] = rec_lo
    hot = jnp.where((lane == g_idx) & (lane < N_GROUPS), 1.0, 0.0)
    cnt_ref[0, 0] = jnp.broadcast_to(jnp.sum(hot, axis=0, keepdims=True), cnt_ref.shape[2:])


def _out_proj(x3, attn3, yf, yb, bonus, gate, ln_g, ln_b, grp, w_out_b, g1, sh2, sc2, norm2_g, w_r, b_r):
    tm = MOE_TM
    b, t, _ = x3.shape
    bm = g1.shape[0]
    mod_map = (lambda i, j: (i, 0, 0)) if bm > 1 else (lambda i, j: (0, 0, 0))
    tok = lambda w: pl.BlockSpec((1, tm, w), lambda i, j: (i, j, 0))
    full = lambda a: pl.BlockSpec(a.shape, lambda i, j: (0,) * a.ndim)
    mod = pl.BlockSpec((1, 1, D_MODEL), mod_map)
    return pl.pallas_call(
        _outproj_kernel,
        grid=(b, t // tm),
        in_specs=[tok(D_MODEL), tok(D_ATTN), tok(D_RWKV), tok(D_RWKV),
                  tok(D_RWKV), tok(D_RWKV), full(ln_g), full(ln_b), full(grp), full(w_out_b),
                  mod, mod, mod, full(norm2_g), full(w_r), full(b_r)],
        out_specs=(tok(D_MODEL), tok(MOE_W), pl.BlockSpec((1, 1, 8, LANES), lambda i, j: (i, j, 0, 0))),
        out_shape=(jax.ShapeDtypeStruct((b, t, D_MODEL), F32), jax.ShapeDtypeStruct((b, t, MOE_W), BF16),
                   jax.ShapeDtypeStruct((b, t // tm, 8, LANES), F32)),
        compiler_params=_cparams(("parallel", "parallel")),
        name="out_proj",
    )(x3, attn3, yf, yb, bonus, gate, ln_g, ln_b, grp, w_out_b, g1, sh2, sc2, norm2_g, w_r, b_r)


def _group_one_hots(rec_hi):
    tm = rec_hi.shape[0]
    sel = ((lax.broadcasted_iota(jnp.int32, (8, LANES), 0) == 0)
           & (lax.broadcasted_iota(jnp.int32, (8, LANES), 1) == 0))
    g_row = _dot_nt(jnp.where(sel, 1.0, 0.0).astype(BF16), rec_hi)[0:1, :]
    g_col = rec_hi.astype(F32)[:, 0:1]
    r = lax.broadcasted_iota(jnp.int32, (tm, tm), 0)
    c = lax.broadcasted_iota(jnp.int32, (tm, tm), 1)
    earlier_same = jnp.where((g_col == g_row) & (r < c), 1.0, 0.0)
    rank_row = jnp.sum(earlier_same, axis=0, keepdims=True)
    rf = r.astype(F32)
    return [jnp.where((g_row == float(g)) & (rank_row == rf), 1.0, 0.0).astype(BF16) for g in range(N_GROUPS)]


def _dispatch_kernel(dst_ref, hx_ref, xs_in_ref, xs_ref, buf, sem, *, tile0):
    del xs_in_ref
    i = pl.program_id(0)
    x = hx_ref[...]
    hots = _group_one_hots(x[:, D_MODEL:D_MODEL + ROUTER_LANES])
    blocks = [_dot(hot, x).astype(BF16) for hot in hots]

    def copy(g, t):
        start = pl.multiple_of(dst_ref[(tile0 + t) * N_GROUPS + g], SEG_ALIGN)
        return pltpu.make_async_copy(buf.at[g], xs_ref.at[pl.ds(start, MOE_TM), :], sem.at[g])

    @pl.when(i > 0)
    def _():
        for g in range(N_GROUPS):
            copy(g, i - 1).wait()

    for g in range(N_GROUPS):
        buf[g] = blocks[g]
        copy(g, i).start()

    @pl.when(i == pl.num_programs(0) - 1)
    def _():
        for g in range(N_GROUPS):
            copy(g, i).wait()


def _moe_dispatch(dst, hx2, xs, tile0):
    n = hx2.shape[0]
    grid_spec = pltpu.PrefetchScalarGridSpec(
        num_scalar_prefetch=1, grid=(n // MOE_TM,),
        in_specs=[pl.BlockSpec((MOE_TM, MOE_W), lambda i, dst: (i, 0)), pl.BlockSpec(memory_space=pl.ANY)],
        out_specs=pl.BlockSpec(memory_space=pl.ANY),
        scratch_shapes=[pltpu.VMEM((N_GROUPS, MOE_TM, MOE_W), BF16), pltpu.SemaphoreType.DMA((N_GROUPS,))])
    return pl.pallas_call(
        functools.partial(_dispatch_kernel, tile0=tile0),
        grid_spec=grid_spec,
        out_shape=jax.ShapeDtypeStruct(xs.shape, xs.dtype),
        input_output_aliases={2: 0},
        compiler_params=_cparams(("arbitrary",)),
        name="moe_dispatch",
    )(dst, hx2, xs)


def _experts_kernel(tg_ref, xs_ref, wg_ref, wu_ref, wd_ref, ys_ref):
    g = tg_ref[pl.program_id(0)]

    @pl.when(g >= N_GROUPS)
    def _():
        ys_ref[...] = jnp.zeros_like(ys_ref)

    @pl.when(g < N_GROUPS)
    def _():
        x = xs_ref[...]
        h = x[:, 0:D_MODEL]
        rec = (x[:, D_MODEL:D_MODEL + ROUTER_LANES].astype(F32) + x[:, D_MODEL + ROUTER_LANES:].astype(F32))
        lane = lax.broadcasted_iota(jnp.int32, rec.shape, 1)
        first = N_GROUPS + EXPERTS_PER_GROUP * g
        scaled = []
        for e in range(EXPERTS_PER_GROUP):
            a = _dot(h, wg_ref[0, e])
            hid = a * jax.nn.sigmoid(a) * _dot(h, wu_ref[0, e])
            c_e = jnp.sum(jnp.where(lane == first + e, rec, 0.0), axis=-1, keepdims=True)
            scaled.append((hid * c_e).astype(BF16))
        ys_ref[...] = _dot(jnp.concatenate(scaled, axis=-1), wd_ref[0]).astype(BF16)


def _moe_experts(tile_group, xs, wg4, wu4, wd4):
    p = xs.shape[0]
    grp_map = lambda nd: (lambda j, tg: (jnp.minimum(tg[j], N_GROUPS - 1),) + (0,) * (nd - 1))
    grid_spec = pltpu.PrefetchScalarGridSpec(
        num_scalar_prefetch=1, grid=(p // MOE_TX,),
        in_specs=[pl.BlockSpec((MOE_TX, MOE_W), lambda j, tg: (j, 0)),
                  pl.BlockSpec((1,) + wg4.shape[1:], grp_map(wg4.ndim)),
                  pl.BlockSpec((1,) + wu4.shape[1:], grp_map(wu4.ndim)),
                  pl.BlockSpec((1,) + wd4.shape[1:], grp_map(wd4.ndim))],
        out_specs=pl.BlockSpec((MOE_TX, D_MODEL), lambda j, tg: (j, 0)))
    return pl.pallas_call(
        _experts_kernel,
        grid_spec=grid_spec,
        out_shape=jax.ShapeDtypeStruct((p, D_MODEL), BF16),
        compiler_params=_cparams(("arbitrary",)),
        name="moe_experts",
    )(tile_group, xs, wg4, wu4, wd4)


def _combine_kernel(dst_ref, rec_ref, x1_ref, g2_ref, ys_ref, o_ref, buf, sem, *, tile0):
    lin = pl.program_id(0) * pl.num_programs(1) + pl.program_id(1)
    n = pl.num_programs(0) * pl.num_programs(1)
    slot = lin % 2

    def copy(g, t, s):
        start = pl.multiple_of(dst_ref[(tile0 + t) * N_GROUPS + g], SEG_ALIGN)
        return pltpu.make_async_copy(ys_ref.at[pl.ds(start, MOE_TM), :], buf.at[s, g], sem.at[s, g])

    @pl.when(lin == 0)
    def _():
        for g in range(N_GROUPS):
            copy(g, 0, 0).start()

    @pl.when(lin + 1 < n)
    def _():
        for g in range(N_GROUPS):
            copy(g, lin + 1, 1 - slot).start()

    hots = _group_one_hots(rec_ref[0])
    for g in range(N_GROUPS):
        copy(g, lin, slot).wait()
    y = _dot_tn(hots[0], buf[slot, 0])
    for g in range(1, N_GROUPS):
        y = y + _dot_tn(hots[g], buf[slot, g])
    o_ref[0] = x1_ref[0] + g2_ref[0] * y


def _moe_combine(dst, hx3, x1, g2, ys, tile0):
    b, t, _ = x1.shape
    bm = g2.shape[0]
    mod_map = (lambda i, j, dst: (i, 0, 0)) if bm > 1 else (lambda i, j, dst: (0, 0, 0))
    rec_blk = D_MODEL // ROUTER_LANES
    grid_spec = pltpu.PrefetchScalarGridSpec(
        num_scalar_prefetch=1, grid=(b, t // MOE_TM),
        in_specs=[pl.BlockSpec((1, MOE_TM, ROUTER_LANES), lambda i, j, dst: (i, j, rec_blk)),
                  pl.BlockSpec((1, MOE_TM, D_MODEL), lambda i, j, dst: (i, j, 0)),
                  pl.BlockSpec((1, 1, D_MODEL), mod_map),
                  pl.BlockSpec(memory_space=pl.ANY)],
        out_specs=pl.BlockSpec((1, MOE_TM, D_MODEL), lambda i, j, dst: (i, j, 0)),
        scratch_shapes=[pltpu.VMEM((2, N_GROUPS, MOE_TM, D_MODEL), BF16),
                        pltpu.SemaphoreType.DMA((2, N_GROUPS))])
    return pl.pallas_call(
        functools.partial(_combine_kernel, tile0=tile0),
        grid_spec=grid_spec,
        out_shape=jax.ShapeDtypeStruct(x1.shape, F32),
        compiler_params=_cparams(("arbitrary", "arbitrary")),
        name="moe_combine",
    )(dst, hx3, x1, g2, ys)


def _moe_plan(cnt, n_rows):
    seg = (cnt + (SEG_ALIGN - 1)) // SEG_ALIGN * SEG_ALIGN
    used = (jnp.sum(seg, axis=0) + (MOE_TX - 1)) // MOE_TX * MOE_TX
    size = used + MOE_TX
    base = jnp.cumsum(size) - size
    dst = base[None, :] + jnp.cumsum(seg, axis=0) - seg
    starts = jnp.arange(n_rows // MOE_TX, dtype=jnp.int32)[:, None] * MOE_TX
    inside = (starts >= base[None, :]) & (starts < (base + used)[None, :])
    tile_group = jnp.where(jnp.any(inside, axis=1), jnp.argmax(inside, axis=1), N_GROUPS)
    return dst.reshape(-1).astype(jnp.int32), tile_group.astype(jnp.int32)


def _moe_rows(n_tokens):
    n_tiles = n_tokens // MOE_TM
    bound = n_tokens + n_tiles * N_GROUPS * (SEG_ALIGN - 1) + N_GROUPS * 2 * MOE_TX
    return (bound + MOE_TX - 1) // MOE_TX * MOE_TX


def _layer(x3, mod6, lp, t_tiles, ctx):
    b, t, _ = x3.shape
    sh1, sc1, g1, sh2, sc2, _ = mod6
    t_tiles = {name: min(size, t) for name, size in t_tiles.items()}
    rope = ctx is not None
    n = b * t
    q5, k4, v4, k, v, rw = _in_proj(x3, sh1, sc1, lp["norm1_g"], lp["w_in_p"], lp["gqk"], lp["grp"],
                                    rope, t_tiles["in_proj"])
    if ctx is not None:
        ctx_k, ctx_v, ctx_state = ctx
        ck = ctx_k.transpose(0, 2, 1, 3)
        cv = ctx_v.transpose(0, 2, 1, 3)
        k4 = jnp.concatenate([k4, ck.astype(BF16)], axis=2)
        v4 = jnp.concatenate([v4, jnp.concatenate([cv, jnp.ones_like(cv)], axis=-1).astype(BF16)], axis=2)
        s0 = ctx_state
    else:
        s0 = jnp.zeros((b, 2, N_RWKV_HEADS, HEAD_DIM, HEAD_DIM), F32)
    attn3 = _attention(q5, k4, v4, t_tiles["attn"])
    r, lw2, kd2, vv, kk, bd2, gate, bonus = _rwkv_prep(
        rw, lp["mu_p"], lp["wd_cat"], lp["w0_cat"], lp["wa_cat"], lp["a0_cat"],
        lp["wg_p"], lp["k_k"], lp["k_a"], lp["r_k"], lp["grp"], t_tiles["prep"])
    yf, yb, s_t = _rwkv_scan(r, lw2, kd2, vv, kk, bd2, s0, t_tiles["chunk"])
    fb, ft = (1, n) if g1.shape[0] == 1 else (b, t)
    flat = lambda a: a.reshape(fb, ft, a.shape[-1])
    x1, hx, cnt = _out_proj(flat(x3), flat(attn3), flat(yf), flat(yb), flat(bonus), flat(gate), lp["ln_g"],
                            lp["ln_b"], lp["grp"], lp["w_out_b"], g1, sh2, sc2, lp["norm2_g"], lp["w_r"], lp["b_r"])
    return (x1, hx, cnt), k.reshape(b, t, N_KV_HEADS, HEAD_DIM), v.reshape(b, t, N_KV_HEADS, HEAD_DIM), s_t


def _moe_both(passes, g2s, lp):
    counts = [p[2][:, :, 0, :N_GROUPS].reshape(-1, N_GROUPS) for p in passes]
    tiles = [c.shape[0] for c in counts]
    n_rows = _moe_rows(sum(tiles) * MOE_TM)
    dst, tile_group = _moe_plan(jnp.concatenate(counts, axis=0).astype(jnp.int32), n_rows)
    xs = jnp.zeros((n_rows, MOE_W), BF16)
    tile0 = 0
    for (x1, hx, _), nt in zip(passes, tiles):
        xs = _moe_dispatch(dst, hx.reshape(-1, MOE_W), xs, tile0)
        tile0 += nt
    ys = _moe_experts(tile_group, xs, lp["wg4"], lp["wu4"], lp["wd4"])
    outs, tile0 = [], 0
    for (x1, hx, _), g2, nt in zip(passes, g2s, tiles):
        outs.append(_moe_combine(dst, hx, x1, g2, ys, tile0))
        tile0 += nt
    return outs


def _block_diag2(w):
    z, l, c = w.shape
    out = jnp.zeros((LANES, z * c), F32)
    for i in range(z):
        out = out.at[i * l:(i + 1) * l, i * c:(i + 1) * c].set(w[i])
    return out


def _layer_params(l, w_in, norm1_g, norm2_g, mu_shift, q_norm_g, k_norm_g, w0, w_lora_up, a0, a_lora_up, g_lora_up,
                  k_k, k_a, r_k, ln_x_g, ln_x_b, w_out, router_c, router_c_b, router_f, router_f_b,
                  exp_gate, exp_up, exp_down):
    lane = np.arange(LANES)
    grp = jnp.asarray((lane[:, None] // HEAD_DIM) == (lane[None, :] // HEAD_DIM), BF16)
    pad_in = D_IN_PAD - w_in.shape[2]
    wd_cat = _block_diag2(w_lora_up[l])
    wa_cat = jnp.roll(_block_diag2(a_lora_up[l]), 2 * DECAY_LORA, axis=0)
    w_r = jnp.zeros((D_MODEL, ROUTER_LANES), F32)
    w_r = w_r.at[:, :N_GROUPS].set(router_c[l]).at[:, N_GROUPS:N_GROUPS + N_EXPERTS].set(router_f[l])
    b_r = jnp.zeros((1, ROUTER_LANES), F32)
    b_r = b_r.at[0, :N_GROUPS].set(router_c_b[l]).at[0, N_GROUPS:N_GROUPS + N_EXPERTS].set(router_f_b[l])

    by_group = lambda w: w.astype(BF16).reshape(N_GROUPS, EXPERTS_PER_GROUP, D_MODEL, D_EXPERT)

    return dict(
        grp=grp,
        norm1_g=norm1_g[l].reshape(1, D_MODEL), norm2_g=norm2_g[l].reshape(1, D_MODEL),
        w_in_p=jnp.pad(w_in[l], ((0, 0), (0, pad_in))).astype(BF16),
        gqk=jnp.concatenate([jnp.tile(q_norm_g[l], N_Q_HEADS), jnp.tile(k_norm_g[l], N_KV_HEADS)]).reshape(1, -1),
        mu_p=jnp.pad(mu_shift[l], ((0, 0), (0, D_RWKV_PAD - D_RWKV_IN))),
        wd_cat=wd_cat, w0_cat=w0[l].reshape(1, 2 * D_RWKV),
        wa_cat=wa_cat, a0_cat=a0[l].reshape(1, 2 * D_RWKV),
        wg_p=jnp.pad(g_lora_up[l], ((0, LANES - GATE_LORA), (0, 0))),
        k_k=k_k[l].reshape(1, D_RWKV), k_a=k_a[l].reshape(1, D_RWKV), r_k=r_k[l].reshape(1, D_RWKV),
        ln_g=ln_x_g[l].reshape(1, D_RWKV), ln_b=ln_x_b[l].reshape(1, D_RWKV),
        w_out_b=w_out[l].astype(BF16), w_r=w_r, b_r=b_r,
        wg4=by_group(exp_gate[l]), wu4=by_group(exp_up[l]),
        wd4=exp_down[l].astype(BF16).reshape(N_GROUPS, EXPERTS_PER_GROUP * D_EXPERT, D_MODEL),
    )


CTX_TILES = dict(in_proj=256, attn=256, prep=256, chunk=128)
SMP_TILES = dict(in_proj=512, attn=128, prep=256, chunk=128)


def kernel(x_prompt, x_sample, cache_k, cache_v, state_rwkv, c, c_ctx, w_mod, b_mod, norm1_g, norm2_g, w_in, mu_shift, q_norm_g, k_norm_g, w0, w_lora_up, a0, a_lora_up, g_lora_up, k_k, k_a, r_k, ln_x_g, ln_x_b, w_out, router_c, router_c_b, router_f, router_f_b, exp_gate, exp_up, exp_down):
    depth = w_mod.shape[0]
    db = x_sample.shape[0]
    y_prompt, y_sample = x_prompt, x_sample
    ks, vs, ss = [], [], []
    cond = jnp.zeros((8, D_MODEL), F32).at[:db].set(c).at[db].set(c_ctx)
    for l in range(depth):
        lp = _layer_params(l, w_in, norm1_g, norm2_g, mu_shift, q_norm_g, k_norm_g, w0, w_lora_up, a0, a_lora_up,
                           g_lora_up, k_k, k_a, r_k, ln_x_g, ln_x_b, w_out, router_c, router_c_b, router_f,
                           router_f_b, exp_gate, exp_up, exp_down)
        mod = _modulation(cond, w_mod[l], b_mod[l])
        mod_s = [mod[:db, i * D_MODEL:(i + 1) * D_MODEL].reshape(db, 1, D_MODEL) for i in range(6)]
        mod_c = [mod[db:db + 1, i * D_MODEL:(i + 1) * D_MODEL].reshape(1, 1, D_MODEL) for i in range(6)]
        pre_c, k_l, v_l, s_l = _layer(y_prompt, mod_c, lp, CTX_TILES, None)
        ks.append(k_l)
        vs.append(v_l)
        ss.append(s_l)
        pre_s, _, _, _ = _layer(y_sample, mod_s, lp, SMP_TILES, (cache_k[:, l], cache_v[:, l], state_rwkv[:, l]))
        out_c, out_s = _moe_both([pre_c, pre_s], [mod_c[5], mod_s[5]], lp)
        y_prompt, y_sample = out_c.reshape(y_prompt.shape), out_s.reshape(y_sample.shape)
    return (y_prompt, y_sample, jnp.stack(ks, axis=1), jnp.stack(vs, axis=1), jnp.stack(ss, axis=1))
```

```python
import functools

import numpy as np
import jax
import jax.numpy as jnp
from jax import lax
from jax.experimental import pallas as pl
from jax.experimental.pallas import tpu as pltpu

F32 = jnp.float32
BF16 = jnp.bfloat16
HIGHEST = lax.Precision.HIGHEST

D_MODEL = 1024
HEAD_DIM = 64
N_Q_HEADS = 8
N_KV_HEADS = 2
GQA_GROUP = N_Q_HEADS // N_KV_HEADS
D_ATTN = N_Q_HEADS * HEAD_DIM
D_KV = N_KV_HEADS * HEAD_DIM
N_RWKV_HEADS = 8
D_RWKV = 512
DECAY_LORA = 32
AAA_LORA = 32
GATE_LORA = 96
D_RWKV_IN = 3 * D_RWKV + 2 * DECAY_LORA + 2 * AAA_LORA + GATE_LORA
D_RWKV_PAD = 1792
D_QKV = D_ATTN + 2 * D_KV
D_IN_PAD = D_QKV + D_RWKV_PAD
N_GROUPS = 4
EXPERTS_PER_GROUP = 4
N_EXPERTS = 16
D_EXPERT = 512
GRID_W = 64
ROPE_THETA = 10000.0
NORM_EPS = 1e-6
GN_EPS = 64e-5
DECAY_SCALE = 0.6065306597
QK_EXP2_SCALE = (HEAD_DIM ** -0.5) * float(np.log2(np.e))
LANES = 128
ROUTER_LANES = 128
VMEM_LIMIT = 56 * 1024 * 1024
ATTN_ROWS = 128
MOE_W = D_MODEL + 2 * ROUTER_LANES
MOE_TM = 256
OUT_PROJ_TM = 512
MOE_TX = 512
SEG_ALIGN = 16


def _cparams(sem):
    return pltpu.CompilerParams(dimension_semantics=sem, vmem_limit_bytes=VMEM_LIMIT)


def _dot(a, b, precision=None):
    return jnp.dot(a, b, preferred_element_type=F32, precision=precision)


def _dot_nt(a, b, precision=None):
    return lax.dot_general(a, b, (((1,), (1,)), ((), ())), preferred_element_type=F32, precision=precision)


def _dot_tn(a, b, precision=None):
    return lax.dot_general(a, b, (((0,), (0,)), ((), ())), preferred_element_type=F32, precision=precision)


def _split2(x):
    hi = x.astype(BF16)
    return hi, (x - hi.astype(F32)).astype(BF16)


def _dot3(a, b):
    a_hi, a_lo = _split2(a)
    b_hi, b_lo = _split2(b)
    return _dot(a_hi, b_hi) + (_dot(a_hi, b_lo) + _dot(a_lo, b_hi))


def _head_sum(x, g):
    hi, lo = _split2(x)
    n = x.shape[-1] // LANES
    cols = [slice(j * LANES, (j + 1) * LANES) for j in range(n)]
    return jnp.concatenate([_dot(hi[:, c], g) + _dot(lo[:, c], g) for c in cols], axis=-1)


def _mod_kernel(c_ref, w_ref, b_ref, o_ref):
    c = c_ref[...]
    s = c * jax.nn.sigmoid(c)
    o_ref[...] = _dot(s, w_ref[...], HIGHEST) + b_ref[...]


def _modulation(cond, w_mod, b_mod):
    n = w_mod.shape[1]
    tn = 1024
    return pl.pallas_call(
        _mod_kernel,
        grid=(n // tn,),
        in_specs=[pl.BlockSpec((8, D_MODEL), lambda j: (0, 0)),
                  pl.BlockSpec((D_MODEL, tn), lambda j: (0, j)),
                  pl.BlockSpec((1, tn), lambda j: (0, j))],
        out_specs=pl.BlockSpec((8, tn), lambda j: (0, j)),
        out_shape=jax.ShapeDtypeStruct((8, n), F32),
        compiler_params=_cparams(("arbitrary",)),
        name="mod",
    )(cond, w_mod, b_mod.reshape(1, n))


def _rope_tables(t_len):
    half = HEAD_DIM // 2
    inv = ROPE_THETA ** (-np.arange(0, half, 2, dtype=np.float64) / half)
    t = np.arange(t_len)
    row, col = t // GRID_W, t % GRID_W
    lane = np.arange(LANES)
    i = lane % HEAD_DIM
    pos = np.where((i // half)[None, :] == 0, row[:, None], col[:, None]).astype(np.float64)
    j = i % half
    ang = pos * inv[j % (half // 2)][None, :]
    cos, sin = np.cos(ang), np.sin(ang)
    first = (j < half // 2)[None, :]
    s_up = np.where(first, -sin, 0.0)
    s_dn = np.where(first, 0.0, sin)
    return (jnp.asarray(cos, F32), jnp.asarray(s_up, F32), jnp.asarray(s_dn, F32))


def _inproj_kernel(x_ref, sh_ref, sc_ref, g_ref, w_ref, gqk_ref, grp_ref, *rest, rope):
    if rope:
        cos_ref, sup_ref, sdn_ref, q_ref, k_ref, v_ref, kf_ref, vf_ref, rw_ref = rest
    else:
        q_ref, k_ref, v_ref, kf_ref, vf_ref, rw_ref = rest
    x = x_ref[0]
    ms = jnp.mean(x * x, axis=-1, keepdims=True)
    h = x * lax.rsqrt(ms + NORM_EPS) * g_ref[...]
    h = h * (1.0 + sc_ref[0]) + sh_ref[0]
    proj = _dot(h.astype(BF16), w_ref[...])
    grp = grp_ref[...]
    lo_half = lax.broadcasted_iota(jnp.int32, (x.shape[0], LANES), 1) < HEAD_DIM
    for j in range((D_ATTN + D_KV) // LANES):
        blk = proj[:, j * LANES:(j + 1) * LANES]
        ss = _head_sum(blk * blk, grp) * (1.0 / HEAD_DIM)
        nb = blk * lax.rsqrt(ss + NORM_EPS) * gqk_ref[:, j * LANES:(j + 1) * LANES]
        if rope:
            nb = (nb * cos_ref[...] + pltpu.roll(nb, LANES - 16, 1) * sup_ref[...]
                  + pltpu.roll(nb, 16, 1) * sdn_ref[...])
        if j < D_ATTN // LANES:
            nbq = nb * QK_EXP2_SCALE
            for half in range(2):
                hq = 2 * j + half
                q_ref[0, hq // GQA_GROUP, hq % GQA_GROUP] = nbq[:, half * HEAD_DIM:(half + 1) * HEAD_DIM].astype(BF16)
        else:
            kf_ref[0] = nb
            k_ref[0, 0] = nb[:, :HEAD_DIM].astype(BF16)
            k_ref[0, 1] = nb[:, HEAD_DIM:].astype(BF16)
    vblk = proj[:, D_ATTN + D_KV:D_QKV]
    vf_ref[0] = vblk
    v_ref[0, 0] = jnp.where(lo_half, vblk, 1.0).astype(BF16)
    v_ref[0, 1] = jnp.where(lo_half, pltpu.roll(vblk, HEAD_DIM, 1), 1.0).astype(BF16)
    rw_ref[0] = proj[:, D_QKV:]


def _in_proj(x3, shift, scale, norm_g, w_in_p, gqk, grp, rope, tm):
    b, t, _ = x3.shape
    bm = shift.shape[0]
    mod_map = (lambda i, j: (i, 0, 0)) if bm > 1 else (lambda i, j: (0, 0, 0))
    full = lambda a: pl.BlockSpec(a.shape, lambda i, j: (0,) * a.ndim)
    tok = lambda w: pl.BlockSpec((1, tm, w), lambda i, j: (i, j, 0))
    in_specs = [tok(D_MODEL), pl.BlockSpec((1, 1, D_MODEL), mod_map), pl.BlockSpec((1, 1, D_MODEL), mod_map),
                full(norm_g), full(w_in_p), full(gqk), full(grp)]
    args = [x3, shift, scale, norm_g, w_in_p, gqk, grp]
    if rope:
        in_specs += [pl.BlockSpec((tm, LANES), lambda i, j: (j, 0))] * 3
        args += list(_rope_tables(t))
    out_shape = (jax.ShapeDtypeStruct((b, N_KV_HEADS, GQA_GROUP, t, HEAD_DIM), BF16),
                 jax.ShapeDtypeStruct((b, N_KV_HEADS, t, HEAD_DIM), BF16),
                 jax.ShapeDtypeStruct((b, N_KV_HEADS, t, 2 * HEAD_DIM), BF16),
                 jax.ShapeDtypeStruct((b, t, D_KV), F32), jax.ShapeDtypeStruct((b, t, D_KV), F32),
                 jax.ShapeDtypeStruct((b, t, D_RWKV_PAD), F32))
    out_specs = (pl.BlockSpec((1, N_KV_HEADS, GQA_GROUP, tm, HEAD_DIM), lambda i, j: (i, 0, 0, j, 0)),
                 pl.BlockSpec((1, N_KV_HEADS, tm, HEAD_DIM), lambda i, j: (i, 0, j, 0)),
                 pl.BlockSpec((1, N_KV_HEADS, tm, 2 * HEAD_DIM), lambda i, j: (i, 0, j, 0)),
                 tok(D_KV), tok(D_KV), tok(D_RWKV_PAD))
    return pl.pallas_call(
        functools.partial(_inproj_kernel, rope=rope),
        grid=(b, t // tm), in_specs=in_specs, out_specs=out_specs, out_shape=out_shape,
        compiler_params=_cparams(("parallel", "parallel")),
        name="in_proj_rope" if rope else "in_proj",
    )(*args)


def _attn_kernel(q_ref, k_ref, v_ref, o_ref):
    g, tq, hd = q_ref.shape[2:]
    sub = min(ATTN_ROWS, tq)
    k = k_ref[0, 0]
    v = v_ref[0, 0]
    slabs = [slice(i * sub, (i + 1) * sub) for i in range(tq // sub)]
    qs = [q_ref[0, 0, :, sl, :].reshape(g * sub, hd) for sl in slabs]
    ss = [_dot_nt(q, k) for q in qs]
    ps = [jnp.exp2(s - jnp.max(s, axis=-1, keepdims=True)).astype(BF16) for s in ss]
    for sl, p in zip(slabs, ps):
        o = _dot(p, v)
        o = o[:, :hd] / pltpu.roll(o, hd, 1)[:, :hd]
        o_ref[0, sl, :] = jnp.concatenate([o[i * sub:(i + 1) * sub] for i in range(g)], axis=-1)


def _attention(q5, k4, v4, tq):
    b, hk, g, t, hd = q5.shape
    tk = k4.shape[2]
    return pl.pallas_call(
        _attn_kernel,
        grid=(b, hk, t // tq),
        in_specs=[pl.BlockSpec((1, 1, g, tq, hd), lambda i, j, l: (i, j, 0, l, 0)),
                  pl.BlockSpec((1, 1, tk, hd), lambda i, j, l: (i, j, 0, 0)),
                  pl.BlockSpec((1, 1, tk, 2 * hd), lambda i, j, l: (i, j, 0, 0))],
        out_specs=pl.BlockSpec((1, tq, g * hd), lambda i, j, l: (i, l, j)),
        out_shape=jax.ShapeDtypeStruct((b, t, hk * g * hd), F32),
        compiler_params=_cparams(("parallel", "parallel", "arbitrary")),
        name="attn",
    )(q5, k4, v4)


def _prep_kernel(rw_ref, hp_ref, hn_ref, mu_ref, wd_ref, w0_ref, wa_ref, a0_ref, wg_ref, kk_ref, ka_ref, rk_ref,
                 grp_ref, r_o, lw_o, kd_o, v_o, kk_o, bd_o, g_o, bonus_o):
    i = pl.program_id(1)
    n = pl.num_programs(1)
    cur = rw_ref[0]
    tt = cur.shape[0]
    rid = lax.broadcasted_iota(jnp.int32, cur.shape, 0)
    prev_row = jnp.where(i > 0, hp_ref[0, 7:8, :], 0.0)
    next_row = jnp.where(i < n - 1, hn_ref[0, 0:1, :], 0.0)
    prev = jnp.where(rid == 0, prev_row, pltpu.roll(cur, 1, 0))
    nxt = jnp.where(rid == tt - 1, next_row, pltpu.roll(cur, tt - 1, 0))
    p = cur + mu_ref[0:1, :] * (prev - cur) + mu_ref[1:2, :] * (nxt - cur)
    r = p[:, 0:D_RWKV]
    k = p[:, D_RWKV:2 * D_RWKV]
    v = p[:, 2 * D_RWKV:3 * D_RWKV]
    lo = p[:, 3 * D_RWKV:3 * D_RWKV + LANES]
    gd = p[:, 3 * D_RWKV + LANES:]
    grp = grp_ref[...]
    wlog = _dot3(jnp.tanh(lo), wd_ref[...]) + w0_ref[...]
    alog = _dot3(lo, wa_ref[...]) + a0_ref[...]
    g_o[0] = _dot3(jax.nn.sigmoid(gd), wg_ref[...])
    kx = k * kk_ref[...]
    kk = kx * lax.rsqrt(_head_sum(kx * kx, grp) + 1e-12)
    r_o[0] = r
    v_o[0] = v
    kk_o[0] = kk
    bonus_o[0] = _head_sum(r * k * rk_ref[...], grp) * v
    for z in range(2):
        a = jax.nn.sigmoid(alog[:, z * D_RWKV:(z + 1) * D_RWKV])
        lw_o[z, 0] = -DECAY_SCALE * jax.nn.sigmoid(wlog[:, z * D_RWKV:(z + 1) * D_RWKV])
        kd_o[z, 0] = k * (1.0 + (a - 1.0) * ka_ref[...])
        bd_o[z, 0] = kk * a


def _rwkv_prep(rw3, mu_p, wd_cat, w0_cat, wa_cat, a0_cat, wg_p, k_k, k_a, r_k, grp, tt):
    b, t, _ = rw3.shape
    nt = t // tt
    hb = tt // 8
    one = jax.ShapeDtypeStruct((b, t, D_RWKV), F32)
    two = jax.ShapeDtypeStruct((2, b, t, D_RWKV), F32)
    s_one = pl.BlockSpec((1, tt, D_RWKV), lambda i, j: (i, j, 0))
    s_two = pl.BlockSpec((2, 1, tt, D_RWKV), lambda i, j: (0, i, j, 0))
    full = lambda a: pl.BlockSpec(a.shape, lambda i, j: (0,) * a.ndim)
    consts = [mu_p, wd_cat, w0_cat, wa_cat, a0_cat, wg_p, k_k, k_a, r_k, grp]
    return pl.pallas_call(
        _prep_kernel,
        grid=(b, nt),
        in_specs=[pl.BlockSpec((1, tt, D_RWKV_PAD), lambda i, j: (i, j, 0)),
                  pl.BlockSpec((1, 8, D_RWKV_PAD), lambda i, j: (i, jnp.maximum(j * hb - 1, 0), 0)),
                  pl.BlockSpec((1, 8, D_RWKV_PAD), lambda i, j: (i, jnp.minimum((j + 1) * hb, t // 8 - 1), 0))]
                 + [full(a) for a in consts],
        out_specs=(s_one, s_two, s_two, s_one, s_one, s_two, s_one, s_one),
        out_shape=(one, two, two, one, one, two, one, one),
        compiler_params=_cparams(("parallel", "parallel")),
        name="rwkv_prep",
    )(rw3, rw3, rw3, *consts)


INV_BASE = 16


def _bdot(a, b):
    return _dot(a.astype(BF16), b.astype(BF16))


def _scan_kernel(*refs, chunk):
    ins, (s0_ref, yf_ref, yb_ref, sT_ref, s_scr) = (refs[0:6], refs[6:12]), refs[12:]
    y_refs = (yf_ref, yb_ref)
    c = chunk

    @pl.when(pl.program_id(1) == 0)
    def _():
        s_scr[...] = s0_ref[0]

    row = lax.broadcasted_iota(jnp.int32, (c, c), 0)
    col = lax.broadcasted_iota(jnp.int32, (c, c), 1)
    eye = (row == col).astype(F32)
    same_blk = {}
    n = INV_BASE
    while n <= c:
        sh = jnp.int32(n.bit_length() - 1)
        same_blk[n] = lax.shift_right_logical(row, sh) == lax.shift_right_logical(col, sh)
        n *= 2
    heads = range(N_RWKV_HEADS)
    sls = [slice(h * HEAD_DIM, (h + 1) * HEAD_DIM) for h in heads]

    incl, strict, ar_abs, ar_mid, kb_inv, kb_end, v, p_tot = [], [], [], [], [], [], [], []
    for d, (r_ref, lw_ref, k_ref, v_ref, a_ref, b_ref) in enumerate(ins):
        incl.append(row >= col if d == 0 else row <= col)
        strict.append(row > col if d == 0 else row < col)
        lw = lw_ref[0, 0]
        lw_hi = lw.astype(BF16)
        lw_mid, lw_lo = _split2(lw - lw_hi.astype(F32))
        inclb = jnp.where(incl[d], 1.0, 0.0).astype(BF16)
        cl = _dot(inclb, lw_hi) + (_dot(inclb, lw_mid) + _dot(inclb, lw_lo))
        tot = jnp.sum(lw, axis=0, keepdims=True)
        mid = 0.5 * tot
        e_inv = jnp.exp(mid - cl)
        e_end = jnp.exp(tot - cl)
        s_mid = jnp.exp(-mid)
        r_abs = r_ref[0] * jnp.exp(cl)
        a_abs = a_ref[0] * jnp.exp(cl - lw)
        ar_abs.append(jnp.concatenate([a_abs, r_abs], axis=0).astype(BF16))
        ar_mid.append(jnp.concatenate([a_abs * s_mid, r_abs * s_mid], axis=0).astype(BF16))
        kb_inv.append(jnp.concatenate([k_ref[0, 0] * e_inv, b_ref[0, 0] * e_inv], axis=0).astype(BF16))
        kb_end.append(jnp.concatenate([k_ref[0, 0] * e_end, b_ref[0, 0] * e_end], axis=0).astype(BF16))
        v.append(v_ref[0].astype(BF16))
        p_tot.append(jnp.exp(tot))

    chains = [(d, h) for d in range(2) for h in heads]
    ids = range(len(chains))
    s_old = [s_scr[d, h] for d, h in chains]
    vhs = [v[d][:, sls[h]] for d, h in chains]
    grams = [_dot_nt(ar_mid[d][:, sls[h]], kb_inv[d][:, sls[h]]) for d, h in chains]
    from_s = [_dot_nt(ar_abs[d][:, sls[h]], s_old[i].astype(BF16)) for i, (d, h) in enumerate(chains)]
    masked = [jnp.concatenate([jnp.where(strict[d], grams[i][:c, :c], 0.0),
                               jnp.where(incl[d], grams[i][c:, :c], 0.0)], axis=0) for i, (d, h) in enumerate(chains)]
    from_v = [_bdot(masked[i], vhs[i]) for i in ids]
    lmats = [jnp.where(strict[d], grams[i][:c, c:], 0.0) for i, (d, h) in enumerate(chains)]
    l0s = [jnp.where(same_blk[INV_BASE], lm, 0.0) for lm in lmats]
    xs = [eye - l0 for l0 in l0s]
    pws = [_bdot(l0, l0) for l0 in l0s]
    span = 2
    while 2 * span < INV_BASE:
        both = [_bdot(jnp.concatenate([xs[i], pws[i]], axis=0), pws[i]) for i in ids]
        xs = [xs[i] + both[i][:c] for i in ids]
        pws = [both[i][c:] for i in ids]
        span *= 2
    xs = [xs[i] + _bdot(xs[i], pws[i]) for i in ids]
    n = INV_BASE
    while n < c:
        pair = same_blk[2 * n] & jnp.logical_not(same_blk[n])
        ts = [_bdot(jnp.where(pair, lmats[i], 0.0), xs[i]) for i in ids]
        xs = [xs[i] - _bdot(xs[i], ts[i]) for i in ids]
        n *= 2
    us = [_bdot(xs[i], from_s[i][:c] + from_v[i][:c]) for i in ids]
    yu = [_bdot(jnp.where(incl[d], grams[i][c:, c:], 0.0), us[i]) for i, (d, h) in enumerate(chains)]
    ds = [_dot_tn(jnp.concatenate([vhs[i], (-us[i]).astype(BF16)], axis=0), kb_end[d][:, sls[h]])
          for i, (d, h) in enumerate(chains)]
    for d in range(2):
        y_refs[d][0] = jnp.concatenate([from_s[i][c:] + from_v[i][c:] - yu[i]
                                        for i, (dd, h) in enumerate(chains) if dd == d], axis=-1)
    for i, (d, h) in enumerate(chains):
        s_scr[d, h] = s_old[i] * p_tot[d][:, sls[h]] + ds[i]

    @pl.when(pl.program_id(1) == pl.num_programs(1) - 1)
    def _():
        sT_ref[0] = s_scr[...]


def _rwkv_scan(r, lw2, kd2, v, kk, bd2, s0, chunk):
    b, t, _ = r.shape
    nc = t // chunk
    in_specs, args = [], []
    for d in range(2):
        tmap = (lambda j: j) if d == 0 else (lambda j: nc - 1 - j)
        s_one = pl.BlockSpec((1, chunk, D_RWKV), lambda i, j, tmap=tmap: (i, tmap(j), 0))
        s_two = pl.BlockSpec((1, 1, chunk, D_RWKV), lambda i, j, tmap=tmap, d=d: (d, i, tmap(j), 0))
        in_specs += [s_one, s_two, s_two, s_one, s_one, s_two]
        args += [r, lw2, kd2, v, kk, bd2]
    s_st = pl.BlockSpec((1, 2, N_RWKV_HEADS, HEAD_DIM, HEAD_DIM), lambda i, j: (i, 0, 0, 0, 0))
    y_specs = tuple(pl.BlockSpec((1, chunk, D_RWKV), lambda i, j, tmap=tmap: (i, tmap(j), 0))
                    for tmap in ((lambda j: j), (lambda j: nc - 1 - j)))
    y_shape = jax.ShapeDtypeStruct((b, t, D_RWKV), F32)
    return pl.pallas_call(
        functools.partial(_scan_kernel, chunk=chunk),
        grid=(b, nc),
        in_specs=in_specs + [s_st],
        out_specs=y_specs + (s_st,),
        out_shape=(y_shape, y_shape, jax.ShapeDtypeStruct(s0.shape, F32)),
        scratch_shapes=[pltpu.VMEM((2, N_RWKV_HEADS, HEAD_DIM, HEAD_DIM), F32)],
        compiler_params=_cparams(("parallel", "arbitrary")),
        name="rwkv_scan",
    )(*args, s0)


def _outproj_kernel(x_ref, at_ref, yf_ref, yb_ref, bonus_ref, gate_ref, lng_ref, lnb_ref, grp_ref, wo_ref,
                    g1_ref, sh_ref, sc_ref, n2_ref, wr_ref, br_ref, x1_ref, hx_ref, cnt_ref):
    grp = grp_ref[...]
    y = yf_ref[0] + yb_ref[0]
    mean = _head_sum(y, grp) * (1.0 / HEAD_DIM)
    yc = y - mean
    var = _head_sum(yc * yc, grp) * (1.0 / HEAD_DIM)
    yn = yc * lax.rsqrt(var + GN_EPS) * lng_ref[...] + lnb_ref[...]
    rw_out = (yn + bonus_ref[0]) * gate_ref[0]
    mix = (_dot(at_ref[0].astype(BF16), wo_ref[0:D_ATTN, :])
           + _dot(rw_out.astype(BF16), wo_ref[D_ATTN:, :]))
    x1 = x_ref[0] + g1_ref[0] * mix
    x1_ref[0] = x1
    ms = jnp.mean(x1 * x1, axis=-1, keepdims=True)
    h2 = x1 * lax.rsqrt(ms + NORM_EPS) * n2_ref[...]
    h2 = h2 * (1.0 + sc_ref[0]) + sh_ref[0]
    hx_ref[0, :, 0:D_MODEL] = h2.astype(BF16)
    logits = _dot3(h2, wr_ref[...]) + br_ref[...]
    lane = lax.broadcasted_iota(jnp.int32, logits.shape, 1)
    neg = -jnp.inf
    big = jnp.int32(1 << 20)
    lc = jnp.where(lane < N_GROUPS, logits, neg)
    mc = jnp.max(lc, axis=-1, keepdims=True)
    g_w = 1.0 / jnp.sum(jnp.exp(lc - mc), axis=-1, keepdims=True)
    g_idx = jnp.min(jnp.where(lc == mc, lane, big), axis=-1, keepdims=True)
    eid = lane - N_GROUPS
    in_grp = (eid >= 0) & (eid < N_EXPERTS) & (lax.shift_right_arithmetic(eid, 2) == g_idx)
    lf = jnp.where(in_grp, logits, neg)
    m1 = jnp.max(lf, axis=-1, keepdims=True)
    i1 = jnp.min(jnp.where(lf == m1, lane, big), axis=-1, keepdims=True)
    lf2 = jnp.where(lane == i1, neg, lf)
    m2 = jnp.max(lf2, axis=-1, keepdims=True)
    i2 = jnp.min(jnp.where(lf2 == m2, lane, big), axis=-1, keepdims=True)
    e2 = jnp.exp(m2 - m1)
    w1 = 1.0 / (1.0 + e2)
    w2 = e2 * w1
    cmb = g_w * (jnp.where(lane == i1, w1, 0.0) + jnp.where(lane == i2, w2, 0.0))
    rec = jnp.where(lane == 0, g_idx.astype(F32), cmb)
    rec_hi, rec_lo = _split2(rec)
    hx_ref[0, :, D_MODEL:D_MODEL + ROUTER_LANES] = rec_hi
    hx_ref[0, :, D_MODEL + ROUTER_LANES:] = rec_lo
    hot = jnp.where((lane == g_idx) & (lane < N_GROUPS), 1.0, 0.0)
    for s in range(cnt_ref.shape[1]):
        part = jnp.sum(hot[s * MOE_TM:(s + 1) * MOE_TM], axis=0, keepdims=True)
        cnt_ref[0, s] = jnp.broadcast_to(part, cnt_ref.shape[2:])


def _out_proj(x3, attn3, yf, yb, bonus, gate, ln_g, ln_b, grp, w_out_b, g1, sh2, sc2, norm2_g, w_r, b_r):
    b, t, _ = x3.shape
    tm = OUT_PROJ_TM if t % OUT_PROJ_TM == 0 else MOE_TM
    bm = g1.shape[0]
    mod_map = (lambda i, j: (i, 0, 0)) if bm > 1 else (lambda i, j: (0, 0, 0))
    tok = lambda w: pl.BlockSpec((1, tm, w), lambda i, j: (i, j, 0))
    full = lambda a: pl.BlockSpec(a.shape, lambda i, j: (0,) * a.ndim)
    mod = pl.BlockSpec((1, 1, D_MODEL), mod_map)
    return pl.pallas_call(
        _outproj_kernel,
        grid=(b, t // tm),
        in_specs=[tok(D_MODEL), tok(D_ATTN), tok(D_RWKV), tok(D_RWKV),
                  tok(D_RWKV), tok(D_RWKV), full(ln_g), full(ln_b), full(grp), full(w_out_b),
                  mod, mod, mod, full(norm2_g), full(w_r), full(b_r)],
        out_specs=(tok(D_MODEL), tok(MOE_W), pl.BlockSpec((1, tm // MOE_TM, 8, LANES), lambda i, j: (i, j, 0, 0))),
        out_shape=(jax.ShapeDtypeStruct((b, t, D_MODEL), F32), jax.ShapeDtypeStruct((b, t, MOE_W), BF16),
                   jax.ShapeDtypeStruct((b, t // MOE_TM, 8, LANES), F32)),
        compiler_params=_cparams(("parallel", "parallel")),
        name="out_proj",
    )(x3, attn3, yf, yb, bonus, gate, ln_g, ln_b, grp, w_out_b, g1, sh2, sc2, norm2_g, w_r, b_r)


def _group_one_hots(rec_hi):
    tm = rec_hi.shape[0]
    sel = ((lax.broadcasted_iota(jnp.int32, (8, LANES), 0) == 0)
           & (lax.broadcasted_iota(jnp.int32, (8, LANES), 1) == 0))
    g_row = _dot_nt(jnp.where(sel, 1.0, 0.0).astype(BF16), rec_hi)[0:1, :]
    g_col = rec_hi.astype(F32)[:, 0:1]
    r = lax.broadcasted_iota(jnp.int32, (tm, tm), 0)
    c = lax.broadcasted_iota(jnp.int32, (tm, tm), 1)
    earlier_same = jnp.where((g_col == g_row) & (r < c), 1.0, 0.0)
    rank_row = jnp.sum(earlier_same, axis=0, keepdims=True)
    rf = r.astype(F32)
    return [jnp.where((g_row == float(g)) & (rank_row == rf), 1.0, 0.0).astype(BF16) for g in range(N_GROUPS)]


def _dispatch_kernel(dst_ref, hx_ref, xs_in_ref, xs_ref, buf, sem, *, tile0):
    del xs_in_ref
    i = pl.program_id(0)
    x = hx_ref[...]
    hots = _group_one_hots(x[:, D_MODEL:D_MODEL + ROUTER_LANES])
    blocks = [_dot(hot, x).astype(BF16) for hot in hots]

    def copy(g, t):
        start = pl.multiple_of(dst_ref[(tile0 + t) * N_GROUPS + g], SEG_ALIGN)
        return pltpu.make_async_copy(buf.at[g], xs_ref.at[pl.ds(start, MOE_TM), :], sem.at[g])

    @pl.when(i > 0)
    def _():
        for g in range(N_GROUPS):
            copy(g, i - 1).wait()

    for g in range(N_GROUPS):
        buf[g] = blocks[g]
        copy(g, i).start()

    @pl.when(i == pl.num_programs(0) - 1)
    def _():
        for g in range(N_GROUPS):
            copy(g, i).wait()


def _moe_dispatch(dst, hx2, xs, tile0):
    n = hx2.shape[0]
    grid_spec = pltpu.PrefetchScalarGridSpec(
        num_scalar_prefetch=1, grid=(n // MOE_TM,),
        in_specs=[pl.BlockSpec((MOE_TM, MOE_W), lambda i, dst: (i, 0)), pl.BlockSpec(memory_space=pl.ANY)],
        out_specs=pl.BlockSpec(memory_space=pl.ANY),
        scratch_shapes=[pltpu.VMEM((N_GROUPS, MOE_TM, MOE_W), BF16), pltpu.SemaphoreType.DMA((N_GROUPS,))])
    return pl.pallas_call(
        functools.partial(_dispatch_kernel, tile0=tile0),
        grid_spec=grid_spec,
        out_shape=jax.ShapeDtypeStruct(xs.shape, xs.dtype),
        input_output_aliases={2: 0},
        compiler_params=_cparams(("arbitrary",)),
        name="moe_dispatch",
    )(dst, hx2, xs)


def _experts_kernel(tg_ref, xs_ref, wg_ref, wu_ref, wd_ref, ys_ref):
    g = tg_ref[pl.program_id(0)]

    @pl.when(g >= N_GROUPS)
    def _():
        ys_ref[...] = jnp.zeros_like(ys_ref)

    @pl.when(g < N_GROUPS)
    def _():
        x = xs_ref[...]
        h = x[:, 0:D_MODEL]
        rec = (x[:, D_MODEL:D_MODEL + ROUTER_LANES].astype(F32) + x[:, D_MODEL + ROUTER_LANES:].astype(F32))
        lane = lax.broadcasted_iota(jnp.int32, rec.shape, 1)
        first = N_GROUPS + EXPERTS_PER_GROUP * g
        scaled = []
        for e in range(EXPERTS_PER_GROUP):
            a = _dot(h, wg_ref[0, e])
            hid = a * jax.nn.sigmoid(a) * _dot(h, wu_ref[0, e])
            c_e = jnp.sum(jnp.where(lane == first + e, rec, 0.0), axis=-1, keepdims=True)
            scaled.append((hid * c_e).astype(BF16))
        ys_ref[...] = _dot(jnp.concatenate(scaled, axis=-1), wd_ref[0]).astype(BF16)


def _moe_experts(tile_group, xs, wg4, wu4, wd4):
    p = xs.shape[0]
    grp_map = lambda nd: (lambda j, tg: (jnp.minimum(tg[j], N_GROUPS - 1),) + (0,) * (nd - 1))
    grid_spec = pltpu.PrefetchScalarGridSpec(
        num_scalar_prefetch=1, grid=(p // MOE_TX,),
        in_specs=[pl.BlockSpec((MOE_TX, MOE_W), lambda j, tg: (j, 0)),
                  pl.BlockSpec((1,) + wg4.shape[1:], grp_map(wg4.ndim)),
                  pl.BlockSpec((1,) + wu4.shape[1:], grp_map(wu4.ndim)),
                  pl.BlockSpec((1,) + wd4.shape[1:], grp_map(wd4.ndim))],
        out_specs=pl.BlockSpec((MOE_TX, D_MODEL), lambda j, tg: (j, 0)))
    return pl.pallas_call(
        _experts_kernel,
        grid_spec=grid_spec,
        out_shape=jax.ShapeDtypeStruct((p, D_MODEL), BF16),
        compiler_params=_cparams(("arbitrary",)),
        name="moe_experts",
    )(tile_group, xs, wg4, wu4, wd4)


def _combine_kernel(dst_ref, rec_ref, x1_ref, g2_ref, ys_ref, o_ref, buf, sem, *, tile0):
    lin = pl.program_id(0) * pl.num_programs(1) + pl.program_id(1)
    n = pl.num_programs(0) * pl.num_programs(1)
    slot = lin % 2

    def copy(g, t, s):
        start = pl.multiple_of(dst_ref[(tile0 + t) * N_GROUPS + g], SEG_ALIGN)
        return pltpu.make_async_copy(ys_ref.at[pl.ds(start, MOE_TM), :], buf.at[s, g], sem.at[s, g])

    @pl.when(lin == 0)
    def _():
        for g in range(N_GROUPS):
            copy(g, 0, 0).start()

    @pl.when(lin + 1 < n)
    def _():
        for g in range(N_GROUPS):
            copy(g, lin + 1, 1 - slot).start()

    hots = _group_one_hots(rec_ref[0])
    for g in range(N_GROUPS):
        copy(g, lin, slot).wait()
    y = _dot_tn(hots[0], buf[slot, 0])
    for g in range(1, N_GROUPS):
        y = y + _dot_tn(hots[g], buf[slot, g])
    o_ref[0] = x1_ref[0] + g2_ref[0] * y


def _moe_combine(dst, hx3, x1, g2, ys, tile0):
    b, t, _ = x1.shape
    bm = g2.shape[0]
    mod_map = (lambda i, j, dst: (i, 0, 0)) if bm > 1 else (lambda i, j, dst: (0, 0, 0))
    rec_blk = D_MODEL // ROUTER_LANES
    grid_spec = pltpu.PrefetchScalarGridSpec(
        num_scalar_prefetch=1, grid=(b, t // MOE_TM),
        in_specs=[pl.BlockSpec((1, MOE_TM, ROUTER_LANES), lambda i, j, dst: (i, j, rec_blk)),
                  pl.BlockSpec((1, MOE_TM, D_MODEL), lambda i, j, dst: (i, j, 0)),
                  pl.BlockSpec((1, 1, D_MODEL), mod_map),
                  pl.BlockSpec(memory_space=pl.ANY)],
        out_specs=pl.BlockSpec((1, MOE_TM, D_MODEL), lambda i, j, dst: (i, j, 0)),
        scratch_shapes=[pltpu.VMEM((2, N_GROUPS, MOE_TM, D_MODEL), BF16),
                        pltpu.SemaphoreType.DMA((2, N_GROUPS))])
    return pl.pallas_call(
        functools.partial(_combine_kernel, tile0=tile0),
        grid_spec=grid_spec,
        out_shape=jax.ShapeDtypeStruct(x1.shape, F32),
        compiler_params=_cparams(("arbitrary", "arbitrary")),
        name="moe_combine",
    )(dst, hx3, x1, g2, ys)


def _moe_plan(cnt, n_rows):
    seg = (cnt + (SEG_ALIGN - 1)) // SEG_ALIGN * SEG_ALIGN
    used = (jnp.sum(seg, axis=0) + (MOE_TX - 1)) // MOE_TX * MOE_TX
    size = used + MOE_TX
    base = jnp.cumsum(size) - size
    dst = base[None, :] + jnp.cumsum(seg, axis=0) - seg
    starts = jnp.arange(n_rows // MOE_TX, dtype=jnp.int32)[:, None] * MOE_TX
    inside = (starts >= base[None, :]) & (starts < (base + used)[None, :])
    tile_group = jnp.where(jnp.any(inside, axis=1), jnp.argmax(inside, axis=1), N_GROUPS)
    return dst.reshape(-1).astype(jnp.int32), tile_group.astype(jnp.int32)


def _moe_rows(n_tokens):
    n_tiles = n_tokens // MOE_TM
    bound = n_tokens + n_tiles * N_GROUPS * (SEG_ALIGN - 1) + N_GROUPS * 2 * MOE_TX
    return (bound + MOE_TX - 1) // MOE_TX * MOE_TX


def _layer(x3, mod6, lp, t_tiles, ctx):
    b, t, _ = x3.shape
    sh1, sc1, g1, sh2, sc2, _ = mod6
    t_tiles = {name: min(size, t) for name, size in t_tiles.items()}
    rope = ctx is not None
    n = b * t
    q5, k4, v4, k, v, rw = _in_proj(x3, sh1, sc1, lp["norm1_g"], lp["w_in_p"], lp["gqk"], lp["grp"],
                                    rope, t_tiles["in_proj"])
    if ctx is not None:
        ctx_k, ctx_v, ctx_state = ctx
        ck = ctx_k.transpose(0, 2, 1, 3)
        cv = ctx_v.transpose(0, 2, 1, 3)
        k4 = jnp.concatenate([k4, ck.astype(BF16)], axis=2)
        v4 = jnp.concatenate([v4, jnp.concatenate([cv, jnp.ones_like(cv)], axis=-1).astype(BF16)], axis=2)
        s0 = ctx_state
    else:
        s0 = jnp.zeros((b, 2, N_RWKV_HEADS, HEAD_DIM, HEAD_DIM), F32)
    attn3 = _attention(q5, k4, v4, t_tiles["attn"])
    r, lw2, kd2, vv, kk, bd2, gate, bonus = _rwkv_prep(
        rw, lp["mu_p"], lp["wd_cat"], lp["w0_cat"], lp["wa_cat"], lp["a0_cat"],
        lp["wg_p"], lp["k_k"], lp["k_a"], lp["r_k"], lp["grp"], t_tiles["prep"])
    yf, yb, s_t = _rwkv_scan(r, lw2, kd2, vv, kk, bd2, s0, t_tiles["chunk"])
    fb, ft = (1, n) if g1.shape[0] == 1 else (b, t)
    flat = lambda a: a.reshape(fb, ft, a.shape[-1])
    x1, hx, cnt = _out_proj(flat(x3), flat(attn3), flat(yf), flat(yb), flat(bonus), flat(gate), lp["ln_g"],
                            lp["ln_b"], lp["grp"], lp["w_out_b"], g1, sh2, sc2, lp["norm2_g"], lp["w_r"], lp["b_r"])
    return (x1, hx, cnt), k.reshape(b, t, N_KV_HEADS, HEAD_DIM), v.reshape(b, t, N_KV_HEADS, HEAD_DIM), s_t


def _moe_both(passes, g2s, lp):
    counts = [p[2][:, :, 0, :N_GROUPS].reshape(-1, N_GROUPS) for p in passes]
    tiles = [c.shape[0] for c in counts]
    n_rows = _moe_rows(sum(tiles) * MOE_TM)
    dst, tile_group = _moe_plan(jnp.concatenate(counts, axis=0).astype(jnp.int32), n_rows)
    xs = jnp.zeros((n_rows, MOE_W), BF16)
    tile0 = 0
    for (x1, hx, _), nt in zip(passes, tiles):
        xs = _moe_dispatch(dst, hx.reshape(-1, MOE_W), xs, tile0)
        tile0 += nt
    ys = _moe_experts(tile_group, xs, lp["wg4"], lp["wu4"], lp["wd4"])
    outs, tile0 = [], 0
    for (x1, hx, _), g2, nt in zip(passes, g2s, tiles):
        outs.append(_moe_combine(dst, hx, x1, g2, ys, tile0))
        tile0 += nt
    return outs


def _block_diag2(w):
    z, l, c = w.shape
    out = jnp.zeros((LANES, z * c), F32)
    for i in range(z):
        out = out.at[i * l:(i + 1) * l, i * c:(i + 1) * c].set(w[i])
    return out


def _layer_params(l, w_in, norm1_g, norm2_g, mu_shift, q_norm_g, k_norm_g, w0, w_lora_up, a0, a_lora_up, g_lora_up,
                  k_k, k_a, r_k, ln_x_g, ln_x_b, w_out, router_c, router_c_b, router_f, router_f_b,
                  exp_gate, exp_up, exp_down):
    lane = np.arange(LANES)
    grp = jnp.asarray((lane[:, None] // HEAD_DIM) == (lane[None, :] // HEAD_DIM), BF16)
    pad_in = D_IN_PAD - w_in.shape[2]
    wd_cat = _block_diag2(w_lora_up[l])
    wa_cat = jnp.roll(_block_diag2(a_lora_up[l]), 2 * DECAY_LORA, axis=0)
    w_r = jnp.zeros((D_MODEL, ROUTER_LANES), F32)
    w_r = w_r.at[:, :N_GROUPS].set(router_c[l]).at[:, N_GROUPS:N_GROUPS + N_EXPERTS].set(router_f[l])
    b_r = jnp.zeros((1, ROUTER_LANES), F32)
    b_r = b_r.at[0, :N_GROUPS].set(router_c_b[l]).at[0, N_GROUPS:N_GROUPS + N_EXPERTS].set(router_f_b[l])

    by_group = lambda w: w.astype(BF16).reshape(N_GROUPS, EXPERTS_PER_GROUP, D_MODEL, D_EXPERT)

    return dict(
        grp=grp,
        norm1_g=norm1_g[l].reshape(1, D_MODEL), norm2_g=norm2_g[l].reshape(1, D_MODEL),
        w_in_p=jnp.pad(w_in[l], ((0, 0), (0, pad_in))).astype(BF16),
        gqk=jnp.concatenate([jnp.tile(q_norm_g[l], N_Q_HEADS), jnp.tile(k_norm_g[l], N_KV_HEADS)]).reshape(1, -1),
        mu_p=jnp.pad(mu_shift[l], ((0, 0), (0, D_RWKV_PAD - D_RWKV_IN))),
        wd_cat=wd_cat, w0_cat=w0[l].reshape(1, 2 * D_RWKV),
        wa_cat=wa_cat, a0_cat=a0[l].reshape(1, 2 * D_RWKV),
        wg_p=jnp.pad(g_lora_up[l], ((0, LANES - GATE_LORA), (0, 0))),
        k_k=k_k[l].reshape(1, D_RWKV), k_a=k_a[l].reshape(1, D_RWKV), r_k=r_k[l].reshape(1, D_RWKV),
        ln_g=ln_x_g[l].reshape(1, D_RWKV), ln_b=ln_x_b[l].reshape(1, D_RWKV),
        w_out_b=w_out[l].astype(BF16), w_r=w_r, b_r=b_r,
        wg4=by_group(exp_gate[l]), wu4=by_group(exp_up[l]),
        wd4=exp_down[l].astype(BF16).reshape(N_GROUPS, EXPERTS_PER_GROUP * D_EXPERT, D_MODEL),
    )


CTX_TILES = dict(in_proj=256, attn=256, prep=256, chunk=128)
SMP_TILES = dict(in_proj=512, attn=256, prep=512, chunk=128)


def kernel(x_prompt, x_sample, cache_k, cache_v, state_rwkv, c, c_ctx, w_mod, b_mod, norm1_g, norm2_g, w_in, mu_shift, q_norm_g, k_norm_g, w0, w_lora_up, a0, a_lora_up, g_lora_up, k_k, k_a, r_k, ln_x_g, ln_x_b, w_out, router_c, router_c_b, router_f, router_f_b, exp_gate, exp_up, exp_down):
    depth = w_mod.shape[0]
    db = x_sample.shape[0]
    y_prompt, y_sample = x_prompt, x_sample
    ks, vs, ss = [], [], []
    cond = jnp.zeros((8, D_MODEL), F32).at[:db].set(c).at[db].set(c_ctx)
    for l in range(depth):
        lp = _layer_params(l, w_in, norm1_g, norm2_g, mu_shift, q_norm_g, k_norm_g, w0, w_lora_up, a0, a_lora_up,
                           g_lora_up, k_k, k_a, r_k, ln_x_g, ln_x_b, w_out, router_c, router_c_b, router_f,
                           router_f_b, exp_gate, exp_up, exp_down)
        mod = _modulation(cond, w_mod[l], b_mod[l])
        mod_s = [mod[:db, i * D_MODEL:(i + 1) * D_MODEL].reshape(db, 1, D_MODEL) for i in range(6)]
        mod_c = [mod[db:db + 1, i * D_MODEL:(i + 1) * D_MODEL].reshape(1, 1, D_MODEL) for i in range(6)]
        pre_c, k_l, v_l, s_l = _layer(y_prompt, mod_c, lp, CTX_TILES, None)
        ks.append(k_l)
        vs.append(v_l)
        ss.append(s_l)
        pre_s, _, _, _ = _layer(y_sample, mod_s, lp, SMP_TILES, (cache_k[:, l], cache_v[:, l], state_rwkv[:, l]))
        out_c, out_s = _moe_both([pre_c, pre_s], [mod_c[5], mod_s[5]], lp)
        y_prompt, y_sample = out_c.reshape(y_prompt.shape), out_s.reshape(y_sample.shape)
    return (y_prompt, y_sample, jnp.stack(ks, axis=1), jnp.stack(vs, axis=1), jnp.stack(ss, axis=1))
```

```python
import functools

import numpy as np
import jax
import jax.numpy as jnp
from jax import lax
from jax.experimental import pallas as pl
from jax.experimental.pallas import tpu as pltpu

F32 = jnp.float32
BF16 = jnp.bfloat16
HIGHEST = lax.Precision.HIGHEST

D_MODEL = 1024
HEAD_DIM = 64
N_Q_HEADS = 8
N_KV_HEADS = 2
GQA_GROUP = N_Q_HEADS // N_KV_HEADS
D_ATTN = N_Q_HEADS * HEAD_DIM
D_KV = N_KV_HEADS * HEAD_DIM
N_RWKV_HEADS = 8
D_RWKV = 512
DECAY_LORA = 32
AAA_LORA = 32
GATE_LORA = 96
D_RWKV_IN = 3 * D_RWKV + 2 * DECAY_LORA + 2 * AAA_LORA + GATE_LORA
D_RWKV_PAD = 1792
D_QKV = D_ATTN + 2 * D_KV
D_IN_PAD = D_QKV + D_RWKV_PAD
N_GROUPS = 4
EXPERTS_PER_GROUP = 4
N_EXPERTS = 16
D_EXPERT = 512
GRID_W = 64
ROPE_THETA = 10000.0
NORM_EPS = 1e-6
GN_EPS = 64e-5
DECAY_SCALE = 0.6065306597
QK_EXP2_SCALE = (HEAD_DIM ** -0.5) * float(np.log2(np.e))
LANES = 128
ROUTER_LANES = 128
VMEM_LIMIT = 56 * 1024 * 1024
ATTN_ROWS = 128
MOE_W = D_MODEL + 2 * ROUTER_LANES
MOE_TM = 256
OUT_PROJ_TM = 512
MOE_SUB = 64
MOE_TX = 512
SEG_ALIGN = 16


def _cparams(sem):
    return pltpu.CompilerParams(dimension_semantics=sem, vmem_limit_bytes=VMEM_LIMIT)


def _dot(a, b, precision=None):
    return jnp.dot(a, b, preferred_element_type=F32, precision=precision)


def _dot_nt(a, b, precision=None):
    return lax.dot_general(a, b, (((1,), (1,)), ((), ())), preferred_element_type=F32, precision=precision)


def _dot_tn(a, b, precision=None):
    return lax.dot_general(a, b, (((0,), (0,)), ((), ())), preferred_element_type=F32, precision=precision)


def _split2(x):
    hi = x.astype(BF16)
    return hi, (x - hi.astype(F32)).astype(BF16)


def _dot3(a, b):
    a_hi, a_lo = _split2(a)
    b_hi, b_lo = _split2(b)
    return _dot(a_hi, b_hi) + (_dot(a_hi, b_lo) + _dot(a_lo, b_hi))


def _head_sum(x, g):
    hi, lo = _split2(x)
    n = x.shape[-1] // LANES
    cols = [slice(j * LANES, (j + 1) * LANES) for j in range(n)]
    return jnp.concatenate([_dot(hi[:, c], g) + _dot(lo[:, c], g) for c in cols], axis=-1)


def _mod_kernel(c_ref, w_ref, b_ref, o_ref):
    c = c_ref[...]
    s = c * jax.nn.sigmoid(c)
    o_ref[...] = _dot(s, w_ref[...], HIGHEST) + b_ref[...]


def _modulation(cond, w_mod, b_mod):
    n = w_mod.shape[1]
    tn = 1024
    return pl.pallas_call(
        _mod_kernel,
        grid=(n // tn,),
        in_specs=[pl.BlockSpec((8, D_MODEL), lambda j: (0, 0)),
                  pl.BlockSpec((D_MODEL, tn), lambda j: (0, j)),
                  pl.BlockSpec((1, tn), lambda j: (0, j))],
        out_specs=pl.BlockSpec((8, tn), lambda j: (0, j)),
        out_shape=jax.ShapeDtypeStruct((8, n), F32),
        compiler_params=_cparams(("arbitrary",)),
        name="mod",
    )(cond, w_mod, b_mod.reshape(1, n))


def _rope_tables(t_len):
    half = HEAD_DIM // 2
    inv = ROPE_THETA ** (-np.arange(0, half, 2, dtype=np.float64) / half)
    t = np.arange(t_len)
    row, col = t // GRID_W, t % GRID_W
    lane = np.arange(LANES)
    i = lane % HEAD_DIM
    pos = np.where((i // half)[None, :] == 0, row[:, None], col[:, None]).astype(np.float64)
    j = i % half
    ang = pos * inv[j % (half // 2)][None, :]
    cos, sin = np.cos(ang), np.sin(ang)
    first = (j < half // 2)[None, :]
    s_up = np.where(first, -sin, 0.0)
    s_dn = np.where(first, 0.0, sin)
    return (jnp.asarray(cos, F32), jnp.asarray(s_up, F32), jnp.asarray(s_dn, F32))


def _inproj_kernel(x_ref, sh_ref, sc_ref, g_ref, w_ref, gqk_ref, grp_ref, *rest, rope):
    if rope:
        cos_ref, sup_ref, sdn_ref, q_ref, k_ref, v_ref, kf_ref, vf_ref, rw_ref = rest
    else:
        q_ref, k_ref, v_ref, kf_ref, vf_ref, rw_ref = rest
    x = x_ref[0]
    ms = jnp.mean(x * x, axis=-1, keepdims=True)
    h = x * lax.rsqrt(ms + NORM_EPS) * g_ref[...]
    h = h * (1.0 + sc_ref[0]) + sh_ref[0]
    proj = _dot(h.astype(BF16), w_ref[...])
    grp = grp_ref[...]
    lo_half = lax.broadcasted_iota(jnp.int32, (x.shape[0], LANES), 1) < HEAD_DIM
    for j in range((D_ATTN + D_KV) // LANES):
        blk = proj[:, j * LANES:(j + 1) * LANES]
        ss = _head_sum(blk * blk, grp) * (1.0 / HEAD_DIM)
        nb = blk * lax.rsqrt(ss + NORM_EPS) * gqk_ref[:, j * LANES:(j + 1) * LANES]
        if rope:
            nb = (nb * cos_ref[...] + pltpu.roll(nb, LANES - 16, 1) * sup_ref[...]
                  + pltpu.roll(nb, 16, 1) * sdn_ref[...])
        if j < D_ATTN // LANES:
            nbq = nb * QK_EXP2_SCALE
            for half in range(2):
                hq = 2 * j + half
                q_ref[0, hq // GQA_GROUP, hq % GQA_GROUP] = nbq[:, half * HEAD_DIM:(half + 1) * HEAD_DIM].astype(BF16)
        else:
            kf_ref[0] = nb
            k_ref[0, 0] = nb[:, :HEAD_DIM].astype(BF16)
            k_ref[0, 1] = nb[:, HEAD_DIM:].astype(BF16)
    vblk = proj[:, D_ATTN + D_KV:D_QKV]
    vf_ref[0] = vblk
    v_ref[0, 0] = jnp.where(lo_half, vblk, 1.0).astype(BF16)
    v_ref[0, 1] = jnp.where(lo_half, pltpu.roll(vblk, HEAD_DIM, 1), 1.0).astype(BF16)
    rw_ref[0] = proj[:, D_QKV:]


def _in_proj(x3, shift, scale, norm_g, w_in_p, gqk, grp, rope, tm):
    b, t, _ = x3.shape
    bm = shift.shape[0]
    mod_map = (lambda i, j: (i, 0, 0)) if bm > 1 else (lambda i, j: (0, 0, 0))
    full = lambda a: pl.BlockSpec(a.shape, lambda i, j: (0,) * a.ndim)
    tok = lambda w: pl.BlockSpec((1, tm, w), lambda i, j: (i, j, 0))
    in_specs = [tok(D_MODEL), pl.BlockSpec((1, 1, D_MODEL), mod_map), pl.BlockSpec((1, 1, D_MODEL), mod_map),
                full(norm_g), full(w_in_p), full(gqk), full(grp)]
    args = [x3, shift, scale, norm_g, w_in_p, gqk, grp]
    if rope:
        in_specs += [pl.BlockSpec((tm, LANES), lambda i, j: (j, 0))] * 3
        args += list(_rope_tables(t))
    out_shape = (jax.ShapeDtypeStruct((b, N_KV_HEADS, GQA_GROUP, t, HEAD_DIM), BF16),
                 jax.ShapeDtypeStruct((b, N_KV_HEADS, t, HEAD_DIM), BF16),
                 jax.ShapeDtypeStruct((b, N_KV_HEADS, t, 2 * HEAD_DIM), BF16),
                 jax.ShapeDtypeStruct((b, t, D_KV), F32), jax.ShapeDtypeStruct((b, t, D_KV), F32),
                 jax.ShapeDtypeStruct((b, t, D_RWKV_PAD), F32))
    out_specs = (pl.BlockSpec((1, N_KV_HEADS, GQA_GROUP, tm, HEAD_DIM), lambda i, j: (i, 0, 0, j, 0)),
                 pl.BlockSpec((1, N_KV_HEADS, tm, HEAD_DIM), lambda i, j: (i, 0, j, 0)),
                 pl.BlockSpec((1, N_KV_HEADS, tm, 2 * HEAD_DIM), lambda i, j: (i, 0, j, 0)),
                 tok(D_KV), tok(D_KV), tok(D_RWKV_PAD))
    return pl.pallas_call(
        functools.partial(_inproj_kernel, rope=rope),
        grid=(b, t // tm), in_specs=in_specs, out_specs=out_specs, out_shape=out_shape,
        compiler_params=_cparams(("parallel", "parallel")),
        name="in_proj_rope" if rope else "in_proj",
    )(*args)


def _attn_kernel(q_ref, k_ref, v_ref, o_ref):
    g, tq, hd = q_ref.shape[2:]
    sub = min(ATTN_ROWS, tq)
    k = k_ref[0, 0]
    v = v_ref[0, 0]
    slabs = [slice(i * sub, (i + 1) * sub) for i in range(tq // sub)]
    qs = [q_ref[0, 0, :, sl, :].reshape(g * sub, hd) for sl in slabs]
    ss = [_dot_nt(q, k) for q in qs]
    ps = [jnp.exp2(s - jnp.max(s, axis=-1, keepdims=True)).astype(BF16) for s in ss]
    for sl, p in zip(slabs, ps):
        o = _dot(p, v)
        o = o[:, :hd] / pltpu.roll(o, hd, 1)[:, :hd]
        o_ref[0, sl, :] = jnp.concatenate([o[i * sub:(i + 1) * sub] for i in range(g)], axis=-1)


def _attention(q5, k4, v4, tq):
    b, hk, g, t, hd = q5.shape
    tk = k4.shape[2]
    return pl.pallas_call(
        _attn_kernel,
        grid=(b, hk, t // tq),
        in_specs=[pl.BlockSpec((1, 1, g, tq, hd), lambda i, j, l: (i, j, 0, l, 0)),
                  pl.BlockSpec((1, 1, tk, hd), lambda i, j, l: (i, j, 0, 0)),
                  pl.BlockSpec((1, 1, tk, 2 * hd), lambda i, j, l: (i, j, 0, 0))],
        out_specs=pl.BlockSpec((1, tq, g * hd), lambda i, j, l: (i, l, j)),
        out_shape=jax.ShapeDtypeStruct((b, t, hk * g * hd), F32),
        compiler_params=_cparams(("parallel", "parallel", "arbitrary")),
        name="attn",
    )(q5, k4, v4)


def _prep_kernel(rw_ref, hp_ref, hn_ref, mu_ref, wd_ref, w0_ref, wa_ref, a0_ref, wg_ref, kk_ref, ka_ref, rk_ref,
                 grp_ref, r_o, lw_o, kd_o, v_o, kk_o, bd_o, g_o, bonus_o):
    i = pl.program_id(1)
    n = pl.num_programs(1)
    cur = rw_ref[0]
    tt = cur.shape[0]
    rid = lax.broadcasted_iota(jnp.int32, cur.shape, 0)
    prev_row = jnp.where(i > 0, hp_ref[0, 7:8, :], 0.0)
    next_row = jnp.where(i < n - 1, hn_ref[0, 0:1, :], 0.0)
    prev = jnp.where(rid == 0, prev_row, pltpu.roll(cur, 1, 0))
    nxt = jnp.where(rid == tt - 1, next_row, pltpu.roll(cur, tt - 1, 0))
    p = cur + mu_ref[0:1, :] * (prev - cur) + mu_ref[1:2, :] * (nxt - cur)
    r = p[:, 0:D_RWKV]
    k = p[:, D_RWKV:2 * D_RWKV]
    v = p[:, 2 * D_RWKV:3 * D_RWKV]
    lo = p[:, 3 * D_RWKV:3 * D_RWKV + LANES]
    gd = p[:, 3 * D_RWKV + LANES:]
    grp = grp_ref[...]
    wlog = _dot3(jnp.tanh(lo), wd_ref[...]) + w0_ref[...]
    alog = _dot3(lo, wa_ref[...]) + a0_ref[...]
    g_o[0] = _dot3(jax.nn.sigmoid(gd), wg_ref[...])
    kx = k * kk_ref[...]
    kk = kx * lax.rsqrt(_head_sum(kx * kx, grp) + 1e-12)
    r_o[0] = r
    v_o[0] = v
    kk_o[0] = kk
    bonus_o[0] = _head_sum(r * k * rk_ref[...], grp) * v
    for z in range(2):
        a = jax.nn.sigmoid(alog[:, z * D_RWKV:(z + 1) * D_RWKV])
        lw_o[z, 0] = -DECAY_SCALE * jax.nn.sigmoid(wlog[:, z * D_RWKV:(z + 1) * D_RWKV])
        kd_o[z, 0] = k * (1.0 + (a - 1.0) * ka_ref[...])
        bd_o[z, 0] = kk * a


def _rwkv_prep(rw3, mu_p, wd_cat, w0_cat, wa_cat, a0_cat, wg_p, k_k, k_a, r_k, grp, tt):
    b, t, _ = rw3.shape
    nt = t // tt
    hb = tt // 8
    one = jax.ShapeDtypeStruct((b, t, D_RWKV), F32)
    two = jax.ShapeDtypeStruct((2, b, t, D_RWKV), F32)
    s_one = pl.BlockSpec((1, tt, D_RWKV), lambda i, j: (i, j, 0))
    s_two = pl.BlockSpec((2, 1, tt, D_RWKV), lambda i, j: (0, i, j, 0))
    full = lambda a: pl.BlockSpec(a.shape, lambda i, j: (0,) * a.ndim)
    consts = [mu_p, wd_cat, w0_cat, wa_cat, a0_cat, wg_p, k_k, k_a, r_k, grp]
    return pl.pallas_call(
        _prep_kernel,
        grid=(b, nt),
        in_specs=[pl.BlockSpec((1, tt, D_RWKV_PAD), lambda i, j: (i, j, 0)),
                  pl.BlockSpec((1, 8, D_RWKV_PAD), lambda i, j: (i, jnp.maximum(j * hb - 1, 0), 0)),
                  pl.BlockSpec((1, 8, D_RWKV_PAD), lambda i, j: (i, jnp.minimum((j + 1) * hb, t // 8 - 1), 0))]
                 + [full(a) for a in consts],
        out_specs=(s_one, s_two, s_two, s_one, s_one, s_two, s_one, s_one),
        out_shape=(one, two, two, one, one, two, one, one),
        compiler_params=_cparams(("parallel", "parallel")),
        name="rwkv_prep",
    )(rw3, rw3, rw3, *consts)


INV_BASE = 16


def _bdot(a, b):
    return _dot(a.astype(BF16), b.astype(BF16))


def _scan_kernel(*refs, chunk):
    ins, (s0_ref, yf_ref, yb_ref, sT_ref, s_scr) = (refs[0:6], refs[6:12]), refs[12:]
    y_refs = (yf_ref, yb_ref)
    c = chunk

    @pl.when(pl.program_id(1) == 0)
    def _():
        s_scr[...] = s0_ref[0]

    row = lax.broadcasted_iota(jnp.int32, (c, c), 0)
    col = lax.broadcasted_iota(jnp.int32, (c, c), 1)
    eye = (row == col).astype(F32)
    same_blk = {}
    n = INV_BASE
    while n <= c:
        sh = jnp.int32(n.bit_length() - 1)
        same_blk[n] = lax.shift_right_logical(row, sh) == lax.shift_right_logical(col, sh)
        n *= 2
    heads = range(N_RWKV_HEADS)
    sls = [slice(h * HEAD_DIM, (h + 1) * HEAD_DIM) for h in heads]

    incl, strict, ar_abs, ar_mid, kb_inv, kb_end, v, p_tot = [], [], [], [], [], [], [], []
    for d, (r_ref, lw_ref, k_ref, v_ref, a_ref, b_ref) in enumerate(ins):
        incl.append(row >= col if d == 0 else row <= col)
        strict.append(row > col if d == 0 else row < col)
        lw = lw_ref[0, 0]
        lw_hi = lw.astype(BF16)
        lw_mid, lw_lo = _split2(lw - lw_hi.astype(F32))
        inclb = jnp.where(incl[d], 1.0, 0.0).astype(BF16)
        cl = _dot(inclb, lw_hi) + (_dot(inclb, lw_mid) + _dot(inclb, lw_lo))
        tot = jnp.sum(lw, axis=0, keepdims=True)
        mid = 0.5 * tot
        e_inv = jnp.exp(mid - cl)
        e_end = jnp.exp(tot - cl)
        s_mid = jnp.exp(-mid)
        r_abs = r_ref[0] * jnp.exp(cl)
        a_abs = a_ref[0] * jnp.exp(cl - lw)
        ar_abs.append(jnp.concatenate([a_abs, r_abs], axis=0).astype(BF16))
        ar_mid.append(jnp.concatenate([a_abs * s_mid, r_abs * s_mid], axis=0).astype(BF16))
        kb_inv.append(jnp.concatenate([k_ref[0, 0] * e_inv, b_ref[0, 0] * e_inv], axis=0).astype(BF16))
        kb_end.append(jnp.concatenate([k_ref[0, 0] * e_end, b_ref[0, 0] * e_end], axis=0).astype(BF16))
        v.append(v_ref[0].astype(BF16))
        p_tot.append(jnp.exp(tot))

    chains = [(d, h) for d in range(2) for h in heads]
    ids = range(len(chains))
    s_old = [s_scr[d, h] for d, h in chains]
    vhs = [v[d][:, sls[h]] for d, h in chains]
    grams = [_dot_nt(ar_mid[d][:, sls[h]], kb_inv[d][:, sls[h]]) for d, h in chains]
    from_s = [_dot_nt(ar_abs[d][:, sls[h]], s_old[i].astype(BF16)) for i, (d, h) in enumerate(chains)]
    masked = [jnp.concatenate([jnp.where(strict[d], grams[i][:c, :c], 0.0),
                               jnp.where(incl[d], grams[i][c:, :c], 0.0)], axis=0) for i, (d, h) in enumerate(chains)]
    from_v = [_bdot(masked[i], vhs[i]) for i in ids]
    lmats = [jnp.where(strict[d], grams[i][:c, c:], 0.0) for i, (d, h) in enumerate(chains)]
    l0s = [jnp.where(same_blk[INV_BASE], lm, 0.0) for lm in lmats]
    xs = [eye - l0 for l0 in l0s]
    pws = [_bdot(l0, l0) for l0 in l0s]
    span = 2
    while 2 * span < INV_BASE:
        both = [_bdot(jnp.concatenate([xs[i], pws[i]], axis=0), pws[i]) for i in ids]
        xs = [xs[i] + both[i][:c] for i in ids]
        pws = [both[i][c:] for i in ids]
        span *= 2
    xs = [xs[i] + _bdot(xs[i], pws[i]) for i in ids]
    n = INV_BASE
    while n < c:
        pair = same_blk[2 * n] & jnp.logical_not(same_blk[n])
        ts = [_bdot(jnp.where(pair, lmats[i], 0.0), xs[i]) for i in ids]
        xs = [xs[i] - _bdot(xs[i], ts[i]) for i in ids]
        n *= 2
    us = [_bdot(xs[i], from_s[i][:c] + from_v[i][:c]) for i in ids]
    yu = [_bdot(jnp.where(incl[d], grams[i][c:, c:], 0.0), us[i]) for i, (d, h) in enumerate(chains)]
    ds = [_dot_tn(jnp.concatenate([vhs[i], (-us[i]).astype(BF16)], axis=0), kb_end[d][:, sls[h]])
          for i, (d, h) in enumerate(chains)]
    for d in range(2):
        y_refs[d][0] = jnp.concatenate([from_s[i][c:] + from_v[i][c:] - yu[i]
                                        for i, (dd, h) in enumerate(chains) if dd == d], axis=-1)
    for i, (d, h) in enumerate(chains):
        s_scr[d, h] = s_old[i] * p_tot[d][:, sls[h]] + ds[i]

    @pl.when(pl.program_id(1) == pl.num_programs(1) - 1)
    def _():
        sT_ref[0] = s_scr[...]


def _rwkv_scan(r, lw2, kd2, v, kk, bd2, s0, chunk):
    b, t, _ = r.shape
    nc = t // chunk
    in_specs, args = [], []
    for d in range(2):
        tmap = (lambda j: j) if d == 0 else (lambda j: nc - 1 - j)
        s_one = pl.BlockSpec((1, chunk, D_RWKV), lambda i, j, tmap=tmap: (i, tmap(j), 0))
        s_two = pl.BlockSpec((1, 1, chunk, D_RWKV), lambda i, j, tmap=tmap, d=d: (d, i, tmap(j), 0))
        in_specs += [s_one, s_two, s_two, s_one, s_one, s_two]
        args += [r, lw2, kd2, v, kk, bd2]
    s_st = pl.BlockSpec((1, 2, N_RWKV_HEADS, HEAD_DIM, HEAD_DIM), lambda i, j: (i, 0, 0, 0, 0))
    y_specs = tuple(pl.BlockSpec((1, chunk, D_RWKV), lambda i, j, tmap=tmap: (i, tmap(j), 0))
                    for tmap in ((lambda j: j), (lambda j: nc - 1 - j)))
    y_shape = jax.ShapeDtypeStruct((b, t, D_RWKV), F32)
    return pl.pallas_call(
        functools.partial(_scan_kernel, chunk=chunk),
        grid=(b, nc),
        in_specs=in_specs + [s_st],
        out_specs=y_specs + (s_st,),
        out_shape=(y_shape, y_shape, jax.ShapeDtypeStruct(s0.shape, F32)),
        scratch_shapes=[pltpu.VMEM((2, N_RWKV_HEADS, HEAD_DIM, HEAD_DIM), F32)],
        compiler_params=_cparams(("parallel", "arbitrary")),
        name="rwkv_scan",
    )(*args, s0)


def _outproj_kernel(x_ref, at_ref, yf_ref, yb_ref, bonus_ref, gate_ref, lng_ref, lnb_ref, grp_ref, wo_ref,
                    g1_ref, sh_ref, sc_ref, n2_ref, wr_ref, br_ref, x1_ref, hx_ref, cnt_ref):
    grp = grp_ref[...]
    y = yf_ref[0] + yb_ref[0]
    mean = _head_sum(y, grp) * (1.0 / HEAD_DIM)
    yc = y - mean
    var = _head_sum(yc * yc, grp) * (1.0 / HEAD_DIM)
    yn = yc * lax.rsqrt(var + GN_EPS) * lng_ref[...] + lnb_ref[...]
    rw_out = (yn + bonus_ref[0]) * gate_ref[0]
    mix = (_dot(at_ref[0].astype(BF16), wo_ref[0:D_ATTN, :])
           + _dot(rw_out.astype(BF16), wo_ref[D_ATTN:, :]))
    x1 = x_ref[0] + g1_ref[0] * mix
    x1_ref[0] = x1
    ms = jnp.mean(x1 * x1, axis=-1, keepdims=True)
    h2 = x1 * lax.rsqrt(ms + NORM_EPS) * n2_ref[...]
    h2 = h2 * (1.0 + sc_ref[0]) + sh_ref[0]
    hx_ref[0, :, 0:D_MODEL] = h2.astype(BF16)
    logits = _dot3(h2, wr_ref[...]) + br_ref[...]
    lane = lax.broadcasted_iota(jnp.int32, logits.shape, 1)
    neg = -jnp.inf
    big = jnp.int32(1 << 20)
    lc = jnp.where(lane < N_GROUPS, logits, neg)
    mc = jnp.max(lc, axis=-1, keepdims=True)
    g_w = 1.0 / jnp.sum(jnp.exp(lc - mc), axis=-1, keepdims=True)
    g_idx = jnp.min(jnp.where(lc == mc, lane, big), axis=-1, keepdims=True)
    eid = lane - N_GROUPS
    in_grp = (eid >= 0) & (eid < N_EXPERTS) & (lax.shift_right_arithmetic(eid, 2) == g_idx)
    lf = jnp.where(in_grp, logits, neg)
    m1 = jnp.max(lf, axis=-1, keepdims=True)
    i1 = jnp.min(jnp.where(lf == m1, lane, big), axis=-1, keepdims=True)
    lf2 = jnp.where(lane == i1, neg, lf)
    m2 = jnp.max(lf2, axis=-1, keepdims=True)
    i2 = jnp.min(jnp.where(lf2 == m2, lane, big), axis=-1, keepdims=True)
    e2 = jnp.exp(m2 - m1)
    w1 = 1.0 / (1.0 + e2)
    w2 = e2 * w1
    cmb = g_w * (jnp.where(lane == i1, w1, 0.0) + jnp.where(lane == i2, w2, 0.0))
    rec = jnp.where(lane == 0, g_idx.astype(F32), cmb)
    rec_hi, rec_lo = _split2(rec)
    hx_ref[0, :, D_MODEL:D_MODEL + ROUTER_LANES] = rec_hi
    hx_ref[0, :, D_MODEL + ROUTER_LANES:] = rec_lo
    hot = jnp.where((lane == g_idx) & (lane < N_GROUPS), 1.0, 0.0)
    for s in range(cnt_ref.shape[1]):
        part = jnp.sum(hot[s * MOE_TM:(s + 1) * MOE_TM], axis=0, keepdims=True)
        cnt_ref[0, s] = jnp.broadcast_to(part, cnt_ref.shape[2:])


def _out_proj(x3, attn3, yf, yb, bonus, gate, ln_g, ln_b, grp, w_out_b, g1, sh2, sc2, norm2_g, w_r, b_r):
    b, t, _ = x3.shape
    tm = OUT_PROJ_TM if t % OUT_PROJ_TM == 0 else MOE_TM
    bm = g1.shape[0]
    mod_map = (lambda i, j: (i, 0, 0)) if bm > 1 else (lambda i, j: (0, 0, 0))
    tok = lambda w: pl.BlockSpec((1, tm, w), lambda i, j: (i, j, 0))
    full = lambda a: pl.BlockSpec(a.shape, lambda i, j: (0,) * a.ndim)
    mod = pl.BlockSpec((1, 1, D_MODEL), mod_map)
    return pl.pallas_call(
        _outproj_kernel,
        grid=(b, t // tm),
        in_specs=[tok(D_MODEL), tok(D_ATTN), tok(D_RWKV), tok(D_RWKV),
                  tok(D_RWKV), tok(D_RWKV), full(ln_g), full(ln_b), full(grp), full(w_out_b),
                  mod, mod, mod, full(norm2_g), full(w_r), full(b_r)],
        out_specs=(tok(D_MODEL), tok(MOE_W), pl.BlockSpec((1, tm // MOE_TM, 8, LANES), lambda i, j: (i, j, 0, 0))),
        out_shape=(jax.ShapeDtypeStruct((b, t, D_MODEL), F32), jax.ShapeDtypeStruct((b, t, MOE_W), BF16),
                   jax.ShapeDtypeStruct((b, t // MOE_TM, 8, LANES), F32)),
        compiler_params=_cparams(("parallel", "parallel")),
        name="out_proj",
    )(x3, attn3, yf, yb, bonus, gate, ln_g, ln_b, grp, w_out_b, g1, sh2, sc2, norm2_g, w_r, b_r)


def _group_one_hots(rec_hi):
    tm = rec_hi.shape[0]
    sel = ((lax.broadcasted_iota(jnp.int32, (8, LANES), 0) == 0)
           & (lax.broadcasted_iota(jnp.int32, (8, LANES), 1) == 0))
    g_row = _dot_nt(jnp.where(sel, 1.0, 0.0).astype(BF16), rec_hi)[0:1, :]
    g_col = rec_hi.astype(F32)[:, 0:1]
    r = lax.broadcasted_iota(jnp.int32, (tm, tm), 0)
    c = lax.broadcasted_iota(jnp.int32, (tm, tm), 1)
    earlier_same = jnp.where((g_col == g_row) & (r < c), 1.0, 0.0)
    rank_row = jnp.sum(earlier_same, axis=0, keepdims=True)
    rf = r.astype(F32)
    return [jnp.where((g_row == float(g)) & (rank_row == rf), 1.0, 0.0).astype(BF16) for g in range(N_GROUPS)]


def _plan_entry(plan_ref, tile, g):
    at = (tile * N_GROUPS + g) * 2
    return pl.multiple_of(plan_ref[at], SEG_ALIGN), plan_ref[at + 1]


def _dispatch_kernel(plan_ref, hx_ref, xs_in_ref, xs_ref, buf, sem, *, tile0):
    del xs_in_ref
    i = pl.program_id(0)
    x = hx_ref[...]
    hots = _group_one_hots(x[:, D_MODEL:D_MODEL + ROUTER_LANES])
    blocks = [_dot(hot, x).astype(BF16) for hot in hots]

    def for_each_copy(t, fn):
        for g in range(N_GROUPS):
            start, nsub = _plan_entry(plan_ref, tile0 + t, g)
            for s in range(MOE_TM // MOE_SUB):
                rows = pl.ds(s * MOE_SUB, MOE_SUB)

                @pl.when(s < nsub)
                def _():
                    fn(pltpu.make_async_copy(buf.at[g, rows, :],
                                             xs_ref.at[pl.ds(start + s * MOE_SUB, MOE_SUB), :], sem.at[g]))

    @pl.when(i > 0)
    def _():
        for_each_copy(i - 1, lambda cp: cp.wait())

    for g in range(N_GROUPS):
        buf[g] = blocks[g]
    for_each_copy(i, lambda cp: cp.start())

    @pl.when(i == pl.num_programs(0) - 1)
    def _():
        for_each_copy(i, lambda cp: cp.wait())


def _moe_dispatch(dst, hx2, xs, tile0):
    n = hx2.shape[0]
    grid_spec = pltpu.PrefetchScalarGridSpec(
        num_scalar_prefetch=1, grid=(n // MOE_TM,),
        in_specs=[pl.BlockSpec((MOE_TM, MOE_W), lambda i, dst: (i, 0)), pl.BlockSpec(memory_space=pl.ANY)],
        out_specs=pl.BlockSpec(memory_space=pl.ANY),
        scratch_shapes=[pltpu.VMEM((N_GROUPS, MOE_TM, MOE_W), BF16), pltpu.SemaphoreType.DMA((N_GROUPS,))])
    return pl.pallas_call(
        functools.partial(_dispatch_kernel, tile0=tile0),
        grid_spec=grid_spec,
        out_shape=jax.ShapeDtypeStruct(xs.shape, xs.dtype),
        input_output_aliases={2: 0},
        compiler_params=_cparams(("arbitrary",)),
        name="moe_dispatch",
    )(dst, hx2, xs)


def _experts_kernel(tg_ref, xs_ref, wg_ref, wu_ref, wd_ref, ys_ref):
    g = tg_ref[pl.program_id(0)]

    @pl.when(g >= N_GROUPS)
    def _():
        ys_ref[...] = jnp.zeros_like(ys_ref)

    @pl.when(g < N_GROUPS)
    def _():
        x = xs_ref[...]
        h = x[:, 0:D_MODEL]
        rec = (x[:, D_MODEL:D_MODEL + ROUTER_LANES].astype(F32) + x[:, D_MODEL + ROUTER_LANES:].astype(F32))
        lane = lax.broadcasted_iota(jnp.int32, rec.shape, 1)
        first = N_GROUPS + EXPERTS_PER_GROUP * g
        scaled = []
        for e in range(EXPERTS_PER_GROUP):
            a = _dot(h, wg_ref[0, e])
            hid = a * jax.nn.sigmoid(a) * _dot(h, wu_ref[0, e])
            c_e = jnp.sum(jnp.where(lane == first + e, rec, 0.0), axis=-1, keepdims=True)
            scaled.append((hid * c_e).astype(BF16))
        ys_ref[...] = _dot(jnp.concatenate(scaled, axis=-1), wd_ref[0]).astype(BF16)


def _moe_experts(tile_group, xs, wg4, wu4, wd4):
    p = xs.shape[0]
    grp_map = lambda nd: (lambda j, tg: (jnp.minimum(tg[j], N_GROUPS - 1),) + (0,) * (nd - 1))
    grid_spec = pltpu.PrefetchScalarGridSpec(
        num_scalar_prefetch=1, grid=(p // MOE_TX,),
        in_specs=[pl.BlockSpec((MOE_TX, MOE_W), lambda j, tg: (j, 0)),
                  pl.BlockSpec((1,) + wg4.shape[1:], grp_map(wg4.ndim)),
                  pl.BlockSpec((1,) + wu4.shape[1:], grp_map(wu4.ndim)),
                  pl.BlockSpec((1,) + wd4.shape[1:], grp_map(wd4.ndim))],
        out_specs=pl.BlockSpec((MOE_TX, D_MODEL), lambda j, tg: (j, 0)))
    return pl.pallas_call(
        _experts_kernel,
        grid_spec=grid_spec,
        out_shape=jax.ShapeDtypeStruct((p, D_MODEL), BF16),
        compiler_params=_cparams(("arbitrary",)),
        name="moe_experts",
    )(tile_group, xs, wg4, wu4, wd4)


def _combine_kernel(plan_ref, rec_ref, x1_ref, g2_ref, ys_ref, o_ref, buf, sem, *, tile0):
    lin = pl.program_id(0) * pl.num_programs(1) + pl.program_id(1)
    n = pl.num_programs(0) * pl.num_programs(1)
    slot = lin % 2

    def for_each_copy(t, slot_t, fn):
        for g in range(N_GROUPS):
            start, nsub = _plan_entry(plan_ref, tile0 + t, g)
            for s in range(MOE_TM // MOE_SUB):
                rows = pl.ds(s * MOE_SUB, MOE_SUB)

                @pl.when(s < nsub)
                def _():
                    fn(pltpu.make_async_copy(ys_ref.at[pl.ds(start + s * MOE_SUB, MOE_SUB), :],
                                             buf.at[slot_t, g, rows, :], sem.at[slot_t, g]))

    @pl.when(lin == 0)
    def _():
        buf[...] = jnp.zeros_like(buf)
        for_each_copy(0, 0, lambda cp: cp.start())

    @pl.when(lin + 1 < n)
    def _():
        for_each_copy(lin + 1, 1 - slot, lambda cp: cp.start())

    hots = _group_one_hots(rec_ref[0])
    for_each_copy(lin, slot, lambda cp: cp.wait())
    y = _dot_tn(hots[0], buf[slot, 0])
    for g in range(1, N_GROUPS):
        y = y + _dot_tn(hots[g], buf[slot, g])
    o_ref[0] = x1_ref[0] + g2_ref[0] * y


def _moe_combine(dst, hx3, x1, g2, ys, tile0):
    b, t, _ = x1.shape
    bm = g2.shape[0]
    mod_map = (lambda i, j, dst: (i, 0, 0)) if bm > 1 else (lambda i, j, dst: (0, 0, 0))
    rec_blk = D_MODEL // ROUTER_LANES
    grid_spec = pltpu.PrefetchScalarGridSpec(
        num_scalar_prefetch=1, grid=(b, t // MOE_TM),
        in_specs=[pl.BlockSpec((1, MOE_TM, ROUTER_LANES), lambda i, j, dst: (i, j, rec_blk)),
                  pl.BlockSpec((1, MOE_TM, D_MODEL), lambda i, j, dst: (i, j, 0)),
                  pl.BlockSpec((1, 1, D_MODEL), mod_map),
                  pl.BlockSpec(memory_space=pl.ANY)],
        out_specs=pl.BlockSpec((1, MOE_TM, D_MODEL), lambda i, j, dst: (i, j, 0)),
        scratch_shapes=[pltpu.VMEM((2, N_GROUPS, MOE_TM, D_MODEL), BF16),
                        pltpu.SemaphoreType.DMA((2, N_GROUPS))])
    return pl.pallas_call(
        functools.partial(_combine_kernel, tile0=tile0),
        grid_spec=grid_spec,
        out_shape=jax.ShapeDtypeStruct(x1.shape, F32),
        compiler_params=_cparams(("arbitrary", "arbitrary")),
        name="moe_combine",
    )(dst, hx3, x1, g2, ys)


def _moe_plan(cnt, n_rows):
    seg = (cnt + (SEG_ALIGN - 1)) // SEG_ALIGN * SEG_ALIGN
    used = (jnp.sum(seg, axis=0) + (MOE_TX - 1)) // MOE_TX * MOE_TX
    size = used + MOE_TX
    base = jnp.cumsum(size) - size
    dst = base[None, :] + jnp.cumsum(seg, axis=0) - seg
    nsub = (cnt + (MOE_SUB - 1)) // MOE_SUB
    starts = jnp.arange(n_rows // MOE_TX, dtype=jnp.int32)[:, None] * MOE_TX
    inside = (starts >= base[None, :]) & (starts < (base + used)[None, :])
    tile_group = jnp.where(jnp.any(inside, axis=1), jnp.argmax(inside, axis=1), N_GROUPS)
    plan = jnp.stack([dst, nsub], axis=-1).reshape(-1)
    return plan.astype(jnp.int32), tile_group.astype(jnp.int32)


def _moe_rows(n_tokens):
    n_tiles = n_tokens // MOE_TM
    bound = n_tokens + n_tiles * N_GROUPS * (SEG_ALIGN - 1) + N_GROUPS * 2 * MOE_TX
    return (bound + MOE_TX - 1) // MOE_TX * MOE_TX


def _layer(x3, mod6, lp, t_tiles, ctx):
    b, t, _ = x3.shape
    sh1, sc1, g1, sh2, sc2, _ = mod6
    t_tiles = {name: min(size, t) for name, size in t_tiles.items()}
    rope = ctx is not None
    n = b * t
    q5, k4, v4, k, v, rw = _in_proj(x3, sh1, sc1, lp["norm1_g"], lp["w_in_p"], lp["gqk"], lp["grp"],
                                    rope, t_tiles["in_proj"])
    if ctx is not None:
        ctx_k, ctx_v, ctx_state = ctx
        ck = ctx_k.transpose(0, 2, 1, 3)
        cv = ctx_v.transpose(0, 2, 1, 3)
        k4 = jnp.concatenate([k4, ck.astype(BF16)], axis=2)
        v4 = jnp.concatenate([v4, jnp.concatenate([cv, jnp.ones_like(cv)], axis=-1).astype(BF16)], axis=2)
        s0 = ctx_state
    else:
        s0 = jnp.zeros((b, 2, N_RWKV_HEADS, HEAD_DIM, HEAD_DIM), F32)
    attn3 = _attention(q5, k4, v4, t_tiles["attn"])
    r, lw2, kd2, vv, kk, bd2, gate, bonus = _rwkv_prep(
        rw, lp["mu_p"], lp["wd_cat"], lp["w0_cat"], lp["wa_cat"], lp["a0_cat"],
        lp["wg_p"], lp["k_k"], lp["k_a"], lp["r_k"], lp["grp"], t_tiles["prep"])
    yf, yb, s_t = _rwkv_scan(r, lw2, kd2, vv, kk, bd2, s0, t_tiles["chunk"])
    fb, ft = (1, n) if g1.shape[0] == 1 else (b, t)
    flat = lambda a: a.reshape(fb, ft, a.shape[-1])
    x1, hx, cnt = _out_proj(flat(x3), flat(attn3), flat(yf), flat(yb), flat(bonus), flat(gate), lp["ln_g"],
                            lp["ln_b"], lp["grp"], lp["w_out_b"], g1, sh2, sc2, lp["norm2_g"], lp["w_r"], lp["b_r"])
    return (x1, hx, cnt), k.reshape(b, t, N_KV_HEADS, HEAD_DIM), v.reshape(b, t, N_KV_HEADS, HEAD_DIM), s_t


def _moe_both(passes, g2s, lp):
    counts = [p[2][:, :, 0, :N_GROUPS].reshape(-1, N_GROUPS) for p in passes]
    tiles = [c.shape[0] for c in counts]
    n_rows = _moe_rows(sum(tiles) * MOE_TM)
    dst, tile_group = _moe_plan(jnp.concatenate(counts, axis=0).astype(jnp.int32), n_rows)
    xs = jnp.zeros((n_rows, MOE_W), BF16)
    tile0 = 0
    for (x1, hx, _), nt in zip(passes, tiles):
        xs = _moe_dispatch(dst, hx.reshape(-1, MOE_W), xs, tile0)
        tile0 += nt
    ys = _moe_experts(tile_group, xs, lp["wg4"], lp["wu4"], lp["wd4"])
    outs, tile0 = [], 0
    for (x1, hx, _), g2, nt in zip(passes, g2s, tiles):
        outs.append(_moe_combine(dst, hx, x1, g2, ys, tile0))
        tile0 += nt
    return outs


def _block_diag2(w):
    z, l, c = w.shape
    out = jnp.zeros((LANES, z * c), F32)
    for i in range(z):
        out = out.at[i * l:(i + 1) * l, i * c:(i + 1) * c].set(w[i])
    return out


def _layer_params(l, w_in, norm1_g, norm2_g, mu_shift, q_norm_g, k_norm_g, w0, w_lora_up, a0, a_lora_up, g_lora_up,
                  k_k, k_a, r_k, ln_x_g, ln_x_b, w_out, router_c, router_c_b, router_f, router_f_b,
                  exp_gate, exp_up, exp_down):
    lane = np.arange(LANES)
    grp = jnp.asarray((lane[:, None] // HEAD_DIM) == (lane[None, :] // HEAD_DIM), BF16)
    pad_in = D_IN_PAD - w_in.shape[2]
    wd_cat = _block_diag2(w_lora_up[l])
    wa_cat = jnp.roll(_block_diag2(a_lora_up[l]), 2 * DECAY_LORA, axis=0)
    w_r = jnp.zeros((D_MODEL, ROUTER_LANES), F32)
    w_r = w_r.at[:, :N_GROUPS].set(router_c[l]).at[:, N_GROUPS:N_GROUPS + N_EXPERTS].set(router_f[l])
    b_r = jnp.zeros((1, ROUTER_LANES), F32)
    b_r = b_r.at[0, :N_GROUPS].set(router_c_b[l]).at[0, N_GROUPS:N_GROUPS + N_EXPERTS].set(router_f_b[l])

    by_group = lambda w: w.astype(BF16).reshape(N_GROUPS, EXPERTS_PER_GROUP, D_MODEL, D_EXPERT)

    return dict(
        grp=grp,
        norm1_g=norm1_g[l].reshape(1, D_MODEL), norm2_g=norm2_g[l].reshape(1, D_MODEL),
        w_in_p=jnp.pad(w_in[l], ((0, 0), (0, pad_in))).astype(BF16),
        gqk=jnp.concatenate([jnp.tile(q_norm_g[l], N_Q_HEADS), jnp.tile(k_norm_g[l], N_KV_HEADS)]).reshape(1, -1),
        mu_p=jnp.pad(mu_shift[l], ((0, 0), (0, D_RWKV_PAD - D_RWKV_IN))),
        wd_cat=wd_cat, w0_cat=w0[l].reshape(1, 2 * D_RWKV),
        wa_cat=wa_cat, a0_cat=a0[l].reshape(1, 2 * D_RWKV),
        wg_p=jnp.pad(g_lora_up[l], ((0, LANES - GATE_LORA), (0, 0))),
        k_k=k_k[l].reshape(1, D_RWKV), k_a=k_a[l].reshape(1, D_RWKV), r_k=r_k[l].reshape(1, D_RWKV),
        ln_g=ln_x_g[l].reshape(1, D_RWKV), ln_b=ln_x_b[l].reshape(1, D_RWKV),
        w_out_b=w_out[l].astype(BF16), w_r=w_r, b_r=b_r,
        wg4=by_group(exp_gate[l]), wu4=by_group(exp_up[l]),
        wd4=exp_down[l].astype(BF16).reshape(N_GROUPS, EXPERTS_PER_GROUP * D_EXPERT, D_MODEL),
    )


CTX_TILES = dict(in_proj=256, attn=256, prep=256, chunk=128)
SMP_TILES = dict(in_proj=512, attn=256, prep=512, chunk=128)


def kernel(x_prompt, x_sample, cache_k, cache_v, state_rwkv, c, c_ctx, w_mod, b_mod, norm1_g, norm2_g, w_in, mu_shift, q_norm_g, k_norm_g, w0, w_lora_up, a0, a_lora_up, g_lora_up, k_k, k_a, r_k, ln_x_g, ln_x_b, w_out, router_c, router_c_b, router_f, router_f_b, exp_gate, exp_up, exp_down):
    depth = w_mod.shape[0]
    db = x_sample.shape[0]
    y_prompt, y_sample = x_prompt, x_sample
    ks, vs, ss = [], [], []
    cond = jnp.zeros((8, D_MODEL), F32).at[:db].set(c).at[db].set(c_ctx)
    for l in range(depth):
        lp = _layer_params(l, w_in, norm1_g, norm2_g, mu_shift, q_norm_g, k_norm_g, w0, w_lora_up, a0, a_lora_up,
                           g_lora_up, k_k, k_a, r_k, ln_x_g, ln_x_b, w_out, router_c, router_c_b, router_f,
                           router_f_b, exp_gate, exp_up, exp_down)
        mod = _modulation(cond, w_mod[l], b_mod[l])
        mod_s = [mod[:db, i * D_MODEL:(i + 1) * D_MODEL].reshape(db, 1, D_MODEL) for i in range(6)]
        mod_c = [mod[db:db + 1, i * D_MODEL:(i + 1) * D_MODEL].reshape(1, 1, D_MODEL) for i in range(6)]
        pre_c, k_l, v_l, s_l = _layer(y_prompt, mod_c, lp, CTX_TILES, None)
        ks.append(k_l)
        vs.append(v_l)
        ss.append(s_l)
        pre_s, _, _, _ = _layer(y_sample, mod_s, lp, SMP_TILES, (cache_k[:, l], cache_v[:, l], state_rwkv[:, l]))
        out_c, out_s = _moe_both([pre_c, pre_s], [mod_c[5], mod_s[5]], lp)
        y_prompt, y_sample = out_c.reshape(y_prompt.shape), out_s.reshape(y_sample.shape)
    return (y_prompt, y_sample, jnp.stack(ks, axis=1), jnp.stack(vs, axis=1), jnp.stack(ss, axis=1))
```

```python
import functools

import numpy as np
import jax
import jax.numpy as jnp
from jax import lax
from jax.experimental import pallas as pl
from jax.experimental.pallas import tpu as pltpu

F32 = jnp.float32
BF16 = jnp.bfloat16
HIGHEST = lax.Precision.HIGHEST

D_MODEL = 1024
HEAD_DIM = 64
N_Q_HEADS = 8
N_KV_HEADS = 2
GQA_GROUP = N_Q_HEADS // N_KV_HEADS
D_ATTN = N_Q_HEADS * HEAD_DIM
D_KV = N_KV_HEADS * HEAD_DIM
N_RWKV_HEADS = 8
D_RWKV = 512
DECAY_LORA = 32
AAA_LORA = 32
GATE_LORA = 96
D_RWKV_IN = 3 * D_RWKV + 2 * DECAY_LORA + 2 * AAA_LORA + GATE_LORA
D_RWKV_PAD = 1792
D_QKV = D_ATTN + 2 * D_KV
D_IN_PAD = D_QKV + D_RWKV_PAD
N_GROUPS = 4
EXPERTS_PER_GROUP = 4
N_EXPERTS = 16
D_EXPERT = 512
GRID_W = 64
ROPE_THETA = 10000.0
NORM_EPS = 1e-6
GN_EPS = 64e-5
DECAY_SCALE = 0.6065306597
QK_EXP2_SCALE = (HEAD_DIM ** -0.5) * float(np.log2(np.e))
LANES = 128
ROUTER_LANES = 128
VMEM_LIMIT = 56 * 1024 * 1024
ATTN_ROWS = 128
MOE_W = D_MODEL + 2 * ROUTER_LANES
MOE_TM = 256
OUT_PROJ_TM = 512
MOE_SUB = 32
MOE_STAGE = MOE_TM + N_GROUPS * MOE_SUB
MOE_TX = 512
SEG_ALIGN = 16


def _cparams(sem):
    return pltpu.CompilerParams(dimension_semantics=sem, vmem_limit_bytes=VMEM_LIMIT)


def _dot(a, b, precision=None):
    return jnp.dot(a, b, preferred_element_type=F32, precision=precision)


def _dot_nt(a, b, precision=None):
    return lax.dot_general(a, b, (((1,), (1,)), ((), ())), preferred_element_type=F32, precision=precision)


def _dot_tn(a, b, precision=None):
    return lax.dot_general(a, b, (((0,), (0,)), ((), ())), preferred_element_type=F32, precision=precision)


def _split2(x):
    hi = x.astype(BF16)
    return hi, (x - hi.astype(F32)).astype(BF16)


def _dot3(a, b):
    a_hi, a_lo = _split2(a)
    b_hi, b_lo = _split2(b)
    return _dot(a_hi, b_hi) + (_dot(a_hi, b_lo) + _dot(a_lo, b_hi))


def _head_sum(x, g):
    hi, lo = _split2(x)
    n = x.shape[-1] // LANES
    cols = [slice(j * LANES, (j + 1) * LANES) for j in range(n)]
    return jnp.concatenate([_dot(hi[:, c], g) + _dot(lo[:, c], g) for c in cols], axis=-1)


def _mod_kernel(c_ref, w_ref, b_ref, o_ref):
    c = c_ref[...]
    s = c * jax.nn.sigmoid(c)
    o_ref[...] = _dot(s, w_ref[...], HIGHEST) + b_ref[...]


def _modulation(cond, w_mod, b_mod):
    n = w_mod.shape[1]
    tn = 1024
    return pl.pallas_call(
        _mod_kernel,
        grid=(n // tn,),
        in_specs=[pl.BlockSpec((8, D_MODEL), lambda j: (0, 0)),
                  pl.BlockSpec((D_MODEL, tn), lambda j: (0, j)),
                  pl.BlockSpec((1, tn), lambda j: (0, j))],
        out_specs=pl.BlockSpec((8, tn), lambda j: (0, j)),
        out_shape=jax.ShapeDtypeStruct((8, n), F32),
        compiler_params=_cparams(("arbitrary",)),
        name="mod",
    )(cond, w_mod, b_mod.reshape(1, n))


def _rope_tables(t_len):
    half = HEAD_DIM // 2
    inv = ROPE_THETA ** (-np.arange(0, half, 2, dtype=np.float64) / half)
    t = np.arange(t_len)
    row, col = t // GRID_W, t % GRID_W
    lane = np.arange(LANES)
    i = lane % HEAD_DIM
    pos = np.where((i // half)[None, :] == 0, row[:, None], col[:, None]).astype(np.float64)
    j = i % half
    ang = pos * inv[j % (half // 2)][None, :]
    cos, sin = np.cos(ang), np.sin(ang)
    first = (j < half // 2)[None, :]
    s_up = np.where(first, -sin, 0.0)
    s_dn = np.where(first, 0.0, sin)
    return (jnp.asarray(cos, F32), jnp.asarray(s_up, F32), jnp.asarray(s_dn, F32))


def _inproj_kernel(x_ref, sh_ref, sc_ref, g_ref, w_ref, gqk_ref, grp_ref, *rest, rope):
    if rope:
        cos_ref, sup_ref, sdn_ref, q_ref, k_ref, v_ref, kf_ref, vf_ref, rw_ref = rest
    else:
        q_ref, k_ref, v_ref, kf_ref, vf_ref, rw_ref = rest
    x = x_ref[0]
    ms = jnp.mean(x * x, axis=-1, keepdims=True)
    h = x * lax.rsqrt(ms + NORM_EPS) * g_ref[...]
    h = h * (1.0 + sc_ref[0]) + sh_ref[0]
    proj = _dot(h.astype(BF16), w_ref[...])
    grp = grp_ref[...]
    lo_half = lax.broadcasted_iota(jnp.int32, (x.shape[0], LANES), 1) < HEAD_DIM
    for j in range((D_ATTN + D_KV) // LANES):
        blk = proj[:, j * LANES:(j + 1) * LANES]
        ss = _head_sum(blk * blk, grp) * (1.0 / HEAD_DIM)
        nb = blk * lax.rsqrt(ss + NORM_EPS) * gqk_ref[:, j * LANES:(j + 1) * LANES]
        if rope:
            nb = (nb * cos_ref[...] + pltpu.roll(nb, LANES - 16, 1) * sup_ref[...]
                  + pltpu.roll(nb, 16, 1) * sdn_ref[...])
        if j < D_ATTN // LANES:
            nbq = nb * QK_EXP2_SCALE
            for half in range(2):
                hq = 2 * j + half
                q_ref[0, hq // GQA_GROUP, hq % GQA_GROUP] = nbq[:, half * HEAD_DIM:(half + 1) * HEAD_DIM].astype(BF16)
        else:
            kf_ref[0] = nb
            k_ref[0, 0] = nb[:, :HEAD_DIM].astype(BF16)
            k_ref[0, 1] = nb[:, HEAD_DIM:].astype(BF16)
    vblk = proj[:, D_ATTN + D_KV:D_QKV]
    vf_ref[0] = vblk
    v_ref[0, 0] = jnp.where(lo_half, vblk, 1.0).astype(BF16)
    v_ref[0, 1] = jnp.where(lo_half, pltpu.roll(vblk, HEAD_DIM, 1), 1.0).astype(BF16)
    rw_ref[0] = proj[:, D_QKV:]


def _in_proj(x3, shift, scale, norm_g, w_in_p, gqk, grp, rope, tm):
    b, t, _ = x3.shape
    bm = shift.shape[0]
    mod_map = (lambda i, j: (i, 0, 0)) if bm > 1 else (lambda i, j: (0, 0, 0))
    full = lambda a: pl.BlockSpec(a.shape, lambda i, j: (0,) * a.ndim)
    tok = lambda w: pl.BlockSpec((1, tm, w), lambda i, j: (i, j, 0))
    in_specs = [tok(D_MODEL), pl.BlockSpec((1, 1, D_MODEL), mod_map), pl.BlockSpec((1, 1, D_MODEL), mod_map),
                full(norm_g), full(w_in_p), full(gqk), full(grp)]
    args = [x3, shift, scale, norm_g, w_in_p, gqk, grp]
    if rope:
        in_specs += [pl.BlockSpec((tm, LANES), lambda i, j: (j, 0))] * 3
        args += list(_rope_tables(t))
    out_shape = (jax.ShapeDtypeStruct((b, N_KV_HEADS, GQA_GROUP, t, HEAD_DIM), BF16),
                 jax.ShapeDtypeStruct((b, N_KV_HEADS, t, HEAD_DIM), BF16),
                 jax.ShapeDtypeStruct((b, N_KV_HEADS, t, 2 * HEAD_DIM), BF16),
                 jax.ShapeDtypeStruct((b, t, D_KV), F32), jax.ShapeDtypeStruct((b, t, D_KV), F32),
                 jax.ShapeDtypeStruct((b, t, D_RWKV_PAD), F32))
    out_specs = (pl.BlockSpec((1, N_KV_HEADS, GQA_GROUP, tm, HEAD_DIM), lambda i, j: (i, 0, 0, j, 0)),
                 pl.BlockSpec((1, N_KV_HEADS, tm, HEAD_DIM), lambda i, j: (i, 0, j, 0)),
                 pl.BlockSpec((1, N_KV_HEADS, tm, 2 * HEAD_DIM), lambda i, j: (i, 0, j, 0)),
                 tok(D_KV), tok(D_KV), tok(D_RWKV_PAD))
    return pl.pallas_call(
        functools.partial(_inproj_kernel, rope=rope),
        grid=(b, t // tm), in_specs=in_specs, out_specs=out_specs, out_shape=out_shape,
        compiler_params=_cparams(("parallel", "parallel")),
        name="in_proj_rope" if rope else "in_proj",
    )(*args)


def _attn_kernel(q_ref, k_ref, v_ref, o_ref):
    g, tq, hd = q_ref.shape[2:]
    sub = min(ATTN_ROWS, tq)
    k = k_ref[0, 0]
    v = v_ref[0, 0]
    slabs = [slice(i * sub, (i + 1) * sub) for i in range(tq // sub)]
    qs = [q_ref[0, 0, :, sl, :].reshape(g * sub, hd) for sl in slabs]
    ss = [_dot_nt(q, k) for q in qs]
    ps = [jnp.exp2(s - jnp.max(s, axis=-1, keepdims=True)).astype(BF16) for s in ss]
    for sl, p in zip(slabs, ps):
        o = _dot(p, v)
        o = o[:, :hd] / pltpu.roll(o, hd, 1)[:, :hd]
        o_ref[0, sl, :] = jnp.concatenate([o[i * sub:(i + 1) * sub] for i in range(g)], axis=-1)


def _attention(q5, k4, v4, tq):
    b, hk, g, t, hd = q5.shape
    tk = k4.shape[2]
    return pl.pallas_call(
        _attn_kernel,
        grid=(b, hk, t // tq),
        in_specs=[pl.BlockSpec((1, 1, g, tq, hd), lambda i, j, l: (i, j, 0, l, 0)),
                  pl.BlockSpec((1, 1, tk, hd), lambda i, j, l: (i, j, 0, 0)),
                  pl.BlockSpec((1, 1, tk, 2 * hd), lambda i, j, l: (i, j, 0, 0))],
        out_specs=pl.BlockSpec((1, tq, g * hd), lambda i, j, l: (i, l, j)),
        out_shape=jax.ShapeDtypeStruct((b, t, hk * g * hd), F32),
        compiler_params=_cparams(("parallel", "parallel", "arbitrary")),
        name="attn",
    )(q5, k4, v4)


def _prep_kernel(rw_ref, hp_ref, hn_ref, mu_ref, wd_ref, w0_ref, wa_ref, a0_ref, wg_ref, kk_ref, ka_ref, rk_ref,
                 grp_ref, r_o, lw_o, kd_o, v_o, kk_o, bd_o, g_o, bonus_o):
    i = pl.program_id(1)
    n = pl.num_programs(1)
    cur = rw_ref[0]
    tt = cur.shape[0]
    rid = lax.broadcasted_iota(jnp.int32, cur.shape, 0)
    prev_row = jnp.where(i > 0, hp_ref[0, 7:8, :], 0.0)
    next_row = jnp.where(i < n - 1, hn_ref[0, 0:1, :], 0.0)
    prev = jnp.where(rid == 0, prev_row, pltpu.roll(cur, 1, 0))
    nxt = jnp.where(rid == tt - 1, next_row, pltpu.roll(cur, tt - 1, 0))
    p = cur + mu_ref[0:1, :] * (prev - cur) + mu_ref[1:2, :] * (nxt - cur)
    r = p[:, 0:D_RWKV]
    k = p[:, D_RWKV:2 * D_RWKV]
    v = p[:, 2 * D_RWKV:3 * D_RWKV]
    lo = p[:, 3 * D_RWKV:3 * D_RWKV + LANES]
    gd = p[:, 3 * D_RWKV + LANES:]
    grp = grp_ref[...]
    wlog = _dot3(jnp.tanh(lo), wd_ref[...]) + w0_ref[...]
    alog = _dot3(lo, wa_ref[...]) + a0_ref[...]
    g_o[0] = _dot3(jax.nn.sigmoid(gd), wg_ref[...])
    kx = k * kk_ref[...]
    kk = kx * lax.rsqrt(_head_sum(kx * kx, grp) + 1e-12)
    r_o[0] = r
    v_o[0] = v
    kk_o[0] = kk
    bonus_o[0] = _head_sum(r * k * rk_ref[...], grp) * v
    for z in range(2):
        a = jax.nn.sigmoid(alog[:, z * D_RWKV:(z + 1) * D_RWKV])
        lw_o[z, 0] = -DECAY_SCALE * jax.nn.sigmoid(wlog[:, z * D_RWKV:(z + 1) * D_RWKV])
        kd_o[z, 0] = k * (1.0 + (a - 1.0) * ka_ref[...])
        bd_o[z, 0] = kk * a


def _rwkv_prep(rw3, mu_p, wd_cat, w0_cat, wa_cat, a0_cat, wg_p, k_k, k_a, r_k, grp, tt):
    b, t, _ = rw3.shape
    nt = t // tt
    hb = tt // 8
    one = jax.ShapeDtypeStruct((b, t, D_RWKV), F32)
    two = jax.ShapeDtypeStruct((2, b, t, D_RWKV), F32)
    s_one = pl.BlockSpec((1, tt, D_RWKV), lambda i, j: (i, j, 0))
    s_two = pl.BlockSpec((2, 1, tt, D_RWKV), lambda i, j: (0, i, j, 0))
    full = lambda a: pl.BlockSpec(a.shape, lambda i, j: (0,) * a.ndim)
    consts = [mu_p, wd_cat, w0_cat, wa_cat, a0_cat, wg_p, k_k, k_a, r_k, grp]
    return pl.pallas_call(
        _prep_kernel,
        grid=(b, nt),
        in_specs=[pl.BlockSpec((1, tt, D_RWKV_PAD), lambda i, j: (i, j, 0)),
                  pl.BlockSpec((1, 8, D_RWKV_PAD), lambda i, j: (i, jnp.maximum(j * hb - 1, 0), 0)),
                  pl.BlockSpec((1, 8, D_RWKV_PAD), lambda i, j: (i, jnp.minimum((j + 1) * hb, t // 8 - 1), 0))]
                 + [full(a) for a in consts],
        out_specs=(s_one, s_two, s_two, s_one, s_one, s_two, s_one, s_one),
        out_shape=(one, two, two, one, one, two, one, one),
        compiler_params=_cparams(("parallel", "parallel")),
        name="rwkv_prep",
    )(rw3, rw3, rw3, *consts)


INV_BASE = 16


def _bdot(a, b):
    return _dot(a.astype(BF16), b.astype(BF16))


def _scan_kernel(*refs, chunk):
    ins, (s0_ref, yf_ref, yb_ref, sT_ref, s_scr) = (refs[0:6], refs[6:12]), refs[12:]
    y_refs = (yf_ref, yb_ref)
    c = chunk

    @pl.when(pl.program_id(1) == 0)
    def _():
        s_scr[...] = s0_ref[0]

    row = lax.broadcasted_iota(jnp.int32, (c, c), 0)
    col = lax.broadcasted_iota(jnp.int32, (c, c), 1)
    eye = (row == col).astype(F32)
    same_blk = {}
    n = INV_BASE
    while n <= c:
        sh = jnp.int32(n.bit_length() - 1)
        same_blk[n] = lax.shift_right_logical(row, sh) == lax.shift_right_logical(col, sh)
        n *= 2
    heads = range(N_RWKV_HEADS)
    sls = [slice(h * HEAD_DIM, (h + 1) * HEAD_DIM) for h in heads]

    incl, strict, ar_abs, ar_mid, kb_inv, kb_end, v, p_tot = [], [], [], [], [], [], [], []
    for d, (r_ref, lw_ref, k_ref, v_ref, a_ref, b_ref) in enumerate(ins):
        incl.append(row >= col if d == 0 else row <= col)
        strict.append(row > col if d == 0 else row < col)
        lw = lw_ref[0, 0]
        lw_hi = lw.astype(BF16)
        lw_mid, lw_lo = _split2(lw - lw_hi.astype(F32))
        inclb = jnp.where(incl[d], 1.0, 0.0).astype(BF16)
        cl = _dot(inclb, lw_hi) + (_dot(inclb, lw_mid) + _dot(inclb, lw_lo))
        tot = jnp.sum(lw, axis=0, keepdims=True)
        mid = 0.5 * tot
        e_inv = jnp.exp(mid - cl)
        e_end = jnp.exp(tot - cl)
        s_mid = jnp.exp(-mid)
        r_abs = r_ref[0] * jnp.exp(cl)
        a_abs = a_ref[0] * jnp.exp(cl - lw)
        ar_abs.append(jnp.concatenate([a_abs, r_abs], axis=0).astype(BF16))
        ar_mid.append(jnp.concatenate([a_abs * s_mid, r_abs * s_mid], axis=0).astype(BF16))
        kb_inv.append(jnp.concatenate([k_ref[0, 0] * e_inv, b_ref[0, 0] * e_inv], axis=0).astype(BF16))
        kb_end.append(jnp.concatenate([k_ref[0, 0] * e_end, b_ref[0, 0] * e_end], axis=0).astype(BF16))
        v.append(v_ref[0].astype(BF16))
        p_tot.append(jnp.exp(tot))

    chains = [(d, h) for d in range(2) for h in heads]
    ids = range(len(chains))
    s_old = [s_scr[d, h] for d, h in chains]
    vhs = [v[d][:, sls[h]] for d, h in chains]
    grams = [_dot_nt(ar_mid[d][:, sls[h]], kb_inv[d][:, sls[h]]) for d, h in chains]
    from_s = [_dot_nt(ar_abs[d][:, sls[h]], s_old[i].astype(BF16)) for i, (d, h) in enumerate(chains)]
    masked = [jnp.concatenate([jnp.where(strict[d], grams[i][:c, :c], 0.0),
                               jnp.where(incl[d], grams[i][c:, :c], 0.0)], axis=0) for i, (d, h) in enumerate(chains)]
    from_v = [_bdot(masked[i], vhs[i]) for i in ids]
    lmats = [jnp.where(strict[d], grams[i][:c, c:], 0.0) for i, (d, h) in enumerate(chains)]
    l0s = [jnp.where(same_blk[INV_BASE], lm, 0.0) for lm in lmats]
    xs = [eye - l0 for l0 in l0s]
    pws = [_bdot(l0, l0) for l0 in l0s]
    span = 2
    while 2 * span < INV_BASE:
        both = [_bdot(jnp.concatenate([xs[i], pws[i]], axis=0), pws[i]) for i in ids]
        xs = [xs[i] + both[i][:c] for i in ids]
        pws = [both[i][c:] for i in ids]
        span *= 2
    xs = [xs[i] + _bdot(xs[i], pws[i]) for i in ids]
    n = INV_BASE
    while n < c:
        pair = same_blk[2 * n] & jnp.logical_not(same_blk[n])
        ts = [_bdot(jnp.where(pair, lmats[i], 0.0), xs[i]) for i in ids]
        xs = [xs[i] - _bdot(xs[i], ts[i]) for i in ids]
        n *= 2
    us = [_bdot(xs[i], from_s[i][:c] + from_v[i][:c]) for i in ids]
    yu = [_bdot(jnp.where(incl[d], grams[i][c:, c:], 0.0), us[i]) for i, (d, h) in enumerate(chains)]
    ds = [_dot_tn(jnp.concatenate([vhs[i], (-us[i]).astype(BF16)], axis=0), kb_end[d][:, sls[h]])
          for i, (d, h) in enumerate(chains)]
    for d in range(2):
        y_refs[d][0] = jnp.concatenate([from_s[i][c:] + from_v[i][c:] - yu[i]
                                        for i, (dd, h) in enumerate(chains) if dd == d], axis=-1)
    for i, (d, h) in enumerate(chains):
        s_scr[d, h] = s_old[i] * p_tot[d][:, sls[h]] + ds[i]

    @pl.when(pl.program_id(1) == pl.num_programs(1) - 1)
    def _():
        sT_ref[0] = s_scr[...]


def _rwkv_scan(r, lw2, kd2, v, kk, bd2, s0, chunk):
    b, t, _ = r.shape
    nc = t // chunk
    in_specs, args = [], []
    for d in range(2):
        tmap = (lambda j: j) if d == 0 else (lambda j: nc - 1 - j)
        s_one = pl.BlockSpec((1, chunk, D_RWKV), lambda i, j, tmap=tmap: (i, tmap(j), 0))
        s_two = pl.BlockSpec((1, 1, chunk, D_RWKV), lambda i, j, tmap=tmap, d=d: (d, i, tmap(j), 0))
        in_specs += [s_one, s_two, s_two, s_one, s_one, s_two]
        args += [r, lw2, kd2, v, kk, bd2]
    s_st = pl.BlockSpec((1, 2, N_RWKV_HEADS, HEAD_DIM, HEAD_DIM), lambda i, j: (i, 0, 0, 0, 0))
    y_specs = tuple(pl.BlockSpec((1, chunk, D_RWKV), lambda i, j, tmap=tmap: (i, tmap(j), 0))
                    for tmap in ((lambda j: j), (lambda j: nc - 1 - j)))
    y_shape = jax.ShapeDtypeStruct((b, t, D_RWKV), F32)
    return pl.pallas_call(
        functools.partial(_scan_kernel, chunk=chunk),
        grid=(b, nc),
        in_specs=in_specs + [s_st],
        out_specs=y_specs + (s_st,),
        out_shape=(y_shape, y_shape, jax.ShapeDtypeStruct(s0.shape, F32)),
        scratch_shapes=[pltpu.VMEM((2, N_RWKV_HEADS, HEAD_DIM, HEAD_DIM), F32)],
        compiler_params=_cparams(("parallel", "arbitrary")),
        name="rwkv_scan",
    )(*args, s0)


def _outproj_kernel(x_ref, at_ref, yf_ref, yb_ref, bonus_ref, gate_ref, lng_ref, lnb_ref, grp_ref, wo_ref,
                    g1_ref, sh_ref, sc_ref, n2_ref, wr_ref, br_ref, x1_ref, hx_ref, cnt_ref):
    grp = grp_ref[...]
    y = yf_ref[0] + yb_ref[0]
    mean = _head_sum(y, grp) * (1.0 / HEAD_DIM)
    yc = y - mean
    var = _head_sum(yc * yc, grp) * (1.0 / HEAD_DIM)
    yn = yc * lax.rsqrt(var + GN_EPS) * lng_ref[...] + lnb_ref[...]
    rw_out = (yn + bonus_ref[0]) * gate_ref[0]
    mix = (_dot(at_ref[0].astype(BF16), wo_ref[0:D_ATTN, :])
           + _dot(rw_out.astype(BF16), wo_ref[D_ATTN:, :]))
    x1 = x_ref[0] + g1_ref[0] * mix
    x1_ref[0] = x1
    ms = jnp.mean(x1 * x1, axis=-1, keepdims=True)
    h2 = x1 * lax.rsqrt(ms + NORM_EPS) * n2_ref[...]
    h2 = h2 * (1.0 + sc_ref[0]) + sh_ref[0]
    hx_ref[0, :, 0:D_MODEL] = h2.astype(BF16)
    logits = _dot3(h2, wr_ref[...]) + br_ref[...]
    lane = lax.broadcasted_iota(jnp.int32, logits.shape, 1)
    neg = -jnp.inf
    big = jnp.int32(1 << 20)
    lc = jnp.where(lane < N_GROUPS, logits, neg)
    mc = jnp.max(lc, axis=-1, keepdims=True)
    g_w = 1.0 / jnp.sum(jnp.exp(lc - mc), axis=-1, keepdims=True)
    g_idx = jnp.min(jnp.where(lc == mc, lane, big), axis=-1, keepdims=True)
    eid = lane - N_GROUPS
    in_grp = (eid >= 0) & (eid < N_EXPERTS) & (lax.shift_right_arithmetic(eid, 2) == g_idx)
    lf = jnp.where(in_grp, logits, neg)
    m1 = jnp.max(lf, axis=-1, keepdims=True)
    i1 = jnp.min(jnp.where(lf == m1, lane, big), axis=-1, keepdims=True)
    lf2 = jnp.where(lane == i1, neg, lf)
    m2 = jnp.max(lf2, axis=-1, keepdims=True)
    i2 = jnp.min(jnp.where(lf2 == m2, lane, big), axis=-1, keepdims=True)
    e2 = jnp.exp(m2 - m1)
    w1 = 1.0 / (1.0 + e2)
    w2 = e2 * w1
    cmb = g_w * (jnp.where(lane == i1, w1, 0.0) + jnp.where(lane == i2, w2, 0.0))
    rec = jnp.where(lane == 0, g_idx.astype(F32), cmb)
    rec_hi, rec_lo = _split2(rec)
    hx_ref[0, :, D_MODEL:D_MODEL + ROUTER_LANES] = rec_hi
    hx_ref[0, :, D_MODEL + ROUTER_LANES:] = rec_lo
    hot = jnp.where((lane == g_idx) & (lane < N_GROUPS), 1.0, 0.0)
    for s in range(cnt_ref.shape[1]):
        part = jnp.sum(hot[s * MOE_TM:(s + 1) * MOE_TM], axis=0, keepdims=True)
        cnt_ref[0, s] = jnp.broadcast_to(part, cnt_ref.shape[2:])


def _out_proj(x3, attn3, yf, yb, bonus, gate, ln_g, ln_b, grp, w_out_b, g1, sh2, sc2, norm2_g, w_r, b_r):
    b, t, _ = x3.shape
    tm = OUT_PROJ_TM if t % OUT_PROJ_TM == 0 else MOE_TM
    bm = g1.shape[0]
    mod_map = (lambda i, j: (i, 0, 0)) if bm > 1 else (lambda i, j: (0, 0, 0))
    tok = lambda w: pl.BlockSpec((1, tm, w), lambda i, j: (i, j, 0))
    full = lambda a: pl.BlockSpec(a.shape, lambda i, j: (0,) * a.ndim)
    mod = pl.BlockSpec((1, 1, D_MODEL), mod_map)
    return pl.pallas_call(
        _outproj_kernel,
        grid=(b, t // tm),
        in_specs=[tok(D_MODEL), tok(D_ATTN), tok(D_RWKV), tok(D_RWKV),
                  tok(D_RWKV), tok(D_RWKV), full(ln_g), full(ln_b), full(grp), full(w_out_b),
                  mod, mod, mod, full(norm2_g), full(w_r), full(b_r)],
        out_specs=(tok(D_MODEL), tok(MOE_W), pl.BlockSpec((1, tm // MOE_TM, 8, LANES), lambda i, j: (i, j, 0, 0))),
        out_shape=(jax.ShapeDtypeStruct((b, t, D_MODEL), F32), jax.ShapeDtypeStruct((b, t, MOE_W), BF16),
                   jax.ShapeDtypeStruct((b, t // MOE_TM, 8, LANES), F32)),
        compiler_params=_cparams(("parallel", "parallel")),
        name="out_proj",
    )(x3, attn3, yf, yb, bonus, gate, ln_g, ln_b, grp, w_out_b, g1, sh2, sc2, norm2_g, w_r, b_r)


def _tile_sort_matrix(rec_hi, plan_ref, tile):
    tm = rec_hi.shape[0]
    sel = ((lax.broadcasted_iota(jnp.int32, (8, LANES), 0) == 0)
           & (lax.broadcasted_iota(jnp.int32, (8, LANES), 1) == 0))
    g_row = _dot_nt(jnp.where(sel, 1.0, 0.0).astype(BF16), rec_hi)[0:1, :]
    g_col = rec_hi.astype(F32)[:, 0:1]
    r = lax.broadcasted_iota(jnp.int32, (tm, tm), 0)
    c = lax.broadcasted_iota(jnp.int32, (tm, tm), 1)
    earlier_same = jnp.where((g_col == g_row) & (r < c), 1.0, 0.0)
    pos_row = jnp.sum(earlier_same, axis=0, keepdims=True)
    for g in range(N_GROUPS):
        local = _plan_entry(plan_ref, tile, g)[2].astype(F32)
        pos_row = pos_row + jnp.where(g_row == float(g), local, 0.0)
    stage_row = lax.broadcasted_iota(jnp.int32, (MOE_STAGE, tm), 0).astype(F32)
    return jnp.where(stage_row == pos_row, 1.0, 0.0).astype(BF16)


def _plan_entry(plan_ref, tile, g):
    at = (tile * N_GROUPS + g) * 3
    return pl.multiple_of(plan_ref[at], SEG_ALIGN), plan_ref[at + 1], pl.multiple_of(plan_ref[at + 2], MOE_SUB)


def _for_each_granule(plan_ref, tile, fn):
    for g in range(N_GROUPS):
        start, n_gran, local = _plan_entry(plan_ref, tile, g)

        def body(s, carry, g=g, start=start, local=local):
            fn(g, pl.multiple_of(start + s * MOE_SUB, SEG_ALIGN), pl.multiple_of(local + s * MOE_SUB, MOE_SUB))
            return carry

        lax.fori_loop(0, n_gran, body, 0)


def _dispatch_kernel(plan_ref, hx_ref, xs_in_ref, xs_ref, buf, sem, *, tile0):
    del xs_in_ref
    i = pl.program_id(0)
    x = hx_ref[...]
    perm = _tile_sort_matrix(x[:, D_MODEL:D_MODEL + ROUTER_LANES], plan_ref, tile0 + i)
    staged = _dot(perm, x).astype(BF16)

    def copy(g, seg_row, stage_row):
        return pltpu.make_async_copy(buf.at[pl.ds(stage_row, MOE_SUB), :],
                                     xs_ref.at[pl.ds(seg_row, MOE_SUB), :], sem.at[g])

    @pl.when(i > 0)
    def _():
        _for_each_granule(plan_ref, tile0 + i - 1, lambda *a: copy(*a).wait())

    buf[...] = staged
    _for_each_granule(plan_ref, tile0 + i, lambda *a: copy(*a).start())

    @pl.when(i == pl.num_programs(0) - 1)
    def _():
        _for_each_granule(plan_ref, tile0 + i, lambda *a: copy(*a).wait())


def _moe_dispatch(plan, hx2, xs, tile0):
    n = hx2.shape[0]
    grid_spec = pltpu.PrefetchScalarGridSpec(
        num_scalar_prefetch=1, grid=(n // MOE_TM,),
        in_specs=[pl.BlockSpec((MOE_TM, MOE_W), lambda i, plan: (i, 0)), pl.BlockSpec(memory_space=pl.ANY)],
        out_specs=pl.BlockSpec(memory_space=pl.ANY),
        scratch_shapes=[pltpu.VMEM((MOE_STAGE, MOE_W), BF16), pltpu.SemaphoreType.DMA((N_GROUPS,))])
    return pl.pallas_call(
        functools.partial(_dispatch_kernel, tile0=tile0),
        grid_spec=grid_spec,
        out_shape=jax.ShapeDtypeStruct(xs.shape, xs.dtype),
        input_output_aliases={2: 0},
        compiler_params=_cparams(("arbitrary",)),
        name="moe_dispatch",
    )(plan, hx2, xs)


def _experts_kernel(tg_ref, xs_ref, wg_ref, wu_ref, wd_ref, ys_ref):
    g = tg_ref[pl.program_id(0)]

    @pl.when(g >= N_GROUPS)
    def _():
        ys_ref[...] = jnp.zeros_like(ys_ref)

    @pl.when(g < N_GROUPS)
    def _():
        x = xs_ref[...]
        h = x[:, 0:D_MODEL]
        rec = (x[:, D_MODEL:D_MODEL + ROUTER_LANES].astype(F32) + x[:, D_MODEL + ROUTER_LANES:].astype(F32))
        lane = lax.broadcasted_iota(jnp.int32, rec.shape, 1)
        first = N_GROUPS + EXPERTS_PER_GROUP * g
        scaled = []
        for e in range(EXPERTS_PER_GROUP):
            a = _dot(h, wg_ref[0, e])
            hid = a * jax.nn.sigmoid(a) * _dot(h, wu_ref[0, e])
            c_e = jnp.sum(jnp.where(lane == first + e, rec, 0.0), axis=-1, keepdims=True)
            scaled.append((hid * c_e).astype(BF16))
        ys_ref[...] = _dot(jnp.concatenate(scaled, axis=-1), wd_ref[0]).astype(BF16)


def _moe_experts(tile_group, xs, wg4, wu4, wd4):
    p = xs.shape[0]
    grp_map = lambda nd: (lambda j, tg: (jnp.minimum(tg[j], N_GROUPS - 1),) + (0,) * (nd - 1))
    grid_spec = pltpu.PrefetchScalarGridSpec(
        num_scalar_prefetch=1, grid=(p // MOE_TX,),
        in_specs=[pl.BlockSpec((MOE_TX, MOE_W), lambda j, tg: (j, 0)),
                  pl.BlockSpec((1,) + wg4.shape[1:], grp_map(wg4.ndim)),
                  pl.BlockSpec((1,) + wu4.shape[1:], grp_map(wu4.ndim)),
                  pl.BlockSpec((1,) + wd4.shape[1:], grp_map(wd4.ndim))],
        out_specs=pl.BlockSpec((MOE_TX, D_MODEL), lambda j, tg: (j, 0)))
    return pl.pallas_call(
        _experts_kernel,
        grid_spec=grid_spec,
        out_shape=jax.ShapeDtypeStruct((p, D_MODEL), BF16),
        compiler_params=_cparams(("arbitrary",)),
        name="moe_experts",
    )(tile_group, xs, wg4, wu4, wd4)


def _combine_kernel(plan_ref, rec_ref, x1_ref, g2_ref, ys_ref, o_ref, buf, sem, *, tile0):
    lin = pl.program_id(0) * pl.num_programs(1) + pl.program_id(1)
    n = pl.num_programs(0) * pl.num_programs(1)
    slot = lin % 2

    def copy(slot_t):
        return lambda g, seg_row, stage_row: pltpu.make_async_copy(
            ys_ref.at[pl.ds(seg_row, MOE_SUB), :], buf.at[slot_t, pl.ds(stage_row, MOE_SUB), :], sem.at[slot_t, g])

    @pl.when(lin == 0)
    def _():
        buf[...] = jnp.zeros_like(buf)
        _for_each_granule(plan_ref, tile0, lambda *a: copy(0)(*a).start())

    @pl.when(lin + 1 < n)
    def _():
        _for_each_granule(plan_ref, tile0 + lin + 1, lambda *a: copy(1 - slot)(*a).start())

    perm = _tile_sort_matrix(rec_ref[0], plan_ref, tile0 + lin)
    _for_each_granule(plan_ref, tile0 + lin, lambda *a: copy(slot)(*a).wait())
    y = _dot_tn(perm, buf[slot])
    o_ref[0] = x1_ref[0] + g2_ref[0] * y


def _moe_combine(plan, hx3, x1, g2, ys, tile0):
    b, t, _ = x1.shape
    bm = g2.shape[0]
    mod_map = (lambda i, j, plan: (i, 0, 0)) if bm > 1 else (lambda i, j, plan: (0, 0, 0))
    rec_blk = D_MODEL // ROUTER_LANES
    grid_spec = pltpu.PrefetchScalarGridSpec(
        num_scalar_prefetch=1, grid=(b, t // MOE_TM),
        in_specs=[pl.BlockSpec((1, MOE_TM, ROUTER_LANES), lambda i, j, plan: (i, j, rec_blk)),
                  pl.BlockSpec((1, MOE_TM, D_MODEL), lambda i, j, plan: (i, j, 0)),
                  pl.BlockSpec((1, 1, D_MODEL), mod_map),
                  pl.BlockSpec(memory_space=pl.ANY)],
        out_specs=pl.BlockSpec((1, MOE_TM, D_MODEL), lambda i, j, plan: (i, j, 0)),
        scratch_shapes=[pltpu.VMEM((2, MOE_STAGE, D_MODEL), BF16), pltpu.SemaphoreType.DMA((2, N_GROUPS))])
    return pl.pallas_call(
        functools.partial(_combine_kernel, tile0=tile0),
        grid_spec=grid_spec,
        out_shape=jax.ShapeDtypeStruct(x1.shape, F32),
        compiler_params=_cparams(("arbitrary", "arbitrary")),
        name="moe_combine",
    )(plan, hx3, x1, g2, ys)


def _moe_plan(cnt, n_rows):
    seg = (cnt + (SEG_ALIGN - 1)) // SEG_ALIGN * SEG_ALIGN
    used = (jnp.sum(seg, axis=0) + (MOE_TX - 1)) // MOE_TX * MOE_TX
    size = used + MOE_TX
    base = jnp.cumsum(size) - size
    start = base[None, :] + jnp.cumsum(seg, axis=0) - seg
    n_gran = (cnt + (MOE_SUB - 1)) // MOE_SUB
    stage = n_gran * MOE_SUB
    local = jnp.cumsum(stage, axis=1) - stage
    starts = jnp.arange(n_rows // MOE_TX, dtype=jnp.int32)[:, None] * MOE_TX
    inside = (starts >= base[None, :]) & (starts < (base + used)[None, :])
    tile_group = jnp.where(jnp.any(inside, axis=1), jnp.argmax(inside, axis=1), N_GROUPS)
    plan = jnp.stack([start, n_gran, local], axis=-1).reshape(-1)
    return plan.astype(jnp.int32), tile_group.astype(jnp.int32)


def _moe_rows(n_tokens):
    n_tiles = n_tokens // MOE_TM
    bound = n_tokens + n_tiles * N_GROUPS * (SEG_ALIGN - 1) + N_GROUPS * 2 * MOE_TX
    return (bound + MOE_TX - 1) // MOE_TX * MOE_TX


def _layer(x3, mod6, lp, t_tiles, ctx):
    b, t, _ = x3.shape
    sh1, sc1, g1, sh2, sc2, _ = mod6
    t_tiles = {name: min(size, t) for name, size in t_tiles.items()}
    rope = ctx is not None
    n = b * t
    q5, k4, v4, k, v, rw = _in_proj(x3, sh1, sc1, lp["norm1_g"], lp["w_in_p"], lp["gqk"], lp["grp"],
                                    rope, t_tiles["in_proj"])
    if ctx is not None:
        ctx_k, ctx_v, ctx_state = ctx
        ck = ctx_k.transpose(0, 2, 1, 3)
        cv = ctx_v.transpose(0, 2, 1, 3)
        k4 = jnp.concatenate([k4, ck.astype(BF16)], axis=2)
        v4 = jnp.concatenate([v4, jnp.concatenate([cv, jnp.ones_like(cv)], axis=-1).astype(BF16)], axis=2)
        s0 = ctx_state
    else:
        s0 = jnp.zeros((b, 2, N_RWKV_HEADS, HEAD_DIM, HEAD_DIM), F32)
    attn3 = _attention(q5, k4, v4, t_tiles["attn"])
    r, lw2, kd2, vv, kk, bd2, gate, bonus = _rwkv_prep(
        rw, lp["mu_p"], lp["wd_cat"], lp["w0_cat"], lp["wa_cat"], lp["a0_cat"],
        lp["wg_p"], lp["k_k"], lp["k_a"], lp["r_k"], lp["grp"], t_tiles["prep"])
    yf, yb, s_t = _rwkv_scan(r, lw2, kd2, vv, kk, bd2, s0, t_tiles["chunk"])
    fb, ft = (1, n) if g1.shape[0] == 1 else (b, t)
    flat = lambda a: a.reshape(fb, ft, a.shape[-1])
    x1, hx, cnt = _out_proj(flat(x3), flat(attn3), flat(yf), flat(yb), flat(bonus), flat(gate), lp["ln_g"],
                            lp["ln_b"], lp["grp"], lp["w_out_b"], g1, sh2, sc2, lp["norm2_g"], lp["w_r"], lp["b_r"])
    return (x1, hx, cnt), k.reshape(b, t, N_KV_HEADS, HEAD_DIM), v.reshape(b, t, N_KV_HEADS, HEAD_DIM), s_t


def _moe_both(passes, g2s, lp):
    counts = [p[2][:, :, 0, :N_GROUPS].reshape(-1, N_GROUPS) for p in passes]
    tiles = [c.shape[0] for c in counts]
    n_rows = _moe_rows(sum(tiles) * MOE_TM)
    plan, tile_group = _moe_plan(jnp.concatenate(counts, axis=0).astype(jnp.int32), n_rows)
    xs = jnp.zeros((n_rows, MOE_W), BF16)
    tile0 = 0
    for (x1, hx, _), nt in zip(passes, tiles):
        xs = _moe_dispatch(plan, hx.reshape(-1, MOE_W), xs, tile0)
        tile0 += nt
    ys = _moe_experts(tile_group, xs, lp["wg4"], lp["wu4"], lp["wd4"])
    outs, tile0 = [], 0
    for (x1, hx, _), g2, nt in zip(passes, g2s, tiles):
        outs.append(_moe_combine(plan, hx, x1, g2, ys, tile0))
        tile0 += nt
    return outs


def _block_diag2(w):
    z, l, c = w.shape
    out = jnp.zeros((LANES, z * c), F32)
    for i in range(z):
        out = out.at[i * l:(i + 1) * l, i * c:(i + 1) * c].set(w[i])
    return out


def _layer_params(l, w_in, norm1_g, norm2_g, mu_shift, q_norm_g, k_norm_g, w0, w_lora_up, a0, a_lora_up, g_lora_up,
                  k_k, k_a, r_k, ln_x_g, ln_x_b, w_out, router_c, router_c_b, router_f, router_f_b,
                  exp_gate, exp_up, exp_down):
    lane = np.arange(LANES)
    grp = jnp.asarray((lane[:, None] // HEAD_DIM) == (lane[None, :] // HEAD_DIM), BF16)
    pad_in = D_IN_PAD - w_in.shape[2]
    wd_cat = _block_diag2(w_lora_up[l])
    wa_cat = jnp.roll(_block_diag2(a_lora_up[l]), 2 * DECAY_LORA, axis=0)
    w_r = jnp.zeros((D_MODEL, ROUTER_LANES), F32)
    w_r = w_r.at[:, :N_GROUPS].set(router_c[l]).at[:, N_GROUPS:N_GROUPS + N_EXPERTS].set(router_f[l])
    b_r = jnp.zeros((1, ROUTER_LANES), F32)
    b_r = b_r.at[0, :N_GROUPS].set(router_c_b[l]).at[0, N_GROUPS:N_GROUPS + N_EXPERTS].set(router_f_b[l])

    by_group = lambda w: w.astype(BF16).reshape(N_GROUPS, EXPERTS_PER_GROUP, D_MODEL, D_EXPERT)

    return dict(
        grp=grp,
        norm1_g=norm1_g[l].reshape(1, D_MODEL), norm2_g=norm2_g[l].reshape(1, D_MODEL),
        w_in_p=jnp.pad(w_in[l], ((0, 0), (0, pad_in))).astype(BF16),
        gqk=jnp.concatenate([jnp.tile(q_norm_g[l], N_Q_HEADS), jnp.tile(k_norm_g[l], N_KV_HEADS)]).reshape(1, -1),
        mu_p=jnp.pad(mu_shift[l], ((0, 0), (0, D_RWKV_PAD - D_RWKV_IN))),
        wd_cat=wd_cat, w0_cat=w0[l].reshape(1, 2 * D_RWKV),
        wa_cat=wa_cat, a0_cat=a0[l].reshape(1, 2 * D_RWKV),
        wg_p=jnp.pad(g_lora_up[l], ((0, LANES - GATE_LORA), (0, 0))),
        k_k=k_k[l].reshape(1, D_RWKV), k_a=k_a[l].reshape(1, D_RWKV), r_k=r_k[l].reshape(1, D_RWKV),
        ln_g=ln_x_g[l].reshape(1, D_RWKV), ln_b=ln_x_b[l].reshape(1, D_RWKV),
        w_out_b=w_out[l].astype(BF16), w_r=w_r, b_r=b_r,
        wg4=by_group(exp_gate[l]), wu4=by_group(exp_up[l]),
        wd4=exp_down[l].astype(BF16).reshape(N_GROUPS, EXPERTS_PER_GROUP * D_EXPERT, D_MODEL),
    )


CTX_TILES = dict(in_proj=256, attn=256, prep=256, chunk=128)
SMP_TILES = dict(in_proj=512, attn=256, prep=512, chunk=128)


def kernel(x_prompt, x_sample, cache_k, cache_v, state_rwkv, c, c_ctx, w_mod, b_mod, norm1_g, norm2_g, w_in, mu_shift, q_norm_g, k_norm_g, w0, w_lora_up, a0, a_lora_up, g_lora_up, k_k, k_a, r_k, ln_x_g, ln_x_b, w_out, router_c, router_c_b, router_f, router_f_b, exp_gate, exp_up, exp_down):
    depth = w_mod.shape[0]
    db = x_sample.shape[0]
    y_prompt, y_sample = x_prompt, x_sample
    ks, vs, ss = [], [], []
    cond = jnp.zeros((8, D_MODEL), F32).at[:db].set(c).at[db].set(c_ctx)
    for l in range(depth):
        lp = _layer_params(l, w_in, norm1_g, norm2_g, mu_shift, q_norm_g, k_norm_g, w0, w_lora_up, a0, a_lora_up,
                           g_lora_up, k_k, k_a, r_k, ln_x_g, ln_x_b, w_out, router_c, router_c_b, router_f,
                           router_f_b, exp_gate, exp_up, exp_down)
        mod = _modulation(cond, w_mod[l], b_mod[l])
        mod_s = [mod[:db, i * D_MODEL:(i + 1) * D_MODEL].reshape(db, 1, D_MODEL) for i in range(6)]
        mod_c = [mod[db:db + 1, i * D_MODEL:(i + 1) * D_MODEL].reshape(1, 1, D_MODEL) for i in range(6)]
        pre_c, k_l, v_l, s_l = _layer(y_prompt, mod_c, lp, CTX_TILES, None)
        ks.append(k_l)
        vs.append(v_l)
        ss.append(s_l)
        pre_s, _, _, _ = _layer(y_sample, mod_s, lp, SMP_TILES, (cache_k[:, l], cache_v[:, l], state_rwkv[:, l]))
        out_c, out_s = _moe_both([pre_c, pre_s], [mod_c[5], mod_s[5]], lp)
        y_prompt, y_sample = out_c.reshape(y_prompt.shape), out_s.reshape(y_sample.shape)
    return (y_prompt, y_sample, jnp.stack(ks, axis=1), jnp.stack(vs, axis=1), jnp.stack(ss, axis=1))
```

```python
import functools

import numpy as np
import jax
import jax.numpy as jnp
from jax import lax
from jax.experimental import pallas as pl
from jax.experimental.pallas import tpu as pltpu

F32 = jnp.float32
BF16 = jnp.bfloat16
HIGHEST = lax.Precision.HIGHEST

D_MODEL = 1024
HEAD_DIM = 64
N_Q_HEADS = 8
N_KV_HEADS = 2
GQA_GROUP = N_Q_HEADS // N_KV_HEADS
D_ATTN = N_Q_HEADS * HEAD_DIM
D_KV = N_KV_HEADS * HEAD_DIM
N_RWKV_HEADS = 8
D_RWKV = 512
DECAY_LORA = 32
AAA_LORA = 32
GATE_LORA = 96
D_RWKV_IN = 3 * D_RWKV + 2 * DECAY_LORA + 2 * AAA_LORA + GATE_LORA
D_RWKV_PAD = 1792
D_QKV = D_ATTN + 2 * D_KV
D_IN_PAD = D_QKV + D_RWKV_PAD
N_GROUPS = 4
EXPERTS_PER_GROUP = 4
N_EXPERTS = 16
D_EXPERT = 512
GRID_W = 64
ROPE_THETA = 10000.0
NORM_EPS = 1e-6
GN_EPS = 64e-5
DECAY_SCALE = 0.6065306597
QK_EXP2_SCALE = (HEAD_DIM ** -0.5) * float(np.log2(np.e))
LANES = 128
ROUTER_LANES = 128
VMEM_LIMIT = 56 * 1024 * 1024
ATTN_ROWS = 128
MOE_W = D_MODEL + 2 * ROUTER_LANES
MOE_TM = 256
OUT_PROJ_TM = 512
MOE_SUB = 32
MOE_STAGE = MOE_TM + N_GROUPS * MOE_SUB
MOE_TX = 512
SEG_ALIGN = 16


def _cparams(sem):
    return pltpu.CompilerParams(dimension_semantics=sem, vmem_limit_bytes=VMEM_LIMIT)


def _dot(a, b, precision=None):
    return jnp.dot(a, b, preferred_element_type=F32, precision=precision)


def _dot_nt(a, b, precision=None):
    return lax.dot_general(a, b, (((1,), (1,)), ((), ())), preferred_element_type=F32, precision=precision)


def _dot_tn(a, b, precision=None):
    return lax.dot_general(a, b, (((0,), (0,)), ((), ())), preferred_element_type=F32, precision=precision)


def _split2(x):
    hi = x.astype(BF16)
    return hi, (x - hi.astype(F32)).astype(BF16)


def _dot3(a, b):
    a_hi, a_lo = _split2(a)
    b_hi, b_lo = _split2(b)
    return _dot(a_hi, b_hi) + (_dot(a_hi, b_lo) + _dot(a_lo, b_hi))


def _head_sum(x, g):
    hi, lo = _split2(x)
    n = x.shape[-1] // LANES
    cols = [slice(j * LANES, (j + 1) * LANES) for j in range(n)]
    return jnp.concatenate([_dot(hi[:, c], g) + _dot(lo[:, c], g) for c in cols], axis=-1)


def _mod_kernel(c_ref, w_ref, b_ref, o_ref):
    c = c_ref[...]
    s = c * jax.nn.sigmoid(c)
    o_ref[...] = _dot(s, w_ref[...], HIGHEST) + b_ref[...]


def _modulation(cond, w_mod, b_mod):
    n = w_mod.shape[1]
    tn = 1024
    return pl.pallas_call(
        _mod_kernel,
        grid=(n // tn,),
        in_specs=[pl.BlockSpec((8, D_MODEL), lambda j: (0, 0)),
                  pl.BlockSpec((D_MODEL, tn), lambda j: (0, j)),
                  pl.BlockSpec((1, tn), lambda j: (0, j))],
        out_specs=pl.BlockSpec((8, tn), lambda j: (0, j)),
        out_shape=jax.ShapeDtypeStruct((8, n), F32),
        compiler_params=_cparams(("arbitrary",)),
        name="mod",
    )(cond, w_mod, b_mod.reshape(1, n))


def _rope_tables(t_len):
    half = HEAD_DIM // 2
    inv = ROPE_THETA ** (-np.arange(0, half, 2, dtype=np.float64) / half)
    t = np.arange(t_len)
    row, col = t // GRID_W, t % GRID_W
    lane = np.arange(LANES)
    i = lane % HEAD_DIM
    pos = np.where((i // half)[None, :] == 0, row[:, None], col[:, None]).astype(np.float64)
    j = i % half
    ang = pos * inv[j % (half // 2)][None, :]
    cos, sin = np.cos(ang), np.sin(ang)
    first = (j < half // 2)[None, :]
    s_up = np.where(first, -sin, 0.0)
    s_dn = np.where(first, 0.0, sin)
    return (jnp.asarray(cos, F32), jnp.asarray(s_up, F32), jnp.asarray(s_dn, F32))


def _inproj_kernel(x_ref, sh_ref, sc_ref, g_ref, w_ref, gqk_ref, grp_ref, *rest, rope):
    if rope:
        cos_ref, sup_ref, sdn_ref, q_ref, k_ref, v_ref, kf_ref, vf_ref, rw_ref = rest
    else:
        q_ref, k_ref, v_ref, kf_ref, vf_ref, rw_ref = rest
    x = x_ref[0]
    ms = jnp.mean(x * x, axis=-1, keepdims=True)
    h = x * lax.rsqrt(ms + NORM_EPS) * g_ref[...]
    h = h * (1.0 + sc_ref[0]) + sh_ref[0]
    proj = _dot(h.astype(BF16), w_ref[...])
    grp = grp_ref[...]
    lo_half = lax.broadcasted_iota(jnp.int32, (x.shape[0], LANES), 1) < HEAD_DIM
    for j in range((D_ATTN + D_KV) // LANES):
        blk = proj[:, j * LANES:(j + 1) * LANES]
        ss = _head_sum(blk * blk, grp) * (1.0 / HEAD_DIM)
        nb = blk * lax.rsqrt(ss + NORM_EPS) * gqk_ref[:, j * LANES:(j + 1) * LANES]
        if rope:
            nb = (nb * cos_ref[...] + pltpu.roll(nb, LANES - 16, 1) * sup_ref[...]
                  + pltpu.roll(nb, 16, 1) * sdn_ref[...])
        if j < D_ATTN // LANES:
            nbq = nb * QK_EXP2_SCALE
            for half in range(2):
                hq = 2 * j + half
                q_ref[0, hq // GQA_GROUP, hq % GQA_GROUP] = nbq[:, half * HEAD_DIM:(half + 1) * HEAD_DIM].astype(BF16)
        else:
            kf_ref[0] = nb
            k_ref[0, 0] = nb[:, :HEAD_DIM].astype(BF16)
            k_ref[0, 1] = nb[:, HEAD_DIM:].astype(BF16)
    vblk = proj[:, D_ATTN + D_KV:D_QKV]
    vf_ref[0] = vblk
    v_ref[0, 0] = jnp.where(lo_half, vblk, 1.0).astype(BF16)
    v_ref[0, 1] = jnp.where(lo_half, pltpu.roll(vblk, HEAD_DIM, 1), 1.0).astype(BF16)
    rw_ref[0] = proj[:, D_QKV:]


def _in_proj(x3, shift, scale, norm_g, w_in_p, gqk, grp, rope, tm):
    b, t, _ = x3.shape
    bm = shift.shape[0]
    mod_map = (lambda i, j: (i, 0, 0)) if bm > 1 else (lambda i, j: (0, 0, 0))
    full = lambda a: pl.BlockSpec(a.shape, lambda i, j: (0,) * a.ndim)
    tok = lambda w: pl.BlockSpec((1, tm, w), lambda i, j: (i, j, 0))
    in_specs = [tok(D_MODEL), pl.BlockSpec((1, 1, D_MODEL), mod_map), pl.BlockSpec((1, 1, D_MODEL), mod_map),
                full(norm_g), full(w_in_p), full(gqk), full(grp)]
    args = [x3, shift, scale, norm_g, w_in_p, gqk, grp]
    if rope:
        in_specs += [pl.BlockSpec((tm, LANES), lambda i, j: (j, 0))] * 3
        args += list(_rope_tables(t))
    out_shape = (jax.ShapeDtypeStruct((b, N_KV_HEADS, GQA_GROUP, t, HEAD_DIM), BF16),
                 jax.ShapeDtypeStruct((b, N_KV_HEADS, t, HEAD_DIM), BF16),
                 jax.ShapeDtypeStruct((b, N_KV_HEADS, t, 2 * HEAD_DIM), BF16),
                 jax.ShapeDtypeStruct((b, t, D_KV), F32), jax.ShapeDtypeStruct((b, t, D_KV), F32),
                 jax.ShapeDtypeStruct((b, t, D_RWKV_PAD), F32))
    out_specs = (pl.BlockSpec((1, N_KV_HEADS, GQA_GROUP, tm, HEAD_DIM), lambda i, j: (i, 0, 0, j, 0)),
                 pl.BlockSpec((1, N_KV_HEADS, tm, HEAD_DIM), lambda i, j: (i, 0, j, 0)),
                 pl.BlockSpec((1, N_KV_HEADS, tm, 2 * HEAD_DIM), lambda i, j: (i, 0, j, 0)),
                 tok(D_KV), tok(D_KV), tok(D_RWKV_PAD))
    return pl.pallas_call(
        functools.partial(_inproj_kernel, rope=rope),
        grid=(b, t // tm), in_specs=in_specs, out_specs=out_specs, out_shape=out_shape,
        compiler_params=_cparams(("parallel", "parallel")),
        name="in_proj_rope" if rope else "in_proj",
    )(*args)


def _attn_kernel(q_ref, k_ref, v_ref, o_ref):
    g, tq, hd = q_ref.shape[2:]
    sub = min(ATTN_ROWS, tq)
    k = k_ref[0, 0]
    v = v_ref[0, 0]
    slabs = [slice(i * sub, (i + 1) * sub) for i in range(tq // sub)]
    qs = [q_ref[0, 0, :, sl, :].reshape(g * sub, hd) for sl in slabs]
    ss = [_dot_nt(q, k) for q in qs]
    ps = [jnp.exp2(s - jnp.max(s, axis=-1, keepdims=True)).astype(BF16) for s in ss]
    for sl, p in zip(slabs, ps):
        o = _dot(p, v)
        o = o[:, :hd] / pltpu.roll(o, hd, 1)[:, :hd]
        o_ref[0, sl, :] = jnp.concatenate([o[i * sub:(i + 1) * sub] for i in range(g)], axis=-1)


def _attention(q5, k4, v4, tq):
    b, hk, g, t, hd = q5.shape
    tk = k4.shape[2]
    return pl.pallas_call(
        _attn_kernel,
        grid=(b, hk, t // tq),
        in_specs=[pl.BlockSpec((1, 1, g, tq, hd), lambda i, j, l: (i, j, 0, l, 0)),
                  pl.BlockSpec((1, 1, tk, hd), lambda i, j, l: (i, j, 0, 0)),
                  pl.BlockSpec((1, 1, tk, 2 * hd), lambda i, j, l: (i, j, 0, 0))],
        out_specs=pl.BlockSpec((1, tq, g * hd), lambda i, j, l: (i, l, j)),
        out_shape=jax.ShapeDtypeStruct((b, t, hk * g * hd), F32),
        compiler_params=_cparams(("parallel", "parallel", "arbitrary")),
        name="attn",
    )(q5, k4, v4)


def _prep_kernel(rw_ref, hp_ref, hn_ref, mu_ref, wd_ref, w0_ref, wa_ref, a0_ref, wg_ref, kk_ref, ka_ref, rk_ref,
                 grp_ref, r_o, lw_o, kd_o, v_o, kk_o, bd_o, g_o, bonus_o):
    i = pl.program_id(1)
    n = pl.num_programs(1)
    cur = rw_ref[0]
    tt = cur.shape[0]
    rid = lax.broadcasted_iota(jnp.int32, cur.shape, 0)
    prev_row = jnp.where(i > 0, hp_ref[0, 7:8, :], 0.0)
    next_row = jnp.where(i < n - 1, hn_ref[0, 0:1, :], 0.0)
    prev = jnp.where(rid == 0, prev_row, pltpu.roll(cur, 1, 0))
    nxt = jnp.where(rid == tt - 1, next_row, pltpu.roll(cur, tt - 1, 0))
    p = cur + mu_ref[0:1, :] * (prev - cur) + mu_ref[1:2, :] * (nxt - cur)
    r = p[:, 0:D_RWKV]
    k = p[:, D_RWKV:2 * D_RWKV]
    v = p[:, 2 * D_RWKV:3 * D_RWKV]
    lo = p[:, 3 * D_RWKV:3 * D_RWKV + LANES]
    gd = p[:, 3 * D_RWKV + LANES:]
    grp = grp_ref[...]
    wlog = _dot3(jnp.tanh(lo), wd_ref[...]) + w0_ref[...]
    alog = _dot3(lo, wa_ref[...]) + a0_ref[...]
    g_o[0] = _dot3(jax.nn.sigmoid(gd), wg_ref[...])
    kx = k * kk_ref[...]
    kk = kx * lax.rsqrt(_head_sum(kx * kx, grp) + 1e-12)
    r_o[0] = r
    v_o[0] = v
    kk_o[0] = kk
    bonus_o[0] = _head_sum(r * k * rk_ref[...], grp) * v
    for z in range(2):
        a = jax.nn.sigmoid(alog[:, z * D_RWKV:(z + 1) * D_RWKV])
        lw_o[z, 0] = -DECAY_SCALE * jax.nn.sigmoid(wlog[:, z * D_RWKV:(z + 1) * D_RWKV])
        kd_o[z, 0] = k * (1.0 + (a - 1.0) * ka_ref[...])
        bd_o[z, 0] = kk * a


def _rwkv_prep(rw3, mu_p, wd_cat, w0_cat, wa_cat, a0_cat, wg_p, k_k, k_a, r_k, grp, tt):
    b, t, _ = rw3.shape
    nt = t // tt
    hb = tt // 8
    one = jax.ShapeDtypeStruct((b, t, D_RWKV), F32)
    two = jax.ShapeDtypeStruct((2, b, t, D_RWKV), F32)
    s_one = pl.BlockSpec((1, tt, D_RWKV), lambda i, j: (i, j, 0))
    s_two = pl.BlockSpec((2, 1, tt, D_RWKV), lambda i, j: (0, i, j, 0))
    full = lambda a: pl.BlockSpec(a.shape, lambda i, j: (0,) * a.ndim)
    consts = [mu_p, wd_cat, w0_cat, wa_cat, a0_cat, wg_p, k_k, k_a, r_k, grp]
    return pl.pallas_call(
        _prep_kernel,
        grid=(b, nt),
        in_specs=[pl.BlockSpec((1, tt, D_RWKV_PAD), lambda i, j: (i, j, 0)),
                  pl.BlockSpec((1, 8, D_RWKV_PAD), lambda i, j: (i, jnp.maximum(j * hb - 1, 0), 0)),
                  pl.BlockSpec((1, 8, D_RWKV_PAD), lambda i, j: (i, jnp.minimum((j + 1) * hb, t // 8 - 1), 0))]
                 + [full(a) for a in consts],
        out_specs=(s_one, s_two, s_two, s_one, s_one, s_two, s_one, s_one),
        out_shape=(one, two, two, one, one, two, one, one),
        compiler_params=_cparams(("parallel", "parallel")),
        name="rwkv_prep",
    )(rw3, rw3, rw3, *consts)


INV_BASE = 8
FILL_PER_STAGE = 4


def _bdot(a, b):
    return _dot(a.astype(BF16), b.astype(BF16))


def _scan_kernel(*refs, chunk):
    ins, (s0_ref, yf_ref, yb_ref, sT_ref, s_scr, ops_scr) = (refs[0:6], refs[6:12]), refs[12:]
    y_refs = (yf_ref, yb_ref)
    c = chunk

    @pl.when(pl.program_id(1) == 0)
    def _():
        s_scr[...] = s0_ref[0]

    row = lax.broadcasted_iota(jnp.int32, (c, c), 0)
    col = lax.broadcasted_iota(jnp.int32, (c, c), 1)
    eye = (row == col).astype(F32)
    same_blk = {}
    n = INV_BASE
    while n <= c:
        sh = jnp.int32(n.bit_length() - 1)
        same_blk[n] = lax.shift_right_logical(row, sh) == lax.shift_right_logical(col, sh)
        n *= 2
    heads = range(N_RWKV_HEADS)
    sls = [slice(h * HEAD_DIM, (h + 1) * HEAD_DIM) for h in heads]

    incl, strict, v, p_tot = [], [], [], []
    for d, (r_ref, lw_ref, k_ref, v_ref, a_ref, b_ref) in enumerate(ins):
        incl.append(row >= col if d == 0 else row <= col)
        strict.append(row > col if d == 0 else row < col)
        lw = lw_ref[0, 0]
        lw_hi, lw_lo = _split2(lw)
        inclb = jnp.where(incl[d], 1.0, 0.0).astype(BF16)
        cl_all = _dot(inclb, lw_hi) + _dot(inclb, lw_lo)
        tot_all = jnp.sum(lw, axis=0, keepdims=True)
        for blk in range(D_RWKV // LANES):
            cs = slice(blk * LANES, (blk + 1) * LANES)
            cl, tot = cl_all[:, cs], tot_all[:, cs]
            mid = 0.5 * tot
            e_inv = jnp.exp(mid - cl)
            e_end = jnp.exp(tot - cl)
            s_mid = jnp.exp(-mid)
            r_abs = r_ref[0, :, cs] * jnp.exp(cl)
            a_abs = a_ref[0, :, cs] * jnp.exp(cl - lw_ref[0, 0, :, cs])
            k_blk, b_blk = k_ref[0, 0, :, cs], b_ref[0, 0, :, cs]
            for which, (top, bottom) in enumerate(((a_abs, r_abs), (a_abs * s_mid, r_abs * s_mid),
                                                   (k_blk * e_inv, b_blk * e_inv), (k_blk * e_end, b_blk * e_end))):
                ops_scr[d, which, 0:c, cs] = top.astype(BF16)
                ops_scr[d, which, c:2 * c, cs] = bottom.astype(BF16)
        v.append(v_ref[0].astype(BF16))
        p_tot.append(jnp.exp(tot_all))
    ar_abs, ar_mid, kb_inv, kb_end = ([ops_scr.at[d, which] for d in range(2)] for which in range(4))

    chains = [(d, h) for d in range(2) for h in heads]
    ids = range(len(chains))
    s_old = [s_scr[d, h] for d, h in chains]
    vhs = [v[d][:, sls[h]] for d, h in chains]
    grams = [_dot_nt(ar_mid[d][:, sls[h]], kb_inv[d][:, sls[h]]) for d, h in chains]
    masked = [jnp.concatenate([jnp.where(strict[d], grams[i][:c, :c], 0.0),
                               jnp.where(incl[d], grams[i][c:, :c], 0.0)], axis=0) for i, (d, h) in enumerate(chains)]
    lmats = [jnp.where(strict[d], grams[i][:c, c:], 0.0) for i, (d, h) in enumerate(chains)]
    rb_incl = [jnp.where(incl[d], grams[i][c:, c:], 0.0) for i, (d, h) in enumerate(chains)]
    from_s, from_v, fillers = [None] * len(chains), [None] * len(chains), []
    for i, (d, h) in enumerate(chains):
        fillers.append(lambda i=i, d=d, h=h: from_s.__setitem__(
            i, _dot_nt(ar_abs[d][:, sls[h]], s_old[i].astype(BF16))))
        fillers.append(lambda i=i: from_v.__setitem__(i, _bdot(masked[i], vhs[i])))

    def emit_fillers(k):
        for _ in range(min(k, len(fillers))):
            fillers.pop(0)()

    l0s = [jnp.where(same_blk[INV_BASE], lm, 0.0) for lm in lmats]
    xs = [eye - l0 for l0 in l0s]
    pws = [_bdot(l0, l0) for l0 in l0s]
    span = 2
    while 2 * span < INV_BASE:
        both = [_bdot(jnp.concatenate([xs[i], pws[i]], axis=0), pws[i]) for i in ids]
        xs = [xs[i] + both[i][:c] for i in ids]
        pws = [both[i][c:] for i in ids]
        span *= 2
    xs = [xs[i] + _bdot(xs[i], pws[i]) for i in ids]
    n = INV_BASE
    while n < c:
        def take(m, d, n=n):
            return m.reshape(c // (2 * n), 2, n, m.shape[-1])[:, 1 - d].reshape(c // 2, m.shape[-1])

        def put(hm, d, n=n):
            h4 = hm.reshape(c // (2 * n), 1, n, hm.shape[-1])
            parts = [jnp.zeros_like(h4), h4] if d == 0 else [h4, jnp.zeros_like(h4)]
            return jnp.concatenate(parts, axis=1).reshape(c, hm.shape[-1])

        pair = same_blk[2 * n] & jnp.logical_not(same_blk[n])
        ts = [_bdot(take(jnp.where(pair, lmats[i], 0.0), d), xs[i]) for i, (d, h) in enumerate(chains)]
        emit_fillers(FILL_PER_STAGE)
        dx = [_bdot(take(xs[i], d), put(ts[i], d)) for i, (d, h) in enumerate(chains)]
        emit_fillers(FILL_PER_STAGE)
        xs = [xs[i] - put(dx[i], d) for i, (d, h) in enumerate(chains)]
        n *= 2
    emit_fillers(len(fillers))
    us = [_bdot(xs[i], from_s[i][:c] + from_v[i][:c]) for i in ids]
    yu = [_bdot(rb_incl[i], us[i]) for i in ids]
    ds = [_dot_tn(jnp.concatenate([vhs[i], (-us[i]).astype(BF16)], axis=0), kb_end[d][:, sls[h]])
          for i, (d, h) in enumerate(chains)]
    for d in range(2):
        y_refs[d][0] = jnp.concatenate([from_s[i][c:] + from_v[i][c:] - yu[i]
                                        for i, (dd, h) in enumerate(chains) if dd == d], axis=-1)
    for i, (d, h) in enumerate(chains):
        s_scr[d, h] = s_old[i] * p_tot[d][:, sls[h]] + ds[i]

    @pl.when(pl.program_id(1) == pl.num_programs(1) - 1)
    def _():
        sT_ref[0] = s_scr[...]


def _rwkv_scan(r, lw2, kd2, v, kk, bd2, s0, chunk):
    b, t, _ = r.shape
    nc = t // chunk
    in_specs, args = [], []
    for d in range(2):
        tmap = (lambda j: j) if d == 0 else (lambda j: nc - 1 - j)
        s_one = pl.BlockSpec((1, chunk, D_RWKV), lambda i, j, tmap=tmap: (i, tmap(j), 0))
        s_two = pl.BlockSpec((1, 1, chunk, D_RWKV), lambda i, j, tmap=tmap, d=d: (d, i, tmap(j), 0))
        in_specs += [s_one, s_two, s_two, s_one, s_one, s_two]
        args += [r, lw2, kd2, v, kk, bd2]
    s_st = pl.BlockSpec((1, 2, N_RWKV_HEADS, HEAD_DIM, HEAD_DIM), lambda i, j: (i, 0, 0, 0, 0))
    y_specs = tuple(pl.BlockSpec((1, chunk, D_RWKV), lambda i, j, tmap=tmap: (i, tmap(j), 0))
                    for tmap in ((lambda j: j), (lambda j: nc - 1 - j)))
    y_shape = jax.ShapeDtypeStruct((b, t, D_RWKV), F32)
    return pl.pallas_call(
        functools.partial(_scan_kernel, chunk=chunk),
        grid=(b, nc),
        in_specs=in_specs + [s_st],
        out_specs=y_specs + (s_st,),
        out_shape=(y_shape, y_shape, jax.ShapeDtypeStruct(s0.shape, F32)),
        scratch_shapes=[pltpu.VMEM((2, N_RWKV_HEADS, HEAD_DIM, HEAD_DIM), F32),
                        pltpu.VMEM((2, 4, 2 * chunk, D_RWKV), BF16)],
        compiler_params=_cparams(("parallel", "arbitrary")),
        name="rwkv_scan",
    )(*args, s0)


def _outproj_kernel(x_ref, at_ref, yf_ref, yb_ref, bonus_ref, gate_ref, lng_ref, lnb_ref, grp_ref, wo_ref,
                    g1_ref, sh_ref, sc_ref, n2_ref, wr_ref, br_ref, x1_ref, hx_ref, cnt_ref):
    grp = grp_ref[...]
    y = yf_ref[0] + yb_ref[0]
    mean = _head_sum(y, grp) * (1.0 / HEAD_DIM)
    yc = y - mean
    var = _head_sum(yc * yc, grp) * (1.0 / HEAD_DIM)
    yn = yc * lax.rsqrt(var + GN_EPS) * lng_ref[...] + lnb_ref[...]
    rw_out = (yn + bonus_ref[0]) * gate_ref[0]
    mix = (_dot(at_ref[0].astype(BF16), wo_ref[0:D_ATTN, :])
           + _dot(rw_out.astype(BF16), wo_ref[D_ATTN:, :]))
    x1 = x_ref[0] + g1_ref[0] * mix
    x1_ref[0] = x1
    ms = jnp.mean(x1 * x1, axis=-1, keepdims=True)
    h2 = x1 * lax.rsqrt(ms + NORM_EPS) * n2_ref[...]
    h2 = h2 * (1.0 + sc_ref[0]) + sh_ref[0]
    hx_ref[0, :, 0:D_MODEL] = h2.astype(BF16)
    logits = _dot3(h2, wr_ref[...]) + br_ref[...]
    lane = lax.broadcasted_iota(jnp.int32, logits.shape, 1)
    neg = -jnp.inf
    big = jnp.int32(1 << 20)
    lc = jnp.where(lane < N_GROUPS, logits, neg)
    mc = jnp.max(lc, axis=-1, keepdims=True)
    g_w = 1.0 / jnp.sum(jnp.exp(lc - mc), axis=-1, keepdims=True)
    g_idx = jnp.min(jnp.where(lc == mc, lane, big), axis=-1, keepdims=True)
    eid = lane - N_GROUPS
    in_grp = (eid >= 0) & (eid < N_EXPERTS) & (lax.shift_right_arithmetic(eid, 2) == g_idx)
    lf = jnp.where(in_grp, logits, neg)
    m1 = jnp.max(lf, axis=-1, keepdims=True)
    i1 = jnp.min(jnp.where(lf == m1, lane, big), axis=-1, keepdims=True)
    lf2 = jnp.where(lane == i1, neg, lf)
    m2 = jnp.max(lf2, axis=-1, keepdims=True)
    i2 = jnp.min(jnp.where(lf2 == m2, lane, big), axis=-1, keepdims=True)
    e2 = jnp.exp(m2 - m1)
    w1 = 1.0 / (1.0 + e2)
    w2 = e2 * w1
    cmb = g_w * (jnp.where(lane == i1, w1, 0.0) + jnp.where(lane == i2, w2, 0.0))
    rec = jnp.where(lane == 0, g_idx.astype(F32), cmb)
    rec_hi, rec_lo = _split2(rec)
    hx_ref[0, :, D_MODEL:D_MODEL + ROUTER_LANES] = rec_hi
    hx_ref[0, :, D_MODEL + ROUTER_LANES:] = rec_lo
    hot = jnp.where((lane == g_idx) & (lane < N_GROUPS), 1.0, 0.0)
    for s in range(cnt_ref.shape[1]):
        part = jnp.sum(hot[s * MOE_TM:(s + 1) * MOE_TM], axis=0, keepdims=True)
        cnt_ref[0, s] = jnp.broadcast_to(part, cnt_ref.shape[2:])


def _out_proj(x3, attn3, yf, yb, bonus, gate, ln_g, ln_b, grp, w_out_b, g1, sh2, sc2, norm2_g, w_r, b_r):
    b, t, _ = x3.shape
    tm = OUT_PROJ_TM if t % OUT_PROJ_TM == 0 else MOE_TM
    bm = g1.shape[0]
    mod_map = (lambda i, j: (i, 0, 0)) if bm > 1 else (lambda i, j: (0, 0, 0))
    tok = lambda w: pl.BlockSpec((1, tm, w), lambda i, j: (i, j, 0))
    full = lambda a: pl.BlockSpec(a.shape, lambda i, j: (0,) * a.ndim)
    mod = pl.BlockSpec((1, 1, D_MODEL), mod_map)
    return pl.pallas_call(
        _outproj_kernel,
        grid=(b, t // tm),
        in_specs=[tok(D_MODEL), tok(D_ATTN), tok(D_RWKV), tok(D_RWKV),
                  tok(D_RWKV), tok(D_RWKV), full(ln_g), full(ln_b), full(grp), full(w_out_b),
                  mod, mod, mod, full(norm2_g), full(w_r), full(b_r)],
        out_specs=(tok(D_MODEL), tok(MOE_W), pl.BlockSpec((1, tm // MOE_TM, 8, LANES), lambda i, j: (i, j, 0, 0))),
        out_shape=(jax.ShapeDtypeStruct((b, t, D_MODEL), F32), jax.ShapeDtypeStruct((b, t, MOE_W), BF16),
                   jax.ShapeDtypeStruct((b, t // MOE_TM, 8, LANES), F32)),
        compiler_params=_cparams(("parallel", "parallel")),
        name="out_proj",
    )(x3, attn3, yf, yb, bonus, gate, ln_g, ln_b, grp, w_out_b, g1, sh2, sc2, norm2_g, w_r, b_r)


def _tile_sort_matrix(rec_hi, plan_ref, tile):
    tm = rec_hi.shape[0]
    sel = ((lax.broadcasted_iota(jnp.int32, (8, LANES), 0) == 0)
           & (lax.broadcasted_iota(jnp.int32, (8, LANES), 1) == 0))
    g_row = _dot_nt(jnp.where(sel, 1.0, 0.0).astype(BF16), rec_hi)[0:1, :]
    g_col = rec_hi.astype(F32)[:, 0:1]
    r = lax.broadcasted_iota(jnp.int32, (tm, tm), 0)
    c = lax.broadcasted_iota(jnp.int32, (tm, tm), 1)
    earlier_same = jnp.where((g_col == g_row) & (r < c), 1.0, 0.0)
    pos_row = jnp.sum(earlier_same, axis=0, keepdims=True)
    for g in range(N_GROUPS):
        local = _plan_entry(plan_ref, tile, g)[2].astype(F32)
        pos_row = pos_row + jnp.where(g_row == float(g), local, 0.0)
    stage_row = lax.broadcasted_iota(jnp.int32, (MOE_STAGE, tm), 0).astype(F32)
    return jnp.where(stage_row == pos_row, 1.0, 0.0).astype(BF16)


def _plan_entry(plan_ref, tile, g):
    at = (tile * N_GROUPS + g) * 3
    return pl.multiple_of(plan_ref[at], SEG_ALIGN), plan_ref[at + 1], pl.multiple_of(plan_ref[at + 2], MOE_SUB)


def _for_each_granule(plan_ref, tile, fn):
    for g in range(N_GROUPS):
        start, n_gran, local = _plan_entry(plan_ref, tile, g)

        def body(s, carry, g=g, start=start, local=local):
            fn(g, pl.multiple_of(start + s * MOE_SUB, SEG_ALIGN), pl.multiple_of(local + s * MOE_SUB, MOE_SUB))
            return carry

        lax.fori_loop(0, n_gran, body, 0)


def _dispatch_kernel(plan_ref, hx_ref, xs_in_ref, xs_ref, buf, sem, *, tile0):
    del xs_in_ref
    i = pl.program_id(0)
    x = hx_ref[...]
    perm = _tile_sort_matrix(x[:, D_MODEL:D_MODEL + ROUTER_LANES], plan_ref, tile0 + i)
    staged = _dot(perm, x).astype(BF16)

    def copy(g, seg_row, stage_row):
        return pltpu.make_async_copy(buf.at[pl.ds(stage_row, MOE_SUB), :],
                                     xs_ref.at[pl.ds(seg_row, MOE_SUB), :], sem.at[g])

    @pl.when(i > 0)
    def _():
        _for_each_granule(plan_ref, tile0 + i - 1, lambda *a: copy(*a).wait())

    buf[...] = staged
    _for_each_granule(plan_ref, tile0 + i, lambda *a: copy(*a).start())

    @pl.when(i == pl.num_programs(0) - 1)
    def _():
        _for_each_granule(plan_ref, tile0 + i, lambda *a: copy(*a).wait())


def _moe_dispatch(plan, hx2, xs, tile0):
    n = hx2.shape[0]
    grid_spec = pltpu.PrefetchScalarGridSpec(
        num_scalar_prefetch=1, grid=(n // MOE_TM,),
        in_specs=[pl.BlockSpec((MOE_TM, MOE_W), lambda i, plan: (i, 0)), pl.BlockSpec(memory_space=pl.ANY)],
        out_specs=pl.BlockSpec(memory_space=pl.ANY),
        scratch_shapes=[pltpu.VMEM((MOE_STAGE, MOE_W), BF16), pltpu.SemaphoreType.DMA((N_GROUPS,))])
    return pl.pallas_call(
        functools.partial(_dispatch_kernel, tile0=tile0),
        grid_spec=grid_spec,
        out_shape=jax.ShapeDtypeStruct(xs.shape, xs.dtype),
        input_output_aliases={2: 0},
        compiler_params=_cparams(("arbitrary",)),
        name="moe_dispatch",
    )(plan, hx2, xs)


def _experts_kernel(tg_ref, xs_ref, wg_ref, wu_ref, wd_ref, ys_ref):
    g = tg_ref[pl.program_id(0)]

    @pl.when(g >= N_GROUPS)
    def _():
        ys_ref[...] = jnp.zeros_like(ys_ref)

    @pl.when(g < N_GROUPS)
    def _():
        x = xs_ref[...]
        h = x[:, 0:D_MODEL]
        rec = (x[:, D_MODEL:D_MODEL + ROUTER_LANES].astype(F32) + x[:, D_MODEL + ROUTER_LANES:].astype(F32))
        lane = lax.broadcasted_iota(jnp.int32, rec.shape, 1)
        first = N_GROUPS + EXPERTS_PER_GROUP * g
        scaled = []
        for e in range(EXPERTS_PER_GROUP):
            a = _dot(h, wg_ref[0, e])
            hid = a * jax.nn.sigmoid(a) * _dot(h, wu_ref[0, e])
            c_e = jnp.sum(jnp.where(lane == first + e, rec, 0.0), axis=-1, keepdims=True)
            scaled.append((hid * c_e).astype(BF16))
        ys_ref[...] = _dot(jnp.concatenate(scaled, axis=-1), wd_ref[0]).astype(BF16)


def _moe_experts(tile_group, xs, wg4, wu4, wd4):
    p = xs.shape[0]
    grp_map = lambda nd: (lambda j, tg: (jnp.minimum(tg[j], N_GROUPS - 1),) + (0,) * (nd - 1))
    grid_spec = pltpu.PrefetchScalarGridSpec(
        num_scalar_prefetch=1, grid=(p // MOE_TX,),
        in_specs=[pl.BlockSpec((MOE_TX, MOE_W), lambda j, tg: (j, 0)),
                  pl.BlockSpec((1,) + wg4.shape[1:], grp_map(wg4.ndim)),
                  pl.BlockSpec((1,) + wu4.shape[1:], grp_map(wu4.ndim)),
                  pl.BlockSpec((1,) + wd4.shape[1:], grp_map(wd4.ndim))],
        out_specs=pl.BlockSpec((MOE_TX, D_MODEL), lambda j, tg: (j, 0)))
    return pl.pallas_call(
        _experts_kernel,
        grid_spec=grid_spec,
        out_shape=jax.ShapeDtypeStruct((p, D_MODEL), BF16),
        compiler_params=_cparams(("arbitrary",)),
        name="moe_experts",
    )(tile_group, xs, wg4, wu4, wd4)


def _combine_kernel(plan_ref, rec_ref, x1_ref, g2_ref, ys_ref, o_ref, buf, sem, *, tile0):
    lin = pl.program_id(0) * pl.num_programs(1) + pl.program_id(1)
    n = pl.num_programs(0) * pl.num_programs(1)
    slot = lin % 2

    def copy(slot_t):
        return lambda g, seg_row, stage_row: pltpu.make_async_copy(
            ys_ref.at[pl.ds(seg_row, MOE_SUB), :], buf.at[slot_t, pl.ds(stage_row, MOE_SUB), :], sem.at[slot_t, g])

    @pl.when(lin == 0)
    def _():
        buf[...] = jnp.zeros_like(buf)
        _for_each_granule(plan_ref, tile0, lambda *a: copy(0)(*a).start())

    @pl.when(lin + 1 < n)
    def _():
        _for_each_granule(plan_ref, tile0 + lin + 1, lambda *a: copy(1 - slot)(*a).start())

    perm = _tile_sort_matrix(rec_ref[0], plan_ref, tile0 + lin)
    _for_each_granule(plan_ref, tile0 + lin, lambda *a: copy(slot)(*a).wait())
    y = _dot_tn(perm, buf[slot])
    o_ref[0] = x1_ref[0] + g2_ref[0] * y


def _moe_combine(plan, hx3, x1, g2, ys, tile0):
    b, t, _ = x1.shape
    bm = g2.shape[0]
    mod_map = (lambda i, j, plan: (i, 0, 0)) if bm > 1 else (lambda i, j, plan: (0, 0, 0))
    rec_blk = D_MODEL // ROUTER_LANES
    grid_spec = pltpu.PrefetchScalarGridSpec(
        num_scalar_prefetch=1, grid=(b, t // MOE_TM),
        in_specs=[pl.BlockSpec((1, MOE_TM, ROUTER_LANES), lambda i, j, plan: (i, j, rec_blk)),
                  pl.BlockSpec((1, MOE_TM, D_MODEL), lambda i, j, plan: (i, j, 0)),
                  pl.BlockSpec((1, 1, D_MODEL), mod_map),
                  pl.BlockSpec(memory_space=pl.ANY)],
        out_specs=pl.BlockSpec((1, MOE_TM, D_MODEL), lambda i, j, plan: (i, j, 0)),
        scratch_shapes=[pltpu.VMEM((2, MOE_STAGE, D_MODEL), BF16), pltpu.SemaphoreType.DMA((2, N_GROUPS))])
    return pl.pallas_call(
        functools.partial(_combine_kernel, tile0=tile0),
        grid_spec=grid_spec,
        out_shape=jax.ShapeDtypeStruct(x1.shape, F32),
        compiler_params=_cparams(("arbitrary", "arbitrary")),
        name="moe_combine",
    )(plan, hx3, x1, g2, ys)


def _moe_plan(cnt, n_rows):
    seg = (cnt + (SEG_ALIGN - 1)) // SEG_ALIGN * SEG_ALIGN
    used = (jnp.sum(seg, axis=0) + (MOE_TX - 1)) // MOE_TX * MOE_TX
    size = used + MOE_TX
    base = jnp.cumsum(size) - size
    start = base[None, :] + jnp.cumsum(seg, axis=0) - seg
    n_gran = (cnt + (MOE_SUB - 1)) // MOE_SUB
    stage = n_gran * MOE_SUB
    local = jnp.cumsum(stage, axis=1) - stage
    starts = jnp.arange(n_rows // MOE_TX, dtype=jnp.int32)[:, None] * MOE_TX
    inside = (starts >= base[None, :]) & (starts < (base + used)[None, :])
    tile_group = jnp.where(jnp.any(inside, axis=1), jnp.argmax(inside, axis=1), N_GROUPS)
    plan = jnp.stack([start, n_gran, local], axis=-1).reshape(-1)
    return plan.astype(jnp.int32), tile_group.astype(jnp.int32)


def _moe_rows(n_tokens):
    n_tiles = n_tokens // MOE_TM
    bound = n_tokens + n_tiles * N_GROUPS * (SEG_ALIGN - 1) + N_GROUPS * 2 * MOE_TX
    return (bound + MOE_TX - 1) // MOE_TX * MOE_TX


def _layer(x3, mod6, lp, t_tiles, ctx):
    b, t, _ = x3.shape
    sh1, sc1, g1, sh2, sc2, _ = mod6
    t_tiles = {name: min(size, t) for name, size in t_tiles.items()}
    rope = ctx is not None
    n = b * t
    q5, k4, v4, k, v, rw = _in_proj(x3, sh1, sc1, lp["norm1_g"], lp["w_in_p"], lp["gqk"], lp["grp"],
                                    rope, t_tiles["in_proj"])
    if ctx is not None:
        ctx_k, ctx_v, ctx_state = ctx
        ck = ctx_k.transpose(0, 2, 1, 3)
        cv = ctx_v.transpose(0, 2, 1, 3)
        k4 = jnp.concatenate([k4, ck.astype(BF16)], axis=2)
        v4 = jnp.concatenate([v4, jnp.concatenate([cv, jnp.ones_like(cv)], axis=-1).astype(BF16)], axis=2)
        s0 = ctx_state
    else:
        s0 = jnp.zeros((b, 2, N_RWKV_HEADS, HEAD_DIM, HEAD_DIM), F32)
    attn3 = _attention(q5, k4, v4, t_tiles["attn"])
    r, lw2, kd2, vv, kk, bd2, gate, bonus = _rwkv_prep(
        rw, lp["mu_p"], lp["wd_cat"], lp["w0_cat"], lp["wa_cat"], lp["a0_cat"],
        lp["wg_p"], lp["k_k"], lp["k_a"], lp["r_k"], lp["grp"], t_tiles["prep"])
    yf, yb, s_t = _rwkv_scan(r, lw2, kd2, vv, kk, bd2, s0, t_tiles["chunk"])
    fb, ft = (1, n) if g1.shape[0] == 1 else (b, t)
    flat = lambda a: a.reshape(fb, ft, a.shape[-1])
    x1, hx, cnt = _out_proj(flat(x3), flat(attn3), flat(yf), flat(yb), flat(bonus), flat(gate), lp["ln_g"],
                            lp["ln_b"], lp["grp"], lp["w_out_b"], g1, sh2, sc2, lp["norm2_g"], lp["w_r"], lp["b_r"])
    return (x1, hx, cnt), k.reshape(b, t, N_KV_HEADS, HEAD_DIM), v.reshape(b, t, N_KV_HEADS, HEAD_DIM), s_t


def _moe_both(passes, g2s, lp):
    counts = [p[2][:, :, 0, :N_GROUPS].reshape(-1, N_GROUPS) for p in passes]
    tiles = [c.shape[0] for c in counts]
    n_rows = _moe_rows(sum(tiles) * MOE_TM)
    plan, tile_group = _moe_plan(jnp.concatenate(counts, axis=0).astype(jnp.int32), n_rows)
    xs = jnp.zeros((n_rows, MOE_W), BF16)
    tile0 = 0
    for (x1, hx, _), nt in zip(passes, tiles):
        xs = _moe_dispatch(plan, hx.reshape(-1, MOE_W), xs, tile0)
        tile0 += nt
    ys = _moe_experts(tile_group, xs, lp["wg4"], lp["wu4"], lp["wd4"])
    outs, tile0 = [], 0
    for (x1, hx, _), g2, nt in zip(passes, g2s, tiles):
        outs.append(_moe_combine(plan, hx, x1, g2, ys, tile0))
        tile0 += nt
    return outs


def _block_diag2(w):
    z, l, c = w.shape
    out = jnp.zeros((LANES, z * c), F32)
    for i in range(z):
        out = out.at[i * l:(i + 1) * l, i * c:(i + 1) * c].set(w[i])
    return out


def _layer_params(l, w_in, norm1_g, norm2_g, mu_shift, q_norm_g, k_norm_g, w0, w_lora_up, a0, a_lora_up, g_lora_up,
                  k_k, k_a, r_k, ln_x_g, ln_x_b, w_out, router_c, router_c_b, router_f, router_f_b,
                  exp_gate, exp_up, exp_down):
    lane = np.arange(LANES)
    grp = jnp.asarray((lane[:, None] // HEAD_DIM) == (lane[None, :] // HEAD_DIM), BF16)
    pad_in = D_IN_PAD - w_in.shape[2]
    wd_cat = _block_diag2(w_lora_up[l])
    wa_cat = jnp.roll(_block_diag2(a_lora_up[l]), 2 * DECAY_LORA, axis=0)
    w_r = jnp.zeros((D_MODEL, ROUTER_LANES), F32)
    w_r = w_r.at[:, :N_GROUPS].set(router_c[l]).at[:, N_GROUPS:N_GROUPS + N_EXPERTS].set(router_f[l])
    b_r = jnp.zeros((1, ROUTER_LANES), F32)
    b_r = b_r.at[0, :N_GROUPS].set(router_c_b[l]).at[0, N_GROUPS:N_GROUPS + N_EXPERTS].set(router_f_b[l])

    by_group = lambda w: w.astype(BF16).reshape(N_GROUPS, EXPERTS_PER_GROUP, D_MODEL, D_EXPERT)

    return dict(
        grp=grp,
        norm1_g=norm1_g[l].reshape(1, D_MODEL), norm2_g=norm2_g[l].reshape(1, D_MODEL),
        w_in_p=jnp.pad(w_in[l], ((0, 0), (0, pad_in))).astype(BF16),
        gqk=jnp.concatenate([jnp.tile(q_norm_g[l], N_Q_HEADS), jnp.tile(k_norm_g[l], N_KV_HEADS)]).reshape(1, -1),
        mu_p=jnp.pad(mu_shift[l], ((0, 0), (0, D_RWKV_PAD - D_RWKV_IN))),
        wd_cat=wd_cat, w0_cat=w0[l].reshape(1, 2 * D_RWKV),
        wa_cat=wa_cat, a0_cat=a0[l].reshape(1, 2 * D_RWKV),
        wg_p=jnp.pad(g_lora_up[l], ((0, LANES - GATE_LORA), (0, 0))),
        k_k=k_k[l].reshape(1, D_RWKV), k_a=k_a[l].reshape(1, D_RWKV), r_k=r_k[l].reshape(1, D_RWKV),
        ln_g=ln_x_g[l].reshape(1, D_RWKV), ln_b=ln_x_b[l].reshape(1, D_RWKV),
        w_out_b=w_out[l].astype(BF16), w_r=w_r, b_r=b_r,
        wg4=by_group(exp_gate[l]), wu4=by_group(exp_up[l]),
        wd4=exp_down[l].astype(BF16).reshape(N_GROUPS, EXPERTS_PER_GROUP * D_EXPERT, D_MODEL),
    )


CTX_TILES = dict(in_proj=256, attn=256, prep=256, chunk=128)
SMP_TILES = dict(in_proj=512, attn=256, prep=512, chunk=128)


def kernel(x_prompt, x_sample, cache_k, cache_v, state_rwkv, c, c_ctx, w_mod, b_mod, norm1_g, norm2_g, w_in, mu_shift, q_norm_g, k_norm_g, w0, w_lora_up, a0, a_lora_up, g_lora_up, k_k, k_a, r_k, ln_x_g, ln_x_b, w_out, router_c, router_c_b, router_f, router_f_b, exp_gate, exp_up, exp_down):
    depth = w_mod.shape[0]
    db = x_sample.shape[0]
    y_prompt, y_sample = x_prompt, x_sample
    ks, vs, ss = [], [], []
    cond = jnp.zeros((8, D_MODEL), F32).at[:db].set(c).at[db].set(c_ctx)
    for l in range(depth):
        lp = _layer_params(l, w_in, norm1_g, norm2_g, mu_shift, q_norm_g, k_norm_g, w0, w_lora_up, a0, a_lora_up,
                           g_lora_up, k_k, k_a, r_k, ln_x_g, ln_x_b, w_out, router_c, router_c_b, router_f,
                           router_f_b, exp_gate, exp_up, exp_down)
        mod = _modulation(cond, w_mod[l], b_mod[l])
        mod_s = [mod[:db, i * D_MODEL:(i + 1) * D_MODEL].reshape(db, 1, D_MODEL) for i in range(6)]
        mod_c = [mod[db:db + 1, i * D_MODEL:(i + 1) * D_MODEL].reshape(1, 1, D_MODEL) for i in range(6)]
        pre_c, k_l, v_l, s_l = _layer(y_prompt, mod_c, lp, CTX_TILES, None)
        ks.append(k_l)
        vs.append(v_l)
        ss.append(s_l)
        pre_s, _, _, _ = _layer(y_sample, mod_s, lp, SMP_TILES, (cache_k[:, l], cache_v[:, l], state_rwkv[:, l]))
        out_c, out_s = _moe_both([pre_c, pre_s], [mod_c[5], mod_s[5]], lp)
        y_prompt, y_sample = out_c.reshape(y_prompt.shape), out_s.reshape(y_sample.shape)
    return (y_prompt, y_sample, jnp.stack(ks, axis=1), jnp.stack(vs, axis=1), jnp.stack(ss, axis=1))
```

```python
import functools

import numpy as np
import jax
import jax.numpy as jnp
from jax import lax
from jax.experimental import pallas as pl
from jax.experimental.pallas import tpu as pltpu

F32 = jnp.float32
BF16 = jnp.bfloat16
HIGHEST = lax.Precision.HIGHEST

D_MODEL = 1024
HEAD_DIM = 64
N_Q_HEADS = 8
N_KV_HEADS = 2
GQA_GROUP = N_Q_HEADS // N_KV_HEADS
D_ATTN = N_Q_HEADS * HEAD_DIM
D_KV = N_KV_HEADS * HEAD_DIM
N_RWKV_HEADS = 8
D_RWKV = 512
DECAY_LORA = 32
AAA_LORA = 32
GATE_LORA = 96
D_RWKV_IN = 3 * D_RWKV + 2 * DECAY_LORA + 2 * AAA_LORA + GATE_LORA
D_RWKV_PAD = 1792
D_QKV = D_ATTN + 2 * D_KV
D_IN_PAD = D_QKV + D_RWKV_PAD
N_GROUPS = 4
EXPERTS_PER_GROUP = 4
N_EXPERTS = 16
D_EXPERT = 512
GRID_W = 64
ROPE_THETA = 10000.0
NORM_EPS = 1e-6
GN_EPS = 64e-5
DECAY_SCALE = 0.6065306597
QK_EXP2_SCALE = (HEAD_DIM ** -0.5) * float(np.log2(np.e))
LANES = 128
ROUTER_LANES = 128
VMEM_LIMIT = 56 * 1024 * 1024
ATTN_ROWS = 128
MOE_W = D_MODEL + 2 * ROUTER_LANES
MOE_TM = 256
OUT_PROJ_TM = 512
MOE_SUB = 32
MOE_STAGE = MOE_TM + N_GROUPS * MOE_SUB
MOE_TX = 512
SEG_ALIGN = 16


def _cparams(sem):
    return pltpu.CompilerParams(dimension_semantics=sem, vmem_limit_bytes=VMEM_LIMIT)


def _dot(a, b, precision=None):
    return jnp.dot(a, b, preferred_element_type=F32, precision=precision)


def _dot_nt(a, b, precision=None):
    return lax.dot_general(a, b, (((1,), (1,)), ((), ())), preferred_element_type=F32, precision=precision)


def _dot_tn(a, b, precision=None):
    return lax.dot_general(a, b, (((0,), (0,)), ((), ())), preferred_element_type=F32, precision=precision)


def _split2(x):
    hi = x.astype(BF16)
    return hi, (x - hi.astype(F32)).astype(BF16)


def _dot3(a, b):
    a_hi, a_lo = _split2(a)
    b_hi, b_lo = _split2(b)
    return _dot(a_hi, b_hi) + (_dot(a_hi, b_lo) + _dot(a_lo, b_hi))


def _head_sum(x, g):
    hi, lo = _split2(x)
    n = x.shape[-1] // LANES
    cols = [slice(j * LANES, (j + 1) * LANES) for j in range(n)]
    return jnp.concatenate([_dot(hi[:, c], g) + _dot(lo[:, c], g) for c in cols], axis=-1)


def _mod_kernel(c_ref, w_ref, b_ref, o_ref):
    c = c_ref[...]
    s = c * jax.nn.sigmoid(c)
    o_ref[...] = _dot(s, w_ref[...], HIGHEST) + b_ref[...]


def _modulation(cond, w_mod, b_mod):
    n = w_mod.shape[1]
    tn = 1024
    return pl.pallas_call(
        _mod_kernel,
        grid=(n // tn,),
        in_specs=[pl.BlockSpec((8, D_MODEL), lambda j: (0, 0)),
                  pl.BlockSpec((D_MODEL, tn), lambda j: (0, j)),
                  pl.BlockSpec((1, tn), lambda j: (0, j))],
        out_specs=pl.BlockSpec((8, tn), lambda j: (0, j)),
        out_shape=jax.ShapeDtypeStruct((8, n), F32),
        compiler_params=_cparams(("arbitrary",)),
        name="mod",
    )(cond, w_mod, b_mod.reshape(1, n))


def _rope_tables(t_len):
    half = HEAD_DIM // 2
    inv = ROPE_THETA ** (-np.arange(0, half, 2, dtype=np.float64) / half)
    t = np.arange(t_len)
    row, col = t // GRID_W, t % GRID_W
    lane = np.arange(LANES)
    i = lane % HEAD_DIM
    pos = np.where((i // half)[None, :] == 0, row[:, None], col[:, None]).astype(np.float64)
    j = i % half
    ang = pos * inv[j % (half // 2)][None, :]
    cos, sin = np.cos(ang), np.sin(ang)
    first = (j < half // 2)[None, :]
    s_up = np.where(first, -sin, 0.0)
    s_dn = np.where(first, 0.0, sin)
    return (jnp.asarray(cos, F32), jnp.asarray(s_up, F32), jnp.asarray(s_dn, F32))


def _project(x_ref, sh_ref, sc_ref, g_ref, w_ref, gqk_ref, grp_ref, rope_refs, q_ref, k_ref, v_ref, kf_ref, vf_ref):
    rope = rope_refs is not None
    if rope:
        cos_ref, sup_ref, sdn_ref = rope_refs
    x = x_ref[0]
    ms = jnp.mean(x * x, axis=-1, keepdims=True)
    h = x * lax.rsqrt(ms + NORM_EPS) * g_ref[...]
    h = h * (1.0 + sc_ref[0]) + sh_ref[0]
    proj = _dot(h.astype(BF16), w_ref[...])
    grp = grp_ref[...]
    lo_half = lax.broadcasted_iota(jnp.int32, (x.shape[0], LANES), 1) < HEAD_DIM
    for j in range((D_ATTN + D_KV) // LANES):
        blk = proj[:, j * LANES:(j + 1) * LANES]
        ss = _head_sum(blk * blk, grp) * (1.0 / HEAD_DIM)
        nb = blk * lax.rsqrt(ss + NORM_EPS) * gqk_ref[:, j * LANES:(j + 1) * LANES]
        if rope:
            nb = (nb * cos_ref[...] + pltpu.roll(nb, LANES - 16, 1) * sup_ref[...]
                  + pltpu.roll(nb, 16, 1) * sdn_ref[...])
        if j < D_ATTN // LANES:
            nbq = nb * QK_EXP2_SCALE
            for half in range(2):
                hq = 2 * j + half
                q_ref[0, hq // GQA_GROUP, hq % GQA_GROUP] = nbq[:, half * HEAD_DIM:(half + 1) * HEAD_DIM].astype(BF16)
        else:
            kf_ref[0] = nb
            k_ref[0, 0] = nb[:, :HEAD_DIM].astype(BF16)
            k_ref[0, 1] = nb[:, HEAD_DIM:].astype(BF16)
    vblk = proj[:, D_ATTN + D_KV:D_QKV]
    vf_ref[0] = vblk
    v_ref[0, 0] = jnp.where(lo_half, vblk, 1.0).astype(BF16)
    v_ref[0, 1] = jnp.where(lo_half, pltpu.roll(vblk, HEAD_DIM, 1), 1.0).astype(BF16)
    return proj[:, D_QKV:]


def _rwkv_features(cur, prev_row, next_row, mu_ref, wd_ref, w0_ref, wa_ref, a0_ref, wg_ref, kk_ref, ka_ref, rk_ref,
                   grp_ref, r_o, lw_o, kd_o, v_o, kk_o, bd_o, g_o, bonus_o):
    tt = cur.shape[0]
    rid = lax.broadcasted_iota(jnp.int32, cur.shape, 0)
    prev = jnp.where(rid == 0, prev_row, pltpu.roll(cur, 1, 0))
    nxt = jnp.where(rid == tt - 1, next_row, pltpu.roll(cur, tt - 1, 0))
    p = cur + mu_ref[0:1, :] * (prev - cur) + mu_ref[1:2, :] * (nxt - cur)
    r = p[:, 0:D_RWKV]
    k = p[:, D_RWKV:2 * D_RWKV]
    v = p[:, 2 * D_RWKV:3 * D_RWKV]
    lo = p[:, 3 * D_RWKV:3 * D_RWKV + LANES]
    gd = p[:, 3 * D_RWKV + LANES:]
    grp = grp_ref[...]
    wlog = _dot3(jnp.tanh(lo), wd_ref[...]) + w0_ref[...]
    alog = _dot3(lo, wa_ref[...]) + a0_ref[...]
    g_o[0] = _dot3(jax.nn.sigmoid(gd), wg_ref[...])
    kx = k * kk_ref[...]
    kk = kx * lax.rsqrt(_head_sum(kx * kx, grp) + 1e-12)
    r_o[0] = r
    v_o[0] = v
    kk_o[0] = kk
    bonus_o[0] = _head_sum(r * k * rk_ref[...], grp) * v
    for z in range(2):
        a = jax.nn.sigmoid(alog[:, z * D_RWKV:(z + 1) * D_RWKV])
        lw_o[z, 0] = -DECAY_SCALE * jax.nn.sigmoid(wlog[:, z * D_RWKV:(z + 1) * D_RWKV])
        kd_o[z, 0] = k * (1.0 + (a - 1.0) * ka_ref[...])
        bd_o[z, 0] = kk * a


N_PROJ_IN = 7
N_PREP_CONST = 10
N_PROJ_OUT = 5
N_PREP_OUT = 8


def _inproj_prep_kernel(*refs, rope, nt):
    n_rope = 3 if rope else 0
    proj_in = refs[:N_PROJ_IN]
    rope_refs = refs[N_PROJ_IN:N_PROJ_IN + n_rope] if rope else None
    at = N_PROJ_IN + n_rope
    consts = refs[at:at + N_PREP_CONST]
    proj_out = refs[at + N_PREP_CONST:at + N_PREP_CONST + N_PROJ_OUT]
    prep_out = refs[at + N_PREP_CONST + N_PROJ_OUT:at + N_PREP_CONST + N_PROJ_OUT + N_PREP_OUT]
    slab_scr, tail_scr = refs[-2:]
    j = pl.program_id(1)
    zero_row = jnp.zeros((1, D_RWKV_PAD), F32)

    def project():
        return _project(*proj_in, rope_refs, *proj_out)

    def features(next_row, first_tile):
        cur = slab_scr[...]
        prev_row = zero_row if first_tile else tail_scr[7:8, :]
        _rwkv_features(cur, prev_row, next_row, *consts, *prep_out)
        tail_scr[...] = cur[cur.shape[0] - 8:, :]

    @pl.when(j == 0)
    def _():
        slab_scr[...] = project()

    if nt > 1:
        @pl.when(j == 1)
        def _():
            slab = project()
            features(slab[0:1, :], True)
            slab_scr[...] = slab

    if nt > 2:
        @pl.when((j > 1) & (j < nt))
        def _():
            slab = project()
            features(slab[0:1, :], False)
            slab_scr[...] = slab

    @pl.when(j == nt)
    def _():
        features(zero_row, nt == 1)


def _in_proj_prep(x3, shift, scale, norm_g, w_in_p, gqk, grp, prep_consts, rope, tm):
    b, t, _ = x3.shape
    nt = t // tm
    bm = shift.shape[0]
    mod_map = (lambda i, j: (i, 0, 0)) if bm > 1 else (lambda i, j: (0, 0, 0))
    full = lambda a: pl.BlockSpec(a.shape, lambda i, j: (0,) * a.ndim)
    here = lambda j: jnp.minimum(j, nt - 1)
    back = lambda j: jnp.maximum(j - 1, 0)
    in_specs = [pl.BlockSpec((1, tm, D_MODEL), lambda i, j: (i, here(j), 0)),
                pl.BlockSpec((1, 1, D_MODEL), mod_map), pl.BlockSpec((1, 1, D_MODEL), mod_map),
                full(norm_g), full(w_in_p), full(gqk), full(grp)]
    args = [x3, shift, scale, norm_g, w_in_p, gqk, grp]
    if rope:
        in_specs += [pl.BlockSpec((tm, LANES), lambda i, j: (here(j), 0))] * 3
        args += list(_rope_tables(t))
    in_specs += [full(a) for a in prep_consts]
    args += list(prep_consts)
    one = jax.ShapeDtypeStruct((b, t, D_RWKV), F32)
    two = jax.ShapeDtypeStruct((2, b, t, D_RWKV), F32)
    s_one = pl.BlockSpec((1, tm, D_RWKV), lambda i, j: (i, back(j), 0))
    s_two = pl.BlockSpec((2, 1, tm, D_RWKV), lambda i, j: (0, i, back(j), 0))
    out_shape = (jax.ShapeDtypeStruct((b, N_KV_HEADS, GQA_GROUP, t, HEAD_DIM), BF16),
                 jax.ShapeDtypeStruct((b, N_KV_HEADS, t, HEAD_DIM), BF16),
                 jax.ShapeDtypeStruct((b, N_KV_HEADS, t, 2 * HEAD_DIM), BF16),
                 jax.ShapeDtypeStruct((b, t, D_KV), F32), jax.ShapeDtypeStruct((b, t, D_KV), F32),
                 one, two, two, one, one, two, one, one)
    out_specs = (pl.BlockSpec((1, N_KV_HEADS, GQA_GROUP, tm, HEAD_DIM), lambda i, j: (i, 0, 0, here(j), 0)),
                 pl.BlockSpec((1, N_KV_HEADS, tm, HEAD_DIM), lambda i, j: (i, 0, here(j), 0)),
                 pl.BlockSpec((1, N_KV_HEADS, tm, 2 * HEAD_DIM), lambda i, j: (i, 0, here(j), 0)),
                 pl.BlockSpec((1, tm, D_KV), lambda i, j: (i, here(j), 0)),
                 pl.BlockSpec((1, tm, D_KV), lambda i, j: (i, here(j), 0)),
                 s_one, s_two, s_two, s_one, s_one, s_two, s_one, s_one)
    return pl.pallas_call(
        functools.partial(_inproj_prep_kernel, rope=rope, nt=nt),
        grid=(b, nt + 1), in_specs=in_specs, out_specs=out_specs, out_shape=out_shape,
        scratch_shapes=[pltpu.VMEM((tm, D_RWKV_PAD), F32), pltpu.VMEM((8, D_RWKV_PAD), F32)],
        compiler_params=_cparams(("parallel", "arbitrary")),
        name="in_proj_rope" if rope else "in_proj",
    )(*args)


def _attn_kernel(q_ref, k_ref, v_ref, o_ref):
    g, tq, hd = q_ref.shape[2:]
    sub = min(ATTN_ROWS, tq)
    k = k_ref[0, 0]
    v = v_ref[0, 0]
    slabs = [slice(i * sub, (i + 1) * sub) for i in range(tq // sub)]
    qs = [q_ref[0, 0, :, sl, :].reshape(g * sub, hd) for sl in slabs]
    ss = [_dot_nt(q, k) for q in qs]
    ps = [jnp.exp2(s - jnp.max(s, axis=-1, keepdims=True)).astype(BF16) for s in ss]
    for sl, p in zip(slabs, ps):
        o = _dot(p, v)
        o = o[:, :hd] / pltpu.roll(o, hd, 1)[:, :hd]
        o_ref[0, sl, :] = jnp.concatenate([o[i * sub:(i + 1) * sub] for i in range(g)], axis=-1)


def _attention(q5, k4, v4, tq):
    b, hk, g, t, hd = q5.shape
    tk = k4.shape[2]
    return pl.pallas_call(
        _attn_kernel,
        grid=(b, hk, t // tq),
        in_specs=[pl.BlockSpec((1, 1, g, tq, hd), lambda i, j, l: (i, j, 0, l, 0)),
                  pl.BlockSpec((1, 1, tk, hd), lambda i, j, l: (i, j, 0, 0)),
                  pl.BlockSpec((1, 1, tk, 2 * hd), lambda i, j, l: (i, j, 0, 0))],
        out_specs=pl.BlockSpec((1, tq, g * hd), lambda i, j, l: (i, l, j)),
        out_shape=jax.ShapeDtypeStruct((b, t, hk * g * hd), F32),
        compiler_params=_cparams(("parallel", "parallel", "arbitrary")),
        name="attn",
    )(q5, k4, v4)


INV_BASE = 16


def _bdot(a, b):
    return _dot(a.astype(BF16), b.astype(BF16))


def _scan_kernel(*refs, chunk):
    ins, (s0_ref, yf_ref, yb_ref, sT_ref, s_scr) = (refs[0:6], refs[6:12]), refs[12:]
    y_refs = (yf_ref, yb_ref)
    c = chunk

    @pl.when(pl.program_id(1) == 0)
    def _():
        s_scr[...] = s0_ref[0]

    row = lax.broadcasted_iota(jnp.int32, (c, c), 0)
    col = lax.broadcasted_iota(jnp.int32, (c, c), 1)
    eye = (row == col).astype(F32)
    same_blk = {}
    n = INV_BASE
    while n <= c:
        sh = jnp.int32(n.bit_length() - 1)
        same_blk[n] = lax.shift_right_logical(row, sh) == lax.shift_right_logical(col, sh)
        n *= 2
    heads = range(N_RWKV_HEADS)
    sls = [slice(h * HEAD_DIM, (h + 1) * HEAD_DIM) for h in heads]

    incl, strict, ar_abs, ar_mid, kb_inv, kb_end, v, p_tot = [], [], [], [], [], [], [], []
    for d, (r_ref, lw_ref, k_ref, v_ref, a_ref, b_ref) in enumerate(ins):
        incl.append(row >= col if d == 0 else row <= col)
        strict.append(row > col if d == 0 else row < col)
        lw = lw_ref[0, 0]
        lw_hi = lw.astype(BF16)
        lw_mid, lw_lo = _split2(lw - lw_hi.astype(F32))
        inclb = jnp.where(incl[d], 1.0, 0.0).astype(BF16)
        cl = _dot(inclb, lw_hi) + (_dot(inclb, lw_mid) + _dot(inclb, lw_lo))
        tot = jnp.sum(lw, axis=0, keepdims=True)
        mid = 0.5 * tot
        e_inv = jnp.exp(mid - cl)
        e_end = jnp.exp(tot - cl)
        s_mid = jnp.exp(-mid)
        r_abs = r_ref[0] * jnp.exp(cl)
        a_abs = a_ref[0] * jnp.exp(cl - lw)
        ar_abs.append(jnp.concatenate([a_abs, r_abs], axis=0).astype(BF16))
        ar_mid.append(jnp.concatenate([a_abs * s_mid, r_abs * s_mid], axis=0).astype(BF16))
        kb_inv.append(jnp.concatenate([k_ref[0, 0] * e_inv, b_ref[0, 0] * e_inv], axis=0).astype(BF16))
        kb_end.append(jnp.concatenate([k_ref[0, 0] * e_end, b_ref[0, 0] * e_end], axis=0).astype(BF16))
        v.append(v_ref[0].astype(BF16))
        p_tot.append(jnp.exp(tot))

    chains = [(d, h) for d in range(2) for h in heads]
    ids = range(len(chains))
    s_old = [s_scr[d, h] for d, h in chains]
    vhs = [v[d][:, sls[h]] for d, h in chains]
    grams = [_dot_nt(ar_mid[d][:, sls[h]], kb_inv[d][:, sls[h]]) for d, h in chains]
    from_s = [_dot_nt(ar_abs[d][:, sls[h]], s_old[i].astype(BF16)) for i, (d, h) in enumerate(chains)]
    masked = [jnp.concatenate([jnp.where(strict[d], grams[i][:c, :c], 0.0),
                               jnp.where(incl[d], grams[i][c:, :c], 0.0)], axis=0) for i, (d, h) in enumerate(chains)]
    from_v = [_bdot(masked[i], vhs[i]) for i in ids]
    lmats = [jnp.where(strict[d], grams[i][:c, c:], 0.0) for i, (d, h) in enumerate(chains)]
    l0s = [jnp.where(same_blk[INV_BASE], lm, 0.0) for lm in lmats]
    xs = [eye - l0 for l0 in l0s]
    pws = [_bdot(l0, l0) for l0 in l0s]
    span = 2
    while 2 * span < INV_BASE:
        both = [_bdot(jnp.concatenate([xs[i], pws[i]], axis=0), pws[i]) for i in ids]
        xs = [xs[i] + both[i][:c] for i in ids]
        pws = [both[i][c:] for i in ids]
        span *= 2
    xs = [xs[i] + _bdot(xs[i], pws[i]) for i in ids]
    n = INV_BASE
    while n < c:
        pair = same_blk[2 * n] & jnp.logical_not(same_blk[n])
        ts = [_bdot(jnp.where(pair, lmats[i], 0.0), xs[i]) for i in ids]
        xs = [xs[i] - _bdot(xs[i], ts[i]) for i in ids]
        n *= 2
    us = [_bdot(xs[i], from_s[i][:c] + from_v[i][:c]) for i in ids]
    yu = [_bdot(jnp.where(incl[d], grams[i][c:, c:], 0.0), us[i]) for i, (d, h) in enumerate(chains)]
    ds = [_dot_tn(jnp.concatenate([vhs[i], (-us[i]).astype(BF16)], axis=0), kb_end[d][:, sls[h]])
          for i, (d, h) in enumerate(chains)]
    for d in range(2):
        y_refs[d][0] = jnp.concatenate([from_s[i][c:] + from_v[i][c:] - yu[i]
                                        for i, (dd, h) in enumerate(chains) if dd == d], axis=-1)
    for i, (d, h) in enumerate(chains):
        s_scr[d, h] = s_old[i] * p_tot[d][:, sls[h]] + ds[i]

    @pl.when(pl.program_id(1) == pl.num_programs(1) - 1)
    def _():
        sT_ref[0] = s_scr[...]


def _rwkv_scan(r, lw2, kd2, v, kk, bd2, s0, chunk):
    b, t, _ = r.shape
    nc = t // chunk
    in_specs, args = [], []
    for d in range(2):
        tmap = (lambda j: j) if d == 0 else (lambda j: nc - 1 - j)
        s_one = pl.BlockSpec((1, chunk, D_RWKV), lambda i, j, tmap=tmap: (i, tmap(j), 0))
        s_two = pl.BlockSpec((1, 1, chunk, D_RWKV), lambda i, j, tmap=tmap, d=d: (d, i, tmap(j), 0))
        in_specs += [s_one, s_two, s_two, s_one, s_one, s_two]
        args += [r, lw2, kd2, v, kk, bd2]
    s_st = pl.BlockSpec((1, 2, N_RWKV_HEADS, HEAD_DIM, HEAD_DIM), lambda i, j: (i, 0, 0, 0, 0))
    y_specs = tuple(pl.BlockSpec((1, chunk, D_RWKV), lambda i, j, tmap=tmap: (i, tmap(j), 0))
                    for tmap in ((lambda j: j), (lambda j: nc - 1 - j)))
    y_shape = jax.ShapeDtypeStruct((b, t, D_RWKV), F32)
    return pl.pallas_call(
        functools.partial(_scan_kernel, chunk=chunk),
        grid=(b, nc),
        in_specs=in_specs + [s_st],
        out_specs=y_specs + (s_st,),
        out_shape=(y_shape, y_shape, jax.ShapeDtypeStruct(s0.shape, F32)),
        scratch_shapes=[pltpu.VMEM((2, N_RWKV_HEADS, HEAD_DIM, HEAD_DIM), F32)],
        compiler_params=_cparams(("parallel", "arbitrary")),
        name="rwkv_scan",
    )(*args, s0)


def _outproj_kernel(x_ref, at_ref, yf_ref, yb_ref, bonus_ref, gate_ref, lng_ref, lnb_ref, grp_ref, wo_ref,
                    g1_ref, sh_ref, sc_ref, n2_ref, wr_ref, br_ref, x1_ref, hx_ref, cnt_ref):
    grp = grp_ref[...]
    y = yf_ref[0] + yb_ref[0]
    mean = _head_sum(y, grp) * (1.0 / HEAD_DIM)
    yc = y - mean
    var = _head_sum(yc * yc, grp) * (1.0 / HEAD_DIM)
    yn = yc * lax.rsqrt(var + GN_EPS) * lng_ref[...] + lnb_ref[...]
    rw_out = (yn + bonus_ref[0]) * gate_ref[0]
    mix = (_dot(at_ref[0].astype(BF16), wo_ref[0:D_ATTN, :])
           + _dot(rw_out.astype(BF16), wo_ref[D_ATTN:, :]))
    x1 = x_ref[0] + g1_ref[0] * mix
    x1_ref[0] = x1
    ms = jnp.mean(x1 * x1, axis=-1, keepdims=True)
    h2 = x1 * lax.rsqrt(ms + NORM_EPS) * n2_ref[...]
    h2 = h2 * (1.0 + sc_ref[0]) + sh_ref[0]
    hx_ref[0, :, 0:D_MODEL] = h2.astype(BF16)
    logits = _dot3(h2, wr_ref[...]) + br_ref[...]
    lane = lax.broadcasted_iota(jnp.int32, logits.shape, 1)
    neg = -jnp.inf
    big = jnp.int32(1 << 20)
    lc = jnp.where(lane < N_GROUPS, logits, neg)
    mc = jnp.max(lc, axis=-1, keepdims=True)
    g_w = 1.0 / jnp.sum(jnp.exp(lc - mc), axis=-1, keepdims=True)
    g_idx = jnp.min(jnp.where(lc == mc, lane, big), axis=-1, keepdims=True)
    eid = lane - N_GROUPS
    in_grp = (eid >= 0) & (eid < N_EXPERTS) & (lax.shift_right_arithmetic(eid, 2) == g_idx)
    lf = jnp.where(in_grp, logits, neg)
    m1 = jnp.max(lf, axis=-1, keepdims=True)
    i1 = jnp.min(jnp.where(lf == m1, lane, big), axis=-1, keepdims=True)
    lf2 = jnp.where(lane == i1, neg, lf)
    m2 = jnp.max(lf2, axis=-1, keepdims=True)
    i2 = jnp.min(jnp.where(lf2 == m2, lane, big), axis=-1, keepdims=True)
    e2 = jnp.exp(m2 - m1)
    w1 = 1.0 / (1.0 + e2)
    w2 = e2 * w1
    cmb = g_w * (jnp.where(lane == i1, w1, 0.0) + jnp.where(lane == i2, w2, 0.0))
    rec = jnp.where(lane == 0, g_idx.astype(F32), cmb)
    rec_hi, rec_lo = _split2(rec)
    hx_ref[0, :, D_MODEL:D_MODEL + ROUTER_LANES] = rec_hi
    hx_ref[0, :, D_MODEL + ROUTER_LANES:] = rec_lo
    hot = jnp.where((lane == g_idx) & (lane < N_GROUPS), 1.0, 0.0)
    for s in range(cnt_ref.shape[1]):
        part = jnp.sum(hot[s * MOE_TM:(s + 1) * MOE_TM], axis=0, keepdims=True)
        cnt_ref[0, s] = jnp.broadcast_to(part, cnt_ref.shape[2:])


def _out_proj(x3, attn3, yf, yb, bonus, gate, ln_g, ln_b, grp, w_out_b, g1, sh2, sc2, norm2_g, w_r, b_r):
    b, t, _ = x3.shape
    tm = OUT_PROJ_TM if t % OUT_PROJ_TM == 0 else MOE_TM
    bm = g1.shape[0]
    mod_map = (lambda i, j: (i, 0, 0)) if bm > 1 else (lambda i, j: (0, 0, 0))
    tok = lambda w: pl.BlockSpec((1, tm, w), lambda i, j: (i, j, 0))
    full = lambda a: pl.BlockSpec(a.shape, lambda i, j: (0,) * a.ndim)
    mod = pl.BlockSpec((1, 1, D_MODEL), mod_map)
    return pl.pallas_call(
        _outproj_kernel,
        grid=(b, t // tm),
        in_specs=[tok(D_MODEL), tok(D_ATTN), tok(D_RWKV), tok(D_RWKV),
                  tok(D_RWKV), tok(D_RWKV), full(ln_g), full(ln_b), full(grp), full(w_out_b),
                  mod, mod, mod, full(norm2_g), full(w_r), full(b_r)],
        out_specs=(tok(D_MODEL), tok(MOE_W), pl.BlockSpec((1, tm // MOE_TM, 8, LANES), lambda i, j: (i, j, 0, 0))),
        out_shape=(jax.ShapeDtypeStruct((b, t, D_MODEL), F32), jax.ShapeDtypeStruct((b, t, MOE_W), BF16),
                   jax.ShapeDtypeStruct((b, t // MOE_TM, 8, LANES), F32)),
        compiler_params=_cparams(("parallel", "parallel")),
        name="out_proj",
    )(x3, attn3, yf, yb, bonus, gate, ln_g, ln_b, grp, w_out_b, g1, sh2, sc2, norm2_g, w_r, b_r)


def _tile_sort_matrix(rec_hi, plan_ref, tile):
    tm = rec_hi.shape[0]
    sel = ((lax.broadcasted_iota(jnp.int32, (8, LANES), 0) == 0)
           & (lax.broadcasted_iota(jnp.int32, (8, LANES), 1) == 0))
    g_row = _dot_nt(jnp.where(sel, 1.0, 0.0).astype(BF16), rec_hi)[0:1, :]
    g_col = rec_hi.astype(F32)[:, 0:1]
    r = lax.broadcasted_iota(jnp.int32, (tm, tm), 0)
    c = lax.broadcasted_iota(jnp.int32, (tm, tm), 1)
    earlier_same = jnp.where((g_col == g_row) & (r < c), 1.0, 0.0)
    pos_row = jnp.sum(earlier_same, axis=0, keepdims=True)
    for g in range(N_GROUPS):
        local = _plan_entry(plan_ref, tile, g)[2].astype(F32)
        pos_row = pos_row + jnp.where(g_row == float(g), local, 0.0)
    stage_row = lax.broadcasted_iota(jnp.int32, (MOE_STAGE, tm), 0).astype(F32)
    return jnp.where(stage_row == pos_row, 1.0, 0.0).astype(BF16)


def _plan_entry(plan_ref, tile, g):
    at = (tile * N_GROUPS + g) * 3
    return pl.multiple_of(plan_ref[at], SEG_ALIGN), plan_ref[at + 1], pl.multiple_of(plan_ref[at + 2], MOE_SUB)


def _for_each_granule(plan_ref, tile, fn):
    for g in range(N_GROUPS):
        start, n_gran, local = _plan_entry(plan_ref, tile, g)

        def body(s, carry, g=g, start=start, local=local):
            fn(g, pl.multiple_of(start + s * MOE_SUB, SEG_ALIGN), pl.multiple_of(local + s * MOE_SUB, MOE_SUB))
            return carry

        lax.fori_loop(0, n_gran, body, 0)


def _dispatch_kernel(plan_ref, hx_ref, xs_in_ref, xs_ref, buf, sem, *, tile0):
    del xs_in_ref
    i = pl.program_id(0)
    x = hx_ref[...]
    perm = _tile_sort_matrix(x[:, D_MODEL:D_MODEL + ROUTER_LANES], plan_ref, tile0 + i)
    staged = _dot(perm, x).astype(BF16)

    def copy(g, seg_row, stage_row):
        return pltpu.make_async_copy(buf.at[pl.ds(stage_row, MOE_SUB), :],
                                     xs_ref.at[pl.ds(seg_row, MOE_SUB), :], sem.at[g])

    @pl.when(i > 0)
    def _():
        _for_each_granule(plan_ref, tile0 + i - 1, lambda *a: copy(*a).wait())

    buf[...] = staged
    _for_each_granule(plan_ref, tile0 + i, lambda *a: copy(*a).start())

    @pl.when(i == pl.num_programs(0) - 1)
    def _():
        _for_each_granule(plan_ref, tile0 + i, lambda *a: copy(*a).wait())


def _moe_dispatch(plan, hx2, xs, tile0):
    n = hx2.shape[0]
    grid_spec = pltpu.PrefetchScalarGridSpec(
        num_scalar_prefetch=1, grid=(n // MOE_TM,),
        in_specs=[pl.BlockSpec((MOE_TM, MOE_W), lambda i, plan: (i, 0)), pl.BlockSpec(memory_space=pl.ANY)],
        out_specs=pl.BlockSpec(memory_space=pl.ANY),
        scratch_shapes=[pltpu.VMEM((MOE_STAGE, MOE_W), BF16), pltpu.SemaphoreType.DMA((N_GROUPS,))])
    return pl.pallas_call(
        functools.partial(_dispatch_kernel, tile0=tile0),
        grid_spec=grid_spec,
        out_shape=jax.ShapeDtypeStruct(xs.shape, xs.dtype),
        input_output_aliases={2: 0},
        compiler_params=_cparams(("arbitrary",)),
        name="moe_dispatch",
    )(plan, hx2, xs)


def _experts_kernel(tg_ref, xs_ref, wg_ref, wu_ref, wd_ref, ys_ref):
    g = tg_ref[pl.program_id(0)]

    @pl.when(g >= N_GROUPS)
    def _():
        ys_ref[...] = jnp.zeros_like(ys_ref)

    @pl.when(g < N_GROUPS)
    def _():
        x = xs_ref[...]
        h = x[:, 0:D_MODEL]
        rec = (x[:, D_MODEL:D_MODEL + ROUTER_LANES].astype(F32) + x[:, D_MODEL + ROUTER_LANES:].astype(F32))
        lane = lax.broadcasted_iota(jnp.int32, rec.shape, 1)
        first = N_GROUPS + EXPERTS_PER_GROUP * g
        scaled = []
        for e in range(EXPERTS_PER_GROUP):
            a = _dot(h, wg_ref[0, e])
            hid = a * jax.nn.sigmoid(a) * _dot(h, wu_ref[0, e])
            c_e = jnp.sum(jnp.where(lane == first + e, rec, 0.0), axis=-1, keepdims=True)
            scaled.append((hid * c_e).astype(BF16))
        ys_ref[...] = _dot(jnp.concatenate(scaled, axis=-1), wd_ref[0]).astype(BF16)


def _moe_experts(tile_group, xs, wg4, wu4, wd4):
    p = xs.shape[0]
    grp_map = lambda nd: (lambda j, tg: (jnp.minimum(tg[j], N_GROUPS - 1),) + (0,) * (nd - 1))
    grid_spec = pltpu.PrefetchScalarGridSpec(
        num_scalar_prefetch=1, grid=(p // MOE_TX,),
        in_specs=[pl.BlockSpec((MOE_TX, MOE_W), lambda j, tg: (j, 0)),
                  pl.BlockSpec((1,) + wg4.shape[1:], grp_map(wg4.ndim)),
                  pl.BlockSpec((1,) + wu4.shape[1:], grp_map(wu4.ndim)),
                  pl.BlockSpec((1,) + wd4.shape[1:], grp_map(wd4.ndim))],
        out_specs=pl.BlockSpec((MOE_TX, D_MODEL), lambda j, tg: (j, 0)))
    return pl.pallas_call(
        _experts_kernel,
        grid_spec=grid_spec,
        out_shape=jax.ShapeDtypeStruct((p, D_MODEL), BF16),
        compiler_params=_cparams(("arbitrary",)),
        name="moe_experts",
    )(tile_group, xs, wg4, wu4, wd4)


def _combine_kernel(plan_ref, rec_ref, x1_ref, g2_ref, ys_ref, o_ref, buf, sem, *, tile0):
    lin = pl.program_id(0) * pl.num_programs(1) + pl.program_id(1)
    n = pl.num_programs(0) * pl.num_programs(1)
    slot = lin % 2

    def copy(slot_t):
        return lambda g, seg_row, stage_row: pltpu.make_async_copy(
            ys_ref.at[pl.ds(seg_row, MOE_SUB), :], buf.at[slot_t, pl.ds(stage_row, MOE_SUB), :], sem.at[slot_t, g])

    @pl.when(lin == 0)
    def _():
        buf[...] = jnp.zeros_like(buf)
        _for_each_granule(plan_ref, tile0, lambda *a: copy(0)(*a).start())

    @pl.when(lin + 1 < n)
    def _():
        _for_each_granule(plan_ref, tile0 + lin + 1, lambda *a: copy(1 - slot)(*a).start())

    perm = _tile_sort_matrix(rec_ref[0], plan_ref, tile0 + lin)
    _for_each_granule(plan_ref, tile0 + lin, lambda *a: copy(slot)(*a).wait())
    y = _dot_tn(perm, buf[slot])
    o_ref[0] = x1_ref[0] + g2_ref[0] * y


def _moe_combine(plan, hx3, x1, g2, ys, tile0):
    b, t, _ = x1.shape
    bm = g2.shape[0]
    mod_map = (lambda i, j, plan: (i, 0, 0)) if bm > 1 else (lambda i, j, plan: (0, 0, 0))
    rec_blk = D_MODEL // ROUTER_LANES
    grid_spec = pltpu.PrefetchScalarGridSpec(
        num_scalar_prefetch=1, grid=(b, t // MOE_TM),
        in_specs=[pl.BlockSpec((1, MOE_TM, ROUTER_LANES), lambda i, j, plan: (i, j, rec_blk)),
                  pl.BlockSpec((1, MOE_TM, D_MODEL), lambda i, j, plan: (i, j, 0)),
                  pl.BlockSpec((1, 1, D_MODEL), mod_map),
                  pl.BlockSpec(memory_space=pl.ANY)],
        out_specs=pl.BlockSpec((1, MOE_TM, D_MODEL), lambda i, j, plan: (i, j, 0)),
        scratch_shapes=[pltpu.VMEM((2, MOE_STAGE, D_MODEL), BF16), pltpu.SemaphoreType.DMA((2, N_GROUPS))])
    return pl.pallas_call(
        functools.partial(_combine_kernel, tile0=tile0),
        grid_spec=grid_spec,
        out_shape=jax.ShapeDtypeStruct(x1.shape, F32),
        compiler_params=_cparams(("arbitrary", "arbitrary")),
        name="moe_combine",
    )(plan, hx3, x1, g2, ys)


def _moe_plan(cnt, n_rows):
    seg = (cnt + (SEG_ALIGN - 1)) // SEG_ALIGN * SEG_ALIGN
    used = (jnp.sum(seg, axis=0) + (MOE_TX - 1)) // MOE_TX * MOE_TX
    size = used + MOE_TX
    base = jnp.cumsum(size) - size
    start = base[None, :] + jnp.cumsum(seg, axis=0) - seg
    n_gran = (cnt + (MOE_SUB - 1)) // MOE_SUB
    stage = n_gran * MOE_SUB
    local = jnp.cumsum(stage, axis=1) - stage
    starts = jnp.arange(n_rows // MOE_TX, dtype=jnp.int32)[:, None] * MOE_TX
    inside = (starts >= base[None, :]) & (starts < (base + used)[None, :])
    tile_group = jnp.where(jnp.any(inside, axis=1), jnp.argmax(inside, axis=1), N_GROUPS)
    plan = jnp.stack([start, n_gran, local], axis=-1).reshape(-1)
    return plan.astype(jnp.int32), tile_group.astype(jnp.int32)


def _moe_rows(n_tokens):
    n_tiles = n_tokens // MOE_TM
    bound = n_tokens + n_tiles * N_GROUPS * (SEG_ALIGN - 1) + N_GROUPS * 2 * MOE_TX
    return (bound + MOE_TX - 1) // MOE_TX * MOE_TX


def _layer(x3, mod6, lp, t_tiles, ctx):
    b, t, _ = x3.shape
    sh1, sc1, g1, sh2, sc2, _ = mod6
    t_tiles = {name: min(size, t) for name, size in t_tiles.items()}
    rope = ctx is not None
    n = b * t
    prep_consts = [lp[name] for name in ("mu_p", "wd_cat", "w0_cat", "wa_cat", "a0_cat", "wg_p", "k_k", "k_a", "r_k",
                                         "grp")]
    q5, k4, v4, k, v, r, lw2, kd2, vv, kk, bd2, gate, bonus = _in_proj_prep(
        x3, sh1, sc1, lp["norm1_g"], lp["w_in_p"], lp["gqk"], lp["grp"], prep_consts, rope, t_tiles["in_proj"])
    if ctx is not None:
        ctx_k, ctx_v, ctx_state = ctx
        ck = ctx_k.transpose(0, 2, 1, 3)
        cv = ctx_v.transpose(0, 2, 1, 3)
        k4 = jnp.concatenate([k4, ck.astype(BF16)], axis=2)
        v4 = jnp.concatenate([v4, jnp.concatenate([cv, jnp.ones_like(cv)], axis=-1).astype(BF16)], axis=2)
        s0 = ctx_state
    else:
        s0 = jnp.zeros((b, 2, N_RWKV_HEADS, HEAD_DIM, HEAD_DIM), F32)
    attn3 = _attention(q5, k4, v4, t_tiles["attn"])
    yf, yb, s_t = _rwkv_scan(r, lw2, kd2, vv, kk, bd2, s0, t_tiles["chunk"])
    fb, ft = (1, n) if g1.shape[0] == 1 else (b, t)
    flat = lambda a: a.reshape(fb, ft, a.shape[-1])
    x1, hx, cnt = _out_proj(flat(x3), flat(attn3), flat(yf), flat(yb), flat(bonus), flat(gate), lp["ln_g"],
                            lp["ln_b"], lp["grp"], lp["w_out_b"], g1, sh2, sc2, lp["norm2_g"], lp["w_r"], lp["b_r"])
    return (x1, hx, cnt), k.reshape(b, t, N_KV_HEADS, HEAD_DIM), v.reshape(b, t, N_KV_HEADS, HEAD_DIM), s_t


def _moe_both(passes, g2s, lp):
    counts = [p[2][:, :, 0, :N_GROUPS].reshape(-1, N_GROUPS) for p in passes]
    tiles = [c.shape[0] for c in counts]
    n_rows = _moe_rows(sum(tiles) * MOE_TM)
    plan, tile_group = _moe_plan(jnp.concatenate(counts, axis=0).astype(jnp.int32), n_rows)
    xs = jnp.zeros((n_rows, MOE_W), BF16)
    tile0 = 0
    for (x1, hx, _), nt in zip(passes, tiles):
        xs = _moe_dispatch(plan, hx.reshape(-1, MOE_W), xs, tile0)
        tile0 += nt
    ys = _moe_experts(tile_group, xs, lp["wg4"], lp["wu4"], lp["wd4"])
    outs, tile0 = [], 0
    for (x1, hx, _), g2, nt in zip(passes, g2s, tiles):
        outs.append(_moe_combine(plan, hx, x1, g2, ys, tile0))
        tile0 += nt
    return outs


def _block_diag2(w):
    z, l, c = w.shape
    out = jnp.zeros((LANES, z * c), F32)
    for i in range(z):
        out = out.at[i * l:(i + 1) * l, i * c:(i + 1) * c].set(w[i])
    return out


def _layer_params(l, w_in, norm1_g, norm2_g, mu_shift, q_norm_g, k_norm_g, w0, w_lora_up, a0, a_lora_up, g_lora_up,
                  k_k, k_a, r_k, ln_x_g, ln_x_b, w_out, router_c, router_c_b, router_f, router_f_b,
                  exp_gate, exp_up, exp_down):
    lane = np.arange(LANES)
    grp = jnp.asarray((lane[:, None] // HEAD_DIM) == (lane[None, :] // HEAD_DIM), BF16)
    pad_in = D_IN_PAD - w_in.shape[2]
    wd_cat = _block_diag2(w_lora_up[l])
    wa_cat = jnp.roll(_block_diag2(a_lora_up[l]), 2 * DECAY_LORA, axis=0)
    w_r = jnp.zeros((D_MODEL, ROUTER_LANES), F32)
    w_r = w_r.at[:, :N_GROUPS].set(router_c[l]).at[:, N_GROUPS:N_GROUPS + N_EXPERTS].set(router_f[l])
    b_r = jnp.zeros((1, ROUTER_LANES), F32)
    b_r = b_r.at[0, :N_GROUPS].set(router_c_b[l]).at[0, N_GROUPS:N_GROUPS + N_EXPERTS].set(router_f_b[l])

    by_group = lambda w: w.astype(BF16).reshape(N_GROUPS, EXPERTS_PER_GROUP, D_MODEL, D_EXPERT)

    return dict(
        grp=grp,
        norm1_g=norm1_g[l].reshape(1, D_MODEL), norm2_g=norm2_g[l].reshape(1, D_MODEL),
        w_in_p=jnp.pad(w_in[l], ((0, 0), (0, pad_in))).astype(BF16),
        gqk=jnp.concatenate([jnp.tile(q_norm_g[l], N_Q_HEADS), jnp.tile(k_norm_g[l], N_KV_HEADS)]).reshape(1, -1),
        mu_p=jnp.pad(mu_shift[l], ((0, 0), (0, D_RWKV_PAD - D_RWKV_IN))),
        wd_cat=wd_cat, w0_cat=w0[l].reshape(1, 2 * D_RWKV),
        wa_cat=wa_cat, a0_cat=a0[l].reshape(1, 2 * D_RWKV),
        wg_p=jnp.pad(g_lora_up[l], ((0, LANES - GATE_LORA), (0, 0))),
        k_k=k_k[l].reshape(1, D_RWKV), k_a=k_a[l].reshape(1, D_RWKV), r_k=r_k[l].reshape(1, D_RWKV),
        ln_g=ln_x_g[l].reshape(1, D_RWKV), ln_b=ln_x_b[l].reshape(1, D_RWKV),
        w_out_b=w_out[l].astype(BF16), w_r=w_r, b_r=b_r,
        wg4=by_group(exp_gate[l]), wu4=by_group(exp_up[l]),
        wd4=exp_down[l].astype(BF16).reshape(N_GROUPS, EXPERTS_PER_GROUP * D_EXPERT, D_MODEL),
    )


CTX_TILES = dict(in_proj=128, attn=256, chunk=128)
SMP_TILES = dict(in_proj=256, attn=256, chunk=128)


def kernel(x_prompt, x_sample, cache_k, cache_v, state_rwkv, c, c_ctx, w_mod, b_mod, norm1_g, norm2_g, w_in, mu_shift, q_norm_g, k_norm_g, w0, w_lora_up, a0, a_lora_up, g_lora_up, k_k, k_a, r_k, ln_x_g, ln_x_b, w_out, router_c, router_c_b, router_f, router_f_b, exp_gate, exp_up, exp_down):
    depth = w_mod.shape[0]
    db = x_sample.shape[0]
    y_prompt, y_sample = x_prompt, x_sample
    ks, vs, ss = [], [], []
    cond = jnp.zeros((8, D_MODEL), F32).at[:db].set(c).at[db].set(c_ctx)
    for l in range(depth):
        lp = _layer_params(l, w_in, norm1_g, norm2_g, mu_shift, q_norm_g, k_norm_g, w0, w_lora_up, a0, a_lora_up,
                           g_lora_up, k_k, k_a, r_k, ln_x_g, ln_x_b, w_out, router_c, router_c_b, router_f,
                           router_f_b, exp_gate, exp_up, exp_down)
        mod = _modulation(cond, w_mod[l], b_mod[l])
        mod_s = [mod[:db, i * D_MODEL:(i + 1) * D_MODEL].reshape(db, 1, D_MODEL) for i in range(6)]
        mod_c = [mod[db:db + 1, i * D_MODEL:(i + 1) * D_MODEL].reshape(1, 1, D_MODEL) for i in range(6)]
        pre_c, k_l, v_l, s_l = _layer(y_prompt, mod_c, lp, CTX_TILES, None)
        ks.append(k_l)
        vs.append(v_l)
        ss.append(s_l)
        pre_s, _, _, _ = _layer(y_sample, mod_s, lp, SMP_TILES, (cache_k[:, l], cache_v[:, l], state_rwkv[:, l]))
        out_c, out_s = _moe_both([pre_c, pre_s], [mod_c[5], mod_s[5]], lp)
        y_prompt, y_sample = out_c.reshape(y_prompt.shape), out_s.reshape(y_sample.shape)
    return (y_prompt, y_sample, jnp.stack(ks, axis=1), jnp.stack(vs, axis=1), jnp.stack(ss, axis=1))
```

```python
import functools

import numpy as np
import jax
import jax.numpy as jnp
from jax import lax
from jax.experimental import pallas as pl
from jax.experimental.pallas import tpu as pltpu

F32 = jnp.float32
BF16 = jnp.bfloat16
HIGHEST = lax.Precision.HIGHEST

D_MODEL = 1024
HEAD_DIM = 64
N_Q_HEADS = 8
N_KV_HEADS = 2
GQA_GROUP = N_Q_HEADS // N_KV_HEADS
D_ATTN = N_Q_HEADS * HEAD_DIM
D_KV = N_KV_HEADS * HEAD_DIM
N_RWKV_HEADS = 8
D_RWKV = 512
DECAY_LORA = 32
AAA_LORA = 32
GATE_LORA = 96
D_RWKV_IN = 3 * D_RWKV + 2 * DECAY_LORA + 2 * AAA_LORA + GATE_LORA
D_RWKV_PAD = 1792
D_QKV = D_ATTN + 2 * D_KV
D_IN_PAD = D_QKV + D_RWKV_PAD
N_GROUPS = 4
EXPERTS_PER_GROUP = 4
N_EXPERTS = 16
D_EXPERT = 512
GRID_W = 64
ROPE_THETA = 10000.0
NORM_EPS = 1e-6
GN_EPS = 64e-5
DECAY_SCALE = 0.6065306597
QK_EXP2_SCALE = (HEAD_DIM ** -0.5) * float(np.log2(np.e))
LANES = 128
ROUTER_LANES = 128
VMEM_LIMIT = 56 * 1024 * 1024
ATTN_ROWS = 128
MOE_W = D_MODEL + 2 * ROUTER_LANES
MOE_TM = 256
OUT_PROJ_TM = 1024
MOE_SUB = 32
MOE_STAGE = MOE_TM + N_GROUPS * MOE_SUB
MOE_TX = 512
SEG_ALIGN = 16


def _cparams(sem):
    return pltpu.CompilerParams(dimension_semantics=sem, vmem_limit_bytes=VMEM_LIMIT)


def _dot(a, b, precision=None):
    return jnp.dot(a, b, preferred_element_type=F32, precision=precision)


def _dot_nt(a, b, precision=None):
    return lax.dot_general(a, b, (((1,), (1,)), ((), ())), preferred_element_type=F32, precision=precision)


def _dot_tn(a, b, precision=None):
    return lax.dot_general(a, b, (((0,), (0,)), ((), ())), preferred_element_type=F32, precision=precision)


def _split2(x):
    hi = x.astype(BF16)
    return hi, (x - hi.astype(F32)).astype(BF16)


def _dot3(a, b):
    a_hi, a_lo = _split2(a)
    b_hi, b_lo = _split2(b)
    return _dot(a_hi, b_hi) + (_dot(a_hi, b_lo) + _dot(a_lo, b_hi))


def _head_sum(x, g):
    hi, lo = _split2(x)
    n = x.shape[-1] // LANES
    cols = [slice(j * LANES, (j + 1) * LANES) for j in range(n)]
    return jnp.concatenate([_dot(hi[:, c], g) + _dot(lo[:, c], g) for c in cols], axis=-1)


def _mod_kernel(c_ref, w_ref, b_ref, o_ref):
    c = c_ref[...]
    s = c * jax.nn.sigmoid(c)
    o_ref[...] = _dot(s, w_ref[...], HIGHEST) + b_ref[...]


def _modulation(cond, w_mod, b_mod):
    n = w_mod.shape[1]
    tn = 1024
    return pl.pallas_call(
        _mod_kernel,
        grid=(n // tn,),
        in_specs=[pl.BlockSpec((8, D_MODEL), lambda j: (0, 0)),
                  pl.BlockSpec((D_MODEL, tn), lambda j: (0, j)),
                  pl.BlockSpec((1, tn), lambda j: (0, j))],
        out_specs=pl.BlockSpec((8, tn), lambda j: (0, j)),
        out_shape=jax.ShapeDtypeStruct((8, n), F32),
        compiler_params=_cparams(("arbitrary",)),
        name="mod",
    )(cond, w_mod, b_mod.reshape(1, n))


def _rope_tables(t_len):
    half = HEAD_DIM // 2
    inv = ROPE_THETA ** (-np.arange(0, half, 2, dtype=np.float64) / half)
    t = np.arange(t_len)
    row, col = t // GRID_W, t % GRID_W
    lane = np.arange(LANES)
    i = lane % HEAD_DIM
    pos = np.where((i // half)[None, :] == 0, row[:, None], col[:, None]).astype(np.float64)
    j = i % half
    ang = pos * inv[j % (half // 2)][None, :]
    cos, sin = np.cos(ang), np.sin(ang)
    first = (j < half // 2)[None, :]
    s_up = np.where(first, -sin, 0.0)
    s_dn = np.where(first, 0.0, sin)
    return (jnp.asarray(cos, F32), jnp.asarray(s_up, F32), jnp.asarray(s_dn, F32))


def _project(x_ref, sh_ref, sc_ref, g_ref, w_ref, gqk_ref, grp_ref, rope_refs, q_ref, k_ref, v_ref, kf_ref, vf_ref):
    rope = rope_refs is not None
    if rope:
        cos_ref, sup_ref, sdn_ref = rope_refs
    x = x_ref[0]
    ms = jnp.mean(x * x, axis=-1, keepdims=True)
    h = x * lax.rsqrt(ms + NORM_EPS) * g_ref[...]
    h = h * (1.0 + sc_ref[0]) + sh_ref[0]
    proj = _dot(h.astype(BF16), w_ref[...])
    grp = grp_ref[...]
    lo_half = lax.broadcasted_iota(jnp.int32, (x.shape[0], LANES), 1) < HEAD_DIM
    for j in range((D_ATTN + D_KV) // LANES):
        blk = proj[:, j * LANES:(j + 1) * LANES]
        ss = _head_sum(blk * blk, grp) * (1.0 / HEAD_DIM)
        nb = blk * lax.rsqrt(ss + NORM_EPS) * gqk_ref[:, j * LANES:(j + 1) * LANES]
        if rope:
            nb = (nb * cos_ref[...] + pltpu.roll(nb, LANES - 16, 1) * sup_ref[...]
                  + pltpu.roll(nb, 16, 1) * sdn_ref[...])
        if j < D_ATTN // LANES:
            nbq = nb * QK_EXP2_SCALE
            for half in range(2):
                hq = 2 * j + half
                q_ref[0, hq // GQA_GROUP, hq % GQA_GROUP] = nbq[:, half * HEAD_DIM:(half + 1) * HEAD_DIM].astype(BF16)
        else:
            kf_ref[0] = nb
            k_ref[0, 0] = nb[:, :HEAD_DIM].astype(BF16)
            k_ref[0, 1] = nb[:, HEAD_DIM:].astype(BF16)
    vblk = proj[:, D_ATTN + D_KV:D_QKV]
    vf_ref[0] = vblk
    v_ref[0, 0] = jnp.where(lo_half, vblk, 1.0).astype(BF16)
    v_ref[0, 1] = jnp.where(lo_half, pltpu.roll(vblk, HEAD_DIM, 1), 1.0).astype(BF16)
    return proj[:, D_QKV:]


def _rwkv_features(cur, prev_row, next_row, mu_ref, wd_ref, w0_ref, wa_ref, a0_ref, wg_ref, kk_ref, ka_ref, rk_ref,
                   grp_ref, r_o, lw_o, kd_o, v_o, kk_o, bd_o, g_o, bonus_o):
    tt = cur.shape[0]
    rid = lax.broadcasted_iota(jnp.int32, cur.shape, 0)
    prev = jnp.where(rid == 0, prev_row, pltpu.roll(cur, 1, 0))
    nxt = jnp.where(rid == tt - 1, next_row, pltpu.roll(cur, tt - 1, 0))
    p = cur + mu_ref[0:1, :] * (prev - cur) + mu_ref[1:2, :] * (nxt - cur)
    r = p[:, 0:D_RWKV]
    k = p[:, D_RWKV:2 * D_RWKV]
    v = p[:, 2 * D_RWKV:3 * D_RWKV]
    lo = p[:, 3 * D_RWKV:3 * D_RWKV + LANES]
    gd = p[:, 3 * D_RWKV + LANES:]
    grp = grp_ref[...]
    wlog = _dot3(jnp.tanh(lo), wd_ref[...]) + w0_ref[...]
    alog = _dot3(lo, wa_ref[...]) + a0_ref[...]
    g_o[0] = _dot3(jax.nn.sigmoid(gd), wg_ref[...])
    kx = k * kk_ref[...]
    kk = kx * lax.rsqrt(_head_sum(kx * kx, grp) + 1e-12)
    r_o[0] = r
    v_o[0] = v
    kk_o[0] = kk
    bonus_o[0] = _head_sum(r * k * rk_ref[...], grp) * v
    for z in range(2):
        a = jax.nn.sigmoid(alog[:, z * D_RWKV:(z + 1) * D_RWKV])
        lw_o[z, 0] = -DECAY_SCALE * jax.nn.sigmoid(wlog[:, z * D_RWKV:(z + 1) * D_RWKV])
        kd_o[z, 0] = k * (1.0 + (a - 1.0) * ka_ref[...])
        bd_o[z, 0] = kk * a


N_PROJ_IN = 7
N_PREP_CONST = 10
N_PROJ_OUT = 5
N_PREP_OUT = 8


def _inproj_prep_kernel(*refs, rope, nt):
    n_rope = 3 if rope else 0
    proj_in = refs[:N_PROJ_IN]
    rope_refs = refs[N_PROJ_IN:N_PROJ_IN + n_rope] if rope else None
    at = N_PROJ_IN + n_rope
    consts = refs[at:at + N_PREP_CONST]
    proj_out = refs[at + N_PREP_CONST:at + N_PREP_CONST + N_PROJ_OUT]
    prep_out = refs[at + N_PREP_CONST + N_PROJ_OUT:at + N_PREP_CONST + N_PROJ_OUT + N_PREP_OUT]
    slab_scr, tail_scr = refs[-2:]
    j = pl.program_id(1)
    zero_row = jnp.zeros((1, D_RWKV_PAD), F32)

    def project():
        return _project(*proj_in, rope_refs, *proj_out)

    def features(next_row, first_tile):
        cur = slab_scr[...]
        prev_row = zero_row if first_tile else tail_scr[7:8, :]
        _rwkv_features(cur, prev_row, next_row, *consts, *prep_out)
        tail_scr[...] = cur[cur.shape[0] - 8:, :]

    @pl.when(j == 0)
    def _():
        slab_scr[...] = project()

    if nt > 1:
        @pl.when(j == 1)
        def _():
            slab = project()
            features(slab[0:1, :], True)
            slab_scr[...] = slab

    if nt > 2:
        @pl.when((j > 1) & (j < nt))
        def _():
            slab = project()
            features(slab[0:1, :], False)
            slab_scr[...] = slab

    @pl.when(j == nt)
    def _():
        features(zero_row, nt == 1)


def _in_proj_prep(x3, shift, scale, norm_g, w_in_p, gqk, grp, prep_consts, rope, tm):
    b, t, _ = x3.shape
    nt = t // tm
    bm = shift.shape[0]
    mod_map = (lambda i, j: (i, 0, 0)) if bm > 1 else (lambda i, j: (0, 0, 0))
    full = lambda a: pl.BlockSpec(a.shape, lambda i, j: (0,) * a.ndim)
    here = lambda j: jnp.minimum(j, nt - 1)
    back = lambda j: jnp.maximum(j - 1, 0)
    in_specs = [pl.BlockSpec((1, tm, D_MODEL), lambda i, j: (i, here(j), 0)),
                pl.BlockSpec((1, 1, D_MODEL), mod_map), pl.BlockSpec((1, 1, D_MODEL), mod_map),
                full(norm_g), full(w_in_p), full(gqk), full(grp)]
    args = [x3, shift, scale, norm_g, w_in_p, gqk, grp]
    if rope:
        in_specs += [pl.BlockSpec((tm, LANES), lambda i, j: (here(j), 0))] * 3
        args += list(_rope_tables(t))
    in_specs += [full(a) for a in prep_consts]
    args += list(prep_consts)
    one = jax.ShapeDtypeStruct((b, t, D_RWKV), F32)
    two = jax.ShapeDtypeStruct((2, b, t, D_RWKV), F32)
    s_one = pl.BlockSpec((1, tm, D_RWKV), lambda i, j: (i, back(j), 0))
    s_two = pl.BlockSpec((2, 1, tm, D_RWKV), lambda i, j: (0, i, back(j), 0))
    out_shape = (jax.ShapeDtypeStruct((b, N_KV_HEADS, GQA_GROUP, t, HEAD_DIM), BF16),
                 jax.ShapeDtypeStruct((b, N_KV_HEADS, t, HEAD_DIM), BF16),
                 jax.ShapeDtypeStruct((b, N_KV_HEADS, t, 2 * HEAD_DIM), BF16),
                 jax.ShapeDtypeStruct((b, t, D_KV), F32), jax.ShapeDtypeStruct((b, t, D_KV), F32),
                 one, two, two, one, one, two, one, one)
    out_specs = (pl.BlockSpec((1, N_KV_HEADS, GQA_GROUP, tm, HEAD_DIM), lambda i, j: (i, 0, 0, here(j), 0)),
                 pl.BlockSpec((1, N_KV_HEADS, tm, HEAD_DIM), lambda i, j: (i, 0, here(j), 0)),
                 pl.BlockSpec((1, N_KV_HEADS, tm, 2 * HEAD_DIM), lambda i, j: (i, 0, here(j), 0)),
                 pl.BlockSpec((1, tm, D_KV), lambda i, j: (i, here(j), 0)),
                 pl.BlockSpec((1, tm, D_KV), lambda i, j: (i, here(j), 0)),
                 s_one, s_two, s_two, s_one, s_one, s_two, s_one, s_one)
    return pl.pallas_call(
        functools.partial(_inproj_prep_kernel, rope=rope, nt=nt),
        grid=(b, nt + 1), in_specs=in_specs, out_specs=out_specs, out_shape=out_shape,
        scratch_shapes=[pltpu.VMEM((tm, D_RWKV_PAD), F32), pltpu.VMEM((8, D_RWKV_PAD), F32)],
        compiler_params=_cparams(("parallel", "arbitrary")),
        name="in_proj_rope" if rope else "in_proj",
    )(*args)


def _attn_kernel(q_ref, k_ref, v_ref, o_ref):
    g, tq, hd = q_ref.shape[2:]
    sub = min(ATTN_ROWS, tq)
    k = k_ref[0, 0]
    v = v_ref[0, 0]
    slabs = [slice(i * sub, (i + 1) * sub) for i in range(tq // sub)]
    qs = [q_ref[0, 0, :, sl, :].reshape(g * sub, hd) for sl in slabs]
    ss = [_dot_nt(q, k) for q in qs]
    ps = [jnp.exp2(s - jnp.max(s, axis=-1, keepdims=True)).astype(BF16) for s in ss]
    for sl, p in zip(slabs, ps):
        o = _dot(p, v)
        o = o[:, :hd] / pltpu.roll(o, hd, 1)[:, :hd]
        o_ref[0, sl, :] = jnp.concatenate([o[i * sub:(i + 1) * sub] for i in range(g)], axis=-1)


def _attention(q5, k4, v4, tq):
    b, hk, g, t, hd = q5.shape
    tk = k4.shape[2]
    return pl.pallas_call(
        _attn_kernel,
        grid=(b, hk, t // tq),
        in_specs=[pl.BlockSpec((1, 1, g, tq, hd), lambda i, j, l: (i, j, 0, l, 0)),
                  pl.BlockSpec((1, 1, tk, hd), lambda i, j, l: (i, j, 0, 0)),
                  pl.BlockSpec((1, 1, tk, 2 * hd), lambda i, j, l: (i, j, 0, 0))],
        out_specs=pl.BlockSpec((1, tq, g * hd), lambda i, j, l: (i, l, j)),
        out_shape=jax.ShapeDtypeStruct((b, t, hk * g * hd), F32),
        compiler_params=_cparams(("parallel", "parallel", "arbitrary")),
        name="attn",
    )(q5, k4, v4)


INV_BASE = 16


def _bdot(a, b):
    return _dot(a.astype(BF16), b.astype(BF16))


def _scan_kernel(*refs, chunk):
    ins, (s0_ref, yf_ref, yb_ref, sT_ref, s_scr) = (refs[0:6], refs[6:12]), refs[12:]
    y_refs = (yf_ref, yb_ref)
    c = chunk

    @pl.when(pl.program_id(1) == 0)
    def _():
        s_scr[...] = s0_ref[0]

    row = lax.broadcasted_iota(jnp.int32, (c, c), 0)
    col = lax.broadcasted_iota(jnp.int32, (c, c), 1)
    eye = (row == col).astype(F32)
    same_blk = {}
    n = INV_BASE
    while n <= c:
        sh = jnp.int32(n.bit_length() - 1)
        same_blk[n] = lax.shift_right_logical(row, sh) == lax.shift_right_logical(col, sh)
        n *= 2
    heads = range(N_RWKV_HEADS)
    sls = [slice(h * HEAD_DIM, (h + 1) * HEAD_DIM) for h in heads]

    incl, strict, ar_abs, ar_mid, kb_inv, kb_end, v, p_tot = [], [], [], [], [], [], [], []
    for d, (r_ref, lw_ref, k_ref, v_ref, a_ref, b_ref) in enumerate(ins):
        incl.append(row >= col if d == 0 else row <= col)
        strict.append(row > col if d == 0 else row < col)
        lw = lw_ref[0, 0]
        lw_hi = lw.astype(BF16)
        lw_mid, lw_lo = _split2(lw - lw_hi.astype(F32))
        inclb = jnp.where(incl[d], 1.0, 0.0).astype(BF16)
        cl = _dot(inclb, lw_hi) + (_dot(inclb, lw_mid) + _dot(inclb, lw_lo))
        tot = jnp.sum(lw, axis=0, keepdims=True)
        mid = 0.5 * tot
        e_inv = jnp.exp(mid - cl)
        e_end = jnp.exp(tot - cl)
        s_mid = jnp.exp(-mid)
        r_abs = r_ref[0] * jnp.exp(cl)
        a_abs = a_ref[0] * jnp.exp(cl - lw)
        ar_abs.append(jnp.concatenate([a_abs, r_abs], axis=0).astype(BF16))
        ar_mid.append(jnp.concatenate([a_abs * s_mid, r_abs * s_mid], axis=0).astype(BF16))
        kb_inv.append(jnp.concatenate([k_ref[0, 0] * e_inv, b_ref[0, 0] * e_inv], axis=0).astype(BF16))
        kb_end.append(jnp.concatenate([k_ref[0, 0] * e_end, b_ref[0, 0] * e_end], axis=0).astype(BF16))
        v.append(v_ref[0].astype(BF16))
        p_tot.append(jnp.exp(tot))

    chains = [(d, h) for d in range(2) for h in heads]
    ids = range(len(chains))
    s_old = [s_scr[d, h] for d, h in chains]
    vhs = [v[d][:, sls[h]] for d, h in chains]
    grams = [_dot_nt(ar_mid[d][:, sls[h]], kb_inv[d][:, sls[h]]) for d, h in chains]
    from_s = [_dot_nt(ar_abs[d][:, sls[h]], s_old[i].astype(BF16)) for i, (d, h) in enumerate(chains)]
    masked = [jnp.concatenate([jnp.where(strict[d], grams[i][:c, :c], 0.0),
                               jnp.where(incl[d], grams[i][c:, :c], 0.0)], axis=0) for i, (d, h) in enumerate(chains)]
    from_v = [_bdot(masked[i], vhs[i]) for i in ids]
    lmats = [jnp.where(strict[d], grams[i][:c, c:], 0.0) for i, (d, h) in enumerate(chains)]
    l0s = [jnp.where(same_blk[INV_BASE], lm, 0.0) for lm in lmats]
    xs = [eye - l0 for l0 in l0s]
    pws = [_bdot(l0, l0) for l0 in l0s]
    span = 2
    while 2 * span < INV_BASE:
        both = [_bdot(jnp.concatenate([xs[i], pws[i]], axis=0), pws[i]) for i in ids]
        xs = [xs[i] + both[i][:c] for i in ids]
        pws = [both[i][c:] for i in ids]
        span *= 2
    xs = [xs[i] + _bdot(xs[i], pws[i]) for i in ids]
    n = INV_BASE
    while n < c:
        pair = same_blk[2 * n] & jnp.logical_not(same_blk[n])
        ts = [_bdot(jnp.where(pair, lmats[i], 0.0), xs[i]) for i in ids]
        xs = [xs[i] - _bdot(xs[i], ts[i]) for i in ids]
        n *= 2
    us = [_bdot(xs[i], from_s[i][:c] + from_v[i][:c]) for i in ids]
    yu = [_bdot(jnp.where(incl[d], grams[i][c:, c:], 0.0), us[i]) for i, (d, h) in enumerate(chains)]
    ds = [_dot_tn(jnp.concatenate([vhs[i], (-us[i]).astype(BF16)], axis=0), kb_end[d][:, sls[h]])
          for i, (d, h) in enumerate(chains)]
    for d in range(2):
        y_refs[d][0] = jnp.concatenate([from_s[i][c:] + from_v[i][c:] - yu[i]
                                        for i, (dd, h) in enumerate(chains) if dd == d], axis=-1)
    for i, (d, h) in enumerate(chains):
        s_scr[d, h] = s_old[i] * p_tot[d][:, sls[h]] + ds[i]

    @pl.when(pl.program_id(1) == pl.num_programs(1) - 1)
    def _():
        sT_ref[0] = s_scr[...]


def _rwkv_scan(r, lw2, kd2, v, kk, bd2, s0, chunk):
    b, t, _ = r.shape
    nc = t // chunk
    in_specs, args = [], []
    for d in range(2):
        tmap = (lambda j: j) if d == 0 else (lambda j: nc - 1 - j)
        s_one = pl.BlockSpec((1, chunk, D_RWKV), lambda i, j, tmap=tmap: (i, tmap(j), 0))
        s_two = pl.BlockSpec((1, 1, chunk, D_RWKV), lambda i, j, tmap=tmap, d=d: (d, i, tmap(j), 0))
        in_specs += [s_one, s_two, s_two, s_one, s_one, s_two]
        args += [r, lw2, kd2, v, kk, bd2]
    s_st = pl.BlockSpec((1, 2, N_RWKV_HEADS, HEAD_DIM, HEAD_DIM), lambda i, j: (i, 0, 0, 0, 0))
    y_specs = tuple(pl.BlockSpec((1, chunk, D_RWKV), lambda i, j, tmap=tmap: (i, tmap(j), 0))
                    for tmap in ((lambda j: j), (lambda j: nc - 1 - j)))
    y_shape = jax.ShapeDtypeStruct((b, t, D_RWKV), F32)
    return pl.pallas_call(
        functools.partial(_scan_kernel, chunk=chunk),
        grid=(b, nc),
        in_specs=in_specs + [s_st],
        out_specs=y_specs + (s_st,),
        out_shape=(y_shape, y_shape, jax.ShapeDtypeStruct(s0.shape, F32)),
        scratch_shapes=[pltpu.VMEM((2, N_RWKV_HEADS, HEAD_DIM, HEAD_DIM), F32)],
        compiler_params=_cparams(("parallel", "arbitrary")),
        name="rwkv_scan",
    )(*args, s0)


def _outproj_kernel(x_ref, at_ref, yf_ref, yb_ref, bonus_ref, gate_ref, lng_ref, lnb_ref, grp_ref, wo_ref,
                    g1_ref, sh_ref, sc_ref, n2_ref, wr_ref, br_ref, x1_ref, hx_ref, cnt_ref):
    grp = grp_ref[...]
    y = yf_ref[0] + yb_ref[0]
    mean = _head_sum(y, grp) * (1.0 / HEAD_DIM)
    yc = y - mean
    var = _head_sum(yc * yc, grp) * (1.0 / HEAD_DIM)
    yn = yc * lax.rsqrt(var + GN_EPS) * lng_ref[...] + lnb_ref[...]
    rw_out = (yn + bonus_ref[0]) * gate_ref[0]
    mix = (_dot(at_ref[0].astype(BF16), wo_ref[0:D_ATTN, :])
           + _dot(rw_out.astype(BF16), wo_ref[D_ATTN:, :]))
    x1 = x_ref[0] + g1_ref[0] * mix
    x1_ref[0] = x1
    ms = jnp.mean(x1 * x1, axis=-1, keepdims=True)
    h2 = x1 * lax.rsqrt(ms + NORM_EPS) * n2_ref[...]
    h2 = h2 * (1.0 + sc_ref[0]) + sh_ref[0]
    hx_ref[0, :, 0:D_MODEL] = h2.astype(BF16)
    logits = _dot3(h2, wr_ref[...]) + br_ref[...]
    lane = lax.broadcasted_iota(jnp.int32, logits.shape, 1)
    neg = -jnp.inf
    big = jnp.int32(1 << 20)
    lc = jnp.where(lane < N_GROUPS, logits, neg)
    mc = jnp.max(lc, axis=-1, keepdims=True)
    g_w = 1.0 / jnp.sum(jnp.exp(lc - mc), axis=-1, keepdims=True)
    g_idx = jnp.min(jnp.where(lc == mc, lane, big), axis=-1, keepdims=True)
    eid = lane - N_GROUPS
    in_grp = (eid >= 0) & (eid < N_EXPERTS) & (lax.shift_right_arithmetic(eid, 2) == g_idx)
    lf = jnp.where(in_grp, logits, neg)
    m1 = jnp.max(lf, axis=-1, keepdims=True)
    i1 = jnp.min(jnp.where(lf == m1, lane, big), axis=-1, keepdims=True)
    lf2 = jnp.where(lane == i1, neg, lf)
    m2 = jnp.max(lf2, axis=-1, keepdims=True)
    i2 = jnp.min(jnp.where(lf2 == m2, lane, big), axis=-1, keepdims=True)
    e2 = jnp.exp(m2 - m1)
    w1 = 1.0 / (1.0 + e2)
    w2 = e2 * w1
    cmb = g_w * (jnp.where(lane == i1, w1, 0.0) + jnp.where(lane == i2, w2, 0.0))
    rec = jnp.where(lane == 0, g_idx.astype(F32), cmb)
    rec_hi, rec_lo = _split2(rec)
    hx_ref[0, :, D_MODEL:D_MODEL + ROUTER_LANES] = rec_hi
    hx_ref[0, :, D_MODEL + ROUTER_LANES:] = rec_lo
    hot = jnp.where((lane == g_idx) & (lane < N_GROUPS), 1.0, 0.0)
    for s in range(cnt_ref.shape[1]):
        part = jnp.sum(hot[s * MOE_TM:(s + 1) * MOE_TM], axis=0, keepdims=True)
        cnt_ref[0, s] = jnp.broadcast_to(part, cnt_ref.shape[2:])


def _out_proj(x3, attn3, yf, yb, bonus, gate, ln_g, ln_b, grp, w_out_b, g1, sh2, sc2, norm2_g, w_r, b_r):
    b, t, _ = x3.shape
    tm = OUT_PROJ_TM if t % OUT_PROJ_TM == 0 else MOE_TM
    bm = g1.shape[0]
    mod_map = (lambda i, j: (i, 0, 0)) if bm > 1 else (lambda i, j: (0, 0, 0))
    tok = lambda w: pl.BlockSpec((1, tm, w), lambda i, j: (i, j, 0))
    full = lambda a: pl.BlockSpec(a.shape, lambda i, j: (0,) * a.ndim)
    mod = pl.BlockSpec((1, 1, D_MODEL), mod_map)
    return pl.pallas_call(
        _outproj_kernel,
        grid=(b, t // tm),
        in_specs=[tok(D_MODEL), tok(D_ATTN), tok(D_RWKV), tok(D_RWKV),
                  tok(D_RWKV), tok(D_RWKV), full(ln_g), full(ln_b), full(grp), full(w_out_b),
                  mod, mod, mod, full(norm2_g), full(w_r), full(b_r)],
        out_specs=(tok(D_MODEL), tok(MOE_W), pl.BlockSpec((1, tm // MOE_TM, 8, LANES), lambda i, j: (i, j, 0, 0))),
        out_shape=(jax.ShapeDtypeStruct((b, t, D_MODEL), F32), jax.ShapeDtypeStruct((b, t, MOE_W), BF16),
                   jax.ShapeDtypeStruct((b, t // MOE_TM, 8, LANES), F32)),
        compiler_params=_cparams(("parallel", "parallel")),
        name="out_proj",
    )(x3, attn3, yf, yb, bonus, gate, ln_g, ln_b, grp, w_out_b, g1, sh2, sc2, norm2_g, w_r, b_r)


def _tile_sort_matrix(rec_hi, plan_ref, tile):
    tm = rec_hi.shape[0]
    sel = ((lax.broadcasted_iota(jnp.int32, (8, LANES), 0) == 0)
           & (lax.broadcasted_iota(jnp.int32, (8, LANES), 1) == 0))
    g_row = _dot_nt(jnp.where(sel, 1.0, 0.0).astype(BF16), rec_hi)[0:1, :]
    g_col = rec_hi.astype(F32)[:, 0:1]
    r = lax.broadcasted_iota(jnp.int32, (tm, tm), 0)
    c = lax.broadcasted_iota(jnp.int32, (tm, tm), 1)
    earlier_same = jnp.where((g_col == g_row) & (r < c), 1.0, 0.0)
    pos_row = jnp.sum(earlier_same, axis=0, keepdims=True)
    for g in range(N_GROUPS):
        local = _plan_entry(plan_ref, tile, g)[2].astype(F32)
        pos_row = pos_row + jnp.where(g_row == float(g), local, 0.0)
    stage_row = lax.broadcasted_iota(jnp.int32, (MOE_STAGE, tm), 0).astype(F32)
    return jnp.where(stage_row == pos_row, 1.0, 0.0).astype(BF16)


def _plan_entry(plan_ref, tile, g):
    at = (tile * N_GROUPS + g) * 3
    return pl.multiple_of(plan_ref[at], SEG_ALIGN), plan_ref[at + 1], pl.multiple_of(plan_ref[at + 2], MOE_SUB)


def _for_each_granule(plan_ref, tile, fn):
    for g in range(N_GROUPS):
        start, n_gran, local = _plan_entry(plan_ref, tile, g)

        def body(s, carry, g=g, start=start, local=local):
            fn(g, pl.multiple_of(start + s * MOE_SUB, SEG_ALIGN), pl.multiple_of(local + s * MOE_SUB, MOE_SUB))
            return carry

        lax.fori_loop(0, n_gran, body, 0)


def _dispatch_kernel(plan_ref, hx_ref, xs_in_ref, xs_ref, buf, sem, *, tile0):
    del xs_in_ref
    i = pl.program_id(0)
    x = hx_ref[...]
    perm = _tile_sort_matrix(x[:, D_MODEL:D_MODEL + ROUTER_LANES], plan_ref, tile0 + i)
    staged = _dot(perm, x).astype(BF16)

    def copy(g, seg_row, stage_row):
        return pltpu.make_async_copy(buf.at[pl.ds(stage_row, MOE_SUB), :],
                                     xs_ref.at[pl.ds(seg_row, MOE_SUB), :], sem.at[g])

    @pl.when(i > 0)
    def _():
        _for_each_granule(plan_ref, tile0 + i - 1, lambda *a: copy(*a).wait())

    buf[...] = staged
    _for_each_granule(plan_ref, tile0 + i, lambda *a: copy(*a).start())

    @pl.when(i == pl.num_programs(0) - 1)
    def _():
        _for_each_granule(plan_ref, tile0 + i, lambda *a: copy(*a).wait())


def _moe_dispatch(plan, hx2, xs, tile0):
    n = hx2.shape[0]
    grid_spec = pltpu.PrefetchScalarGridSpec(
        num_scalar_prefetch=1, grid=(n // MOE_TM,),
        in_specs=[pl.BlockSpec((MOE_TM, MOE_W), lambda i, plan: (i, 0)), pl.BlockSpec(memory_space=pl.ANY)],
        out_specs=pl.BlockSpec(memory_space=pl.ANY),
        scratch_shapes=[pltpu.VMEM((MOE_STAGE, MOE_W), BF16), pltpu.SemaphoreType.DMA((N_GROUPS,))])
    return pl.pallas_call(
        functools.partial(_dispatch_kernel, tile0=tile0),
        grid_spec=grid_spec,
        out_shape=jax.ShapeDtypeStruct(xs.shape, xs.dtype),
        input_output_aliases={2: 0},
        compiler_params=_cparams(("arbitrary",)),
        name="moe_dispatch",
    )(plan, hx2, xs)


def _experts_kernel(tg_ref, xs_ref, wg_ref, wu_ref, wd_ref, ys_ref):
    g = tg_ref[pl.program_id(0)]

    @pl.when(g >= N_GROUPS)
    def _():
        ys_ref[...] = jnp.zeros_like(ys_ref)

    @pl.when(g < N_GROUPS)
    def _():
        x = xs_ref[...]
        h = x[:, 0:D_MODEL]
        rec = (x[:, D_MODEL:D_MODEL + ROUTER_LANES].astype(F32) + x[:, D_MODEL + ROUTER_LANES:].astype(F32))
        lane = lax.broadcasted_iota(jnp.int32, rec.shape, 1)
        first = N_GROUPS + EXPERTS_PER_GROUP * g
        scaled = []
        for e in range(EXPERTS_PER_GROUP):
            a = _dot(h, wg_ref[0, e])
            hid = a * jax.nn.sigmoid(a) * _dot(h, wu_ref[0, e])
            c_e = jnp.sum(jnp.where(lane == first + e, rec, 0.0), axis=-1, keepdims=True)
            scaled.append((hid * c_e).astype(BF16))
        ys_ref[...] = _dot(jnp.concatenate(scaled, axis=-1), wd_ref[0]).astype(BF16)


def _moe_experts(tile_group, xs, wg4, wu4, wd4):
    p = xs.shape[0]
    grp_map = lambda nd: (lambda j, tg: (jnp.minimum(tg[j], N_GROUPS - 1),) + (0,) * (nd - 1))
    grid_spec = pltpu.PrefetchScalarGridSpec(
        num_scalar_prefetch=1, grid=(p // MOE_TX,),
        in_specs=[pl.BlockSpec((MOE_TX, MOE_W), lambda j, tg: (j, 0)),
                  pl.BlockSpec((1,) + wg4.shape[1:], grp_map(wg4.ndim)),
                  pl.BlockSpec((1,) + wu4.shape[1:], grp_map(wu4.ndim)),
                  pl.BlockSpec((1,) + wd4.shape[1:], grp_map(wd4.ndim))],
        out_specs=pl.BlockSpec((MOE_TX, D_MODEL), lambda j, tg: (j, 0)))
    return pl.pallas_call(
        _experts_kernel,
        grid_spec=grid_spec,
        out_shape=jax.ShapeDtypeStruct((p, D_MODEL), BF16),
        compiler_params=_cparams(("arbitrary",)),
        name="moe_experts",
    )(tile_group, xs, wg4, wu4, wd4)


def _combine_kernel(plan_ref, rec_ref, x1_ref, g2_ref, ys_ref, o_ref, buf, sem, *, tile0):
    lin = pl.program_id(0) * pl.num_programs(1) + pl.program_id(1)
    n = pl.num_programs(0) * pl.num_programs(1)
    slot = lin % 2

    def copy(slot_t):
        return lambda g, seg_row, stage_row: pltpu.make_async_copy(
            ys_ref.at[pl.ds(seg_row, MOE_SUB), :], buf.at[slot_t, pl.ds(stage_row, MOE_SUB), :], sem.at[slot_t, g])

    @pl.when(lin == 0)
    def _():
        buf[...] = jnp.zeros_like(buf)
        _for_each_granule(plan_ref, tile0, lambda *a: copy(0)(*a).start())

    @pl.when(lin + 1 < n)
    def _():
        _for_each_granule(plan_ref, tile0 + lin + 1, lambda *a: copy(1 - slot)(*a).start())

    perm = _tile_sort_matrix(rec_ref[0], plan_ref, tile0 + lin)
    _for_each_granule(plan_ref, tile0 + lin, lambda *a: copy(slot)(*a).wait())
    y = _dot_tn(perm, buf[slot])
    o_ref[0] = x1_ref[0] + g2_ref[0] * y


def _moe_combine(plan, hx3, x1, g2, ys, tile0):
    b, t, _ = x1.shape
    bm = g2.shape[0]
    mod_map = (lambda i, j, plan: (i, 0, 0)) if bm > 1 else (lambda i, j, plan: (0, 0, 0))
    rec_blk = D_MODEL // ROUTER_LANES
    grid_spec = pltpu.PrefetchScalarGridSpec(
        num_scalar_prefetch=1, grid=(b, t // MOE_TM),
        in_specs=[pl.BlockSpec((1, MOE_TM, ROUTER_LANES), lambda i, j, plan: (i, j, rec_blk)),
                  pl.BlockSpec((1, MOE_TM, D_MODEL), lambda i, j, plan: (i, j, 0)),
                  pl.BlockSpec((1, 1, D_MODEL), mod_map),
                  pl.BlockSpec(memory_space=pl.ANY)],
        out_specs=pl.BlockSpec((1, MOE_TM, D_MODEL), lambda i, j, plan: (i, j, 0)),
        scratch_shapes=[pltpu.VMEM((2, MOE_STAGE, D_MODEL), BF16), pltpu.SemaphoreType.DMA((2, N_GROUPS))])
    return pl.pallas_call(
        functools.partial(_combine_kernel, tile0=tile0),
        grid_spec=grid_spec,
        out_shape=jax.ShapeDtypeStruct(x1.shape, F32),
        compiler_params=_cparams(("arbitrary", "arbitrary")),
        name="moe_combine",
    )(plan, hx3, x1, g2, ys)


def _moe_plan(cnt, n_rows):
    seg = (cnt + (SEG_ALIGN - 1)) // SEG_ALIGN * SEG_ALIGN
    used = (jnp.sum(seg, axis=0) + (MOE_TX - 1)) // MOE_TX * MOE_TX
    size = used + MOE_TX
    base = jnp.cumsum(size) - size
    start = base[None, :] + jnp.cumsum(seg, axis=0) - seg
    n_gran = (cnt + (MOE_SUB - 1)) // MOE_SUB
    stage = n_gran * MOE_SUB
    local = jnp.cumsum(stage, axis=1) - stage
    starts = jnp.arange(n_rows // MOE_TX, dtype=jnp.int32)[:, None] * MOE_TX
    inside = (starts >= base[None, :]) & (starts < (base + used)[None, :])
    tile_group = jnp.where(jnp.any(inside, axis=1), jnp.argmax(inside, axis=1), N_GROUPS)
    plan = jnp.stack([start, n_gran, local], axis=-1).reshape(-1)
    return plan.astype(jnp.int32), tile_group.astype(jnp.int32)


def _moe_rows(n_tokens):
    n_tiles = n_tokens // MOE_TM
    bound = n_tokens + n_tiles * N_GROUPS * (SEG_ALIGN - 1) + N_GROUPS * 2 * MOE_TX
    return (bound + MOE_TX - 1) // MOE_TX * MOE_TX


def _layer(x3, mod6, lp, t_tiles, ctx):
    b, t, _ = x3.shape
    sh1, sc1, g1, sh2, sc2, _ = mod6
    t_tiles = {name: min(size, t) for name, size in t_tiles.items()}
    rope = ctx is not None
    n = b * t
    prep_consts = [lp[name] for name in ("mu_p", "wd_cat", "w0_cat", "wa_cat", "a0_cat", "wg_p", "k_k", "k_a", "r_k",
                                         "grp")]
    q5, k4, v4, k, v, r, lw2, kd2, vv, kk, bd2, gate, bonus = _in_proj_prep(
        x3, sh1, sc1, lp["norm1_g"], lp["w_in_p"], lp["gqk"], lp["grp"], prep_consts, rope, t_tiles["in_proj"])
    if ctx is not None:
        ctx_k, ctx_v, ctx_state = ctx
        ck = ctx_k.transpose(0, 2, 1, 3)
        cv = ctx_v.transpose(0, 2, 1, 3)
        k4 = jnp.concatenate([k4, ck.astype(BF16)], axis=2)
        v4 = jnp.concatenate([v4, jnp.concatenate([cv, jnp.ones_like(cv)], axis=-1).astype(BF16)], axis=2)
        s0 = ctx_state
    else:
        s0 = jnp.zeros((b, 2, N_RWKV_HEADS, HEAD_DIM, HEAD_DIM), F32)
    attn3 = _attention(q5, k4, v4, t_tiles["attn"])
    yf, yb, s_t = _rwkv_scan(r, lw2, kd2, vv, kk, bd2, s0, t_tiles["chunk"])
    fb, ft = (1, n) if g1.shape[0] == 1 else (b, t)
    flat = lambda a: a.reshape(fb, ft, a.shape[-1])
    x1, hx, cnt = _out_proj(flat(x3), flat(attn3), flat(yf), flat(yb), flat(bonus), flat(gate), lp["ln_g"],
                            lp["ln_b"], lp["grp"], lp["w_out_b"], g1, sh2, sc2, lp["norm2_g"], lp["w_r"], lp["b_r"])
    return (x1, hx, cnt), k.reshape(b, t, N_KV_HEADS, HEAD_DIM), v.reshape(b, t, N_KV_HEADS, HEAD_DIM), s_t


def _moe_both(passes, g2s, lp):
    counts = [p[2][:, :, 0, :N_GROUPS].reshape(-1, N_GROUPS) for p in passes]
    tiles = [c.shape[0] for c in counts]
    n_rows = _moe_rows(sum(tiles) * MOE_TM)
    plan, tile_group = _moe_plan(jnp.concatenate(counts, axis=0).astype(jnp.int32), n_rows)
    xs = jnp.zeros((n_rows, MOE_W), BF16)
    tile0 = 0
    for (x1, hx, _), nt in zip(passes, tiles):
        xs = _moe_dispatch(plan, hx.reshape(-1, MOE_W), xs, tile0)
        tile0 += nt
    ys = _moe_experts(tile_group, xs, lp["wg4"], lp["wu4"], lp["wd4"])
    outs, tile0 = [], 0
    for (x1, hx, _), g2, nt in zip(passes, g2s, tiles):
        outs.append(_moe_combine(plan, hx, x1, g2, ys, tile0))
        tile0 += nt
    return outs


def _block_diag2(w):
    z, l, c = w.shape
    out = jnp.zeros((LANES, z * c), F32)
    for i in range(z):
        out = out.at[i * l:(i + 1) * l, i * c:(i + 1) * c].set(w[i])
    return out


def _layer_params(l, w_in, norm1_g, norm2_g, mu_shift, q_norm_g, k_norm_g, w0, w_lora_up, a0, a_lora_up, g_lora_up,
                  k_k, k_a, r_k, ln_x_g, ln_x_b, w_out, router_c, router_c_b, router_f, router_f_b,
                  exp_gate, exp_up, exp_down):
    lane = np.arange(LANES)
    grp = jnp.asarray((lane[:, None] // HEAD_DIM) == (lane[None, :] // HEAD_DIM), BF16)
    pad_in = D_IN_PAD - w_in.shape[2]
    wd_cat = _block_diag2(w_lora_up[l])
    wa_cat = jnp.roll(_block_diag2(a_lora_up[l]), 2 * DECAY_LORA, axis=0)
    w_r = jnp.zeros((D_MODEL, ROUTER_LANES), F32)
    w_r = w_r.at[:, :N_GROUPS].set(router_c[l]).at[:, N_GROUPS:N_GROUPS + N_EXPERTS].set(router_f[l])
    b_r = jnp.zeros((1, ROUTER_LANES), F32)
    b_r = b_r.at[0, :N_GROUPS].set(router_c_b[l]).at[0, N_GROUPS:N_GROUPS + N_EXPERTS].set(router_f_b[l])

    by_group = lambda w: w.astype(BF16).reshape(N_GROUPS, EXPERTS_PER_GROUP, D_MODEL, D_EXPERT)

    return dict(
        grp=grp,
        norm1_g=norm1_g[l].reshape(1, D_MODEL), norm2_g=norm2_g[l].reshape(1, D_MODEL),
        w_in_p=jnp.pad(w_in[l], ((0, 0), (0, pad_in))).astype(BF16),
        gqk=jnp.concatenate([jnp.tile(q_norm_g[l], N_Q_HEADS), jnp.tile(k_norm_g[l], N_KV_HEADS)]).reshape(1, -1),
        mu_p=jnp.pad(mu_shift[l], ((0, 0), (0, D_RWKV_PAD - D_RWKV_IN))),
        wd_cat=wd_cat, w0_cat=w0[l].reshape(1, 2 * D_RWKV),
        wa_cat=wa_cat, a0_cat=a0[l].reshape(1, 2 * D_RWKV),
        wg_p=jnp.pad(g_lora_up[l], ((0, LANES - GATE_LORA), (0, 0))),
        k_k=k_k[l].reshape(1, D_RWKV), k_a=k_a[l].reshape(1, D_RWKV), r_k=r_k[l].reshape(1, D_RWKV),
        ln_g=ln_x_g[l].reshape(1, D_RWKV), ln_b=ln_x_b[l].reshape(1, D_RWKV),
        w_out_b=w_out[l].astype(BF16), w_r=w_r, b_r=b_r,
        wg4=by_group(exp_gate[l]), wu4=by_group(exp_up[l]),
        wd4=exp_down[l].astype(BF16).reshape(N_GROUPS, EXPERTS_PER_GROUP * D_EXPERT, D_MODEL),
    )


CTX_TILES = dict(in_proj=256, attn=256, chunk=128)
SMP_TILES = dict(in_proj=512, attn=256, chunk=128)


def kernel(x_prompt, x_sample, cache_k, cache_v, state_rwkv, c, c_ctx, w_mod, b_mod, norm1_g, norm2_g, w_in, mu_shift, q_norm_g, k_norm_g, w0, w_lora_up, a0, a_lora_up, g_lora_up, k_k, k_a, r_k, ln_x_g, ln_x_b, w_out, router_c, router_c_b, router_f, router_f_b, exp_gate, exp_up, exp_down):
    depth = w_mod.shape[0]
    db = x_sample.shape[0]
    y_prompt, y_sample = x_prompt, x_sample
    ks, vs, ss = [], [], []
    cond = jnp.zeros((8, D_MODEL), F32).at[:db].set(c).at[db].set(c_ctx)
    for l in range(depth):
        lp = _layer_params(l, w_in, norm1_g, norm2_g, mu_shift, q_norm_g, k_norm_g, w0, w_lora_up, a0, a_lora_up,
                           g_lora_up, k_k, k_a, r_k, ln_x_g, ln_x_b, w_out, router_c, router_c_b, router_f,
                           router_f_b, exp_gate, exp_up, exp_down)
        mod = _modulation(cond, w_mod[l], b_mod[l])
        mod_s = [mod[:db, i * D_MODEL:(i + 1) * D_MODEL].reshape(db, 1, D_MODEL) for i in range(6)]
        mod_c = [mod[db:db + 1, i * D_MODEL:(i + 1) * D_MODEL].reshape(1, 1, D_MODEL) for i in range(6)]
        pre_c, k_l, v_l, s_l = _layer(y_prompt, mod_c, lp, CTX_TILES, None)
        ks.append(k_l)
        vs.append(v_l)
        ss.append(s_l)
        pre_s, _, _, _ = _layer(y_sample, mod_s, lp, SMP_TILES, (cache_k[:, l], cache_v[:, l], state_rwkv[:, l]))
        out_c, out_s = _moe_both([pre_c, pre_s], [mod_c[5], mod_s[5]], lp)
        y_prompt, y_sample = out_c.reshape(y_prompt.shape), out_s.reshape(y_sample.shape)
    return (y_prompt, y_sample, jnp.stack(ks, axis=1), jnp.stack(vs, axis=1), jnp.stack(ss, axis=1))
```

```python
import functools

import numpy as np
import jax
import jax.numpy as jnp
from jax import lax
from jax.experimental import pallas as pl
from jax.experimental.pallas import tpu as pltpu

F32 = jnp.float32
BF16 = jnp.bfloat16
HIGHEST = lax.Precision.HIGHEST

D_MODEL = 1024
HEAD_DIM = 64
N_Q_HEADS = 8
N_KV_HEADS = 2
GQA_GROUP = N_Q_HEADS // N_KV_HEADS
D_ATTN = N_Q_HEADS * HEAD_DIM
D_KV = N_KV_HEADS * HEAD_DIM
N_RWKV_HEADS = 8
D_RWKV = 512
DECAY_LORA = 32
AAA_LORA = 32
GATE_LORA = 96
D_RWKV_IN = 3 * D_RWKV + 2 * DECAY_LORA + 2 * AAA_LORA + GATE_LORA
D_RWKV_PAD = 1792
D_QKV = D_ATTN + 2 * D_KV
D_IN_PAD = D_QKV + D_RWKV_PAD
N_GROUPS = 4
EXPERTS_PER_GROUP = 4
N_EXPERTS = 16
D_EXPERT = 512
GRID_W = 64
ROPE_THETA = 10000.0
NORM_EPS = 1e-6
GN_EPS = 64e-5
DECAY_SCALE = 0.6065306597
QK_EXP2_SCALE = (HEAD_DIM ** -0.5) * float(np.log2(np.e))
LANES = 128
ROUTER_LANES = 128
VMEM_LIMIT = 56 * 1024 * 1024
ATTN_ROWS = 128
MOE_W = D_MODEL + 2 * ROUTER_LANES
MOE_TM = 256
OUT_PROJ_TM = 1024
MOE_SUB = 32
MOE_STAGE = MOE_TM + N_GROUPS * MOE_SUB
MOE_TX = 512
SEG_ALIGN = 16


def _cparams(sem):
    return pltpu.CompilerParams(dimension_semantics=sem, vmem_limit_bytes=VMEM_LIMIT)


def _dot(a, b, precision=None):
    return jnp.dot(a, b, preferred_element_type=F32, precision=precision)


def _dot_nt(a, b, precision=None):
    return lax.dot_general(a, b, (((1,), (1,)), ((), ())), preferred_element_type=F32, precision=precision)


def _dot_tn(a, b, precision=None):
    return lax.dot_general(a, b, (((0,), (0,)), ((), ())), preferred_element_type=F32, precision=precision)


def _split2(x):
    hi = x.astype(BF16)
    return hi, (x - hi.astype(F32)).astype(BF16)


def _dot3(a, b):
    a_hi, a_lo = _split2(a)
    b_hi, b_lo = _split2(b)
    return _dot(a_hi, b_hi) + (_dot(a_hi, b_lo) + _dot(a_lo, b_hi))


def _head_sum(x, g):
    hi, lo = _split2(x)
    n = x.shape[-1] // LANES
    cols = [slice(j * LANES, (j + 1) * LANES) for j in range(n)]
    return jnp.concatenate([_dot(hi[:, c], g) + _dot(lo[:, c], g) for c in cols], axis=-1)


def _mod_kernel(c_ref, w_ref, b_ref, o_ref):
    c = c_ref[...]
    s = c * jax.nn.sigmoid(c)
    o_ref[...] = _dot(s, w_ref[...], HIGHEST) + b_ref[...]


def _modulation(cond, w_mod, b_mod):
    n = w_mod.shape[1]
    tn = 1024
    return pl.pallas_call(
        _mod_kernel,
        grid=(n // tn,),
        in_specs=[pl.BlockSpec((8, D_MODEL), lambda j: (0, 0)),
                  pl.BlockSpec((D_MODEL, tn), lambda j: (0, j)),
                  pl.BlockSpec((1, tn), lambda j: (0, j))],
        out_specs=pl.BlockSpec((8, tn), lambda j: (0, j)),
        out_shape=jax.ShapeDtypeStruct((8, n), F32),
        compiler_params=_cparams(("arbitrary",)),
        name="mod",
    )(cond, w_mod, b_mod.reshape(1, n))


def _rope_tables(t_len):
    half = HEAD_DIM // 2
    inv = ROPE_THETA ** (-np.arange(0, half, 2, dtype=np.float64) / half)
    t = np.arange(t_len)
    row, col = t // GRID_W, t % GRID_W
    lane = np.arange(LANES)
    i = lane % HEAD_DIM
    pos = np.where((i // half)[None, :] == 0, row[:, None], col[:, None]).astype(np.float64)
    j = i % half
    ang = pos * inv[j % (half // 2)][None, :]
    cos, sin = np.cos(ang), np.sin(ang)
    first = (j < half // 2)[None, :]
    s_up = np.where(first, -sin, 0.0)
    s_dn = np.where(first, 0.0, sin)
    return (jnp.asarray(cos, F32), jnp.asarray(s_up, F32), jnp.asarray(s_dn, F32))


def _project(x_ref, sh_ref, sc_ref, g_ref, w_ref, gqk_ref, grp_ref, rope_refs, q_ref, k_ref, v_ref, kf_ref, vf_ref):
    rope = rope_refs is not None
    if rope:
        cos_ref, sup_ref, sdn_ref = rope_refs
    x = x_ref[0]
    ms = jnp.mean(x * x, axis=-1, keepdims=True)
    h = x * lax.rsqrt(ms + NORM_EPS) * g_ref[...]
    h = h * (1.0 + sc_ref[0]) + sh_ref[0]
    proj = _dot(h.astype(BF16), w_ref[...])
    grp = grp_ref[...]
    lo_half = lax.broadcasted_iota(jnp.int32, (x.shape[0], LANES), 1) < HEAD_DIM
    for j in range((D_ATTN + D_KV) // LANES):
        blk = proj[:, j * LANES:(j + 1) * LANES]
        ss = _head_sum(blk * blk, grp) * (1.0 / HEAD_DIM)
        nb = blk * lax.rsqrt(ss + NORM_EPS) * gqk_ref[:, j * LANES:(j + 1) * LANES]
        if rope:
            nb = (nb * cos_ref[...] + pltpu.roll(nb, LANES - 16, 1) * sup_ref[...]
                  + pltpu.roll(nb, 16, 1) * sdn_ref[...])
        if j < D_ATTN // LANES:
            nbq = nb * QK_EXP2_SCALE
            for half in range(2):
                hq = 2 * j + half
                q_ref[0, hq // GQA_GROUP, hq % GQA_GROUP] = nbq[:, half * HEAD_DIM:(half + 1) * HEAD_DIM].astype(BF16)
        else:
            kf_ref[0] = nb
            k_ref[0, 0] = nb[:, :HEAD_DIM].astype(BF16)
            k_ref[0, 1] = nb[:, HEAD_DIM:].astype(BF16)
    vblk = proj[:, D_ATTN + D_KV:D_QKV]
    vf_ref[0] = vblk
    v_ref[0, 0] = jnp.where(lo_half, vblk, 1.0).astype(BF16)
    v_ref[0, 1] = jnp.where(lo_half, pltpu.roll(vblk, HEAD_DIM, 1), 1.0).astype(BF16)
    return proj[:, D_QKV:]


def _rwkv_features(cur, prev_row, next_row, mu_ref, wd_ref, w0_ref, wa_ref, a0_ref, wg_ref, kk_ref, ka_ref, rk_ref,
                   grp_ref, r_o, lw_o, kd_o, v_o, kk_o, bd_o, g_o, bonus_o):
    tt = cur.shape[0]
    rid = lax.broadcasted_iota(jnp.int32, cur.shape, 0)
    prev = jnp.where(rid == 0, prev_row, pltpu.roll(cur, 1, 0))
    nxt = jnp.where(rid == tt - 1, next_row, pltpu.roll(cur, tt - 1, 0))
    p = cur + mu_ref[0:1, :] * (prev - cur) + mu_ref[1:2, :] * (nxt - cur)
    r = p[:, 0:D_RWKV]
    k = p[:, D_RWKV:2 * D_RWKV]
    v = p[:, 2 * D_RWKV:3 * D_RWKV]
    lo = p[:, 3 * D_RWKV:3 * D_RWKV + LANES]
    gd = p[:, 3 * D_RWKV + LANES:]
    grp = grp_ref[...]
    wlog = _dot3(jnp.tanh(lo), wd_ref[...]) + w0_ref[...]
    alog = _dot3(lo, wa_ref[...]) + a0_ref[...]
    g_o[0] = _dot3(jax.nn.sigmoid(gd), wg_ref[...])
    kx = k * kk_ref[...]
    kk = kx * lax.rsqrt(_head_sum(kx * kx, grp) + 1e-12)
    r_o[0] = r
    v_o[0] = v
    kk_o[0] = kk
    bonus_o[0] = _head_sum(r * k * rk_ref[...], grp) * v
    for z in range(2):
        a = jax.nn.sigmoid(alog[:, z * D_RWKV:(z + 1) * D_RWKV])
        lw_o[z, 0] = -DECAY_SCALE * jax.nn.sigmoid(wlog[:, z * D_RWKV:(z + 1) * D_RWKV])
        kd_o[z, 0] = k * (1.0 + (a - 1.0) * ka_ref[...])
        bd_o[z, 0] = kk * a


N_PROJ_IN = 7
N_PREP_CONST = 10
N_PROJ_OUT = 5
N_PREP_OUT = 8


def _inproj_prep_kernel(*refs, rope, nt):
    n_rope = 3 if rope else 0
    proj_in = refs[:N_PROJ_IN]
    rope_refs = refs[N_PROJ_IN:N_PROJ_IN + n_rope] if rope else None
    at = N_PROJ_IN + n_rope
    consts = refs[at:at + N_PREP_CONST]
    proj_out = refs[at + N_PREP_CONST:at + N_PREP_CONST + N_PROJ_OUT]
    prep_out = refs[at + N_PREP_CONST + N_PROJ_OUT:at + N_PREP_CONST + N_PROJ_OUT + N_PREP_OUT]
    slab_scr, tail_scr = refs[-2:]
    j = pl.program_id(1)
    zero_row = jnp.zeros((1, D_RWKV_PAD), F32)

    def project():
        return _project(*proj_in, rope_refs, *proj_out)

    def features(next_row, first_tile):
        cur = slab_scr[...]
        prev_row = zero_row if first_tile else tail_scr[7:8, :]
        _rwkv_features(cur, prev_row, next_row, *consts, *prep_out)
        tail_scr[...] = cur[cur.shape[0] - 8:, :]

    @pl.when(j == 0)
    def _():
        slab_scr[...] = project()

    if nt > 1:
        @pl.when(j == 1)
        def _():
            slab = project()
            features(slab[0:1, :], True)
            slab_scr[...] = slab

    if nt > 2:
        @pl.when((j > 1) & (j < nt))
        def _():
            slab = project()
            features(slab[0:1, :], False)
            slab_scr[...] = slab

    @pl.when(j == nt)
    def _():
        features(zero_row, nt == 1)


def _in_proj_prep(x3, shift, scale, norm_g, w_in_p, gqk, grp, prep_consts, rope, tm):
    b, t, _ = x3.shape
    nt = t // tm
    bm = shift.shape[0]
    mod_map = (lambda i, j: (i, 0, 0)) if bm > 1 else (lambda i, j: (0, 0, 0))
    full = lambda a: pl.BlockSpec(a.shape, lambda i, j: (0,) * a.ndim)
    here = lambda j: jnp.minimum(j, nt - 1)
    back = lambda j: jnp.maximum(j - 1, 0)
    in_specs = [pl.BlockSpec((1, tm, D_MODEL), lambda i, j: (i, here(j), 0)),
                pl.BlockSpec((1, 1, D_MODEL), mod_map), pl.BlockSpec((1, 1, D_MODEL), mod_map),
                full(norm_g), full(w_in_p), full(gqk), full(grp)]
    args = [x3, shift, scale, norm_g, w_in_p, gqk, grp]
    if rope:
        in_specs += [pl.BlockSpec((tm, LANES), lambda i, j: (here(j), 0))] * 3
        args += list(_rope_tables(t))
    in_specs += [full(a) for a in prep_consts]
    args += list(prep_consts)
    one = jax.ShapeDtypeStruct((b, t, D_RWKV), F32)
    two = jax.ShapeDtypeStruct((2, b, t, D_RWKV), F32)
    s_one = pl.BlockSpec((1, tm, D_RWKV), lambda i, j: (i, back(j), 0))
    s_two = pl.BlockSpec((2, 1, tm, D_RWKV), lambda i, j: (0, i, back(j), 0))
    out_shape = (jax.ShapeDtypeStruct((b, N_KV_HEADS, GQA_GROUP, t, HEAD_DIM), BF16),
                 jax.ShapeDtypeStruct((b, N_KV_HEADS, t, HEAD_DIM), BF16),
                 jax.ShapeDtypeStruct((b, N_KV_HEADS, t, 2 * HEAD_DIM), BF16),
                 jax.ShapeDtypeStruct((b, t, D_KV), F32), jax.ShapeDtypeStruct((b, t, D_KV), F32),
                 one, two, two, one, one, two, one, one)
    out_specs = (pl.BlockSpec((1, N_KV_HEADS, GQA_GROUP, tm, HEAD_DIM), lambda i, j: (i, 0, 0, here(j), 0)),
                 pl.BlockSpec((1, N_KV_HEADS, tm, HEAD_DIM), lambda i, j: (i, 0, here(j), 0)),
                 pl.BlockSpec((1, N_KV_HEADS, tm, 2 * HEAD_DIM), lambda i, j: (i, 0, here(j), 0)),
                 pl.BlockSpec((1, tm, D_KV), lambda i, j: (i, here(j), 0)),
                 pl.BlockSpec((1, tm, D_KV), lambda i, j: (i, here(j), 0)),
                 s_one, s_two, s_two, s_one, s_one, s_two, s_one, s_one)
    return pl.pallas_call(
        functools.partial(_inproj_prep_kernel, rope=rope, nt=nt),
        grid=(b, nt + 1), in_specs=in_specs, out_specs=out_specs, out_shape=out_shape,
        scratch_shapes=[pltpu.VMEM((tm, D_RWKV_PAD), F32), pltpu.VMEM((8, D_RWKV_PAD), F32)],
        compiler_params=_cparams(("parallel", "arbitrary")),
        name="in_proj_rope" if rope else "in_proj",
    )(*args)


def _attn_kernel(q_ref, k_ref, v_ref, o_ref):
    g, tq, hd = q_ref.shape[2:]
    sub = min(ATTN_ROWS, tq)
    k = k_ref[0, 0]
    v = v_ref[0, 0]
    slabs = [slice(i * sub, (i + 1) * sub) for i in range(tq // sub)]
    qs = [q_ref[0, 0, :, sl, :].reshape(g * sub, hd) for sl in slabs]
    ss = [_dot_nt(q, k) for q in qs]
    ps = [jnp.exp2(s - jnp.max(s, axis=-1, keepdims=True)).astype(BF16) for s in ss]
    for sl, p in zip(slabs, ps):
        o = _dot(p, v)
        o = o[:, :hd] / pltpu.roll(o, hd, 1)[:, :hd]
        o_ref[0, sl, :] = jnp.concatenate([o[i * sub:(i + 1) * sub] for i in range(g)], axis=-1).astype(BF16)


def _attention(q5, k4, v4, tq):
    b, hk, g, t, hd = q5.shape
    tk = k4.shape[2]
    return pl.pallas_call(
        _attn_kernel,
        grid=(b, hk, t // tq),
        in_specs=[pl.BlockSpec((1, 1, g, tq, hd), lambda i, j, l: (i, j, 0, l, 0)),
                  pl.BlockSpec((1, 1, tk, hd), lambda i, j, l: (i, j, 0, 0)),
                  pl.BlockSpec((1, 1, tk, 2 * hd), lambda i, j, l: (i, j, 0, 0))],
        out_specs=pl.BlockSpec((1, tq, g * hd), lambda i, j, l: (i, l, j)),
        out_shape=jax.ShapeDtypeStruct((b, t, hk * g * hd), BF16),
        compiler_params=_cparams(("parallel", "parallel", "arbitrary")),
        name="attn",
    )(q5, k4, v4)


INV_BASE = 16


def _bdot(a, b):
    return _dot(a.astype(BF16), b.astype(BF16))


def _scan_kernel(*refs, chunk):
    ins, (s0_ref, yf_ref, yb_ref, sT_ref, s_scr) = (refs[0:6], refs[6:12]), refs[12:]
    y_refs = (yf_ref, yb_ref)
    c = chunk

    @pl.when(pl.program_id(1) == 0)
    def _():
        s_scr[...] = s0_ref[0]

    row = lax.broadcasted_iota(jnp.int32, (c, c), 0)
    col = lax.broadcasted_iota(jnp.int32, (c, c), 1)
    eye = (row == col).astype(F32)
    same_blk = {}
    n = INV_BASE
    while n <= c:
        sh = jnp.int32(n.bit_length() - 1)
        same_blk[n] = lax.shift_right_logical(row, sh) == lax.shift_right_logical(col, sh)
        n *= 2
    heads = range(N_RWKV_HEADS)
    sls = [slice(h * HEAD_DIM, (h + 1) * HEAD_DIM) for h in heads]

    incl, strict, ar_abs, ar_mid, kb_inv, kb_end, v, p_tot = [], [], [], [], [], [], [], []
    for d, (r_ref, lw_ref, k_ref, v_ref, a_ref, b_ref) in enumerate(ins):
        incl.append(row >= col if d == 0 else row <= col)
        strict.append(row > col if d == 0 else row < col)
        lw = lw_ref[0, 0]
        lw_hi = lw.astype(BF16)
        lw_mid, lw_lo = _split2(lw - lw_hi.astype(F32))
        inclb = jnp.where(incl[d], 1.0, 0.0).astype(BF16)
        cl = _dot(inclb, lw_hi) + (_dot(inclb, lw_mid) + _dot(inclb, lw_lo))
        tot = jnp.sum(lw, axis=0, keepdims=True)
        mid = 0.5 * tot
        e_inv = jnp.exp(mid - cl)
        e_end = jnp.exp(tot - cl)
        s_mid = jnp.exp(-mid)
        r_abs = r_ref[0] * jnp.exp(cl)
        a_abs = a_ref[0] * jnp.exp(cl - lw)
        ar_abs.append(jnp.concatenate([a_abs, r_abs], axis=0).astype(BF16))
        ar_mid.append(jnp.concatenate([a_abs * s_mid, r_abs * s_mid], axis=0).astype(BF16))
        kb_inv.append(jnp.concatenate([k_ref[0, 0] * e_inv, b_ref[0, 0] * e_inv], axis=0).astype(BF16))
        kb_end.append(jnp.concatenate([k_ref[0, 0] * e_end, b_ref[0, 0] * e_end], axis=0).astype(BF16))
        v.append(v_ref[0].astype(BF16))
        p_tot.append(jnp.exp(tot))

    chains = [(d, h) for d in range(2) for h in heads]
    ids = range(len(chains))
    s_old = [s_scr[d, h] for d, h in chains]
    vhs = [v[d][:, sls[h]] for d, h in chains]
    grams = [_dot_nt(ar_mid[d][:, sls[h]], kb_inv[d][:, sls[h]]) for d, h in chains]
    from_s = [_dot_nt(ar_abs[d][:, sls[h]], s_old[i].astype(BF16)) for i, (d, h) in enumerate(chains)]
    masked = [jnp.concatenate([jnp.where(strict[d], grams[i][:c, :c], 0.0),
                               jnp.where(incl[d], grams[i][c:, :c], 0.0)], axis=0) for i, (d, h) in enumerate(chains)]
    from_v = [_bdot(masked[i], vhs[i]) for i in ids]
    lmats = [jnp.where(strict[d], grams[i][:c, c:], 0.0) for i, (d, h) in enumerate(chains)]
    l0s = [jnp.where(same_blk[INV_BASE], lm, 0.0) for lm in lmats]
    xs = [eye - l0 for l0 in l0s]
    pws = [_bdot(l0, l0) for l0 in l0s]
    span = 2
    while 2 * span < INV_BASE:
        both = [_bdot(jnp.concatenate([xs[i], pws[i]], axis=0), pws[i]) for i in ids]
        xs = [xs[i] + both[i][:c] for i in ids]
        pws = [both[i][c:] for i in ids]
        span *= 2
    xs = [xs[i] + _bdot(xs[i], pws[i]) for i in ids]
    n = INV_BASE
    while n < c:
        pair = same_blk[2 * n] & jnp.logical_not(same_blk[n])
        ts = [_bdot(jnp.where(pair, lmats[i], 0.0), xs[i]) for i in ids]
        xs = [xs[i] - _bdot(xs[i], ts[i]) for i in ids]
        n *= 2
    us = [_bdot(xs[i], from_s[i][:c] + from_v[i][:c]) for i in ids]
    yu = [_bdot(jnp.where(incl[d], grams[i][c:, c:], 0.0), us[i]) for i, (d, h) in enumerate(chains)]
    ds = [_dot_tn(jnp.concatenate([vhs[i], (-us[i]).astype(BF16)], axis=0), kb_end[d][:, sls[h]])
          for i, (d, h) in enumerate(chains)]
    for d in range(2):
        y_refs[d][0] = jnp.concatenate([from_s[i][c:] + from_v[i][c:] - yu[i]
                                        for i, (dd, h) in enumerate(chains) if dd == d], axis=-1)
    for i, (d, h) in enumerate(chains):
        s_scr[d, h] = s_old[i] * p_tot[d][:, sls[h]] + ds[i]

    @pl.when(pl.program_id(1) == pl.num_programs(1) - 1)
    def _():
        sT_ref[0] = s_scr[...]


def _rwkv_scan(r, lw2, kd2, v, kk, bd2, s0, chunk):
    b, t, _ = r.shape
    nc = t // chunk
    in_specs, args = [], []
    for d in range(2):
        tmap = (lambda j: j) if d == 0 else (lambda j: nc - 1 - j)
        s_one = pl.BlockSpec((1, chunk, D_RWKV), lambda i, j, tmap=tmap: (i, tmap(j), 0))
        s_two = pl.BlockSpec((1, 1, chunk, D_RWKV), lambda i, j, tmap=tmap, d=d: (d, i, tmap(j), 0))
        in_specs += [s_one, s_two, s_two, s_one, s_one, s_two]
        args += [r, lw2, kd2, v, kk, bd2]
    s_st = pl.BlockSpec((1, 2, N_RWKV_HEADS, HEAD_DIM, HEAD_DIM), lambda i, j: (i, 0, 0, 0, 0))
    y_specs = tuple(pl.BlockSpec((1, chunk, D_RWKV), lambda i, j, tmap=tmap: (i, tmap(j), 0))
                    for tmap in ((lambda j: j), (lambda j: nc - 1 - j)))
    y_shape = jax.ShapeDtypeStruct((b, t, D_RWKV), F32)
    return pl.pallas_call(
        functools.partial(_scan_kernel, chunk=chunk),
        grid=(b, nc),
        in_specs=in_specs + [s_st],
        out_specs=y_specs + (s_st,),
        out_shape=(y_shape, y_shape, jax.ShapeDtypeStruct(s0.shape, F32)),
        scratch_shapes=[pltpu.VMEM((2, N_RWKV_HEADS, HEAD_DIM, HEAD_DIM), F32)],
        compiler_params=_cparams(("parallel", "arbitrary")),
        name="rwkv_scan",
    )(*args, s0)


def _outproj_kernel(x_ref, at_ref, yf_ref, yb_ref, bonus_ref, gate_ref, lng_ref, lnb_ref, grp_ref, wo_ref,
                    g1_ref, sh_ref, sc_ref, n2_ref, wr_ref, br_ref, x1_ref, hx_ref, cnt_ref):
    grp = grp_ref[...]
    y = yf_ref[0] + yb_ref[0]
    mean = _head_sum(y, grp) * (1.0 / HEAD_DIM)
    yc = y - mean
    var = _head_sum(yc * yc, grp) * (1.0 / HEAD_DIM)
    yn = yc * lax.rsqrt(var + GN_EPS) * lng_ref[...] + lnb_ref[...]
    rw_out = (yn + bonus_ref[0]) * gate_ref[0]
    mix = (_dot(at_ref[0], wo_ref[0:D_ATTN, :])
           + _dot(rw_out.astype(BF16), wo_ref[D_ATTN:, :]))
    x1 = x_ref[0] + g1_ref[0] * mix
    x1_ref[0] = x1
    ms = jnp.mean(x1 * x1, axis=-1, keepdims=True)
    h2 = x1 * lax.rsqrt(ms + NORM_EPS) * n2_ref[...]
    h2 = h2 * (1.0 + sc_ref[0]) + sh_ref[0]
    hx_ref[0, :, 0:D_MODEL] = h2.astype(BF16)
    logits = _dot3(h2, wr_ref[...]) + br_ref[...]
    lane = lax.broadcasted_iota(jnp.int32, logits.shape, 1)
    neg = -jnp.inf
    big = jnp.int32(1 << 20)
    lc = jnp.where(lane < N_GROUPS, logits, neg)
    mc = jnp.max(lc, axis=-1, keepdims=True)
    g_w = 1.0 / jnp.sum(jnp.exp(lc - mc), axis=-1, keepdims=True)
    g_idx = jnp.min(jnp.where(lc == mc, lane, big), axis=-1, keepdims=True)
    eid = lane - N_GROUPS
    in_grp = (eid >= 0) & (eid < N_EXPERTS) & (lax.shift_right_arithmetic(eid, 2) == g_idx)
    lf = jnp.where(in_grp, logits, neg)
    m1 = jnp.max(lf, axis=-1, keepdims=True)
    i1 = jnp.min(jnp.where(lf == m1, lane, big), axis=-1, keepdims=True)
    lf2 = jnp.where(lane == i1, neg, lf)
    m2 = jnp.max(lf2, axis=-1, keepdims=True)
    i2 = jnp.min(jnp.where(lf2 == m2, lane, big), axis=-1, keepdims=True)
    e2 = jnp.exp(m2 - m1)
    w1 = 1.0 / (1.0 + e2)
    w2 = e2 * w1
    cmb = g_w * (jnp.where(lane == i1, w1, 0.0) + jnp.where(lane == i2, w2, 0.0))
    rec = jnp.where(lane == 0, g_idx.astype(F32), cmb)
    rec_hi, rec_lo = _split2(rec)
    hx_ref[0, :, D_MODEL:D_MODEL + ROUTER_LANES] = rec_hi
    hx_ref[0, :, D_MODEL + ROUTER_LANES:] = rec_lo
    hot = jnp.where((lane == g_idx) & (lane < N_GROUPS), 1.0, 0.0)
    for s in range(cnt_ref.shape[1]):
        part = jnp.sum(hot[s * MOE_TM:(s + 1) * MOE_TM], axis=0, keepdims=True)
        cnt_ref[0, s] = jnp.broadcast_to(part, cnt_ref.shape[2:])


def _out_proj(x3, attn3, yf, yb, bonus, gate, ln_g, ln_b, grp, w_out_b, g1, sh2, sc2, norm2_g, w_r, b_r):
    b, t, _ = x3.shape
    tm = OUT_PROJ_TM if t % OUT_PROJ_TM == 0 else MOE_TM
    bm = g1.shape[0]
    mod_map = (lambda i, j: (i, 0, 0)) if bm > 1 else (lambda i, j: (0, 0, 0))
    tok = lambda w: pl.BlockSpec((1, tm, w), lambda i, j: (i, j, 0))
    full = lambda a: pl.BlockSpec(a.shape, lambda i, j: (0,) * a.ndim)
    mod = pl.BlockSpec((1, 1, D_MODEL), mod_map)
    return pl.pallas_call(
        _outproj_kernel,
        grid=(b, t // tm),
        in_specs=[tok(D_MODEL), tok(D_ATTN), tok(D_RWKV), tok(D_RWKV),
                  tok(D_RWKV), tok(D_RWKV), full(ln_g), full(ln_b), full(grp), full(w_out_b),
                  mod, mod, mod, full(norm2_g), full(w_r), full(b_r)],
        out_specs=(tok(D_MODEL), tok(MOE_W), pl.BlockSpec((1, tm // MOE_TM, 8, LANES), lambda i, j: (i, j, 0, 0))),
        out_shape=(jax.ShapeDtypeStruct((b, t, D_MODEL), F32), jax.ShapeDtypeStruct((b, t, MOE_W), BF16),
                   jax.ShapeDtypeStruct((b, t // MOE_TM, 8, LANES), F32)),
        compiler_params=_cparams(("parallel", "parallel")),
        name="out_proj",
    )(x3, attn3, yf, yb, bonus, gate, ln_g, ln_b, grp, w_out_b, g1, sh2, sc2, norm2_g, w_r, b_r)


def _tile_sort_matrix(rec_hi, plan_ref, tile):
    tm = rec_hi.shape[0]
    sel = ((lax.broadcasted_iota(jnp.int32, (8, LANES), 0) == 0)
           & (lax.broadcasted_iota(jnp.int32, (8, LANES), 1) == 0))
    g_row = _dot_nt(jnp.where(sel, 1.0, 0.0).astype(BF16), rec_hi)[0:1, :]
    g_col = rec_hi.astype(F32)[:, 0:1]
    r = lax.broadcasted_iota(jnp.int32, (tm, tm), 0)
    c = lax.broadcasted_iota(jnp.int32, (tm, tm), 1)
    earlier_same = jnp.where((g_col == g_row) & (r < c), 1.0, 0.0)
    pos_row = jnp.sum(earlier_same, axis=0, keepdims=True)
    for g in range(N_GROUPS):
        local = _plan_entry(plan_ref, tile, g)[2].astype(F32)
        pos_row = pos_row + jnp.where(g_row == float(g), local, 0.0)
    stage_row = lax.broadcasted_iota(jnp.int32, (MOE_STAGE, tm), 0).astype(F32)
    return jnp.where(stage_row == pos_row, 1.0, 0.0).astype(BF16)


def _plan_entry(plan_ref, tile, g):
    at = (tile * N_GROUPS + g) * 3
    return pl.multiple_of(plan_ref[at], SEG_ALIGN), plan_ref[at + 1], pl.multiple_of(plan_ref[at + 2], MOE_SUB)


def _for_each_granule(plan_ref, tile, fn):
    for g in range(N_GROUPS):
        start, n_gran, local = _plan_entry(plan_ref, tile, g)

        def body(s, carry, g=g, start=start, local=local):
            fn(g, pl.multiple_of(start + s * MOE_SUB, SEG_ALIGN), pl.multiple_of(local + s * MOE_SUB, MOE_SUB))
            return carry

        lax.fori_loop(0, n_gran, body, 0)


def _dispatch_kernel(plan_ref, hx_ref, xs_in_ref, xs_ref, buf, sem, *, tile0):
    del xs_in_ref
    i = pl.program_id(0)
    x = hx_ref[...]
    perm = _tile_sort_matrix(x[:, D_MODEL:D_MODEL + ROUTER_LANES], plan_ref, tile0 + i)
    staged = _dot(perm, x).astype(BF16)

    def copy(g, seg_row, stage_row):
        return pltpu.make_async_copy(buf.at[pl.ds(stage_row, MOE_SUB), :],
                                     xs_ref.at[pl.ds(seg_row, MOE_SUB), :], sem.at[g])

    @pl.when(i > 0)
    def _():
        _for_each_granule(plan_ref, tile0 + i - 1, lambda *a: copy(*a).wait())

    buf[...] = staged
    _for_each_granule(plan_ref, tile0 + i, lambda *a: copy(*a).start())

    @pl.when(i == pl.num_programs(0) - 1)
    def _():
        _for_each_granule(plan_ref, tile0 + i, lambda *a: copy(*a).wait())


def _moe_dispatch(plan, hx2, xs, tile0):
    n = hx2.shape[0]
    grid_spec = pltpu.PrefetchScalarGridSpec(
        num_scalar_prefetch=1, grid=(n // MOE_TM,),
        in_specs=[pl.BlockSpec((MOE_TM, MOE_W), lambda i, plan: (i, 0)), pl.BlockSpec(memory_space=pl.ANY)],
        out_specs=pl.BlockSpec(memory_space=pl.ANY),
        scratch_shapes=[pltpu.VMEM((MOE_STAGE, MOE_W), BF16), pltpu.SemaphoreType.DMA((N_GROUPS,))])
    return pl.pallas_call(
        functools.partial(_dispatch_kernel, tile0=tile0),
        grid_spec=grid_spec,
        out_shape=jax.ShapeDtypeStruct(xs.shape, xs.dtype),
        input_output_aliases={2: 0},
        compiler_params=_cparams(("arbitrary",)),
        name="moe_dispatch",
    )(plan, hx2, xs)


def _experts_kernel(tg_ref, xs_ref, wg_ref, wu_ref, wd_ref, ys_ref):
    g = tg_ref[pl.program_id(0)]

    @pl.when(g >= N_GROUPS)
    def _():
        ys_ref[...] = jnp.zeros_like(ys_ref)

    @pl.when(g < N_GROUPS)
    def _():
        x = xs_ref[...]
        h = x[:, 0:D_MODEL]
        rec = (x[:, D_MODEL:D_MODEL + ROUTER_LANES].astype(F32) + x[:, D_MODEL + ROUTER_LANES:].astype(F32))
        lane = lax.broadcasted_iota(jnp.int32, rec.shape, 1)
        first = N_GROUPS + EXPERTS_PER_GROUP * g
        scaled = []
        for e in range(EXPERTS_PER_GROUP):
            a = _dot(h, wg_ref[0, e])
            hid = a * jax.nn.sigmoid(a) * _dot(h, wu_ref[0, e])
            c_e = jnp.sum(jnp.where(lane == first + e, rec, 0.0), axis=-1, keepdims=True)
            scaled.append((hid * c_e).astype(BF16))
        ys_ref[...] = _dot(jnp.concatenate(scaled, axis=-1), wd_ref[0]).astype(BF16)


def _moe_experts(tile_group, xs, wg4, wu4, wd4):
    p = xs.shape[0]
    grp_map = lambda nd: (lambda j, tg: (jnp.minimum(tg[j], N_GROUPS - 1),) + (0,) * (nd - 1))
    grid_spec = pltpu.PrefetchScalarGridSpec(
        num_scalar_prefetch=1, grid=(p // MOE_TX,),
        in_specs=[pl.BlockSpec((MOE_TX, MOE_W), lambda j, tg: (j, 0)),
                  pl.BlockSpec((1,) + wg4.shape[1:], grp_map(wg4.ndim)),
                  pl.BlockSpec((1,) + wu4.shape[1:], grp_map(wu4.ndim)),
                  pl.BlockSpec((1,) + wd4.shape[1:], grp_map(wd4.ndim))],
        out_specs=pl.BlockSpec((MOE_TX, D_MODEL), lambda j, tg: (j, 0)))
    return pl.pallas_call(
        _experts_kernel,
        grid_spec=grid_spec,
        out_shape=jax.ShapeDtypeStruct((p, D_MODEL), BF16),
        compiler_params=_cparams(("arbitrary",)),
        name="moe_experts",
    )(tile_group, xs, wg4, wu4, wd4)


def _combine_kernel(plan_ref, rec_ref, x1_ref, g2_ref, ys_ref, o_ref, buf, sem, *, tile0):
    lin = pl.program_id(0) * pl.num_programs(1) + pl.program_id(1)
    n = pl.num_programs(0) * pl.num_programs(1)
    slot = lin % 2

    def copy(slot_t):
        return lambda g, seg_row, stage_row: pltpu.make_async_copy(
            ys_ref.at[pl.ds(seg_row, MOE_SUB), :], buf.at[slot_t, pl.ds(stage_row, MOE_SUB), :], sem.at[slot_t, g])

    @pl.when(lin == 0)
    def _():
        buf[...] = jnp.zeros_like(buf)
        _for_each_granule(plan_ref, tile0, lambda *a: copy(0)(*a).start())

    @pl.when(lin + 1 < n)
    def _():
        _for_each_granule(plan_ref, tile0 + lin + 1, lambda *a: copy(1 - slot)(*a).start())

    perm = _tile_sort_matrix(rec_ref[0], plan_ref, tile0 + lin)
    _for_each_granule(plan_ref, tile0 + lin, lambda *a: copy(slot)(*a).wait())
    y = _dot_tn(perm, buf[slot])
    o_ref[0] = x1_ref[0] + g2_ref[0] * y


def _moe_combine(plan, hx3, x1, g2, ys, tile0):
    b, t, _ = x1.shape
    bm = g2.shape[0]
    mod_map = (lambda i, j, plan: (i, 0, 0)) if bm > 1 else (lambda i, j, plan: (0, 0, 0))
    rec_blk = D_MODEL // ROUTER_LANES
    grid_spec = pltpu.PrefetchScalarGridSpec(
        num_scalar_prefetch=1, grid=(b, t // MOE_TM),
        in_specs=[pl.BlockSpec((1, MOE_TM, ROUTER_LANES), lambda i, j, plan: (i, j, rec_blk)),
                  pl.BlockSpec((1, MOE_TM, D_MODEL), lambda i, j, plan: (i, j, 0)),
                  pl.BlockSpec((1, 1, D_MODEL), mod_map),
                  pl.BlockSpec(memory_space=pl.ANY)],
        out_specs=pl.BlockSpec((1, MOE_TM, D_MODEL), lambda i, j, plan: (i, j, 0)),
        scratch_shapes=[pltpu.VMEM((2, MOE_STAGE, D_MODEL), BF16), pltpu.SemaphoreType.DMA((2, N_GROUPS))])
    return pl.pallas_call(
        functools.partial(_combine_kernel, tile0=tile0),
        grid_spec=grid_spec,
        out_shape=jax.ShapeDtypeStruct(x1.shape, F32),
        compiler_params=_cparams(("arbitrary", "arbitrary")),
        name="moe_combine",
    )(plan, hx3, x1, g2, ys)


def _moe_plan(cnt, n_rows):
    seg = (cnt + (SEG_ALIGN - 1)) // SEG_ALIGN * SEG_ALIGN
    used = (jnp.sum(seg, axis=0) + (MOE_TX - 1)) // MOE_TX * MOE_TX
    size = used + MOE_TX
    base = jnp.cumsum(size) - size
    start = base[None, :] + jnp.cumsum(seg, axis=0) - seg
    n_gran = (cnt + (MOE_SUB - 1)) // MOE_SUB
    stage = n_gran * MOE_SUB
    local = jnp.cumsum(stage, axis=1) - stage
    starts = jnp.arange(n_rows // MOE_TX, dtype=jnp.int32)[:, None] * MOE_TX
    inside = (starts >= base[None, :]) & (starts < (base + used)[None, :])
    tile_group = jnp.where(jnp.any(inside, axis=1), jnp.argmax(inside, axis=1), N_GROUPS)
    plan = jnp.stack([start, n_gran, local], axis=-1).reshape(-1)
    return plan.astype(jnp.int32), tile_group.astype(jnp.int32)


def _moe_rows(n_tokens):
    n_tiles = n_tokens // MOE_TM
    bound = n_tokens + n_tiles * N_GROUPS * (SEG_ALIGN - 1) + N_GROUPS * 2 * MOE_TX
    return (bound + MOE_TX - 1) // MOE_TX * MOE_TX


def _layer(x3, mod6, lp, t_tiles, ctx):
    b, t, _ = x3.shape
    sh1, sc1, g1, sh2, sc2, _ = mod6
    t_tiles = {name: min(size, t) for name, size in t_tiles.items()}
    rope = ctx is not None
    n = b * t
    prep_consts = [lp[name] for name in ("mu_p", "wd_cat", "w0_cat", "wa_cat", "a0_cat", "wg_p", "k_k", "k_a", "r_k",
                                         "grp")]
    q5, k4, v4, k, v, r, lw2, kd2, vv, kk, bd2, gate, bonus = _in_proj_prep(
        x3, sh1, sc1, lp["norm1_g"], lp["w_in_p"], lp["gqk"], lp["grp"], prep_consts, rope, t_tiles["in_proj"])
    if ctx is not None:
        ctx_k, ctx_v, ctx_state = ctx
        ck = ctx_k.transpose(0, 2, 1, 3)
        cv = ctx_v.transpose(0, 2, 1, 3)
        k4 = jnp.concatenate([k4, ck.astype(BF16)], axis=2)
        v4 = jnp.concatenate([v4, jnp.concatenate([cv, jnp.ones_like(cv)], axis=-1).astype(BF16)], axis=2)
        s0 = ctx_state
    else:
        s0 = jnp.zeros((b, 2, N_RWKV_HEADS, HEAD_DIM, HEAD_DIM), F32)
    attn3 = _attention(q5, k4, v4, t_tiles["attn"])
    yf, yb, s_t = _rwkv_scan(r, lw2, kd2, vv, kk, bd2, s0, t_tiles["chunk"])
    fb, ft = (1, n) if g1.shape[0] == 1 else (b, t)
    flat = lambda a: a.reshape(fb, ft, a.shape[-1])
    x1, hx, cnt = _out_proj(flat(x3), flat(attn3), flat(yf), flat(yb), flat(bonus), flat(gate), lp["ln_g"],
                            lp["ln_b"], lp["grp"], lp["w_out_b"], g1, sh2, sc2, lp["norm2_g"], lp["w_r"], lp["b_r"])
    return (x1, hx, cnt), k.reshape(b, t, N_KV_HEADS, HEAD_DIM), v.reshape(b, t, N_KV_HEADS, HEAD_DIM), s_t


def _moe_both(passes, g2s, lp):
    counts = [p[2][:, :, 0, :N_GROUPS].reshape(-1, N_GROUPS) for p in passes]
    tiles = [c.shape[0] for c in counts]
    n_rows = _moe_rows(sum(tiles) * MOE_TM)
    plan, tile_group = _moe_plan(jnp.concatenate(counts, axis=0).astype(jnp.int32), n_rows)
    xs = jnp.zeros((n_rows, MOE_W), BF16)
    tile0 = 0
    for (x1, hx, _), nt in zip(passes, tiles):
        xs = _moe_dispatch(plan, hx.reshape(-1, MOE_W), xs, tile0)
        tile0 += nt
    ys = _moe_experts(tile_group, xs, lp["wg4"], lp["wu4"], lp["wd4"])
    outs, tile0 = [], 0
    for (x1, hx, _), g2, nt in zip(passes, g2s, tiles):
        outs.append(_moe_combine(plan, hx, x1, g2, ys, tile0))
        tile0 += nt
    return outs


def _block_diag2(w):
    z, l, c = w.shape
    out = jnp.zeros((LANES, z * c), F32)
    for i in range(z):
        out = out.at[i * l:(i + 1) * l, i * c:(i + 1) * c].set(w[i])
    return out


def _layer_params(l, w_in, norm1_g, norm2_g, mu_shift, q_norm_g, k_norm_g, w0, w_lora_up, a0, a_lora_up, g_lora_up,
                  k_k, k_a, r_k, ln_x_g, ln_x_b, w_out, router_c, router_c_b, router_f, router_f_b,
                  exp_gate, exp_up, exp_down):
    lane = np.arange(LANES)
    grp = jnp.asarray((lane[:, None] // HEAD_DIM) == (lane[None, :] // HEAD_DIM), BF16)
    pad_in = D_IN_PAD - w_in.shape[2]
    wd_cat = _block_diag2(w_lora_up[l])
    wa_cat = jnp.roll(_block_diag2(a_lora_up[l]), 2 * DECAY_LORA, axis=0)
    w_r = jnp.zeros((D_MODEL, ROUTER_LANES), F32)
    w_r = w_r.at[:, :N_GROUPS].set(router_c[l]).at[:, N_GROUPS:N_GROUPS + N_EXPERTS].set(router_f[l])
    b_r = jnp.zeros((1, ROUTER_LANES), F32)
    b_r = b_r.at[0, :N_GROUPS].set(router_c_b[l]).at[0, N_GROUPS:N_GROUPS + N_EXPERTS].set(router_f_b[l])

    by_group = lambda w: w.astype(BF16).reshape(N_GROUPS, EXPERTS_PER_GROUP, D_MODEL, D_EXPERT)

    return dict(
        grp=grp,
        norm1_g=norm1_g[l].reshape(1, D_MODEL), norm2_g=norm2_g[l].reshape(1, D_MODEL),
        w_in_p=jnp.pad(w_in[l], ((0, 0), (0, pad_in))).astype(BF16),
        gqk=jnp.concatenate([jnp.tile(q_norm_g[l], N_Q_HEADS), jnp.tile(k_norm_g[l], N_KV_HEADS)]).reshape(1, -1),
        mu_p=jnp.pad(mu_shift[l], ((0, 0), (0, D_RWKV_PAD - D_RWKV_IN))),
        wd_cat=wd_cat, w0_cat=w0[l].reshape(1, 2 * D_RWKV),
        wa_cat=wa_cat, a0_cat=a0[l].reshape(1, 2 * D_RWKV),
        wg_p=jnp.pad(g_lora_up[l], ((0, LANES - GATE_LORA), (0, 0))),
        k_k=k_k[l].reshape(1, D_RWKV), k_a=k_a[l].reshape(1, D_RWKV), r_k=r_k[l].reshape(1, D_RWKV),
        ln_g=ln_x_g[l].reshape(1, D_RWKV), ln_b=ln_x_b[l].reshape(1, D_RWKV),
        w_out_b=w_out[l].astype(BF16), w_r=w_r, b_r=b_r,
        wg4=by_group(exp_gate[l]), wu4=by_group(exp_up[l]),
        wd4=exp_down[l].astype(BF16).reshape(N_GROUPS, EXPERTS_PER_GROUP * D_EXPERT, D_MODEL),
    )


CTX_TILES = dict(in_proj=256, attn=256, chunk=128)
SMP_TILES = dict(in_proj=256, attn=256, chunk=128)


def kernel(x_prompt, x_sample, cache_k, cache_v, state_rwkv, c, c_ctx, w_mod, b_mod, norm1_g, norm2_g, w_in, mu_shift, q_norm_g, k_norm_g, w0, w_lora_up, a0, a_lora_up, g_lora_up, k_k, k_a, r_k, ln_x_g, ln_x_b, w_out, router_c, router_c_b, router_f, router_f_b, exp_gate, exp_up, exp_down):
    depth = w_mod.shape[0]
    db = x_sample.shape[0]
    y_prompt, y_sample = x_prompt, x_sample
    ks, vs, ss = [], [], []
    cond = jnp.zeros((8, D_MODEL), F32).at[:db].set(c).at[db].set(c_ctx)
    for l in range(depth):
        lp = _layer_params(l, w_in, norm1_g, norm2_g, mu_shift, q_norm_g, k_norm_g, w0, w_lora_up, a0, a_lora_up,
                           g_lora_up, k_k, k_a, r_k, ln_x_g, ln_x_b, w_out, router_c, router_c_b, router_f,
                           router_f_b, exp_gate, exp_up, exp_down)
        mod = _modulation(cond, w_mod[l], b_mod[l])
        mod_s = [mod[:db, i * D_MODEL:(i + 1) * D_MODEL].reshape(db, 1, D_MODEL) for i in range(6)]
        mod_c = [mod[db:db + 1, i * D_MODEL:(i + 1) * D_MODEL].reshape(1, 1, D_MODEL) for i in range(6)]
        pre_c, k_l, v_l, s_l = _layer(y_prompt, mod_c, lp, CTX_TILES, None)
        ks.append(k_l)
        vs.append(v_l)
        ss.append(s_l)
        pre_s, _, _, _ = _layer(y_sample, mod_s, lp, SMP_TILES, (cache_k[:, l], cache_v[:, l], state_rwkv[:, l]))
        out_c, out_s = _moe_both([pre_c, pre_s], [mod_c[5], mod_s[5]], lp)
        y_prompt, y_sample = out_c.reshape(y_prompt.shape), out_s.reshape(y_sample.shape)
    return (y_prompt, y_sample, jnp.stack(ks, axis=1), jnp.stack(vs, axis=1), jnp.stack(ss, axis=1))
```

```python
import functools

import numpy as np
import jax
import jax.numpy as jnp
from jax import lax
from jax.experimental import pallas as pl
from jax.experimental.pallas import tpu as pltpu

F32 = jnp.float32
BF16 = jnp.bfloat16
HIGHEST = lax.Precision.HIGHEST

D_MODEL = 1024
HEAD_DIM = 64
N_Q_HEADS = 8
N_KV_HEADS = 2
GQA_GROUP = N_Q_HEADS // N_KV_HEADS
D_ATTN = N_Q_HEADS * HEAD_DIM
D_KV = N_KV_HEADS * HEAD_DIM
N_RWKV_HEADS = 8
D_RWKV = 512
DECAY_LORA = 32
AAA_LORA = 32
GATE_LORA = 96
D_RWKV_IN = 3 * D_RWKV + 2 * DECAY_LORA + 2 * AAA_LORA + GATE_LORA
D_RWKV_PAD = 1792
D_QKV = D_ATTN + 2 * D_KV
D_IN_PAD = D_QKV + D_RWKV_PAD
N_GROUPS = 4
EXPERTS_PER_GROUP = 4
N_EXPERTS = 16
D_EXPERT = 512
GRID_W = 64
ROPE_THETA = 10000.0
NORM_EPS = 1e-6
GN_EPS = 64e-5
DECAY_SCALE = 0.6065306597
QK_EXP2_SCALE = (HEAD_DIM ** -0.5) * float(np.log2(np.e))
LANES = 128
ROUTER_LANES = 128
VMEM_LIMIT = 56 * 1024 * 1024
ATTN_ROWS = 128
MOE_W = D_MODEL + 2 * ROUTER_LANES
MOE_TM = 256
OUT_PROJ_TM = 1024
MOE_SUB = 32
MOE_STAGE = MOE_TM + N_GROUPS * MOE_SUB
MOE_TX = 512
SEG_ALIGN = 16


def _cparams(sem):
    return pltpu.CompilerParams(dimension_semantics=sem, vmem_limit_bytes=VMEM_LIMIT)


def _dot(a, b, precision=None):
    return jnp.dot(a, b, preferred_element_type=F32, precision=precision)


def _dot_nt(a, b, precision=None):
    return lax.dot_general(a, b, (((1,), (1,)), ((), ())), preferred_element_type=F32, precision=precision)


def _dot_tn(a, b, precision=None):
    return lax.dot_general(a, b, (((0,), (0,)), ((), ())), preferred_element_type=F32, precision=precision)


def _split2(x):
    hi = x.astype(BF16)
    return hi, (x - hi.astype(F32)).astype(BF16)


def _dot3(a, b):
    a_hi, a_lo = _split2(a)
    b_hi, b_lo = _split2(b)
    return _dot(a_hi, b_hi) + (_dot(a_hi, b_lo) + _dot(a_lo, b_hi))


def _head_sum(x, g):
    xb = x.astype(BF16)
    n = x.shape[-1] // LANES
    return jnp.concatenate([_dot(xb[:, j * LANES:(j + 1) * LANES], g) for j in range(n)], axis=-1)


def _mod_kernel(c_ref, w_ref, b_ref, o_ref):
    c = c_ref[...]
    s = c * jax.nn.sigmoid(c)
    o_ref[...] = _dot(s, w_ref[...], HIGHEST) + b_ref[...]


def _modulation(cond, w_mod, b_mod):
    n = w_mod.shape[1]
    tn = 1024
    return pl.pallas_call(
        _mod_kernel,
        grid=(n // tn,),
        in_specs=[pl.BlockSpec((8, D_MODEL), lambda j: (0, 0)),
                  pl.BlockSpec((D_MODEL, tn), lambda j: (0, j)),
                  pl.BlockSpec((1, tn), lambda j: (0, j))],
        out_specs=pl.BlockSpec((8, tn), lambda j: (0, j)),
        out_shape=jax.ShapeDtypeStruct((8, n), F32),
        compiler_params=_cparams(("arbitrary",)),
        name="mod",
    )(cond, w_mod, b_mod.reshape(1, n))


def _rope_tables(t_len):
    half = HEAD_DIM // 2
    inv = ROPE_THETA ** (-np.arange(0, half, 2, dtype=np.float64) / half)
    t = np.arange(t_len)
    row, col = t // GRID_W, t % GRID_W
    lane = np.arange(LANES)
    i = lane % HEAD_DIM
    pos = np.where((i // half)[None, :] == 0, row[:, None], col[:, None]).astype(np.float64)
    j = i % half
    ang = pos * inv[j % (half // 2)][None, :]
    cos, sin = np.cos(ang), np.sin(ang)
    first = (j < half // 2)[None, :]
    s_up = np.where(first, -sin, 0.0)
    s_dn = np.where(first, 0.0, sin)
    return (jnp.asarray(cos, F32), jnp.asarray(s_up, F32), jnp.asarray(s_dn, F32))


def _project(x_ref, sh_ref, sc_ref, g_ref, w_ref, gqk_ref, grp_ref, rope_refs, q_ref, k_ref, v_ref, kf_ref, vf_ref):
    rope = rope_refs is not None
    if rope:
        cos_ref, sup_ref, sdn_ref = rope_refs
    x = x_ref[0]
    ms = jnp.mean(x * x, axis=-1, keepdims=True)
    h = x * lax.rsqrt(ms + NORM_EPS) * g_ref[...]
    h = h * (1.0 + sc_ref[0]) + sh_ref[0]
    proj = _dot(h.astype(BF16), w_ref[...])
    grp = grp_ref[...]
    lo_half = lax.broadcasted_iota(jnp.int32, (x.shape[0], LANES), 1) < HEAD_DIM
    for j in range((D_ATTN + D_KV) // LANES):
        blk = proj[:, j * LANES:(j + 1) * LANES]
        ss = _head_sum(blk * blk, grp) * (1.0 / HEAD_DIM)
        nb = blk * lax.rsqrt(ss + NORM_EPS) * gqk_ref[:, j * LANES:(j + 1) * LANES]
        if rope:
            nb = (nb * cos_ref[...] + pltpu.roll(nb, LANES - 16, 1) * sup_ref[...]
                  + pltpu.roll(nb, 16, 1) * sdn_ref[...])
        if j < D_ATTN // LANES:
            nbq = nb * QK_EXP2_SCALE
            for half in range(2):
                hq = 2 * j + half
                q_ref[0, hq // GQA_GROUP, hq % GQA_GROUP] = nbq[:, half * HEAD_DIM:(half + 1) * HEAD_DIM].astype(BF16)
        else:
            kf_ref[0] = nb
            k_ref[0, 0] = nb[:, :HEAD_DIM].astype(BF16)
            k_ref[0, 1] = nb[:, HEAD_DIM:].astype(BF16)
    vblk = proj[:, D_ATTN + D_KV:D_QKV]
    vf_ref[0] = vblk
    v_ref[0, 0] = jnp.where(lo_half, vblk, 1.0).astype(BF16)
    v_ref[0, 1] = jnp.where(lo_half, pltpu.roll(vblk, HEAD_DIM, 1), 1.0).astype(BF16)
    return proj[:, D_QKV:]


def _rwkv_features(cur, prev_row, next_row, mu_ref, wd_ref, w0_ref, wa_ref, a0_ref, wg_ref, kk_ref, ka_ref, rk_ref,
                   grp_ref, r_o, lw_o, kd_o, v_o, kk_o, bd_o, g_o, bonus_o):
    tt = cur.shape[0]
    rid = lax.broadcasted_iota(jnp.int32, cur.shape, 0)
    prev = jnp.where(rid == 0, prev_row, pltpu.roll(cur, 1, 0))
    nxt = jnp.where(rid == tt - 1, next_row, pltpu.roll(cur, tt - 1, 0))
    p = cur + mu_ref[0:1, :] * (prev - cur) + mu_ref[1:2, :] * (nxt - cur)
    r = p[:, 0:D_RWKV]
    k = p[:, D_RWKV:2 * D_RWKV]
    v = p[:, 2 * D_RWKV:3 * D_RWKV]
    lo = p[:, 3 * D_RWKV:3 * D_RWKV + LANES]
    gd = p[:, 3 * D_RWKV + LANES:]
    grp = grp_ref[...]
    wlog = _dot(jnp.tanh(lo).astype(BF16), wd_ref[...]) + w0_ref[...]
    alog = _dot(lo.astype(BF16), wa_ref[...]) + a0_ref[...]
    g_o[0] = _dot(jax.nn.sigmoid(gd).astype(BF16), wg_ref[...])
    kx = k * kk_ref[...]
    kk = kx * lax.rsqrt(_head_sum(kx * kx, grp) + 1e-12)
    r_o[0] = r
    v_o[0] = v
    kk_o[0] = kk
    bonus_o[0] = _head_sum(r * k * rk_ref[...], grp) * v
    for z in range(2):
        a = jax.nn.sigmoid(alog[:, z * D_RWKV:(z + 1) * D_RWKV])
        lw_o[z, 0] = -DECAY_SCALE * jax.nn.sigmoid(wlog[:, z * D_RWKV:(z + 1) * D_RWKV])
        kd_o[z, 0] = k * (1.0 + (a - 1.0) * ka_ref[...])
        bd_o[z, 0] = kk * a


N_PROJ_IN = 7
N_PREP_CONST = 10
N_PROJ_OUT = 5
N_PREP_OUT = 8


def _inproj_prep_kernel(*refs, rope, nt):
    n_rope = 3 if rope else 0
    proj_in = refs[:N_PROJ_IN]
    rope_refs = refs[N_PROJ_IN:N_PROJ_IN + n_rope] if rope else None
    at = N_PROJ_IN + n_rope
    consts = refs[at:at + N_PREP_CONST]
    proj_out = refs[at + N_PREP_CONST:at + N_PREP_CONST + N_PROJ_OUT]
    prep_out = refs[at + N_PREP_CONST + N_PROJ_OUT:at + N_PREP_CONST + N_PROJ_OUT + N_PREP_OUT]
    slab_scr, tail_scr = refs[-2:]
    j = pl.program_id(1)
    zero_row = jnp.zeros((1, D_RWKV_PAD), F32)

    def project():
        return _project(*proj_in, rope_refs, *proj_out)

    def features(next_row, first_tile):
        cur = slab_scr[...]
        prev_row = zero_row if first_tile else tail_scr[7:8, :]
        _rwkv_features(cur, prev_row, next_row, *consts, *prep_out)
        tail_scr[...] = cur[cur.shape[0] - 8:, :]

    @pl.when(j == 0)
    def _():
        slab_scr[...] = project()

    if nt > 1:
        @pl.when(j == 1)
        def _():
            slab = project()
            features(slab[0:1, :], True)
            slab_scr[...] = slab

    if nt > 2:
        @pl.when((j > 1) & (j < nt))
        def _():
            slab = project()
            features(slab[0:1, :], False)
            slab_scr[...] = slab

    @pl.when(j == nt)
    def _():
        features(zero_row, nt == 1)


def _in_proj_prep(x3, shift, scale, norm_g, w_in_p, gqk, grp, prep_consts, rope, tm):
    b, t, _ = x3.shape
    nt = t // tm
    bm = shift.shape[0]
    mod_map = (lambda i, j: (i, 0, 0)) if bm > 1 else (lambda i, j: (0, 0, 0))
    full = lambda a: pl.BlockSpec(a.shape, lambda i, j: (0,) * a.ndim)
    here = lambda j: jnp.minimum(j, nt - 1)
    back = lambda j: jnp.maximum(j - 1, 0)
    in_specs = [pl.BlockSpec((1, tm, D_MODEL), lambda i, j: (i, here(j), 0)),
                pl.BlockSpec((1, 1, D_MODEL), mod_map), pl.BlockSpec((1, 1, D_MODEL), mod_map),
                full(norm_g), full(w_in_p), full(gqk), full(grp)]
    args = [x3, shift, scale, norm_g, w_in_p, gqk, grp]
    if rope:
        in_specs += [pl.BlockSpec((tm, LANES), lambda i, j: (here(j), 0))] * 3
        args += list(_rope_tables(t))
    in_specs += [full(a) for a in prep_consts]
    args += list(prep_consts)
    one = jax.ShapeDtypeStruct((b, t, D_RWKV), F32)
    two = jax.ShapeDtypeStruct((2, b, t, D_RWKV), F32)
    s_one = pl.BlockSpec((1, tm, D_RWKV), lambda i, j: (i, back(j), 0))
    s_two = pl.BlockSpec((2, 1, tm, D_RWKV), lambda i, j: (0, i, back(j), 0))
    out_shape = (jax.ShapeDtypeStruct((b, N_KV_HEADS, GQA_GROUP, t, HEAD_DIM), BF16),
                 jax.ShapeDtypeStruct((b, N_KV_HEADS, t, HEAD_DIM), BF16),
                 jax.ShapeDtypeStruct((b, N_KV_HEADS, t, 2 * HEAD_DIM), BF16),
                 jax.ShapeDtypeStruct((b, t, D_KV), F32), jax.ShapeDtypeStruct((b, t, D_KV), F32),
                 one, two, two, one, one, two, one, one)
    out_specs = (pl.BlockSpec((1, N_KV_HEADS, GQA_GROUP, tm, HEAD_DIM), lambda i, j: (i, 0, 0, here(j), 0)),
                 pl.BlockSpec((1, N_KV_HEADS, tm, HEAD_DIM), lambda i, j: (i, 0, here(j), 0)),
                 pl.BlockSpec((1, N_KV_HEADS, tm, 2 * HEAD_DIM), lambda i, j: (i, 0, here(j), 0)),
                 pl.BlockSpec((1, tm, D_KV), lambda i, j: (i, here(j), 0)),
                 pl.BlockSpec((1, tm, D_KV), lambda i, j: (i, here(j), 0)),
                 s_one, s_two, s_two, s_one, s_one, s_two, s_one, s_one)
    return pl.pallas_call(
        functools.partial(_inproj_prep_kernel, rope=rope, nt=nt),
        grid=(b, nt + 1), in_specs=in_specs, out_specs=out_specs, out_shape=out_shape,
        scratch_shapes=[pltpu.VMEM((tm, D_RWKV_PAD), F32), pltpu.VMEM((8, D_RWKV_PAD), F32)],
        compiler_params=_cparams(("parallel", "arbitrary")),
        name="in_proj_rope" if rope else "in_proj",
    )(*args)


def _attn_kernel(q_ref, k_ref, v_ref, o_ref):
    g, tq, hd = q_ref.shape[2:]
    sub = min(ATTN_ROWS, tq)
    k = k_ref[0, 0]
    v = v_ref[0, 0]
    slabs = [slice(i * sub, (i + 1) * sub) for i in range(tq // sub)]
    qs = [q_ref[0, 0, :, sl, :].reshape(g * sub, hd) for sl in slabs]
    ss = [_dot_nt(q, k) for q in qs]
    ps = [jnp.exp2(s - jnp.max(s, axis=-1, keepdims=True)).astype(BF16) for s in ss]
    for sl, p in zip(slabs, ps):
        o = _dot(p, v)
        o = o[:, :hd] / pltpu.roll(o, hd, 1)[:, :hd]
        o_ref[0, sl, :] = jnp.concatenate([o[i * sub:(i + 1) * sub] for i in range(g)], axis=-1).astype(BF16)


def _attention(q5, k4, v4, tq):
    b, hk, g, t, hd = q5.shape
    tk = k4.shape[2]
    return pl.pallas_call(
        _attn_kernel,
        grid=(b, hk, t // tq),
        in_specs=[pl.BlockSpec((1, 1, g, tq, hd), lambda i, j, l: (i, j, 0, l, 0)),
                  pl.BlockSpec((1, 1, tk, hd), lambda i, j, l: (i, j, 0, 0)),
                  pl.BlockSpec((1, 1, tk, 2 * hd), lambda i, j, l: (i, j, 0, 0))],
        out_specs=pl.BlockSpec((1, tq, g * hd), lambda i, j, l: (i, l, j)),
        out_shape=jax.ShapeDtypeStruct((b, t, hk * g * hd), BF16),
        compiler_params=_cparams(("parallel", "parallel", "arbitrary")),
        name="attn",
    )(q5, k4, v4)


INV_BASE = 16


def _bdot(a, b):
    return _dot(a.astype(BF16), b.astype(BF16))


def _scan_kernel(*refs, chunk):
    ins, (s0_ref, yf_ref, yb_ref, sT_ref, s_scr) = (refs[0:6], refs[6:12]), refs[12:]
    y_refs = (yf_ref, yb_ref)
    c = chunk

    @pl.when(pl.program_id(1) == 0)
    def _():
        s_scr[...] = s0_ref[0]

    row = lax.broadcasted_iota(jnp.int32, (c, c), 0)
    col = lax.broadcasted_iota(jnp.int32, (c, c), 1)
    eye = (row == col).astype(F32)
    same_blk = {}
    n = INV_BASE
    while n <= c:
        sh = jnp.int32(n.bit_length() - 1)
        same_blk[n] = lax.shift_right_logical(row, sh) == lax.shift_right_logical(col, sh)
        n *= 2
    heads = range(N_RWKV_HEADS)
    sls = [slice(h * HEAD_DIM, (h + 1) * HEAD_DIM) for h in heads]

    incl, strict, ar_abs, ar_mid, kb_inv, kb_end, v, p_tot = [], [], [], [], [], [], [], []
    for d, (r_ref, lw_ref, k_ref, v_ref, a_ref, b_ref) in enumerate(ins):
        incl.append(row >= col if d == 0 else row <= col)
        strict.append(row > col if d == 0 else row < col)
        lw = lw_ref[0, 0]
        lw_hi = lw.astype(BF16)
        lw_mid, lw_lo = _split2(lw - lw_hi.astype(F32))
        inclb = jnp.where(incl[d], 1.0, 0.0).astype(BF16)
        cl = _dot(inclb, lw_hi) + (_dot(inclb, lw_mid) + _dot(inclb, lw_lo))
        tot = jnp.sum(lw, axis=0, keepdims=True)
        mid = 0.5 * tot
        e_inv = jnp.exp(mid - cl)
        e_end = jnp.exp(tot - cl)
        s_mid = jnp.exp(-mid)
        r_abs = r_ref[0] * jnp.exp(cl)
        a_abs = a_ref[0] * jnp.exp(cl - lw)
        ar_abs.append(jnp.concatenate([a_abs, r_abs], axis=0).astype(BF16))
        ar_mid.append(jnp.concatenate([a_abs * s_mid, r_abs * s_mid], axis=0).astype(BF16))
        kb_inv.append(jnp.concatenate([k_ref[0, 0] * e_inv, b_ref[0, 0] * e_inv], axis=0).astype(BF16))
        kb_end.append(jnp.concatenate([k_ref[0, 0] * e_end, b_ref[0, 0] * e_end], axis=0).astype(BF16))
        v.append(v_ref[0].astype(BF16))
        p_tot.append(jnp.exp(tot))

    chains = [(d, h) for d in range(2) for h in heads]
    ids = range(len(chains))
    s_old = [s_scr[d, h] for d, h in chains]
    vhs = [v[d][:, sls[h]] for d, h in chains]
    grams = [_dot_nt(ar_mid[d][:, sls[h]], kb_inv[d][:, sls[h]]) for d, h in chains]
    from_s = [_dot_nt(ar_abs[d][:, sls[h]], s_old[i].astype(BF16)) for i, (d, h) in enumerate(chains)]
    masked = [jnp.concatenate([jnp.where(strict[d], grams[i][:c, :c], 0.0),
                               jnp.where(incl[d], grams[i][c:, :c], 0.0)], axis=0) for i, (d, h) in enumerate(chains)]
    from_v = [_bdot(masked[i], vhs[i]) for i in ids]
    lmats = [jnp.where(strict[d], grams[i][:c, c:], 0.0) for i, (d, h) in enumerate(chains)]
    l0s = [jnp.where(same_blk[INV_BASE], lm, 0.0) for lm in lmats]
    xs = [eye - l0 for l0 in l0s]
    pws = [_bdot(l0, l0) for l0 in l0s]
    span = 2
    while 2 * span < INV_BASE:
        both = [_bdot(jnp.concatenate([xs[i], pws[i]], axis=0), pws[i]) for i in ids]
        xs = [xs[i] + both[i][:c] for i in ids]
        pws = [both[i][c:] for i in ids]
        span *= 2
    xs = [xs[i] + _bdot(xs[i], pws[i]) for i in ids]
    n = INV_BASE
    while n < c:
        pair = same_blk[2 * n] & jnp.logical_not(same_blk[n])
        ts = [_bdot(jnp.where(pair, lmats[i], 0.0), xs[i]) for i in ids]
        xs = [xs[i] - _bdot(xs[i], ts[i]) for i in ids]
        n *= 2
    us = [_bdot(xs[i], from_s[i][:c] + from_v[i][:c]) for i in ids]
    yu = [_bdot(jnp.where(incl[d], grams[i][c:, c:], 0.0), us[i]) for i, (d, h) in enumerate(chains)]
    ds = [_dot_tn(jnp.concatenate([vhs[i], (-us[i]).astype(BF16)], axis=0), kb_end[d][:, sls[h]])
          for i, (d, h) in enumerate(chains)]
    for d in range(2):
        y_refs[d][0] = jnp.concatenate([from_s[i][c:] + from_v[i][c:] - yu[i]
                                        for i, (dd, h) in enumerate(chains) if dd == d], axis=-1)
    for i, (d, h) in enumerate(chains):
        s_scr[d, h] = s_old[i] * p_tot[d][:, sls[h]] + ds[i]

    @pl.when(pl.program_id(1) == pl.num_programs(1) - 1)
    def _():
        sT_ref[0] = s_scr[...]


def _rwkv_scan(r, lw2, kd2, v, kk, bd2, s0, chunk):
    b, t, _ = r.shape
    nc = t // chunk
    in_specs, args = [], []
    for d in range(2):
        tmap = (lambda j: j) if d == 0 else (lambda j: nc - 1 - j)
        s_one = pl.BlockSpec((1, chunk, D_RWKV), lambda i, j, tmap=tmap: (i, tmap(j), 0))
        s_two = pl.BlockSpec((1, 1, chunk, D_RWKV), lambda i, j, tmap=tmap, d=d: (d, i, tmap(j), 0))
        in_specs += [s_one, s_two, s_two, s_one, s_one, s_two]
        args += [r, lw2, kd2, v, kk, bd2]
    s_st = pl.BlockSpec((1, 2, N_RWKV_HEADS, HEAD_DIM, HEAD_DIM), lambda i, j: (i, 0, 0, 0, 0))
    y_specs = tuple(pl.BlockSpec((1, chunk, D_RWKV), lambda i, j, tmap=tmap: (i, tmap(j), 0))
                    for tmap in ((lambda j: j), (lambda j: nc - 1 - j)))
    y_shape = jax.ShapeDtypeStruct((b, t, D_RWKV), F32)
    return pl.pallas_call(
        functools.partial(_scan_kernel, chunk=chunk),
        grid=(b, nc),
        in_specs=in_specs + [s_st],
        out_specs=y_specs + (s_st,),
        out_shape=(y_shape, y_shape, jax.ShapeDtypeStruct(s0.shape, F32)),
        scratch_shapes=[pltpu.VMEM((2, N_RWKV_HEADS, HEAD_DIM, HEAD_DIM), F32)],
        compiler_params=_cparams(("parallel", "arbitrary")),
        name="rwkv_scan",
    )(*args, s0)


def _outproj_kernel(x_ref, at_ref, yf_ref, yb_ref, bonus_ref, gate_ref, lng_ref, lnb_ref, grp_ref, wo_ref,
                    g1_ref, sh_ref, sc_ref, n2_ref, wr_ref, br_ref, x1_ref, hx_ref, cnt_ref):
    grp = grp_ref[...]
    y = yf_ref[0] + yb_ref[0]
    mean = _head_sum(y, grp) * (1.0 / HEAD_DIM)
    yc = y - mean
    var = _head_sum(yc * yc, grp) * (1.0 / HEAD_DIM)
    yn = yc * lax.rsqrt(var + GN_EPS) * lng_ref[...] + lnb_ref[...]
    rw_out = (yn + bonus_ref[0]) * gate_ref[0]
    mix = (_dot(at_ref[0], wo_ref[0:D_ATTN, :])
           + _dot(rw_out.astype(BF16), wo_ref[D_ATTN:, :]))
    x1 = x_ref[0] + g1_ref[0] * mix
    x1_ref[0] = x1
    ms = jnp.mean(x1 * x1, axis=-1, keepdims=True)
    h2 = x1 * lax.rsqrt(ms + NORM_EPS) * n2_ref[...]
    h2 = h2 * (1.0 + sc_ref[0]) + sh_ref[0]
    hx_ref[0, :, 0:D_MODEL] = h2.astype(BF16)
    logits = _dot3(h2, wr_ref[...]) + br_ref[...]
    lane = lax.broadcasted_iota(jnp.int32, logits.shape, 1)
    neg = -jnp.inf
    big = jnp.int32(1 << 20)
    lc = jnp.where(lane < N_GROUPS, logits, neg)
    mc = jnp.max(lc, axis=-1, keepdims=True)
    g_w = 1.0 / jnp.sum(jnp.exp(lc - mc), axis=-1, keepdims=True)
    g_idx = jnp.min(jnp.where(lc == mc, lane, big), axis=-1, keepdims=True)
    eid = lane - N_GROUPS
    in_grp = (eid >= 0) & (eid < N_EXPERTS) & (lax.shift_right_arithmetic(eid, 2) == g_idx)
    lf = jnp.where(in_grp, logits, neg)
    m1 = jnp.max(lf, axis=-1, keepdims=True)
    i1 = jnp.min(jnp.where(lf == m1, lane, big), axis=-1, keepdims=True)
    lf2 = jnp.where(lane == i1, neg, lf)
    m2 = jnp.max(lf2, axis=-1, keepdims=True)
    i2 = jnp.min(jnp.where(lf2 == m2, lane, big), axis=-1, keepdims=True)
    e2 = jnp.exp(m2 - m1)
    w1 = 1.0 / (1.0 + e2)
    w2 = e2 * w1
    cmb = g_w * (jnp.where(lane == i1, w1, 0.0) + jnp.where(lane == i2, w2, 0.0))
    rec = jnp.where(lane == 0, g_idx.astype(F32), cmb)
    rec_hi, rec_lo = _split2(rec)
    hx_ref[0, :, D_MODEL:D_MODEL + ROUTER_LANES] = rec_hi
    hx_ref[0, :, D_MODEL + ROUTER_LANES:] = rec_lo
    hot = jnp.where((lane == g_idx) & (lane < N_GROUPS), 1.0, 0.0)
    for s in range(cnt_ref.shape[1]):
        part = jnp.sum(hot[s * MOE_TM:(s + 1) * MOE_TM], axis=0, keepdims=True)
        cnt_ref[0, s] = jnp.broadcast_to(part, cnt_ref.shape[2:])


def _out_proj(x3, attn3, yf, yb, bonus, gate, ln_g, ln_b, grp, w_out_b, g1, sh2, sc2, norm2_g, w_r, b_r):
    b, t, _ = x3.shape
    tm = OUT_PROJ_TM if t % OUT_PROJ_TM == 0 else MOE_TM
    bm = g1.shape[0]
    mod_map = (lambda i, j: (i, 0, 0)) if bm > 1 else (lambda i, j: (0, 0, 0))
    tok = lambda w: pl.BlockSpec((1, tm, w), lambda i, j: (i, j, 0))
    full = lambda a: pl.BlockSpec(a.shape, lambda i, j: (0,) * a.ndim)
    mod = pl.BlockSpec((1, 1, D_MODEL), mod_map)
    return pl.pallas_call(
        _outproj_kernel,
        grid=(b, t // tm),
        in_specs=[tok(D_MODEL), tok(D_ATTN), tok(D_RWKV), tok(D_RWKV),
                  tok(D_RWKV), tok(D_RWKV), full(ln_g), full(ln_b), full(grp), full(w_out_b),
                  mod, mod, mod, full(norm2_g), full(w_r), full(b_r)],
        out_specs=(tok(D_MODEL), tok(MOE_W), pl.BlockSpec((1, tm // MOE_TM, 8, LANES), lambda i, j: (i, j, 0, 0))),
        out_shape=(jax.ShapeDtypeStruct((b, t, D_MODEL), F32), jax.ShapeDtypeStruct((b, t, MOE_W), BF16),
                   jax.ShapeDtypeStruct((b, t // MOE_TM, 8, LANES), F32)),
        compiler_params=_cparams(("parallel", "parallel")),
        name="out_proj",
    )(x3, attn3, yf, yb, bonus, gate, ln_g, ln_b, grp, w_out_b, g1, sh2, sc2, norm2_g, w_r, b_r)


def _tile_sort_matrix(rec_hi, plan_ref, tile):
    tm = rec_hi.shape[0]
    sel = ((lax.broadcasted_iota(jnp.int32, (8, LANES), 0) == 0)
           & (lax.broadcasted_iota(jnp.int32, (8, LANES), 1) == 0))
    g_row = _dot_nt(jnp.where(sel, 1.0, 0.0).astype(BF16), rec_hi)[0:1, :]
    g_col = rec_hi.astype(F32)[:, 0:1]
    r = lax.broadcasted_iota(jnp.int32, (tm, tm), 0)
    c = lax.broadcasted_iota(jnp.int32, (tm, tm), 1)
    earlier_same = jnp.where((g_col == g_row) & (r < c), 1.0, 0.0)
    pos_row = jnp.sum(earlier_same, axis=0, keepdims=True)
    for g in range(N_GROUPS):
        local = _plan_entry(plan_ref, tile, g)[2].astype(F32)
        pos_row = pos_row + jnp.where(g_row == float(g), local, 0.0)
    stage_row = lax.broadcasted_iota(jnp.int32, (MOE_STAGE, tm), 0).astype(F32)
    return jnp.where(stage_row == pos_row, 1.0, 0.0).astype(BF16)


def _plan_entry(plan_ref, tile, g):
    at = (tile * N_GROUPS + g) * 3
    return pl.multiple_of(plan_ref[at], SEG_ALIGN), plan_ref[at + 1], pl.multiple_of(plan_ref[at + 2], MOE_SUB)


def _for_each_granule(plan_ref, tile, fn):
    for g in range(N_GROUPS):
        start, n_gran, local = _plan_entry(plan_ref, tile, g)

        def body(s, carry, g=g, start=start, local=local):
            fn(g, pl.multiple_of(start + s * MOE_SUB, SEG_ALIGN), pl.multiple_of(local + s * MOE_SUB, MOE_SUB))
            return carry

        lax.fori_loop(0, n_gran, body, 0)


def _dispatch_kernel(plan_ref, hx_ref, xs_in_ref, xs_ref, buf, sem, *, tile0):
    del xs_in_ref
    i = pl.program_id(0)
    x = hx_ref[...]
    perm = _tile_sort_matrix(x[:, D_MODEL:D_MODEL + ROUTER_LANES], plan_ref, tile0 + i)
    staged = _dot(perm, x).astype(BF16)

    def copy(g, seg_row, stage_row):
        return pltpu.make_async_copy(buf.at[pl.ds(stage_row, MOE_SUB), :],
                                     xs_ref.at[pl.ds(seg_row, MOE_SUB), :], sem.at[g])

    @pl.when(i > 0)
    def _():
        _for_each_granule(plan_ref, tile0 + i - 1, lambda *a: copy(*a).wait())

    buf[...] = staged
    _for_each_granule(plan_ref, tile0 + i, lambda *a: copy(*a).start())

    @pl.when(i == pl.num_programs(0) - 1)
    def _():
        _for_each_granule(plan_ref, tile0 + i, lambda *a: copy(*a).wait())


def _moe_dispatch(plan, hx2, xs, tile0):
    n = hx2.shape[0]
    grid_spec = pltpu.PrefetchScalarGridSpec(
        num_scalar_prefetch=1, grid=(n // MOE_TM,),
        in_specs=[pl.BlockSpec((MOE_TM, MOE_W), lambda i, plan: (i, 0)), pl.BlockSpec(memory_space=pl.ANY)],
        out_specs=pl.BlockSpec(memory_space=pl.ANY),
        scratch_shapes=[pltpu.VMEM((MOE_STAGE, MOE_W), BF16), pltpu.SemaphoreType.DMA((N_GROUPS,))])
    return pl.pallas_call(
        functools.partial(_dispatch_kernel, tile0=tile0),
        grid_spec=grid_spec,
        out_shape=jax.ShapeDtypeStruct(xs.shape, xs.dtype),
        input_output_aliases={2: 0},
        compiler_params=_cparams(("arbitrary",)),
        name="moe_dispatch",
    )(plan, hx2, xs)


def _experts_kernel(tg_ref, xs_ref, wg_ref, wu_ref, wd_ref, ys_ref):
    g = tg_ref[pl.program_id(0)]

    @pl.when(g >= N_GROUPS)
    def _():
        ys_ref[...] = jnp.zeros_like(ys_ref)

    @pl.when(g < N_GROUPS)
    def _():
        x = xs_ref[...]
        h = x[:, 0:D_MODEL]
        rec = (x[:, D_MODEL:D_MODEL + ROUTER_LANES].astype(F32) + x[:, D_MODEL + ROUTER_LANES:].astype(F32))
        lane = lax.broadcasted_iota(jnp.int32, rec.shape, 1)
        first = N_GROUPS + EXPERTS_PER_GROUP * g
        scaled = []
        for e in range(EXPERTS_PER_GROUP):
            a = _dot(h, wg_ref[0, e])
            hid = a * jax.nn.sigmoid(a) * _dot(h, wu_ref[0, e])
            c_e = jnp.sum(jnp.where(lane == first + e, rec, 0.0), axis=-1, keepdims=True)
            scaled.append((hid * c_e).astype(BF16))
        ys_ref[...] = _dot(jnp.concatenate(scaled, axis=-1), wd_ref[0]).astype(BF16)


def _moe_experts(tile_group, xs, wg4, wu4, wd4):
    p = xs.shape[0]
    grp_map = lambda nd: (lambda j, tg: (jnp.minimum(tg[j], N_GROUPS - 1),) + (0,) * (nd - 1))
    grid_spec = pltpu.PrefetchScalarGridSpec(
        num_scalar_prefetch=1, grid=(p // MOE_TX,),
        in_specs=[pl.BlockSpec((MOE_TX, MOE_W), lambda j, tg: (j, 0)),
                  pl.BlockSpec((1,) + wg4.shape[1:], grp_map(wg4.ndim)),
                  pl.BlockSpec((1,) + wu4.shape[1:], grp_map(wu4.ndim)),
                  pl.BlockSpec((1,) + wd4.shape[1:], grp_map(wd4.ndim))],
        out_specs=pl.BlockSpec((MOE_TX, D_MODEL), lambda j, tg: (j, 0)))
    return pl.pallas_call(
        _experts_kernel,
        grid_spec=grid_spec,
        out_shape=jax.ShapeDtypeStruct((p, D_MODEL), BF16),
        compiler_params=_cparams(("arbitrary",)),
        name="moe_experts",
    )(tile_group, xs, wg4, wu4, wd4)


def _combine_kernel(plan_ref, rec_ref, x1_ref, g2_ref, ys_ref, o_ref, buf, sem, *, tile0):
    lin = pl.program_id(0) * pl.num_programs(1) + pl.program_id(1)
    n = pl.num_programs(0) * pl.num_programs(1)
    slot = lin % 2

    def copy(slot_t):
        return lambda g, seg_row, stage_row: pltpu.make_async_copy(
            ys_ref.at[pl.ds(seg_row, MOE_SUB), :], buf.at[slot_t, pl.ds(stage_row, MOE_SUB), :], sem.at[slot_t, g])

    @pl.when(lin == 0)
    def _():
        buf[...] = jnp.zeros_like(buf)
        _for_each_granule(plan_ref, tile0, lambda *a: copy(0)(*a).start())

    @pl.when(lin + 1 < n)
    def _():
        _for_each_granule(plan_ref, tile0 + lin + 1, lambda *a: copy(1 - slot)(*a).start())

    perm = _tile_sort_matrix(rec_ref[0], plan_ref, tile0 + lin)
    _for_each_granule(plan_ref, tile0 + lin, lambda *a: copy(slot)(*a).wait())
    y = _dot_tn(perm, buf[slot])
    o_ref[0] = x1_ref[0] + g2_ref[0] * y


def _moe_combine(plan, hx3, x1, g2, ys, tile0):
    b, t, _ = x1.shape
    bm = g2.shape[0]
    mod_map = (lambda i, j, plan: (i, 0, 0)) if bm > 1 else (lambda i, j, plan: (0, 0, 0))
    rec_blk = D_MODEL // ROUTER_LANES
    grid_spec = pltpu.PrefetchScalarGridSpec(
        num_scalar_prefetch=1, grid=(b, t // MOE_TM),
        in_specs=[pl.BlockSpec((1, MOE_TM, ROUTER_LANES), lambda i, j, plan: (i, j, rec_blk)),
                  pl.BlockSpec((1, MOE_TM, D_MODEL), lambda i, j, plan: (i, j, 0)),
                  pl.BlockSpec((1, 1, D_MODEL), mod_map),
                  pl.BlockSpec(memory_space=pl.ANY)],
        out_specs=pl.BlockSpec((1, MOE_TM, D_MODEL), lambda i, j, plan: (i, j, 0)),
        scratch_shapes=[pltpu.VMEM((2, MOE_STAGE, D_MODEL), BF16), pltpu.SemaphoreType.DMA((2, N_GROUPS))])
    return pl.pallas_call(
        functools.partial(_combine_kernel, tile0=tile0),
        grid_spec=grid_spec,
        out_shape=jax.ShapeDtypeStruct(x1.shape, F32),
        compiler_params=_cparams(("arbitrary", "arbitrary")),
        name="moe_combine",
    )(plan, hx3, x1, g2, ys)


def _moe_plan(cnt, n_rows):
    seg = (cnt + (SEG_ALIGN - 1)) // SEG_ALIGN * SEG_ALIGN
    used = (jnp.sum(seg, axis=0) + (MOE_TX - 1)) // MOE_TX * MOE_TX
    size = used + MOE_TX
    base = jnp.cumsum(size) - size
    start = base[None, :] + jnp.cumsum(seg, axis=0) - seg
    n_gran = (cnt + (MOE_SUB - 1)) // MOE_SUB
    stage = n_gran * MOE_SUB
    local = jnp.cumsum(stage, axis=1) - stage
    starts = jnp.arange(n_rows // MOE_TX, dtype=jnp.int32)[:, None] * MOE_TX
    inside = (starts >= base[None, :]) & (starts < (base + used)[None, :])
    tile_group = jnp.where(jnp.any(inside, axis=1), jnp.argmax(inside, axis=1), N_GROUPS)
    plan = jnp.stack([start, n_gran, local], axis=-1).reshape(-1)
    return plan.astype(jnp.int32), tile_group.astype(jnp.int32)


def _moe_rows(n_tokens):
    n_tiles = n_tokens // MOE_TM
    bound = n_tokens + n_tiles * N_GROUPS * (SEG_ALIGN - 1) + N_GROUPS * 2 * MOE_TX
    return (bound + MOE_TX - 1) // MOE_TX * MOE_TX


def _layer(x3, mod6, lp, t_tiles, ctx):
    b, t, _ = x3.shape
    sh1, sc1, g1, sh2, sc2, _ = mod6
    t_tiles = {name: min(size, t) for name, size in t_tiles.items()}
    rope = ctx is not None
    n = b * t
    prep_consts = [lp[name] for name in ("mu_p", "wd_cat", "w0_cat", "wa_cat", "a0_cat", "wg_p", "k_k", "k_a", "r_k",
                                         "grp")]
    q5, k4, v4, k, v, r, lw2, kd2, vv, kk, bd2, gate, bonus = _in_proj_prep(
        x3, sh1, sc1, lp["norm1_g"], lp["w_in_p"], lp["gqk"], lp["grp"], prep_consts, rope, t_tiles["in_proj"])
    if ctx is not None:
        ctx_k, ctx_v, ctx_state = ctx
        ck = ctx_k.transpose(0, 2, 1, 3)
        cv = ctx_v.transpose(0, 2, 1, 3)
        k4 = jnp.concatenate([k4, ck.astype(BF16)], axis=2)
        v4 = jnp.concatenate([v4, jnp.concatenate([cv, jnp.ones_like(cv)], axis=-1).astype(BF16)], axis=2)
        s0 = ctx_state
    else:
        s0 = jnp.zeros((b, 2, N_RWKV_HEADS, HEAD_DIM, HEAD_DIM), F32)
    attn3 = _attention(q5, k4, v4, t_tiles["attn"])
    yf, yb, s_t = _rwkv_scan(r, lw2, kd2, vv, kk, bd2, s0, t_tiles["chunk"])
    fb, ft = (1, n) if g1.shape[0] == 1 else (b, t)
    flat = lambda a: a.reshape(fb, ft, a.shape[-1])
    x1, hx, cnt = _out_proj(flat(x3), flat(attn3), flat(yf), flat(yb), flat(bonus), flat(gate), lp["ln_g"],
                            lp["ln_b"], lp["grp"], lp["w_out_b"], g1, sh2, sc2, lp["norm2_g"], lp["w_r"], lp["b_r"])
    return (x1, hx, cnt), k.reshape(b, t, N_KV_HEADS, HEAD_DIM), v.reshape(b, t, N_KV_HEADS, HEAD_DIM), s_t


def _moe_both(passes, g2s, lp):
    counts = [p[2][:, :, 0, :N_GROUPS].reshape(-1, N_GROUPS) for p in passes]
    tiles = [c.shape[0] for c in counts]
    n_rows = _moe_rows(sum(tiles) * MOE_TM)
    plan, tile_group = _moe_plan(jnp.concatenate(counts, axis=0).astype(jnp.int32), n_rows)
    xs = jnp.zeros((n_rows, MOE_W), BF16)
    tile0 = 0
    for (x1, hx, _), nt in zip(passes, tiles):
        xs = _moe_dispatch(plan, hx.reshape(-1, MOE_W), xs, tile0)
        tile0 += nt
    ys = _moe_experts(tile_group, xs, lp["wg4"], lp["wu4"], lp["wd4"])
    outs, tile0 = [], 0
    for (x1, hx, _), g2, nt in zip(passes, g2s, tiles):
        outs.append(_moe_combine(plan, hx, x1, g2, ys, tile0))
        tile0 += nt
    return outs


def _block_diag2(w):
    z, l, c = w.shape
    out = jnp.zeros((LANES, z * c), F32)
    for i in range(z):
        out = out.at[i * l:(i + 1) * l, i * c:(i + 1) * c].set(w[i])
    return out


def _layer_params(l, w_in, norm1_g, norm2_g, mu_shift, q_norm_g, k_norm_g, w0, w_lora_up, a0, a_lora_up, g_lora_up,
                  k_k, k_a, r_k, ln_x_g, ln_x_b, w_out, router_c, router_c_b, router_f, router_f_b,
                  exp_gate, exp_up, exp_down):
    lane = np.arange(LANES)
    grp = jnp.asarray((lane[:, None] // HEAD_DIM) == (lane[None, :] // HEAD_DIM), BF16)
    pad_in = D_IN_PAD - w_in.shape[2]
    wd_cat = _block_diag2(w_lora_up[l])
    wa_cat = jnp.roll(_block_diag2(a_lora_up[l]), 2 * DECAY_LORA, axis=0)
    w_r = jnp.zeros((D_MODEL, ROUTER_LANES), F32)
    w_r = w_r.at[:, :N_GROUPS].set(router_c[l]).at[:, N_GROUPS:N_GROUPS + N_EXPERTS].set(router_f[l])
    b_r = jnp.zeros((1, ROUTER_LANES), F32)
    b_r = b_r.at[0, :N_GROUPS].set(router_c_b[l]).at[0, N_GROUPS:N_GROUPS + N_EXPERTS].set(router_f_b[l])

    by_group = lambda w: w.astype(BF16).reshape(N_GROUPS, EXPERTS_PER_GROUP, D_MODEL, D_EXPERT)

    return dict(
        grp=grp,
        norm1_g=norm1_g[l].reshape(1, D_MODEL), norm2_g=norm2_g[l].reshape(1, D_MODEL),
        w_in_p=jnp.pad(w_in[l], ((0, 0), (0, pad_in))).astype(BF16),
        gqk=jnp.concatenate([jnp.tile(q_norm_g[l], N_Q_HEADS), jnp.tile(k_norm_g[l], N_KV_HEADS)]).reshape(1, -1),
        mu_p=jnp.pad(mu_shift[l], ((0, 0), (0, D_RWKV_PAD - D_RWKV_IN))),
        wd_cat=wd_cat.astype(BF16), w0_cat=w0[l].reshape(1, 2 * D_RWKV),
        wa_cat=wa_cat.astype(BF16), a0_cat=a0[l].reshape(1, 2 * D_RWKV),
        wg_p=jnp.pad(g_lora_up[l], ((0, LANES - GATE_LORA), (0, 0))).astype(BF16),
        k_k=k_k[l].reshape(1, D_RWKV), k_a=k_a[l].reshape(1, D_RWKV), r_k=r_k[l].reshape(1, D_RWKV),
        ln_g=ln_x_g[l].reshape(1, D_RWKV), ln_b=ln_x_b[l].reshape(1, D_RWKV),
        w_out_b=w_out[l].astype(BF16), w_r=w_r, b_r=b_r,
        wg4=by_group(exp_gate[l]), wu4=by_group(exp_up[l]),
        wd4=exp_down[l].astype(BF16).reshape(N_GROUPS, EXPERTS_PER_GROUP * D_EXPERT, D_MODEL),
    )


CTX_TILES = dict(in_proj=256, attn=256, chunk=128)
SMP_TILES = dict(in_proj=256, attn=256, chunk=128)


def kernel(x_prompt, x_sample, cache_k, cache_v, state_rwkv, c, c_ctx, w_mod, b_mod, norm1_g, norm2_g, w_in, mu_shift, q_norm_g, k_norm_g, w0, w_lora_up, a0, a_lora_up, g_lora_up, k_k, k_a, r_k, ln_x_g, ln_x_b, w_out, router_c, router_c_b, router_f, router_f_b, exp_gate, exp_up, exp_down):
    depth = w_mod.shape[0]
    db = x_sample.shape[0]
    y_prompt, y_sample = x_prompt, x_sample
    ks, vs, ss = [], [], []
    cond = jnp.zeros((8, D_MODEL), F32).at[:db].set(c).at[db].set(c_ctx)
    for l in range(depth):
        lp = _layer_params(l, w_in, norm1_g, norm2_g, mu_shift, q_norm_g, k_norm_g, w0, w_lora_up, a0, a_lora_up,
                           g_lora_up, k_k, k_a, r_k, ln_x_g, ln_x_b, w_out, router_c, router_c_b, router_f,
                           router_f_b, exp_gate, exp_up, exp_down)
        mod = _modulation(cond, w_mod[l], b_mod[l])
        mod_s = [mod[:db, i * D_MODEL:(i + 1) * D_MODEL].reshape(db, 1, D_MODEL) for i in range(6)]
        mod_c = [mod[db:db + 1, i * D_MODEL:(i + 1) * D_MODEL].reshape(1, 1, D_MODEL) for i in range(6)]
        pre_c, k_l, v_l, s_l = _layer(y_prompt, mod_c, lp, CTX_TILES, None)
        ks.append(k_l)
        vs.append(v_l)
        ss.append(s_l)
        pre_s, _, _, _ = _layer(y_sample, mod_s, lp, SMP_TILES, (cache_k[:, l], cache_v[:, l], state_rwkv[:, l]))
        out_c, out_s = _moe_both([pre_c, pre_s], [mod_c[5], mod_s[5]], lp)
        y_prompt, y_sample = out_c.reshape(y_prompt.shape), out_s.reshape(y_sample.shape)
    return (y_prompt, y_sample, jnp.stack(ks, axis=1), jnp.stack(vs, axis=1), jnp.stack(ss, axis=1))
```

```python
import functools

import numpy as np
import jax
import jax.numpy as jnp
from jax import lax
from jax.experimental import pallas as pl
from jax.experimental.pallas import tpu as pltpu

F32 = jnp.float32
BF16 = jnp.bfloat16
HIGHEST = lax.Precision.HIGHEST

D_MODEL = 1024
HEAD_DIM = 64
N_Q_HEADS = 8
N_KV_HEADS = 2
GQA_GROUP = N_Q_HEADS // N_KV_HEADS
D_ATTN = N_Q_HEADS * HEAD_DIM
D_KV = N_KV_HEADS * HEAD_DIM
N_RWKV_HEADS = 8
D_RWKV = 512
DECAY_LORA = 32
AAA_LORA = 32
GATE_LORA = 96
D_RWKV_IN = 3 * D_RWKV + 2 * DECAY_LORA + 2 * AAA_LORA + GATE_LORA
D_RWKV_PAD = 1792
D_QKV = D_ATTN + 2 * D_KV
D_IN_PAD = D_QKV + D_RWKV_PAD
N_GROUPS = 4
EXPERTS_PER_GROUP = 4
N_EXPERTS = 16
D_EXPERT = 512
GRID_W = 64
ROPE_THETA = 10000.0
NORM_EPS = 1e-6
GN_EPS = 64e-5
DECAY_SCALE = 0.6065306597
QK_EXP2_SCALE = (HEAD_DIM ** -0.5) * float(np.log2(np.e))
LANES = 128
ROUTER_LANES = 128
VMEM_LIMIT = 56 * 1024 * 1024
ATTN_ROWS = 128
MOE_W = D_MODEL + 2 * ROUTER_LANES
MOE_TM = 256
OUT_PROJ_TM = 1024
MOE_SUB = 32
MOE_STAGE = MOE_TM + N_GROUPS * MOE_SUB
MOE_TX = 512
SEG_ALIGN = 16


def _cparams(sem):
    return pltpu.CompilerParams(dimension_semantics=sem, vmem_limit_bytes=VMEM_LIMIT)


def _dot(a, b, precision=None):
    return jnp.dot(a, b, preferred_element_type=F32, precision=precision)


def _dot_nt(a, b, precision=None):
    return lax.dot_general(a, b, (((1,), (1,)), ((), ())), preferred_element_type=F32, precision=precision)


def _dot_tn(a, b, precision=None):
    return lax.dot_general(a, b, (((0,), (0,)), ((), ())), preferred_element_type=F32, precision=precision)


def _split2(x):
    hi = x.astype(BF16)
    return hi, (x - hi.astype(F32)).astype(BF16)


def _dot3(a, b):
    a_hi, a_lo = _split2(a)
    b_hi, b_lo = _split2(b)
    return _dot(a_hi, b_hi) + (_dot(a_hi, b_lo) + _dot(a_lo, b_hi))


def _head_sum(x, g):
    xb = x.astype(BF16)
    n = x.shape[-1] // LANES
    return jnp.concatenate([_dot(xb[:, j * LANES:(j + 1) * LANES], g) for j in range(n)], axis=-1)


def _mod_kernel(c_ref, w_ref, b_ref, o_ref):
    c = c_ref[...]
    s = c * jax.nn.sigmoid(c)
    o_ref[...] = _dot(s, w_ref[...], HIGHEST) + b_ref[...]


def _modulation(cond, w_mod, b_mod):
    n = w_mod.shape[1]
    tn = 1024
    return pl.pallas_call(
        _mod_kernel,
        grid=(n // tn,),
        in_specs=[pl.BlockSpec((8, D_MODEL), lambda j: (0, 0)),
                  pl.BlockSpec((D_MODEL, tn), lambda j: (0, j)),
                  pl.BlockSpec((1, tn), lambda j: (0, j))],
        out_specs=pl.BlockSpec((8, tn), lambda j: (0, j)),
        out_shape=jax.ShapeDtypeStruct((8, n), F32),
        compiler_params=_cparams(("arbitrary",)),
        name="mod",
    )(cond, w_mod, b_mod.reshape(1, n))


def _rope_tables(t_len):
    half = HEAD_DIM // 2
    inv = ROPE_THETA ** (-np.arange(0, half, 2, dtype=np.float64) / half)
    t = np.arange(t_len)
    row, col = t // GRID_W, t % GRID_W
    lane = np.arange(LANES)
    i = lane % HEAD_DIM
    pos = np.where((i // half)[None, :] == 0, row[:, None], col[:, None]).astype(np.float64)
    j = i % half
    ang = pos * inv[j % (half // 2)][None, :]
    cos, sin = np.cos(ang), np.sin(ang)
    first = (j < half // 2)[None, :]
    s_up = np.where(first, -sin, 0.0)
    s_dn = np.where(first, 0.0, sin)
    return (jnp.asarray(cos, F32), jnp.asarray(s_up, F32), jnp.asarray(s_dn, F32))


def _project(x_ref, sh_ref, sc_ref, g_ref, w_ref, gqk_ref, grp_ref, rope_refs, q_ref, k_ref, v_ref, kf_ref, vf_ref):
    rope = rope_refs is not None
    if rope:
        cos_ref, sup_ref, sdn_ref = rope_refs
    x = x_ref[0]
    ms = jnp.mean(x * x, axis=-1, keepdims=True)
    h = x * lax.rsqrt(ms + NORM_EPS) * g_ref[...]
    h = h * (1.0 + sc_ref[0]) + sh_ref[0]
    proj = _dot(h.astype(BF16), w_ref[...])
    grp = grp_ref[...]
    lo_half = lax.broadcasted_iota(jnp.int32, (x.shape[0], LANES), 1) < HEAD_DIM
    for j in range((D_ATTN + D_KV) // LANES):
        blk = proj[:, j * LANES:(j + 1) * LANES]
        ss = _head_sum(blk * blk, grp) * (1.0 / HEAD_DIM)
        nb = blk * lax.rsqrt(ss + NORM_EPS) * gqk_ref[:, j * LANES:(j + 1) * LANES]
        if rope:
            nb = (nb * cos_ref[...] + pltpu.roll(nb, LANES - 16, 1) * sup_ref[...]
                  + pltpu.roll(nb, 16, 1) * sdn_ref[...])
        if j < D_ATTN // LANES:
            nbq = nb * QK_EXP2_SCALE
            for half in range(2):
                hq = 2 * j + half
                q_ref[0, hq // GQA_GROUP, hq % GQA_GROUP] = nbq[:, half * HEAD_DIM:(half + 1) * HEAD_DIM].astype(BF16)
        else:
            kf_ref[0] = nb
            k_ref[0, 0] = nb[:, :HEAD_DIM].astype(BF16)
            k_ref[0, 1] = nb[:, HEAD_DIM:].astype(BF16)
    vblk = proj[:, D_ATTN + D_KV:D_QKV]
    vf_ref[0] = vblk
    v_ref[0, 0] = jnp.where(lo_half, vblk, 1.0).astype(BF16)
    v_ref[0, 1] = jnp.where(lo_half, pltpu.roll(vblk, HEAD_DIM, 1), 1.0).astype(BF16)
    return proj[:, D_QKV:]


def _rwkv_features(cur, prev_row, next_row, mu_ref, wd_ref, w0_ref, wa_ref, a0_ref, wg_ref, kk_ref, ka_ref, rk_ref,
                   grp_ref, r_o, lw_o, kd_o, v_o, kk_o, bd_o, g_o, bonus_o):
    tt = cur.shape[0]
    rid = lax.broadcasted_iota(jnp.int32, cur.shape, 0)
    prev = jnp.where(rid == 0, prev_row, pltpu.roll(cur, 1, 0))
    nxt = jnp.where(rid == tt - 1, next_row, pltpu.roll(cur, tt - 1, 0))
    p = cur + mu_ref[0:1, :] * (prev - cur) + mu_ref[1:2, :] * (nxt - cur)
    r = p[:, 0:D_RWKV]
    k = p[:, D_RWKV:2 * D_RWKV]
    v = p[:, 2 * D_RWKV:3 * D_RWKV]
    lo = p[:, 3 * D_RWKV:3 * D_RWKV + LANES]
    gd = p[:, 3 * D_RWKV + LANES:]
    grp = grp_ref[...]
    wlog = _dot(jnp.tanh(lo).astype(BF16), wd_ref[...]) + w0_ref[...]
    alog = _dot(lo.astype(BF16), wa_ref[...]) + a0_ref[...]
    g_o[0] = _dot(jax.nn.sigmoid(gd).astype(BF16), wg_ref[...]).astype(BF16)
    kx = k * kk_ref[...]
    kk = kx * lax.rsqrt(_head_sum(kx * kx, grp) + 1e-12)
    r_o[0] = r
    v_o[0] = v
    kk_o[0] = kk
    bonus_o[0] = (_head_sum(r * k * rk_ref[...], grp) * v).astype(BF16)
    for z in range(2):
        a = jax.nn.sigmoid(alog[:, z * D_RWKV:(z + 1) * D_RWKV])
        lw_o[z, 0] = -DECAY_SCALE * jax.nn.sigmoid(wlog[:, z * D_RWKV:(z + 1) * D_RWKV])
        kd_o[z, 0] = k * (1.0 + (a - 1.0) * ka_ref[...])
        bd_o[z, 0] = kk * a


N_PROJ_IN = 7
N_PREP_CONST = 10
N_PROJ_OUT = 5
N_PREP_OUT = 8


def _inproj_prep_kernel(*refs, rope, nt):
    n_rope = 3 if rope else 0
    proj_in = refs[:N_PROJ_IN]
    rope_refs = refs[N_PROJ_IN:N_PROJ_IN + n_rope] if rope else None
    at = N_PROJ_IN + n_rope
    consts = refs[at:at + N_PREP_CONST]
    proj_out = refs[at + N_PREP_CONST:at + N_PREP_CONST + N_PROJ_OUT]
    prep_out = refs[at + N_PREP_CONST + N_PROJ_OUT:at + N_PREP_CONST + N_PROJ_OUT + N_PREP_OUT]
    slab_scr, tail_scr = refs[-2:]
    j = pl.program_id(1)
    zero_row = jnp.zeros((1, D_RWKV_PAD), F32)

    def project():
        return _project(*proj_in, rope_refs, *proj_out)

    def features(next_row, first_tile):
        cur = slab_scr[...]
        prev_row = zero_row if first_tile else tail_scr[7:8, :]
        _rwkv_features(cur, prev_row, next_row, *consts, *prep_out)
        tail_scr[...] = cur[cur.shape[0] - 8:, :]

    @pl.when(j == 0)
    def _():
        slab_scr[...] = project()

    if nt > 1:
        @pl.when(j == 1)
        def _():
            slab = project()
            features(slab[0:1, :], True)
            slab_scr[...] = slab

    if nt > 2:
        @pl.when((j > 1) & (j < nt))
        def _():
            slab = project()
            features(slab[0:1, :], False)
            slab_scr[...] = slab

    @pl.when(j == nt)
    def _():
        features(zero_row, nt == 1)


def _in_proj_prep(x3, shift, scale, norm_g, w_in_p, gqk, grp, prep_consts, rope, tm):
    b, t, _ = x3.shape
    nt = t // tm
    bm = shift.shape[0]
    mod_map = (lambda i, j: (i, 0, 0)) if bm > 1 else (lambda i, j: (0, 0, 0))
    full = lambda a: pl.BlockSpec(a.shape, lambda i, j: (0,) * a.ndim)
    here = lambda j: jnp.minimum(j, nt - 1)
    back = lambda j: jnp.maximum(j - 1, 0)
    in_specs = [pl.BlockSpec((1, tm, D_MODEL), lambda i, j: (i, here(j), 0)),
                pl.BlockSpec((1, 1, D_MODEL), mod_map), pl.BlockSpec((1, 1, D_MODEL), mod_map),
                full(norm_g), full(w_in_p), full(gqk), full(grp)]
    args = [x3, shift, scale, norm_g, w_in_p, gqk, grp]
    if rope:
        in_specs += [pl.BlockSpec((tm, LANES), lambda i, j: (here(j), 0))] * 3
        args += list(_rope_tables(t))
    in_specs += [full(a) for a in prep_consts]
    args += list(prep_consts)
    one = jax.ShapeDtypeStruct((b, t, D_RWKV), F32)
    two = jax.ShapeDtypeStruct((2, b, t, D_RWKV), F32)
    narrow = jax.ShapeDtypeStruct((b, t, D_RWKV), BF16)
    s_one = pl.BlockSpec((1, tm, D_RWKV), lambda i, j: (i, back(j), 0))
    s_two = pl.BlockSpec((2, 1, tm, D_RWKV), lambda i, j: (0, i, back(j), 0))
    out_shape = (jax.ShapeDtypeStruct((b, N_KV_HEADS, GQA_GROUP, t, HEAD_DIM), BF16),
                 jax.ShapeDtypeStruct((b, N_KV_HEADS, t, HEAD_DIM), BF16),
                 jax.ShapeDtypeStruct((b, N_KV_HEADS, t, 2 * HEAD_DIM), BF16),
                 jax.ShapeDtypeStruct((b, t, D_KV), F32), jax.ShapeDtypeStruct((b, t, D_KV), F32),
                 one, two, two, one, one, two, narrow, narrow)
    out_specs = (pl.BlockSpec((1, N_KV_HEADS, GQA_GROUP, tm, HEAD_DIM), lambda i, j: (i, 0, 0, here(j), 0)),
                 pl.BlockSpec((1, N_KV_HEADS, tm, HEAD_DIM), lambda i, j: (i, 0, here(j), 0)),
                 pl.BlockSpec((1, N_KV_HEADS, tm, 2 * HEAD_DIM), lambda i, j: (i, 0, here(j), 0)),
                 pl.BlockSpec((1, tm, D_KV), lambda i, j: (i, here(j), 0)),
                 pl.BlockSpec((1, tm, D_KV), lambda i, j: (i, here(j), 0)),
                 s_one, s_two, s_two, s_one, s_one, s_two, s_one, s_one)
    return pl.pallas_call(
        functools.partial(_inproj_prep_kernel, rope=rope, nt=nt),
        grid=(b, nt + 1), in_specs=in_specs, out_specs=out_specs, out_shape=out_shape,
        scratch_shapes=[pltpu.VMEM((tm, D_RWKV_PAD), F32), pltpu.VMEM((8, D_RWKV_PAD), F32)],
        compiler_params=_cparams(("parallel", "arbitrary")),
        name="in_proj_rope" if rope else "in_proj",
    )(*args)


def _attn_kernel(q_ref, k_ref, v_ref, o_ref):
    g, tq, hd = q_ref.shape[2:]
    sub = min(ATTN_ROWS, tq)
    k = k_ref[0, 0]
    v = v_ref[0, 0]
    slabs = [slice(i * sub, (i + 1) * sub) for i in range(tq // sub)]
    qs = [q_ref[0, 0, :, sl, :].reshape(g * sub, hd) for sl in slabs]
    ss = [_dot_nt(q, k) for q in qs]
    ps = [jnp.exp2(s - jnp.max(s, axis=-1, keepdims=True)).astype(BF16) for s in ss]
    for sl, p in zip(slabs, ps):
        o = _dot(p, v)
        o = o[:, :hd] / pltpu.roll(o, hd, 1)[:, :hd]
        o_ref[0, sl, :] = jnp.concatenate([o[i * sub:(i + 1) * sub] for i in range(g)], axis=-1).astype(BF16)


def _attention(q5, k4, v4, tq):
    b, hk, g, t, hd = q5.shape
    tk = k4.shape[2]
    return pl.pallas_call(
        _attn_kernel,
        grid=(b, hk, t // tq),
        in_specs=[pl.BlockSpec((1, 1, g, tq, hd), lambda i, j, l: (i, j, 0, l, 0)),
                  pl.BlockSpec((1, 1, tk, hd), lambda i, j, l: (i, j, 0, 0)),
                  pl.BlockSpec((1, 1, tk, 2 * hd), lambda i, j, l: (i, j, 0, 0))],
        out_specs=pl.BlockSpec((1, tq, g * hd), lambda i, j, l: (i, l, j)),
        out_shape=jax.ShapeDtypeStruct((b, t, hk * g * hd), BF16),
        compiler_params=_cparams(("parallel", "parallel", "arbitrary")),
        name="attn",
    )(q5, k4, v4)


INV_BASE = 16


def _bdot(a, b):
    return _dot(a.astype(BF16), b.astype(BF16))


def _scan_kernel(*refs, chunk):
    ins, (s0_ref, yf_ref, yb_ref, sT_ref, s_scr) = (refs[0:6], refs[6:12]), refs[12:]
    y_refs = (yf_ref, yb_ref)
    c = chunk

    @pl.when(pl.program_id(1) == 0)
    def _():
        s_scr[...] = s0_ref[0]

    row = lax.broadcasted_iota(jnp.int32, (c, c), 0)
    col = lax.broadcasted_iota(jnp.int32, (c, c), 1)
    eye = (row == col).astype(F32)
    same_blk = {}
    n = INV_BASE
    while n <= c:
        sh = jnp.int32(n.bit_length() - 1)
        same_blk[n] = lax.shift_right_logical(row, sh) == lax.shift_right_logical(col, sh)
        n *= 2
    heads = range(N_RWKV_HEADS)
    sls = [slice(h * HEAD_DIM, (h + 1) * HEAD_DIM) for h in heads]

    incl, strict, ar_abs, ar_mid, kb_inv, kb_end, v, p_tot = [], [], [], [], [], [], [], []
    for d, (r_ref, lw_ref, k_ref, v_ref, a_ref, b_ref) in enumerate(ins):
        incl.append(row >= col if d == 0 else row <= col)
        strict.append(row > col if d == 0 else row < col)
        lw = lw_ref[0, 0]
        lw_hi = lw.astype(BF16)
        lw_mid, lw_lo = _split2(lw - lw_hi.astype(F32))
        inclb = jnp.where(incl[d], 1.0, 0.0).astype(BF16)
        cl = _dot(inclb, lw_hi) + (_dot(inclb, lw_mid) + _dot(inclb, lw_lo))
        tot = jnp.sum(lw, axis=0, keepdims=True)
        mid = 0.5 * tot
        e_inv = jnp.exp(mid - cl)
        e_end = jnp.exp(tot - cl)
        s_mid = jnp.exp(-mid)
        r_abs = r_ref[0] * jnp.exp(cl)
        a_abs = a_ref[0] * jnp.exp(cl - lw)
        ar_abs.append(jnp.concatenate([a_abs, r_abs], axis=0).astype(BF16))
        ar_mid.append(jnp.concatenate([a_abs * s_mid, r_abs * s_mid], axis=0).astype(BF16))
        kb_inv.append(jnp.concatenate([k_ref[0, 0] * e_inv, b_ref[0, 0] * e_inv], axis=0).astype(BF16))
        kb_end.append(jnp.concatenate([k_ref[0, 0] * e_end, b_ref[0, 0] * e_end], axis=0).astype(BF16))
        v.append(v_ref[0].astype(BF16))
        p_tot.append(jnp.exp(tot))

    chains = [(d, h) for d in range(2) for h in heads]
    ids = range(len(chains))
    s_old = [s_scr[d, h] for d, h in chains]
    vhs = [v[d][:, sls[h]] for d, h in chains]
    grams = [_dot_nt(ar_mid[d][:, sls[h]], kb_inv[d][:, sls[h]]) for d, h in chains]
    from_s = [_dot_nt(ar_abs[d][:, sls[h]], s_old[i].astype(BF16)) for i, (d, h) in enumerate(chains)]
    masked = [jnp.concatenate([jnp.where(strict[d], grams[i][:c, :c], 0.0),
                               jnp.where(incl[d], grams[i][c:, :c], 0.0)], axis=0) for i, (d, h) in enumerate(chains)]
    from_v = [_bdot(masked[i], vhs[i]) for i in ids]
    lmats = [jnp.where(strict[d], grams[i][:c, c:], 0.0) for i, (d, h) in enumerate(chains)]
    l0s = [jnp.where(same_blk[INV_BASE], lm, 0.0) for lm in lmats]
    xs = [eye - l0 for l0 in l0s]
    pws = [_bdot(l0, l0) for l0 in l0s]
    span = 2
    while 2 * span < INV_BASE:
        both = [_bdot(jnp.concatenate([xs[i], pws[i]], axis=0), pws[i]) for i in ids]
        xs = [xs[i] + both[i][:c] for i in ids]
        pws = [both[i][c:] for i in ids]
        span *= 2
    xs = [xs[i] + _bdot(xs[i], pws[i]) for i in ids]
    n = INV_BASE
    while n < c:
        pair = same_blk[2 * n] & jnp.logical_not(same_blk[n])
        ts = [_bdot(jnp.where(pair, lmats[i], 0.0), xs[i]) for i in ids]
        xs = [xs[i] - _bdot(xs[i], ts[i]) for i in ids]
        n *= 2
    us = [_bdot(xs[i], from_s[i][:c] + from_v[i][:c]) for i in ids]
    yu = [_bdot(jnp.where(incl[d], grams[i][c:, c:], 0.0), us[i]) for i, (d, h) in enumerate(chains)]
    ds = [_dot_tn(jnp.concatenate([vhs[i], (-us[i]).astype(BF16)], axis=0), kb_end[d][:, sls[h]])
          for i, (d, h) in enumerate(chains)]
    for d in range(2):
        y_refs[d][0] = jnp.concatenate([from_s[i][c:] + from_v[i][c:] - yu[i]
                                        for i, (dd, h) in enumerate(chains) if dd == d], axis=-1)
    for i, (d, h) in enumerate(chains):
        s_scr[d, h] = s_old[i] * p_tot[d][:, sls[h]] + ds[i]

    @pl.when(pl.program_id(1) == pl.num_programs(1) - 1)
    def _():
        sT_ref[0] = s_scr[...]


def _rwkv_scan(r, lw2, kd2, v, kk, bd2, s0, chunk):
    b, t, _ = r.shape
    nc = t // chunk
    in_specs, args = [], []
    for d in range(2):
        tmap = (lambda j: j) if d == 0 else (lambda j: nc - 1 - j)
        s_one = pl.BlockSpec((1, chunk, D_RWKV), lambda i, j, tmap=tmap: (i, tmap(j), 0))
        s_two = pl.BlockSpec((1, 1, chunk, D_RWKV), lambda i, j, tmap=tmap, d=d: (d, i, tmap(j), 0))
        in_specs += [s_one, s_two, s_two, s_one, s_one, s_two]
        args += [r, lw2, kd2, v, kk, bd2]
    s_st = pl.BlockSpec((1, 2, N_RWKV_HEADS, HEAD_DIM, HEAD_DIM), lambda i, j: (i, 0, 0, 0, 0))
    y_specs = tuple(pl.BlockSpec((1, chunk, D_RWKV), lambda i, j, tmap=tmap: (i, tmap(j), 0))
                    for tmap in ((lambda j: j), (lambda j: nc - 1 - j)))
    y_shape = jax.ShapeDtypeStruct((b, t, D_RWKV), F32)
    return pl.pallas_call(
        functools.partial(_scan_kernel, chunk=chunk),
        grid=(b, nc),
        in_specs=in_specs + [s_st],
        out_specs=y_specs + (s_st,),
        out_shape=(y_shape, y_shape, jax.ShapeDtypeStruct(s0.shape, F32)),
        scratch_shapes=[pltpu.VMEM((2, N_RWKV_HEADS, HEAD_DIM, HEAD_DIM), F32)],
        compiler_params=_cparams(("parallel", "arbitrary")),
        name="rwkv_scan",
    )(*args, s0)


def _outproj_kernel(x_ref, at_ref, yf_ref, yb_ref, bonus_ref, gate_ref, lng_ref, lnb_ref, grp_ref, wo_ref,
                    g1_ref, sh_ref, sc_ref, n2_ref, wr_ref, br_ref, x1_ref, hx_ref, cnt_ref):
    grp = grp_ref[...]
    y = yf_ref[0] + yb_ref[0]
    mean = _head_sum(y, grp) * (1.0 / HEAD_DIM)
    yc = y - mean
    var = _head_sum(yc * yc, grp) * (1.0 / HEAD_DIM)
    yn = yc * lax.rsqrt(var + GN_EPS) * lng_ref[...] + lnb_ref[...]
    rw_out = (yn + bonus_ref[0]) * gate_ref[0]
    mix = (_dot(at_ref[0], wo_ref[0:D_ATTN, :])
           + _dot(rw_out.astype(BF16), wo_ref[D_ATTN:, :]))
    x1 = x_ref[0] + g1_ref[0] * mix
    x1_ref[0] = x1
    ms = jnp.mean(x1 * x1, axis=-1, keepdims=True)
    h2 = x1 * lax.rsqrt(ms + NORM_EPS) * n2_ref[...]
    h2 = h2 * (1.0 + sc_ref[0]) + sh_ref[0]
    hx_ref[0, :, 0:D_MODEL] = h2.astype(BF16)
    logits = _dot3(h2, wr_ref[...]) + br_ref[...]
    lane = lax.broadcasted_iota(jnp.int32, logits.shape, 1)
    neg = -jnp.inf
    big = jnp.int32(1 << 20)
    lc = jnp.where(lane < N_GROUPS, logits, neg)
    mc = jnp.max(lc, axis=-1, keepdims=True)
    g_w = 1.0 / jnp.sum(jnp.exp(lc - mc), axis=-1, keepdims=True)
    g_idx = jnp.min(jnp.where(lc == mc, lane, big), axis=-1, keepdims=True)
    eid = lane - N_GROUPS
    in_grp = (eid >= 0) & (eid < N_EXPERTS) & (lax.shift_right_arithmetic(eid, 2) == g_idx)
    lf = jnp.where(in_grp, logits, neg)
    m1 = jnp.max(lf, axis=-1, keepdims=True)
    i1 = jnp.min(jnp.where(lf == m1, lane, big), axis=-1, keepdims=True)
    lf2 = jnp.where(lane == i1, neg, lf)
    m2 = jnp.max(lf2, axis=-1, keepdims=True)
    i2 = jnp.min(jnp.where(lf2 == m2, lane, big), axis=-1, keepdims=True)
    e2 = jnp.exp(m2 - m1)
    w1 = 1.0 / (1.0 + e2)
    w2 = e2 * w1
    cmb = g_w * (jnp.where(lane == i1, w1, 0.0) + jnp.where(lane == i2, w2, 0.0))
    rec = jnp.where(lane == 0, g_idx.astype(F32), cmb)
    rec_hi, rec_lo = _split2(rec)
    hx_ref[0, :, D_MODEL:D_MODEL + ROUTER_LANES] = rec_hi
    hx_ref[0, :, D_MODEL + ROUTER_LANES:] = rec_lo
    hot = jnp.where((lane == g_idx) & (lane < N_GROUPS), 1.0, 0.0)
    for s in range(cnt_ref.shape[1]):
        part = jnp.sum(hot[s * MOE_TM:(s + 1) * MOE_TM], axis=0, keepdims=True)
        cnt_ref[0, s] = jnp.broadcast_to(part, cnt_ref.shape[2:])


def _out_proj(x3, attn3, yf, yb, bonus, gate, ln_g, ln_b, grp, w_out_b, g1, sh2, sc2, norm2_g, w_r, b_r):
    b, t, _ = x3.shape
    tm = OUT_PROJ_TM if t % OUT_PROJ_TM == 0 else MOE_TM
    bm = g1.shape[0]
    mod_map = (lambda i, j: (i, 0, 0)) if bm > 1 else (lambda i, j: (0, 0, 0))
    tok = lambda w: pl.BlockSpec((1, tm, w), lambda i, j: (i, j, 0))
    full = lambda a: pl.BlockSpec(a.shape, lambda i, j: (0,) * a.ndim)
    mod = pl.BlockSpec((1, 1, D_MODEL), mod_map)
    return pl.pallas_call(
        _outproj_kernel,
        grid=(b, t // tm),
        in_specs=[tok(D_MODEL), tok(D_ATTN), tok(D_RWKV), tok(D_RWKV),
                  tok(D_RWKV), tok(D_RWKV), full(ln_g), full(ln_b), full(grp), full(w_out_b),
                  mod, mod, mod, full(norm2_g), full(w_r), full(b_r)],
        out_specs=(tok(D_MODEL), tok(MOE_W), pl.BlockSpec((1, tm // MOE_TM, 8, LANES), lambda i, j: (i, j, 0, 0))),
        out_shape=(jax.ShapeDtypeStruct((b, t, D_MODEL), F32), jax.ShapeDtypeStruct((b, t, MOE_W), BF16),
                   jax.ShapeDtypeStruct((b, t // MOE_TM, 8, LANES), F32)),
        compiler_params=_cparams(("parallel", "parallel")),
        name="out_proj",
    )(x3, attn3, yf, yb, bonus, gate, ln_g, ln_b, grp, w_out_b, g1, sh2, sc2, norm2_g, w_r, b_r)


def _tile_sort_matrix(rec_hi, plan_ref, tile):
    tm = rec_hi.shape[0]
    sel = ((lax.broadcasted_iota(jnp.int32, (8, LANES), 0) == 0)
           & (lax.broadcasted_iota(jnp.int32, (8, LANES), 1) == 0))
    g_row = _dot_nt(jnp.where(sel, 1.0, 0.0).astype(BF16), rec_hi)[0:1, :]
    g_col = rec_hi.astype(F32)[:, 0:1]
    r = lax.broadcasted_iota(jnp.int32, (tm, tm), 0)
    c = lax.broadcasted_iota(jnp.int32, (tm, tm), 1)
    earlier_same = jnp.where((g_col == g_row) & (r < c), 1.0, 0.0)
    pos_row = jnp.sum(earlier_same, axis=0, keepdims=True)
    for g in range(N_GROUPS):
        local = _plan_entry(plan_ref, tile, g)[2].astype(F32)
        pos_row = pos_row + jnp.where(g_row == float(g), local, 0.0)
    stage_row = lax.broadcasted_iota(jnp.int32, (MOE_STAGE, tm), 0).astype(F32)
    return jnp.where(stage_row == pos_row, 1.0, 0.0).astype(BF16)


def _plan_entry(plan_ref, tile, g):
    at = (tile * N_GROUPS + g) * 3
    return pl.multiple_of(plan_ref[at], SEG_ALIGN), plan_ref[at + 1], pl.multiple_of(plan_ref[at + 2], MOE_SUB)


def _for_each_granule(plan_ref, tile, fn):
    for g in range(N_GROUPS):
        start, n_gran, local = _plan_entry(plan_ref, tile, g)

        def body(s, carry, g=g, start=start, local=local):
            fn(g, pl.multiple_of(start + s * MOE_SUB, SEG_ALIGN), pl.multiple_of(local + s * MOE_SUB, MOE_SUB))
            return carry

        lax.fori_loop(0, n_gran, body, 0)


def _dispatch_kernel(plan_ref, hx_ref, xs_in_ref, xs_ref, buf, sem, *, tile0):
    del xs_in_ref
    i = pl.program_id(0)
    x = hx_ref[...]
    perm = _tile_sort_matrix(x[:, D_MODEL:D_MODEL + ROUTER_LANES], plan_ref, tile0 + i)
    staged = _dot(perm, x).astype(BF16)

    def copy(g, seg_row, stage_row):
        return pltpu.make_async_copy(buf.at[pl.ds(stage_row, MOE_SUB), :],
                                     xs_ref.at[pl.ds(seg_row, MOE_SUB), :], sem.at[g])

    @pl.when(i > 0)
    def _():
        _for_each_granule(plan_ref, tile0 + i - 1, lambda *a: copy(*a).wait())

    buf[...] = staged
    _for_each_granule(plan_ref, tile0 + i, lambda *a: copy(*a).start())

    @pl.when(i == pl.num_programs(0) - 1)
    def _():
        _for_each_granule(plan_ref, tile0 + i, lambda *a: copy(*a).wait())


def _moe_dispatch(plan, hx2, xs, tile0):
    n = hx2.shape[0]
    grid_spec = pltpu.PrefetchScalarGridSpec(
        num_scalar_prefetch=1, grid=(n // MOE_TM,),
        in_specs=[pl.BlockSpec((MOE_TM, MOE_W), lambda i, plan: (i, 0)), pl.BlockSpec(memory_space=pl.ANY)],
        out_specs=pl.BlockSpec(memory_space=pl.ANY),
        scratch_shapes=[pltpu.VMEM((MOE_STAGE, MOE_W), BF16), pltpu.SemaphoreType.DMA((N_GROUPS,))])
    return pl.pallas_call(
        functools.partial(_dispatch_kernel, tile0=tile0),
        grid_spec=grid_spec,
        out_shape=jax.ShapeDtypeStruct(xs.shape, xs.dtype),
        input_output_aliases={2: 0},
        compiler_params=_cparams(("arbitrary",)),
        name="moe_dispatch",
    )(plan, hx2, xs)


def _experts_kernel(tg_ref, xs_ref, wg_ref, wu_ref, wd_ref, ys_ref):
    g = tg_ref[pl.program_id(0)]

    @pl.when(g >= N_GROUPS)
    def _():
        ys_ref[...] = jnp.zeros_like(ys_ref)

    @pl.when(g < N_GROUPS)
    def _():
        x = xs_ref[...]
        h = x[:, 0:D_MODEL]
        rec = (x[:, D_MODEL:D_MODEL + ROUTER_LANES].astype(F32) + x[:, D_MODEL + ROUTER_LANES:].astype(F32))
        lane = lax.broadcasted_iota(jnp.int32, rec.shape, 1)
        first = N_GROUPS + EXPERTS_PER_GROUP * g
        scaled = []
        for e in range(EXPERTS_PER_GROUP):
            a = _dot(h, wg_ref[0, e])
            hid = a * jax.nn.sigmoid(a) * _dot(h, wu_ref[0, e])
            c_e = jnp.sum(jnp.where(lane == first + e, rec, 0.0), axis=-1, keepdims=True)
            scaled.append((hid * c_e).astype(BF16))
        ys_ref[...] = _dot(jnp.concatenate(scaled, axis=-1), wd_ref[0]).astype(BF16)


def _moe_experts(tile_group, xs, wg4, wu4, wd4):
    p = xs.shape[0]
    grp_map = lambda nd: (lambda j, tg: (jnp.minimum(tg[j], N_GROUPS - 1),) + (0,) * (nd - 1))
    grid_spec = pltpu.PrefetchScalarGridSpec(
        num_scalar_prefetch=1, grid=(p // MOE_TX,),
        in_specs=[pl.BlockSpec((MOE_TX, MOE_W), lambda j, tg: (j, 0)),
                  pl.BlockSpec((1,) + wg4.shape[1:], grp_map(wg4.ndim)),
                  pl.BlockSpec((1,) + wu4.shape[1:], grp_map(wu4.ndim)),
                  pl.BlockSpec((1,) + wd4.shape[1:], grp_map(wd4.ndim))],
        out_specs=pl.BlockSpec((MOE_TX, D_MODEL), lambda j, tg: (j, 0)))
    return pl.pallas_call(
        _experts_kernel,
        grid_spec=grid_spec,
        out_shape=jax.ShapeDtypeStruct((p, D_MODEL), BF16),
        compiler_params=_cparams(("arbitrary",)),
        name="moe_experts",
    )(tile_group, xs, wg4, wu4, wd4)


def _combine_kernel(plan_ref, rec_ref, x1_ref, g2_ref, ys_ref, o_ref, buf, sem, *, tile0):
    lin = pl.program_id(0) * pl.num_programs(1) + pl.program_id(1)
    n = pl.num_programs(0) * pl.num_programs(1)
    slot = lin % 2

    def copy(slot_t):
        return lambda g, seg_row, stage_row: pltpu.make_async_copy(
            ys_ref.at[pl.ds(seg_row, MOE_SUB), :], buf.at[slot_t, pl.ds(stage_row, MOE_SUB), :], sem.at[slot_t, g])

    @pl.when(lin == 0)
    def _():
        buf[...] = jnp.zeros_like(buf)
        _for_each_granule(plan_ref, tile0, lambda *a: copy(0)(*a).start())

    @pl.when(lin + 1 < n)
    def _():
        _for_each_granule(plan_ref, tile0 + lin + 1, lambda *a: copy(1 - slot)(*a).start())

    perm = _tile_sort_matrix(rec_ref[0], plan_ref, tile0 + lin)
    _for_each_granule(plan_ref, tile0 + lin, lambda *a: copy(slot)(*a).wait())
    y = _dot_tn(perm, buf[slot])
    o_ref[0] = x1_ref[0] + g2_ref[0] * y


def _moe_combine(plan, hx3, x1, g2, ys, tile0):
    b, t, _ = x1.shape
    bm = g2.shape[0]
    mod_map = (lambda i, j, plan: (i, 0, 0)) if bm > 1 else (lambda i, j, plan: (0, 0, 0))
    rec_blk = D_MODEL // ROUTER_LANES
    grid_spec = pltpu.PrefetchScalarGridSpec(
        num_scalar_prefetch=1, grid=(b, t // MOE_TM),
        in_specs=[pl.BlockSpec((1, MOE_TM, ROUTER_LANES), lambda i, j, plan: (i, j, rec_blk)),
                  pl.BlockSpec((1, MOE_TM, D_MODEL), lambda i, j, plan: (i, j, 0)),
                  pl.BlockSpec((1, 1, D_MODEL), mod_map),
                  pl.BlockSpec(memory_space=pl.ANY)],
        out_specs=pl.BlockSpec((1, MOE_TM, D_MODEL), lambda i, j, plan: (i, j, 0)),
        scratch_shapes=[pltpu.VMEM((2, MOE_STAGE, D_MODEL), BF16), pltpu.SemaphoreType.DMA((2, N_GROUPS))])
    return pl.pallas_call(
        functools.partial(_combine_kernel, tile0=tile0),
        grid_spec=grid_spec,
        out_shape=jax.ShapeDtypeStruct(x1.shape, F32),
        compiler_params=_cparams(("arbitrary", "arbitrary")),
        name="moe_combine",
    )(plan, hx3, x1, g2, ys)


def _moe_plan(cnt, n_rows):
    seg = (cnt + (SEG_ALIGN - 1)) // SEG_ALIGN * SEG_ALIGN
    used = (jnp.sum(seg, axis=0) + (MOE_TX - 1)) // MOE_TX * MOE_TX
    size = used + MOE_TX
    base = jnp.cumsum(size) - size
    start = base[None, :] + jnp.cumsum(seg, axis=0) - seg
    n_gran = (cnt + (MOE_SUB - 1)) // MOE_SUB
    stage = n_gran * MOE_SUB
    local = jnp.cumsum(stage, axis=1) - stage
    starts = jnp.arange(n_rows // MOE_TX, dtype=jnp.int32)[:, None] * MOE_TX
    inside = (starts >= base[None, :]) & (starts < (base + used)[None, :])
    tile_group = jnp.where(jnp.any(inside, axis=1), jnp.argmax(inside, axis=1), N_GROUPS)
    plan = jnp.stack([start, n_gran, local], axis=-1).reshape(-1)
    return plan.astype(jnp.int32), tile_group.astype(jnp.int32)


def _moe_rows(n_tokens):
    n_tiles = n_tokens // MOE_TM
    bound = n_tokens + n_tiles * N_GROUPS * (SEG_ALIGN - 1) + N_GROUPS * 2 * MOE_TX
    return (bound + MOE_TX - 1) // MOE_TX * MOE_TX


def _layer(x3, mod6, lp, t_tiles, ctx):
    b, t, _ = x3.shape
    sh1, sc1, g1, sh2, sc2, _ = mod6
    t_tiles = {name: min(size, t) for name, size in t_tiles.items()}
    rope = ctx is not None
    n = b * t
    prep_consts = [lp[name] for name in ("mu_p", "wd_cat", "w0_cat", "wa_cat", "a0_cat", "wg_p", "k_k", "k_a", "r_k",
                                         "grp")]
    q5, k4, v4, k, v, r, lw2, kd2, vv, kk, bd2, gate, bonus = _in_proj_prep(
        x3, sh1, sc1, lp["norm1_g"], lp["w_in_p"], lp["gqk"], lp["grp"], prep_consts, rope, t_tiles["in_proj"])
    if ctx is not None:
        ctx_k, ctx_v, ctx_state = ctx
        ck = ctx_k.transpose(0, 2, 1, 3)
        cv = ctx_v.transpose(0, 2, 1, 3)
        k4 = jnp.concatenate([k4, ck.astype(BF16)], axis=2)
        v4 = jnp.concatenate([v4, jnp.concatenate([cv, jnp.ones_like(cv)], axis=-1).astype(BF16)], axis=2)
        s0 = ctx_state
    else:
        s0 = jnp.zeros((b, 2, N_RWKV_HEADS, HEAD_DIM, HEAD_DIM), F32)
    attn3 = _attention(q5, k4, v4, t_tiles["attn"])
    yf, yb, s_t = _rwkv_scan(r, lw2, kd2, vv, kk, bd2, s0, t_tiles["chunk"])
    fb, ft = (1, n) if g1.shape[0] == 1 else (b, t)
    flat = lambda a: a.reshape(fb, ft, a.shape[-1])
    x1, hx, cnt = _out_proj(flat(x3), flat(attn3), flat(yf), flat(yb), flat(bonus), flat(gate), lp["ln_g"],
                            lp["ln_b"], lp["grp"], lp["w_out_b"], g1, sh2, sc2, lp["norm2_g"], lp["w_r"], lp["b_r"])
    return (x1, hx, cnt), k.reshape(b, t, N_KV_HEADS, HEAD_DIM), v.reshape(b, t, N_KV_HEADS, HEAD_DIM), s_t


def _moe_both(passes, g2s, lp):
    counts = [p[2][:, :, 0, :N_GROUPS].reshape(-1, N_GROUPS) for p in passes]
    tiles = [c.shape[0] for c in counts]
    n_rows = _moe_rows(sum(tiles) * MOE_TM)
    plan, tile_group = _moe_plan(jnp.concatenate(counts, axis=0).astype(jnp.int32), n_rows)
    xs = jnp.zeros((n_rows, MOE_W), BF16)
    tile0 = 0
    for (x1, hx, _), nt in zip(passes, tiles):
        xs = _moe_dispatch(plan, hx.reshape(-1, MOE_W), xs, tile0)
        tile0 += nt
    ys = _moe_experts(tile_group, xs, lp["wg4"], lp["wu4"], lp["wd4"])
    outs, tile0 = [], 0
    for (x1, hx, _), g2, nt in zip(passes, g2s, tiles):
        outs.append(_moe_combine(plan, hx, x1, g2, ys, tile0))
        tile0 += nt
    return outs


def _block_diag2(w):
    z, l, c = w.shape
    out = jnp.zeros((LANES, z * c), F32)
    for i in range(z):
        out = out.at[i * l:(i + 1) * l, i * c:(i + 1) * c].set(w[i])
    return out


def _layer_params(l, w_in, norm1_g, norm2_g, mu_shift, q_norm_g, k_norm_g, w0, w_lora_up, a0, a_lora_up, g_lora_up,
                  k_k, k_a, r_k, ln_x_g, ln_x_b, w_out, router_c, router_c_b, router_f, router_f_b,
                  exp_gate, exp_up, exp_down):
    lane = np.arange(LANES)
    grp = jnp.asarray((lane[:, None] // HEAD_DIM) == (lane[None, :] // HEAD_DIM), BF16)
    pad_in = D_IN_PAD - w_in.shape[2]
    wd_cat = _block_diag2(w_lora_up[l])
    wa_cat = jnp.roll(_block_diag2(a_lora_up[l]), 2 * DECAY_LORA, axis=0)
    w_r = jnp.zeros((D_MODEL, ROUTER_LANES), F32)
    w_r = w_r.at[:, :N_GROUPS].set(router_c[l]).at[:, N_GROUPS:N_GROUPS + N_EXPERTS].set(router_f[l])
    b_r = jnp.zeros((1, ROUTER_LANES), F32)
    b_r = b_r.at[0, :N_GROUPS].set(router_c_b[l]).at[0, N_GROUPS:N_GROUPS + N_EXPERTS].set(router_f_b[l])

    by_group = lambda w: w.astype(BF16).reshape(N_GROUPS, EXPERTS_PER_GROUP, D_MODEL, D_EXPERT)

    return dict(
        grp=grp,
        norm1_g=norm1_g[l].reshape(1, D_MODEL), norm2_g=norm2_g[l].reshape(1, D_MODEL),
        w_in_p=jnp.pad(w_in[l], ((0, 0), (0, pad_in))).astype(BF16),
        gqk=jnp.concatenate([jnp.tile(q_norm_g[l], N_Q_HEADS), jnp.tile(k_norm_g[l], N_KV_HEADS)]).reshape(1, -1),
        mu_p=jnp.pad(mu_shift[l], ((0, 0), (0, D_RWKV_PAD - D_RWKV_IN))),
        wd_cat=wd_cat.astype(BF16), w0_cat=w0[l].reshape(1, 2 * D_RWKV),
        wa_cat=wa_cat.astype(BF16), a0_cat=a0[l].reshape(1, 2 * D_RWKV),
        wg_p=jnp.pad(g_lora_up[l], ((0, LANES - GATE_LORA), (0, 0))).astype(BF16),
        k_k=k_k[l].reshape(1, D_RWKV), k_a=k_a[l].reshape(1, D_RWKV), r_k=r_k[l].reshape(1, D_RWKV),
        ln_g=ln_x_g[l].reshape(1, D_RWKV), ln_b=ln_x_b[l].reshape(1, D_RWKV),
        w_out_b=w_out[l].astype(BF16), w_r=w_r, b_r=b_r,
        wg4=by_group(exp_gate[l]), wu4=by_group(exp_up[l]),
        wd4=exp_down[l].astype(BF16).reshape(N_GROUPS, EXPERTS_PER_GROUP * D_EXPERT, D_MODEL),
    )


CTX_TILES = dict(in_proj=256, attn=256, chunk=128)
SMP_TILES = dict(in_proj=256, attn=256, chunk=128)


def kernel(x_prompt, x_sample, cache_k, cache_v, state_rwkv, c, c_ctx, w_mod, b_mod, norm1_g, norm2_g, w_in, mu_shift, q_norm_g, k_norm_g, w0, w_lora_up, a0, a_lora_up, g_lora_up, k_k, k_a, r_k, ln_x_g, ln_x_b, w_out, router_c, router_c_b, router_f, router_f_b, exp_gate, exp_up, exp_down):
    depth = w_mod.shape[0]
    db = x_sample.shape[0]
    y_prompt, y_sample = x_prompt, x_sample
    ks, vs, ss = [], [], []
    cond = jnp.zeros((8, D_MODEL), F32).at[:db].set(c).at[db].set(c_ctx)
    for l in range(depth):
        lp = _layer_params(l, w_in, norm1_g, norm2_g, mu_shift, q_norm_g, k_norm_g, w0, w_lora_up, a0, a_lora_up,
                           g_lora_up, k_k, k_a, r_k, ln_x_g, ln_x_b, w_out, router_c, router_c_b, router_f,
                           router_f_b, exp_gate, exp_up, exp_down)
        mod = _modulation(cond, w_mod[l], b_mod[l])
        mod_s = [mod[:db, i * D_MODEL:(i + 1) * D_MODEL].reshape(db, 1, D_MODEL) for i in range(6)]
        mod_c = [mod[db:db + 1, i * D_MODEL:(i + 1) * D_MODEL].reshape(1, 1, D_MODEL) for i in range(6)]
        pre_c, k_l, v_l, s_l = _layer(y_prompt, mod_c, lp, CTX_TILES, None)
        ks.append(k_l)
        vs.append(v_l)
        ss.append(s_l)
        pre_s, _, _, _ = _layer(y_sample, mod_s, lp, SMP_TILES, (cache_k[:, l], cache_v[:, l], state_rwkv[:, l]))
        out_c, out_s = _moe_both([pre_c, pre_s], [mod_c[5], mod_s[5]], lp)
        y_prompt, y_sample = out_c.reshape(y_prompt.shape), out_s.reshape(y_sample.shape)
    return (y_prompt, y_sample, jnp.stack(ks, axis=1), jnp.stack(vs, axis=1), jnp.stack(ss, axis=1))
```

```python
import functools

import numpy as np
import jax
import jax.numpy as jnp
from jax import lax
from jax.experimental import pallas as pl
from jax.experimental.pallas import tpu as pltpu

F32 = jnp.float32
BF16 = jnp.bfloat16
HIGHEST = lax.Precision.HIGHEST

D_MODEL = 1024
HEAD_DIM = 64
N_Q_HEADS = 8
N_KV_HEADS = 2
GQA_GROUP = N_Q_HEADS // N_KV_HEADS
D_ATTN = N_Q_HEADS * HEAD_DIM
D_KV = N_KV_HEADS * HEAD_DIM
N_RWKV_HEADS = 8
D_RWKV = 512
DECAY_LORA = 32
AAA_LORA = 32
GATE_LORA = 96
D_RWKV_IN = 3 * D_RWKV + 2 * DECAY_LORA + 2 * AAA_LORA + GATE_LORA
D_RWKV_PAD = 1792
D_QKV = D_ATTN + 2 * D_KV
D_IN_PAD = D_QKV + D_RWKV_PAD
N_GROUPS = 4
EXPERTS_PER_GROUP = 4
N_EXPERTS = 16
D_EXPERT = 512
GRID_W = 64
ROPE_THETA = 10000.0
NORM_EPS = 1e-6
GN_EPS = 64e-5
DECAY_SCALE = 0.6065306597
QK_EXP2_SCALE = (HEAD_DIM ** -0.5) * float(np.log2(np.e))
LANES = 128
ROUTER_LANES = 128
VMEM_LIMIT = 56 * 1024 * 1024
ATTN_ROWS = 128
MOE_W = D_MODEL + 2 * ROUTER_LANES
MOE_TM = 256
OUT_PROJ_TM = 1024
MOE_SUB = 32
MOE_STAGE = MOE_TM + N_GROUPS * MOE_SUB
MOE_TX = 512
SEG_ALIGN = 16


def _cparams(sem):
    return pltpu.CompilerParams(dimension_semantics=sem, vmem_limit_bytes=VMEM_LIMIT)


def _dot(a, b, precision=None):
    return jnp.dot(a, b, preferred_element_type=F32, precision=precision)


def _dot_nt(a, b, precision=None):
    return lax.dot_general(a, b, (((1,), (1,)), ((), ())), preferred_element_type=F32, precision=precision)


def _dot_tn(a, b, precision=None):
    return lax.dot_general(a, b, (((0,), (0,)), ((), ())), preferred_element_type=F32, precision=precision)


def _split2(x):
    hi = x.astype(BF16)
    return hi, (x - hi.astype(F32)).astype(BF16)


def _dot3(a, b):
    a_hi, a_lo = _split2(a)
    b_hi, b_lo = _split2(b)
    return _dot(a_hi, b_hi) + (_dot(a_hi, b_lo) + _dot(a_lo, b_hi))


def _head_sum(x, g):
    xb = x.astype(BF16)
    n = x.shape[-1] // LANES
    return jnp.concatenate([_dot(xb[:, j * LANES:(j + 1) * LANES], g) for j in range(n)], axis=-1)


def _mod_kernel(c_ref, w_ref, b_ref, o_ref):
    c = c_ref[...]
    s = c * jax.nn.sigmoid(c)
    o_ref[...] = _dot(s, w_ref[...], HIGHEST) + b_ref[...]


def _modulation(cond, w_mod, b_mod):
    n = w_mod.shape[1]
    tn = 1024
    return pl.pallas_call(
        _mod_kernel,
        grid=(n // tn,),
        in_specs=[pl.BlockSpec((8, D_MODEL), lambda j: (0, 0)),
                  pl.BlockSpec((D_MODEL, tn), lambda j: (0, j)),
                  pl.BlockSpec((1, tn), lambda j: (0, j))],
        out_specs=pl.BlockSpec((8, tn), lambda j: (0, j)),
        out_shape=jax.ShapeDtypeStruct((8, n), F32),
        compiler_params=_cparams(("arbitrary",)),
        name="mod",
    )(cond, w_mod, b_mod.reshape(1, n))


def _rope_tables(t_len):
    half = HEAD_DIM // 2
    inv = ROPE_THETA ** (-np.arange(0, half, 2, dtype=np.float64) / half)
    t = np.arange(t_len)
    row, col = t // GRID_W, t % GRID_W
    lane = np.arange(LANES)
    i = lane % HEAD_DIM
    pos = np.where((i // half)[None, :] == 0, row[:, None], col[:, None]).astype(np.float64)
    j = i % half
    ang = pos * inv[j % (half // 2)][None, :]
    cos, sin = np.cos(ang), np.sin(ang)
    first = (j < half // 2)[None, :]
    s_up = np.where(first, -sin, 0.0)
    s_dn = np.where(first, 0.0, sin)
    return (jnp.asarray(cos, F32), jnp.asarray(s_up, F32), jnp.asarray(s_dn, F32))


def _project(x_ref, sh_ref, sc_ref, g_ref, w_ref, gqk_ref, grp_ref, rope_refs, q_ref, k_ref, v_ref, kf_ref, vf_ref):
    rope = rope_refs is not None
    if rope:
        cos_ref, sup_ref, sdn_ref = rope_refs
    x = x_ref[0]
    ms = jnp.mean(x * x, axis=-1, keepdims=True)
    h = x * lax.rsqrt(ms + NORM_EPS) * g_ref[...]
    h = h * (1.0 + sc_ref[0]) + sh_ref[0]
    proj = _dot(h.astype(BF16), w_ref[...])
    grp = grp_ref[...]
    lo_half = lax.broadcasted_iota(jnp.int32, (x.shape[0], LANES), 1) < HEAD_DIM
    for j in range((D_ATTN + D_KV) // LANES):
        blk = proj[:, j * LANES:(j + 1) * LANES]
        ss = _head_sum(blk * blk, grp) * (1.0 / HEAD_DIM)
        nb = blk * lax.rsqrt(ss + NORM_EPS) * gqk_ref[:, j * LANES:(j + 1) * LANES]
        if rope:
            nb = (nb * cos_ref[...] + pltpu.roll(nb, LANES - 16, 1) * sup_ref[...]
                  + pltpu.roll(nb, 16, 1) * sdn_ref[...])
        if j < D_ATTN // LANES:
            nbq = nb * QK_EXP2_SCALE
            for half in range(2):
                hq = 2 * j + half
                q_ref[0, hq // GQA_GROUP, hq % GQA_GROUP] = nbq[:, half * HEAD_DIM:(half + 1) * HEAD_DIM].astype(BF16)
        else:
            kf_ref[0] = nb
            k_ref[0, 0] = nb[:, :HEAD_DIM].astype(BF16)
            k_ref[0, 1] = nb[:, HEAD_DIM:].astype(BF16)
    vblk = proj[:, D_ATTN + D_KV:D_QKV]
    vf_ref[0] = vblk
    v_ref[0, 0] = jnp.where(lo_half, vblk, 1.0).astype(BF16)
    v_ref[0, 1] = jnp.where(lo_half, pltpu.roll(vblk, HEAD_DIM, 1), 1.0).astype(BF16)
    return proj[:, D_QKV:]


def _rwkv_features(cur, prev_row, next_row, mu_ref, wd_ref, w0_ref, wa_ref, a0_ref, wg_ref, kk_ref, ka_ref, rk_ref,
                   grp_ref, r_o, lw_o, kd_o, v_o, kk_o, bd_o, g_o, bonus_o):
    tt = cur.shape[0]
    rid = lax.broadcasted_iota(jnp.int32, cur.shape, 0)
    prev = jnp.where(rid == 0, prev_row, pltpu.roll(cur, 1, 0))
    nxt = jnp.where(rid == tt - 1, next_row, pltpu.roll(cur, tt - 1, 0))
    p = cur + mu_ref[0:1, :] * (prev - cur) + mu_ref[1:2, :] * (nxt - cur)
    r = p[:, 0:D_RWKV]
    k = p[:, D_RWKV:2 * D_RWKV]
    v = p[:, 2 * D_RWKV:3 * D_RWKV]
    lo = p[:, 3 * D_RWKV:3 * D_RWKV + LANES]
    gd = p[:, 3 * D_RWKV + LANES:]
    grp = grp_ref[...]
    wlog = _dot(jnp.tanh(lo).astype(BF16), wd_ref[...]) + w0_ref[...]
    alog = _dot(lo.astype(BF16), wa_ref[...]) + a0_ref[...]
    g_o[0] = _dot(jax.nn.sigmoid(gd).astype(BF16), wg_ref[...]).astype(BF16)
    kx = k * kk_ref[...]
    kk = kx * lax.rsqrt(_head_sum(kx * kx, grp) + 1e-12)
    r_o[0] = r
    v_o[0] = v
    kk_o[0] = kk
    bonus_o[0] = (_head_sum(r * k * rk_ref[...], grp) * v).astype(BF16)
    for z in range(2):
        a = jax.nn.sigmoid(alog[:, z * D_RWKV:(z + 1) * D_RWKV])
        lw_o[z, 0] = -DECAY_SCALE * jax.nn.sigmoid(wlog[:, z * D_RWKV:(z + 1) * D_RWKV])
        kd_o[z, 0] = k * (1.0 + (a - 1.0) * ka_ref[...])
        bd_o[z, 0] = kk * a


N_PROJ_IN = 7
N_PREP_CONST = 10
N_PROJ_OUT = 5
N_PREP_OUT = 8


def _inproj_prep_kernel(*refs, rope, nt):
    n_rope = 3 if rope else 0
    proj_in = refs[:N_PROJ_IN]
    rope_refs = refs[N_PROJ_IN:N_PROJ_IN + n_rope] if rope else None
    at = N_PROJ_IN + n_rope
    consts = refs[at:at + N_PREP_CONST]
    proj_out = refs[at + N_PREP_CONST:at + N_PREP_CONST + N_PROJ_OUT]
    prep_out = refs[at + N_PREP_CONST + N_PROJ_OUT:at + N_PREP_CONST + N_PROJ_OUT + N_PREP_OUT]
    slab_scr, tail_scr = refs[-2:]
    j = pl.program_id(1)
    zero_row = jnp.zeros((1, D_RWKV_PAD), F32)

    def project():
        return _project(*proj_in, rope_refs, *proj_out)

    def features(next_row, first_tile):
        cur = slab_scr[...]
        prev_row = zero_row if first_tile else tail_scr[7:8, :]
        _rwkv_features(cur, prev_row, next_row, *consts, *prep_out)
        tail_scr[...] = cur[cur.shape[0] - 8:, :]

    @pl.when(j == 0)
    def _():
        slab_scr[...] = project()

    if nt > 1:
        @pl.when(j == 1)
        def _():
            slab = project()
            features(slab[0:1, :], True)
            slab_scr[...] = slab

    if nt > 2:
        @pl.when((j > 1) & (j < nt))
        def _():
            slab = project()
            features(slab[0:1, :], False)
            slab_scr[...] = slab

    @pl.when(j == nt)
    def _():
        features(zero_row, nt == 1)


def _in_proj_prep(x3, shift, scale, norm_g, w_in_p, gqk, grp, prep_consts, rope, tm):
    b, t, _ = x3.shape
    nt = t // tm
    bm = shift.shape[0]
    mod_map = (lambda i, j: (i, 0, 0)) if bm > 1 else (lambda i, j: (0, 0, 0))
    full = lambda a: pl.BlockSpec(a.shape, lambda i, j: (0,) * a.ndim)
    here = lambda j: jnp.minimum(j, nt - 1)
    back = lambda j: jnp.maximum(j - 1, 0)
    in_specs = [pl.BlockSpec((1, tm, D_MODEL), lambda i, j: (i, here(j), 0)),
                pl.BlockSpec((1, 1, D_MODEL), mod_map), pl.BlockSpec((1, 1, D_MODEL), mod_map),
                full(norm_g), full(w_in_p), full(gqk), full(grp)]
    args = [x3, shift, scale, norm_g, w_in_p, gqk, grp]
    if rope:
        in_specs += [pl.BlockSpec((tm, LANES), lambda i, j: (here(j), 0))] * 3
        args += list(_rope_tables(t))
    in_specs += [full(a) for a in prep_consts]
    args += list(prep_consts)
    one = jax.ShapeDtypeStruct((b, t, D_RWKV), F32)
    two = jax.ShapeDtypeStruct((2, b, t, D_RWKV), F32)
    narrow = jax.ShapeDtypeStruct((b, t, D_RWKV), BF16)
    s_one = pl.BlockSpec((1, tm, D_RWKV), lambda i, j: (i, back(j), 0))
    s_two = pl.BlockSpec((2, 1, tm, D_RWKV), lambda i, j: (0, i, back(j), 0))
    out_shape = (jax.ShapeDtypeStruct((b, N_KV_HEADS, GQA_GROUP, t, HEAD_DIM), BF16),
                 jax.ShapeDtypeStruct((b, N_KV_HEADS, t, HEAD_DIM), BF16),
                 jax.ShapeDtypeStruct((b, N_KV_HEADS, t, 2 * HEAD_DIM), BF16),
                 jax.ShapeDtypeStruct((b, t, D_KV), F32), jax.ShapeDtypeStruct((b, t, D_KV), F32),
                 one, two, two, one, one, two, narrow, narrow)
    out_specs = (pl.BlockSpec((1, N_KV_HEADS, GQA_GROUP, tm, HEAD_DIM), lambda i, j: (i, 0, 0, here(j), 0)),
                 pl.BlockSpec((1, N_KV_HEADS, tm, HEAD_DIM), lambda i, j: (i, 0, here(j), 0)),
                 pl.BlockSpec((1, N_KV_HEADS, tm, 2 * HEAD_DIM), lambda i, j: (i, 0, here(j), 0)),
                 pl.BlockSpec((1, tm, D_KV), lambda i, j: (i, here(j), 0)),
                 pl.BlockSpec((1, tm, D_KV), lambda i, j: (i, here(j), 0)),
                 s_one, s_two, s_two, s_one, s_one, s_two, s_one, s_one)
    return pl.pallas_call(
        functools.partial(_inproj_prep_kernel, rope=rope, nt=nt),
        grid=(b, nt + 1), in_specs=in_specs, out_specs=out_specs, out_shape=out_shape,
        scratch_shapes=[pltpu.VMEM((tm, D_RWKV_PAD), F32), pltpu.VMEM((8, D_RWKV_PAD), F32)],
        compiler_params=_cparams(("parallel", "arbitrary")),
        name="in_proj_rope" if rope else "in_proj",
    )(*args)


def _attn_kernel(q_ref, k_ref, v_ref, o_ref):
    g, tq, hd = q_ref.shape[2:]
    sub = min(ATTN_ROWS, tq)
    k = k_ref[0, 0]
    v = v_ref[0, 0]
    slabs = [slice(i * sub, (i + 1) * sub) for i in range(tq // sub)]
    qs = [q_ref[0, 0, :, sl, :].reshape(g * sub, hd) for sl in slabs]
    ss = [_dot_nt(q, k) for q in qs]
    ps = [jnp.exp2(s - jnp.max(s, axis=-1, keepdims=True)).astype(BF16) for s in ss]
    for sl, p in zip(slabs, ps):
        o = _dot(p, v)
        o = o[:, :hd] / pltpu.roll(o, hd, 1)[:, :hd]
        o_ref[0, sl, :] = jnp.concatenate([o[i * sub:(i + 1) * sub] for i in range(g)], axis=-1).astype(BF16)


def _attention(q5, k4, v4, tq):
    b, hk, g, t, hd = q5.shape
    tk = k4.shape[2]
    return pl.pallas_call(
        _attn_kernel,
        grid=(b, hk, t // tq),
        in_specs=[pl.BlockSpec((1, 1, g, tq, hd), lambda i, j, l: (i, j, 0, l, 0)),
                  pl.BlockSpec((1, 1, tk, hd), lambda i, j, l: (i, j, 0, 0)),
                  pl.BlockSpec((1, 1, tk, 2 * hd), lambda i, j, l: (i, j, 0, 0))],
        out_specs=pl.BlockSpec((1, tq, g * hd), lambda i, j, l: (i, l, j)),
        out_shape=jax.ShapeDtypeStruct((b, t, hk * g * hd), BF16),
        compiler_params=_cparams(("parallel", "parallel", "arbitrary")),
        name="attn",
    )(q5, k4, v4)


INV_BASE = 16


def _bdot(a, b):
    return _dot(a.astype(BF16), b.astype(BF16))


def _scan_kernel(*refs, chunk):
    ins, (s0_ref, yf_ref, yb_ref, sT_ref, s_scr) = (refs[0:6], refs[6:12]), refs[12:]
    y_refs = (yf_ref, yb_ref)
    c = chunk

    @pl.when(pl.program_id(1) == 0)
    def _():
        s_scr[...] = s0_ref[0]

    row = lax.broadcasted_iota(jnp.int32, (c, c), 0)
    col = lax.broadcasted_iota(jnp.int32, (c, c), 1)
    eye = (row == col).astype(F32)
    same_blk = {}
    n = INV_BASE
    while n <= c:
        sh = jnp.int32(n.bit_length() - 1)
        same_blk[n] = lax.shift_right_logical(row, sh) == lax.shift_right_logical(col, sh)
        n *= 2
    heads = range(N_RWKV_HEADS)
    sls = [slice(h * HEAD_DIM, (h + 1) * HEAD_DIM) for h in heads]

    incl, strict, ar_abs, ar_mid, kb_inv, kb_end, v, p_tot = [], [], [], [], [], [], [], []
    for d, (r_ref, lw_ref, k_ref, v_ref, a_ref, b_ref) in enumerate(ins):
        incl.append(row >= col if d == 0 else row <= col)
        strict.append(row > col if d == 0 else row < col)
        lw = lw_ref[0, 0]
        lw_hi, lw_lo = _split2(lw)
        inclb = jnp.where(incl[d], 1.0, 0.0).astype(BF16)
        cl = _dot(inclb, lw_hi) + _dot(inclb, lw_lo)
        tot = jnp.sum(lw, axis=0, keepdims=True)
        mid = 0.5 * tot
        e_inv = jnp.exp(mid - cl)
        e_end = jnp.exp(tot - cl)
        s_mid = jnp.exp(-mid)
        r_abs = r_ref[0] * jnp.exp(cl)
        a_abs = a_ref[0] * jnp.exp(cl - lw)
        ar_abs.append(jnp.concatenate([a_abs, r_abs], axis=0).astype(BF16))
        ar_mid.append(jnp.concatenate([a_abs * s_mid, r_abs * s_mid], axis=0).astype(BF16))
        kb_inv.append(jnp.concatenate([k_ref[0, 0] * e_inv, b_ref[0, 0] * e_inv], axis=0).astype(BF16))
        kb_end.append(jnp.concatenate([k_ref[0, 0] * e_end, b_ref[0, 0] * e_end], axis=0).astype(BF16))
        v.append(v_ref[0].astype(BF16))
        p_tot.append(jnp.exp(tot))

    chains = [(d, h) for d in range(2) for h in heads]
    ids = range(len(chains))
    s_old = [s_scr[d, h] for d, h in chains]
    vhs = [v[d][:, sls[h]] for d, h in chains]
    grams = [_dot_nt(ar_mid[d][:, sls[h]], kb_inv[d][:, sls[h]]) for d, h in chains]
    from_s = [_dot_nt(ar_abs[d][:, sls[h]], s_old[i].astype(BF16)) for i, (d, h) in enumerate(chains)]
    masked = [jnp.concatenate([jnp.where(strict[d], grams[i][:c, :c], 0.0),
                               jnp.where(incl[d], grams[i][c:, :c], 0.0)], axis=0) for i, (d, h) in enumerate(chains)]
    from_v = [_bdot(masked[i], vhs[i]) for i in ids]
    lmats = [jnp.where(strict[d], grams[i][:c, c:], 0.0) for i, (d, h) in enumerate(chains)]
    l0s = [jnp.where(same_blk[INV_BASE], lm, 0.0) for lm in lmats]
    xs = [eye - l0 for l0 in l0s]
    pws = [_bdot(l0, l0) for l0 in l0s]
    span = 2
    while 2 * span < INV_BASE:
        both = [_bdot(jnp.concatenate([xs[i], pws[i]], axis=0), pws[i]) for i in ids]
        xs = [xs[i] + both[i][:c] for i in ids]
        pws = [both[i][c:] for i in ids]
        span *= 2
    xs = [xs[i] + _bdot(xs[i], pws[i]) for i in ids]
    n = INV_BASE
    while n < c:
        pair = same_blk[2 * n] & jnp.logical_not(same_blk[n])
        ts = [_bdot(jnp.where(pair, lmats[i], 0.0), xs[i]) for i in ids]
        xs = [xs[i] - _bdot(xs[i], ts[i]) for i in ids]
        n *= 2
    us = [_bdot(xs[i], from_s[i][:c] + from_v[i][:c]) for i in ids]
    yu = [_bdot(jnp.where(incl[d], grams[i][c:, c:], 0.0), us[i]) for i, (d, h) in enumerate(chains)]
    ds = [_dot_tn(jnp.concatenate([vhs[i], (-us[i]).astype(BF16)], axis=0), kb_end[d][:, sls[h]])
          for i, (d, h) in enumerate(chains)]
    for d in range(2):
        y_refs[d][0] = jnp.concatenate([from_s[i][c:] + from_v[i][c:] - yu[i]
                                        for i, (dd, h) in enumerate(chains) if dd == d], axis=-1)
    for i, (d, h) in enumerate(chains):
        s_scr[d, h] = s_old[i] * p_tot[d][:, sls[h]] + ds[i]

    @pl.when(pl.program_id(1) == pl.num_programs(1) - 1)
    def _():
        sT_ref[0] = s_scr[...]


def _rwkv_scan(r, lw2, kd2, v, kk, bd2, s0, chunk):
    b, t, _ = r.shape
    nc = t // chunk
    in_specs, args = [], []
    for d in range(2):
        tmap = (lambda j: j) if d == 0 else (lambda j: nc - 1 - j)
        s_one = pl.BlockSpec((1, chunk, D_RWKV), lambda i, j, tmap=tmap: (i, tmap(j), 0))
        s_two = pl.BlockSpec((1, 1, chunk, D_RWKV), lambda i, j, tmap=tmap, d=d: (d, i, tmap(j), 0))
        in_specs += [s_one, s_two, s_two, s_one, s_one, s_two]
        args += [r, lw2, kd2, v, kk, bd2]
    s_st = pl.BlockSpec((1, 2, N_RWKV_HEADS, HEAD_DIM, HEAD_DIM), lambda i, j: (i, 0, 0, 0, 0))
    y_specs = tuple(pl.BlockSpec((1, chunk, D_RWKV), lambda i, j, tmap=tmap: (i, tmap(j), 0))
                    for tmap in ((lambda j: j), (lambda j: nc - 1 - j)))
    y_shape = jax.ShapeDtypeStruct((b, t, D_RWKV), F32)
    return pl.pallas_call(
        functools.partial(_scan_kernel, chunk=chunk),
        grid=(b, nc),
        in_specs=in_specs + [s_st],
        out_specs=y_specs + (s_st,),
        out_shape=(y_shape, y_shape, jax.ShapeDtypeStruct(s0.shape, F32)),
        scratch_shapes=[pltpu.VMEM((2, N_RWKV_HEADS, HEAD_DIM, HEAD_DIM), F32)],
        compiler_params=_cparams(("parallel", "arbitrary")),
        name="rwkv_scan",
    )(*args, s0)


def _outproj_kernel(x_ref, at_ref, yf_ref, yb_ref, bonus_ref, gate_ref, lng_ref, lnb_ref, grp_ref, wo_ref,
                    g1_ref, sh_ref, sc_ref, n2_ref, wr_ref, br_ref, x1_ref, hx_ref, cnt_ref):
    grp = grp_ref[...]
    y = yf_ref[0] + yb_ref[0]
    mean = _head_sum(y, grp) * (1.0 / HEAD_DIM)
    yc = y - mean
    var = _head_sum(yc * yc, grp) * (1.0 / HEAD_DIM)
    yn = yc * lax.rsqrt(var + GN_EPS) * lng_ref[...] + lnb_ref[...]
    rw_out = (yn + bonus_ref[0]) * gate_ref[0]
    mix = (_dot(at_ref[0], wo_ref[0:D_ATTN, :])
           + _dot(rw_out.astype(BF16), wo_ref[D_ATTN:, :]))
    x1 = x_ref[0] + g1_ref[0] * mix
    x1_ref[0] = x1
    ms = jnp.mean(x1 * x1, axis=-1, keepdims=True)
    h2 = x1 * lax.rsqrt(ms + NORM_EPS) * n2_ref[...]
    h2 = h2 * (1.0 + sc_ref[0]) + sh_ref[0]
    hx_ref[0, :, 0:D_MODEL] = h2.astype(BF16)
    logits = _dot3(h2, wr_ref[...]) + br_ref[...]
    lane = lax.broadcasted_iota(jnp.int32, logits.shape, 1)
    neg = -jnp.inf
    big = jnp.int32(1 << 20)
    lc = jnp.where(lane < N_GROUPS, logits, neg)
    mc = jnp.max(lc, axis=-1, keepdims=True)
    g_w = 1.0 / jnp.sum(jnp.exp(lc - mc), axis=-1, keepdims=True)
    g_idx = jnp.min(jnp.where(lc == mc, lane, big), axis=-1, keepdims=True)
    eid = lane - N_GROUPS
    in_grp = (eid >= 0) & (eid < N_EXPERTS) & (lax.shift_right_arithmetic(eid, 2) == g_idx)
    lf = jnp.where(in_grp, logits, neg)
    m1 = jnp.max(lf, axis=-1, keepdims=True)
    i1 = jnp.min(jnp.where(lf == m1, lane, big), axis=-1, keepdims=True)
    lf2 = jnp.where(lane == i1, neg, lf)
    m2 = jnp.max(lf2, axis=-1, keepdims=True)
    i2 = jnp.min(jnp.where(lf2 == m2, lane, big), axis=-1, keepdims=True)
    e2 = jnp.exp(m2 - m1)
    w1 = 1.0 / (1.0 + e2)
    w2 = e2 * w1
    cmb = g_w * (jnp.where(lane == i1, w1, 0.0) + jnp.where(lane == i2, w2, 0.0))
    rec = jnp.where(lane == 0, g_idx.astype(F32), cmb)
    rec_hi, rec_lo = _split2(rec)
    hx_ref[0, :, D_MODEL:D_MODEL + ROUTER_LANES] = rec_hi
    hx_ref[0, :, D_MODEL + ROUTER_LANES:] = rec_lo
    hot = jnp.where((lane == g_idx) & (lane < N_GROUPS), 1.0, 0.0)
    for s in range(cnt_ref.shape[1]):
        part = jnp.sum(hot[s * MOE_TM:(s + 1) * MOE_TM], axis=0, keepdims=True)
        cnt_ref[0, s] = jnp.broadcast_to(part, cnt_ref.shape[2:])


def _out_proj(x3, attn3, yf, yb, bonus, gate, ln_g, ln_b, grp, w_out_b, g1, sh2, sc2, norm2_g, w_r, b_r):
    b, t, _ = x3.shape
    tm = OUT_PROJ_TM if t % OUT_PROJ_TM == 0 else MOE_TM
    bm = g1.shape[0]
    mod_map = (lambda i, j: (i, 0, 0)) if bm > 1 else (lambda i, j: (0, 0, 0))
    tok = lambda w: pl.BlockSpec((1, tm, w), lambda i, j: (i, j, 0))
    full = lambda a: pl.BlockSpec(a.shape, lambda i, j: (0,) * a.ndim)
    mod = pl.BlockSpec((1, 1, D_MODEL), mod_map)
    return pl.pallas_call(
        _outproj_kernel,
        grid=(b, t // tm),
        in_specs=[tok(D_MODEL), tok(D_ATTN), tok(D_RWKV), tok(D_RWKV),
                  tok(D_RWKV), tok(D_RWKV), full(ln_g), full(ln_b), full(grp), full(w_out_b),
                  mod, mod, mod, full(norm2_g), full(w_r), full(b_r)],
        out_specs=(tok(D_MODEL), tok(MOE_W), pl.BlockSpec((1, tm // MOE_TM, 8, LANES), lambda i, j: (i, j, 0, 0))),
        out_shape=(jax.ShapeDtypeStruct((b, t, D_MODEL), F32), jax.ShapeDtypeStruct((b, t, MOE_W), BF16),
                   jax.ShapeDtypeStruct((b, t // MOE_TM, 8, LANES), F32)),
        compiler_params=_cparams(("parallel", "parallel")),
        name="out_proj",
    )(x3, attn3, yf, yb, bonus, gate, ln_g, ln_b, grp, w_out_b, g1, sh2, sc2, norm2_g, w_r, b_r)


def _tile_sort_matrix(rec_hi, plan_ref, tile):
    tm = rec_hi.shape[0]
    sel = ((lax.broadcasted_iota(jnp.int32, (8, LANES), 0) == 0)
           & (lax.broadcasted_iota(jnp.int32, (8, LANES), 1) == 0))
    g_row = _dot_nt(jnp.where(sel, 1.0, 0.0).astype(BF16), rec_hi)[0:1, :]
    g_col = rec_hi.astype(F32)[:, 0:1]
    r = lax.broadcasted_iota(jnp.int32, (tm, tm), 0)
    c = lax.broadcasted_iota(jnp.int32, (tm, tm), 1)
    earlier_same = jnp.where((g_col == g_row) & (r < c), 1.0, 0.0)
    pos_row = jnp.sum(earlier_same, axis=0, keepdims=True)
    for g in range(N_GROUPS):
        local = _plan_entry(plan_ref, tile, g)[2].astype(F32)
        pos_row = pos_row + jnp.where(g_row == float(g), local, 0.0)
    stage_row = lax.broadcasted_iota(jnp.int32, (MOE_STAGE, tm), 0).astype(F32)
    return jnp.where(stage_row == pos_row, 1.0, 0.0).astype(BF16)


def _plan_entry(plan_ref, tile, g):
    at = (tile * N_GROUPS + g) * 3
    return pl.multiple_of(plan_ref[at], SEG_ALIGN), plan_ref[at + 1], pl.multiple_of(plan_ref[at + 2], MOE_SUB)


def _for_each_granule(plan_ref, tile, fn):
    for g in range(N_GROUPS):
        start, n_gran, local = _plan_entry(plan_ref, tile, g)

        def body(s, carry, g=g, start=start, local=local):
            fn(g, pl.multiple_of(start + s * MOE_SUB, SEG_ALIGN), pl.multiple_of(local + s * MOE_SUB, MOE_SUB))
            return carry

        lax.fori_loop(0, n_gran, body, 0)


def _dispatch_kernel(plan_ref, hx_ref, xs_in_ref, xs_ref, buf, sem, *, tile0):
    del xs_in_ref
    i = pl.program_id(0)
    x = hx_ref[...]
    perm = _tile_sort_matrix(x[:, D_MODEL:D_MODEL + ROUTER_LANES], plan_ref, tile0 + i)
    staged = _dot(perm, x).astype(BF16)

    def copy(g, seg_row, stage_row):
        return pltpu.make_async_copy(buf.at[pl.ds(stage_row, MOE_SUB), :],
                                     xs_ref.at[pl.ds(seg_row, MOE_SUB), :], sem.at[g])

    @pl.when(i > 0)
    def _():
        _for_each_granule(plan_ref, tile0 + i - 1, lambda *a: copy(*a).wait())

    buf[...] = staged
    _for_each_granule(plan_ref, tile0 + i, lambda *a: copy(*a).start())

    @pl.when(i == pl.num_programs(0) - 1)
    def _():
        _for_each_granule(plan_ref, tile0 + i, lambda *a: copy(*a).wait())


def _moe_dispatch(plan, hx2, xs, tile0):
    n = hx2.shape[0]
    grid_spec = pltpu.PrefetchScalarGridSpec(
        num_scalar_prefetch=1, grid=(n // MOE_TM,),
        in_specs=[pl.BlockSpec((MOE_TM, MOE_W), lambda i, plan: (i, 0)), pl.BlockSpec(memory_space=pl.ANY)],
        out_specs=pl.BlockSpec(memory_space=pl.ANY),
        scratch_shapes=[pltpu.VMEM((MOE_STAGE, MOE_W), BF16), pltpu.SemaphoreType.DMA((N_GROUPS,))])
    return pl.pallas_call(
        functools.partial(_dispatch_kernel, tile0=tile0),
        grid_spec=grid_spec,
        out_shape=jax.ShapeDtypeStruct(xs.shape, xs.dtype),
        input_output_aliases={2: 0},
        compiler_params=_cparams(("arbitrary",)),
        name="moe_dispatch",
    )(plan, hx2, xs)


def _experts_kernel(tg_ref, xs_ref, wg_ref, wu_ref, wd_ref, ys_ref):
    g = tg_ref[pl.program_id(0)]

    @pl.when(g >= N_GROUPS)
    def _():
        ys_ref[...] = jnp.zeros_like(ys_ref)

    @pl.when(g < N_GROUPS)
    def _():
        x = xs_ref[...]
        h = x[:, 0:D_MODEL]
        rec = (x[:, D_MODEL:D_MODEL + ROUTER_LANES].astype(F32) + x[:, D_MODEL + ROUTER_LANES:].astype(F32))
        lane = lax.broadcasted_iota(jnp.int32, rec.shape, 1)
        first = N_GROUPS + EXPERTS_PER_GROUP * g
        scaled = []
        for e in range(EXPERTS_PER_GROUP):
            a = _dot(h, wg_ref[0, e])
            hid = a * jax.nn.sigmoid(a) * _dot(h, wu_ref[0, e])
            c_e = jnp.sum(jnp.where(lane == first + e, rec, 0.0), axis=-1, keepdims=True)
            scaled.append((hid * c_e).astype(BF16))
        ys_ref[...] = _dot(jnp.concatenate(scaled, axis=-1), wd_ref[0]).astype(BF16)


def _moe_experts(tile_group, xs, wg4, wu4, wd4):
    p = xs.shape[0]
    grp_map = lambda nd: (lambda j, tg: (jnp.minimum(tg[j], N_GROUPS - 1),) + (0,) * (nd - 1))
    grid_spec = pltpu.PrefetchScalarGridSpec(
        num_scalar_prefetch=1, grid=(p // MOE_TX,),
        in_specs=[pl.BlockSpec((MOE_TX, MOE_W), lambda j, tg: (j, 0)),
                  pl.BlockSpec((1,) + wg4.shape[1:], grp_map(wg4.ndim)),
                  pl.BlockSpec((1,) + wu4.shape[1:], grp_map(wu4.ndim)),
                  pl.BlockSpec((1,) + wd4.shape[1:], grp_map(wd4.ndim))],
        out_specs=pl.BlockSpec((MOE_TX, D_MODEL), lambda j, tg: (j, 0)))
    return pl.pallas_call(
        _experts_kernel,
        grid_spec=grid_spec,
        out_shape=jax.ShapeDtypeStruct((p, D_MODEL), BF16),
        compiler_params=_cparams(("arbitrary",)),
        name="moe_experts",
    )(tile_group, xs, wg4, wu4, wd4)


def _combine_kernel(plan_ref, rec_ref, x1_ref, g2_ref, ys_ref, o_ref, buf, sem, *, tile0):
    lin = pl.program_id(0) * pl.num_programs(1) + pl.program_id(1)
    n = pl.num_programs(0) * pl.num_programs(1)
    slot = lin % 2

    def copy(slot_t):
        return lambda g, seg_row, stage_row: pltpu.make_async_copy(
            ys_ref.at[pl.ds(seg_row, MOE_SUB), :], buf.at[slot_t, pl.ds(stage_row, MOE_SUB), :], sem.at[slot_t, g])

    @pl.when(lin == 0)
    def _():
        buf[...] = jnp.zeros_like(buf)
        _for_each_granule(plan_ref, tile0, lambda *a: copy(0)(*a).start())

    @pl.when(lin + 1 < n)
    def _():
        _for_each_granule(plan_ref, tile0 + lin + 1, lambda *a: copy(1 - slot)(*a).start())

    perm = _tile_sort_matrix(rec_ref[0], plan_ref, tile0 + lin)
    _for_each_granule(plan_ref, tile0 + lin, lambda *a: copy(slot)(*a).wait())
    y = _dot_tn(perm, buf[slot])
    o_ref[0] = x1_ref[0] + g2_ref[0] * y


def _moe_combine(plan, hx3, x1, g2, ys, tile0):
    b, t, _ = x1.shape
    bm = g2.shape[0]
    mod_map = (lambda i, j, plan: (i, 0, 0)) if bm > 1 else (lambda i, j, plan: (0, 0, 0))
    rec_blk = D_MODEL // ROUTER_LANES
    grid_spec = pltpu.PrefetchScalarGridSpec(
        num_scalar_prefetch=1, grid=(b, t // MOE_TM),
        in_specs=[pl.BlockSpec((1, MOE_TM, ROUTER_LANES), lambda i, j, plan: (i, j, rec_blk)),
                  pl.BlockSpec((1, MOE_TM, D_MODEL), lambda i, j, plan: (i, j, 0)),
                  pl.BlockSpec((1, 1, D_MODEL), mod_map),
                  pl.BlockSpec(memory_space=pl.ANY)],
        out_specs=pl.BlockSpec((1, MOE_TM, D_MODEL), lambda i, j, plan: (i, j, 0)),
        scratch_shapes=[pltpu.VMEM((2, MOE_STAGE, D_MODEL), BF16), pltpu.SemaphoreType.DMA((2, N_GROUPS))])
    return pl.pallas_call(
        functools.partial(_combine_kernel, tile0=tile0),
        grid_spec=grid_spec,
        out_shape=jax.ShapeDtypeStruct(x1.shape, F32),
        compiler_params=_cparams(("arbitrary", "arbitrary")),
        name="moe_combine",
    )(plan, hx3, x1, g2, ys)


def _moe_plan(cnt, n_rows):
    seg = (cnt + (SEG_ALIGN - 1)) // SEG_ALIGN * SEG_ALIGN
    used = (jnp.sum(seg, axis=0) + (MOE_TX - 1)) // MOE_TX * MOE_TX
    size = used + MOE_TX
    base = jnp.cumsum(size) - size
    start = base[None, :] + jnp.cumsum(seg, axis=0) - seg
    n_gran = (cnt + (MOE_SUB - 1)) // MOE_SUB
    stage = n_gran * MOE_SUB
    local = jnp.cumsum(stage, axis=1) - stage
    starts = jnp.arange(n_rows // MOE_TX, dtype=jnp.int32)[:, None] * MOE_TX
    inside = (starts >= base[None, :]) & (starts < (base + used)[None, :])
    tile_group = jnp.where(jnp.any(inside, axis=1), jnp.argmax(inside, axis=1), N_GROUPS)
    plan = jnp.stack([start, n_gran, local], axis=-1).reshape(-1)
    return plan.astype(jnp.int32), tile_group.astype(jnp.int32)


def _moe_rows(n_tokens):
    n_tiles = n_tokens // MOE_TM
    bound = n_tokens + n_tiles * N_GROUPS * (SEG_ALIGN - 1) + N_GROUPS * 2 * MOE_TX
    return (bound + MOE_TX - 1) // MOE_TX * MOE_TX


def _layer(x3, mod6, lp, t_tiles, ctx):
    b, t, _ = x3.shape
    sh1, sc1, g1, sh2, sc2, _ = mod6
    t_tiles = {name: min(size, t) for name, size in t_tiles.items()}
    rope = ctx is not None
    n = b * t
    prep_consts = [lp[name] for name in ("mu_p", "wd_cat", "w0_cat", "wa_cat", "a0_cat", "wg_p", "k_k", "k_a", "r_k",
                                         "grp")]
    q5, k4, v4, k, v, r, lw2, kd2, vv, kk, bd2, gate, bonus = _in_proj_prep(
        x3, sh1, sc1, lp["norm1_g"], lp["w_in_p"], lp["gqk"], lp["grp"], prep_consts, rope, t_tiles["in_proj"])
    if ctx is not None:
        ctx_k, ctx_v, ctx_state = ctx
        ck = ctx_k.transpose(0, 2, 1, 3)
        cv = ctx_v.transpose(0, 2, 1, 3)
        k4 = jnp.concatenate([k4, ck.astype(BF16)], axis=2)
        v4 = jnp.concatenate([v4, jnp.concatenate([cv, jnp.ones_like(cv)], axis=-1).astype(BF16)], axis=2)
        s0 = ctx_state
    else:
        s0 = jnp.zeros((b, 2, N_RWKV_HEADS, HEAD_DIM, HEAD_DIM), F32)
    attn3 = _attention(q5, k4, v4, t_tiles["attn"])
    yf, yb, s_t = _rwkv_scan(r, lw2, kd2, vv, kk, bd2, s0, t_tiles["chunk"])
    fb, ft = (1, n) if g1.shape[0] == 1 else (b, t)
    flat = lambda a: a.reshape(fb, ft, a.shape[-1])
    x1, hx, cnt = _out_proj(flat(x3), flat(attn3), flat(yf), flat(yb), flat(bonus), flat(gate), lp["ln_g"],
                            lp["ln_b"], lp["grp"], lp["w_out_b"], g1, sh2, sc2, lp["norm2_g"], lp["w_r"], lp["b_r"])
    return (x1, hx, cnt), k.reshape(b, t, N_KV_HEADS, HEAD_DIM), v.reshape(b, t, N_KV_HEADS, HEAD_DIM), s_t


def _moe_both(passes, g2s, lp):
    counts = [p[2][:, :, 0, :N_GROUPS].reshape(-1, N_GROUPS) for p in passes]
    tiles = [c.shape[0] for c in counts]
    n_rows = _moe_rows(sum(tiles) * MOE_TM)
    plan, tile_group = _moe_plan(jnp.concatenate(counts, axis=0).astype(jnp.int32), n_rows)
    xs = jnp.zeros((n_rows, MOE_W), BF16)
    tile0 = 0
    for (x1, hx, _), nt in zip(passes, tiles):
        xs = _moe_dispatch(plan, hx.reshape(-1, MOE_W), xs, tile0)
        tile0 += nt
    ys = _moe_experts(tile_group, xs, lp["wg4"], lp["wu4"], lp["wd4"])
    outs, tile0 = [], 0
    for (x1, hx, _), g2, nt in zip(passes, g2s, tiles):
        outs.append(_moe_combine(plan, hx, x1, g2, ys, tile0))
        tile0 += nt
    return outs


def _block_diag2(w):
    z, l, c = w.shape
    out = jnp.zeros((LANES, z * c), F32)
    for i in range(z):
        out = out.at[i * l:(i + 1) * l, i * c:(i + 1) * c].set(w[i])
    return out


def _layer_params(l, w_in, norm1_g, norm2_g, mu_shift, q_norm_g, k_norm_g, w0, w_lora_up, a0, a_lora_up, g_lora_up,
                  k_k, k_a, r_k, ln_x_g, ln_x_b, w_out, router_c, router_c_b, router_f, router_f_b,
                  exp_gate, exp_up, exp_down):
    lane = np.arange(LANES)
    grp = jnp.asarray((lane[:, None] // HEAD_DIM) == (lane[None, :] // HEAD_DIM), BF16)
    pad_in = D_IN_PAD - w_in.shape[2]
    wd_cat = _block_diag2(w_lora_up[l])
    wa_cat = jnp.roll(_block_diag2(a_lora_up[l]), 2 * DECAY_LORA, axis=0)
    w_r = jnp.zeros((D_MODEL, ROUTER_LANES), F32)
    w_r = w_r.at[:, :N_GROUPS].set(router_c[l]).at[:, N_GROUPS:N_GROUPS + N_EXPERTS].set(router_f[l])
    b_r = jnp.zeros((1, ROUTER_LANES), F32)
    b_r = b_r.at[0, :N_GROUPS].set(router_c_b[l]).at[0, N_GROUPS:N_GROUPS + N_EXPERTS].set(router_f_b[l])

    by_group = lambda w: w.astype(BF16).reshape(N_GROUPS, EXPERTS_PER_GROUP, D_MODEL, D_EXPERT)

    return dict(
        grp=grp,
        norm1_g=norm1_g[l].reshape(1, D_MODEL), norm2_g=norm2_g[l].reshape(1, D_MODEL),
        w_in_p=jnp.pad(w_in[l], ((0, 0), (0, pad_in))).astype(BF16),
        gqk=jnp.concatenate([jnp.tile(q_norm_g[l], N_Q_HEADS), jnp.tile(k_norm_g[l], N_KV_HEADS)]).reshape(1, -1),
        mu_p=jnp.pad(mu_shift[l], ((0, 0), (0, D_RWKV_PAD - D_RWKV_IN))),
        wd_cat=wd_cat.astype(BF16), w0_cat=w0[l].reshape(1, 2 * D_RWKV),
        wa_cat=wa_cat.astype(BF16), a0_cat=a0[l].reshape(1, 2 * D_RWKV),
        wg_p=jnp.pad(g_lora_up[l], ((0, LANES - GATE_LORA), (0, 0))).astype(BF16),
        k_k=k_k[l].reshape(1, D_RWKV), k_a=k_a[l].reshape(1, D_RWKV), r_k=r_k[l].reshape(1, D_RWKV),
        ln_g=ln_x_g[l].reshape(1, D_RWKV), ln_b=ln_x_b[l].reshape(1, D_RWKV),
        w_out_b=w_out[l].astype(BF16), w_r=w_r, b_r=b_r,
        wg4=by_group(exp_gate[l]), wu4=by_group(exp_up[l]),
        wd4=exp_down[l].astype(BF16).reshape(N_GROUPS, EXPERTS_PER_GROUP * D_EXPERT, D_MODEL),
    )


CTX_TILES = dict(in_proj=256, attn=256, chunk=128)
SMP_TILES = dict(in_proj=256, attn=256, chunk=128)


def kernel(x_prompt, x_sample, cache_k, cache_v, state_rwkv, c, c_ctx, w_mod, b_mod, norm1_g, norm2_g, w_in, mu_shift, q_norm_g, k_norm_g, w0, w_lora_up, a0, a_lora_up, g_lora_up, k_k, k_a, r_k, ln_x_g, ln_x_b, w_out, router_c, router_c_b, router_f, router_f_b, exp_gate, exp_up, exp_down):
    depth = w_mod.shape[0]
    db = x_sample.shape[0]
    y_prompt, y_sample = x_prompt, x_sample
    ks, vs, ss = [], [], []
    cond = jnp.zeros((8, D_MODEL), F32).at[:db].set(c).at[db].set(c_ctx)
    for l in range(depth):
        lp = _layer_params(l, w_in, norm1_g, norm2_g, mu_shift, q_norm_g, k_norm_g, w0, w_lora_up, a0, a_lora_up,
                           g_lora_up, k_k, k_a, r_k, ln_x_g, ln_x_b, w_out, router_c, router_c_b, router_f,
                           router_f_b, exp_gate, exp_up, exp_down)
        mod = _modulation(cond, w_mod[l], b_mod[l])
        mod_s = [mod[:db, i * D_MODEL:(i + 1) * D_MODEL].reshape(db, 1, D_MODEL) for i in range(6)]
        mod_c = [mod[db:db + 1, i * D_MODEL:(i + 1) * D_MODEL].reshape(1, 1, D_MODEL) for i in range(6)]
        pre_c, k_l, v_l, s_l = _layer(y_prompt, mod_c, lp, CTX_TILES, None)
        ks.append(k_l)
        vs.append(v_l)
        ss.append(s_l)
        pre_s, _, _, _ = _layer(y_sample, mod_s, lp, SMP_TILES, (cache_k[:, l], cache_v[:, l], state_rwkv[:, l]))
        out_c, out_s = _moe_both([pre_c, pre_s], [mod_c[5], mod_s[5]], lp)
        y_prompt, y_sample = out_c.reshape(y_prompt.shape), out_s.reshape(y_sample.shape)
    return (y_prompt, y_sample, jnp.stack(ks, axis=1), jnp.stack(vs, axis=1), jnp.stack(ss, axis=1))
```
